```python
import math
import jax, jax.numpy as jnp
from jax import lax
import numpy as np

D_MODEL = 2048
BATCH = 2
SEQ = 8192
DEPTH = 1

EPS = 1e-6
Q_BLOCK = 128

DA_HEADS = 8
DA_QK_DIM = 64
DA_V_DIM = 128

DL_GROUPS = ((128, 1), (512, 4), (2048, 16))
DL_N_GROUPS = 3
DL_HEADS_PER_GROUP = 4
DL_HEAD_DIM = 128

DA_Q_COLS = DA_HEADS * 2 * DA_QK_DIM
DA_K_COLS = DA_HEADS * 2 * DA_QK_DIM
DA_V_COLS = DA_HEADS * DA_V_DIM
DL_COLS = DL_N_GROUPS * DL_HEADS_PER_GROUP * DL_HEAD_DIM
GATE_COLS = D_MODEL
N_IN = DA_Q_COLS + DA_K_COLS + DA_V_COLS + 3 * DL_COLS + 2 * GATE_COLS
DA_OUT = DA_HEADS * DA_V_DIM
DL_OUT = DL_HEADS_PER_GROUP * DL_HEAD_DIM

MOE_GROUPS = 4
MOE_EXPERTS_PER_GROUP = 8
MOE_N_EXPERTS = MOE_GROUPS * MOE_EXPERTS_PER_GROUP
MOE_TOP_K = 2
MOE_D_FF = 1024
MOE_BLOCK = 128

kernel_name = "hybrid_diffattn_dilated_hiermoe_block"


def rmsnorm(t, gain):
    tf = t.astype(jnp.float32)
    tf = tf * lax.rsqrt(jnp.mean(tf * tf, axis=-1, keepdims=True) + EPS)
    return (tf * gain.astype(jnp.float32)).astype(t.dtype)


def alibi_slopes(n):
    return jnp.asarray(np.array([2.0 ** (-8.0 * (h + 1) / n) for h in range(n)], dtype=np.float32))


def diff_attention(q, k, v, lam, lam_init, sub_gain, slopes):
    B, S, H, _, dk = q.shape
    dv = v.shape[-1]
    nblk = S // Q_BLOCK
    scale = dk ** -0.5
    kpos = jnp.arange(S)
    qb = q.reshape(B, nblk, Q_BLOCK, H, 2, dk).transpose(1, 0, 2, 3, 4, 5)

    def block(args):
        i, qblk = args
        qpos = i * Q_BLOCK + jnp.arange(Q_BLOCK)
        dist = (qpos[:, None] - kpos[None, :]).astype(jnp.float32)
        s = jnp.einsum('bqhmd,bkhmd->bhmqk', qblk, k).astype(jnp.float32) * scale
        s = s - slopes[None, :, None, None, None] * dist
        s = jnp.where(dist >= 0, s, -jnp.inf)
        p = jax.nn.softmax(s, axis=-1)
        a = p[:, :, 0] - lam * p[:, :, 1]
        return jnp.einsum('bhqk,bkhd->bqhd', a.astype(v.dtype), v)

    o = lax.map(block, (jnp.arange(nblk), qb))
    o = o.transpose(1, 0, 2, 3, 4).reshape(B, S, H, dv)
    return rmsnorm(o, sub_gain) * (1.0 - lam_init)


def dilated_group(q, k, v, window, dilation, slopes):
    B, S, H, dh = q.shape
    span = window // dilation
    L = S // dilation
    nb = -(-L // span)
    Lp = nb * span

    def to_sub(t):
        t = t.reshape(B, L, dilation, H, dh).transpose(0, 2, 1, 3, 4)
        t = jnp.pad(t, ((0, 0), (0, 0), (0, Lp - L), (0, 0), (0, 0)))
        return t.reshape(B, dilation, nb, span, H, dh)

    def with_prev(t):
        prev = jnp.pad(t, ((0, 0), (0, 0), (1, 0), (0, 0), (0, 0), (0, 0)))[:, :, :-1]
        return jnp.concatenate([prev, t], axis=3)

    qs = to_sub(q)
    kc = with_prev(to_sub(k))
    vc = with_prev(to_sub(v))
    s = jnp.einsum('brnqhd,brnkhd->brnhqk', qs, kc).astype(jnp.float32) * (dh ** -0.5)
    qi = jnp.arange(span)
    kj = jnp.arange(2 * span)
    step = qi[:, None] + span - kj[None, :]
    blk = jnp.arange(nb)
    valid = (step >= 0) & (step <= span) & ((blk[:, None, None] * span - span + kj[None, None, :]) >= 0)
    s = s - slopes[:, None, None] * (dilation * step).astype(jnp.float32)
    s = jnp.where(valid[:, None], s, -jnp.inf)
    m = jnp.max(s, axis=-1, keepdims=True)
    p = jnp.exp(s - m)
    den = jnp.sum(p, axis=-1, keepdims=True)
    o = jnp.einsum('brnhqk,brnkhd->brnqhd', (p / den).astype(v.dtype), vc)
    lse = (m + jnp.log(den))[..., 0]
    o = o.reshape(B, dilation, Lp, H, dh)[:, :, :L].transpose(0, 2, 1, 3, 4).reshape(B, S, H, dh)
    lse = lse.transpose(0, 1, 2, 4, 3).reshape(B, dilation, Lp, H)[:, :, :L]
    lse = lse.transpose(0, 2, 1, 3).reshape(B, S, H)
    return o, lse


def hier_moe(h, w_group_router, w_expert_router, w_gate_up, w_down):
    B, S, D = h.shape
    T = B * S
    TK = T * MOE_TOP_K
    xt = h.reshape(T, D)
    g_prob = jax.nn.softmax((xt @ w_group_router).astype(jnp.float32), axis=-1)
    g_w, g_idx = lax.top_k(g_prob, 1)
    e_logits = (xt @ w_expert_router).astype(jnp.float32).reshape(T, MOE_GROUPS, MOE_EXPERTS_PER_GROUP)
    e_in_group = jnp.take_along_axis(e_logits, g_idx[:, :, None], axis=1)[:, 0]
    e_top, e_local = lax.top_k(e_in_group, MOE_TOP_K)
    e_w = jax.nn.softmax(e_top, axis=-1) * g_w
    e_idx = g_idx * MOE_EXPERTS_PER_GROUP + e_local

    flat_e = e_idx.reshape(-1)
    order = jnp.argsort(flat_e)
    sorted_e = flat_e[order]
    tok = order // MOE_TOP_K
    sizes = jnp.bincount(flat_e, length=MOE_N_EXPERTS).astype(jnp.int32)
    start = jnp.cumsum(sizes) - sizes
    padded = ((sizes + MOE_BLOCK - 1) // MOE_BLOCK) * MOE_BLOCK
    pad_end = jnp.cumsum(padded)
    pad_start = pad_end - padded
    dest = pad_start[sorted_e] + (jnp.arange(TK) - start[sorted_e])
    n_blocks = -(-TK // MOE_BLOCK) + MOE_N_EXPERTS
    x_pad = jnp.zeros((n_blocks * MOE_BLOCK, D), xt.dtype).at[dest].set(xt[tok])
    blk_expert = jnp.minimum(
        jnp.searchsorted(pad_end, jnp.arange(n_blocks) * MOE_BLOCK, side='right'),
        MOE_N_EXPERTS - 1)

    def expert_block(args):
        e, xb = args
        gate, up = jnp.split(xb @ w_gate_up[e], 2, axis=-1)
        return (jax.nn.silu(gate) * up) @ w_down[e]

    y_pad = lax.map(expert_block, (blk_expert, x_pad.reshape(n_blocks, MOE_BLOCK, D)))
    ys = y_pad.reshape(n_blocks * MOE_BLOCK, D)[dest]
    ys = ys * e_w.reshape(-1)[order][:, None].astype(ys.dtype)
    y = jnp.zeros_like(xt).at[tok].add(ys)
    return y.reshape(B, S, D)


def setup_inputs(seed: int = 0) -> dict:
    key = jax.random.key(seed)
    ks = jax.random.split(key, 20)
    f32 = jnp.float32

    def nrm(k, shape, scale):
        return jax.random.normal(k, shape, f32) * scale

    def gain(k, shape):
        return 1.0 + 0.05 * jax.random.normal(k, shape, f32)

    return {
        "x": jax.random.normal(ks[0], (BATCH, SEQ, D_MODEL), f32),
        "norm_mix": gain(ks[1], (DEPTH, D_MODEL)),
        "w_in": nrm(ks[2], (DEPTH, D_MODEL, N_IN), D_MODEL ** -0.5),
        "da_q_norm": gain(ks[3], (DEPTH, DA_QK_DIM)),
        "da_k_norm": gain(ks[4], (DEPTH, DA_QK_DIM)),
        "da_lambda_q": nrm(ks[5], (DEPTH, 2, DA_QK_DIM), 0.1),
        "da_lambda_k": nrm(ks[6], (DEPTH, 2, DA_QK_DIM), 0.1),
        "da_sub_norm": gain(ks[7], (DEPTH, DA_V_DIM)),
        "dl_q_norm": gain(ks[8], (DEPTH, DL_HEAD_DIM)),
        "dl_k_norm": gain(ks[9], (DEPTH, DL_HEAD_DIM)),
        "w_branch_a": nrm(ks[10], (DEPTH, DA_OUT, D_MODEL), DA_OUT ** -0.5),
        "w_branch_b": nrm(ks[11], (DEPTH, DL_OUT, D_MODEL), DL_OUT ** -0.5),
        "w_out": nrm(ks[12], (DEPTH, D_MODEL, D_MODEL), D_MODEL ** -0.5),
        "norm_ffn": gain(ks[13], (DEPTH, D_MODEL)),
        "w_group_router": nrm(ks[14], (DEPTH, D_MODEL, MOE_GROUPS), D_MODEL ** -0.5),
        "w_expert_router": nrm(ks[15], (DEPTH, D_MODEL, MOE_N_EXPERTS), D_MODEL ** -0.5),
        "w_gate_up": nrm(ks[16], (DEPTH, MOE_N_EXPERTS, D_MODEL, 2 * MOE_D_FF), D_MODEL ** -0.5),
        "w_down": nrm(ks[17], (DEPTH, MOE_N_EXPERTS, MOE_D_FF, D_MODEL), MOE_D_FF ** -0.5),
    }


def reference(x, norm_mix, w_in, da_q_norm, da_k_norm, da_lambda_q, da_lambda_k, da_sub_norm,
              dl_q_norm, dl_k_norm, w_branch_a, w_branch_b, w_out, norm_ffn,
              w_group_router, w_expert_router, w_gate_up, w_down):
    B, S, D = x.shape
    da_slopes = alibi_slopes(DA_HEADS)
    dl_slopes = alibi_slopes(DL_N_GROUPS * DL_HEADS_PER_GROUP)
    offs = np.cumsum([0, DA_Q_COLS, DA_K_COLS, DA_V_COLS, DL_COLS, DL_COLS, DL_COLS, GATE_COLS, GATE_COLS])

    for l in range(DEPTH):
        lam_init = 0.8 - 0.6 * math.exp(-0.3 * l)
        h = rmsnorm(x, norm_mix[l])
        proj = h @ w_in[l]
        cols = [proj[..., int(offs[i]):int(offs[i + 1])] for i in range(8)]
        da_q = rmsnorm(cols[0].reshape(B, S, DA_HEADS, 2, DA_QK_DIM), da_q_norm[l])
        da_k = rmsnorm(cols[1].reshape(B, S, DA_HEADS, 2, DA_QK_DIM), da_k_norm[l])
        da_v = cols[2].reshape(B, S, DA_HEADS, DA_V_DIM)
        dl_q = rmsnorm(cols[3].reshape(B, S, DL_N_GROUPS, DL_HEADS_PER_GROUP, DL_HEAD_DIM), dl_q_norm[l])
        dl_k = rmsnorm(cols[4].reshape(B, S, DL_N_GROUPS, DL_HEADS_PER_GROUP, DL_HEAD_DIM), dl_k_norm[l])
        dl_v = cols[5].reshape(B, S, DL_N_GROUPS, DL_HEADS_PER_GROUP, DL_HEAD_DIM)
        gate_a, gate_b = cols[6], cols[7]

        lq = da_lambda_q[l].astype(jnp.float32)
        lk = da_lambda_k[l].astype(jnp.float32)
        lam = jnp.exp(jnp.sum(lq[0] * lk[0])) - jnp.exp(jnp.sum(lq[1] * lk[1])) + lam_init
        o_a = diff_attention(da_q, da_k, da_v, lam, lam_init, da_sub_norm[l], da_slopes)
        o_a = o_a.reshape(B, S, DA_OUT)

        outs, lses = [], []
        for g, (window, dilation) in enumerate(DL_GROUPS):
            o_g, lse_g = dilated_group(dl_q[:, :, g], dl_k[:, :, g], dl_v[:, :, g], window, dilation,
                                       dl_slopes[g * DL_HEADS_PER_GROUP:(g + 1) * DL_HEADS_PER_GROUP])
            outs.append(o_g)
            lses.append(lse_g)
        w_grp = jax.nn.softmax(jnp.stack(lses, axis=0), axis=0)
        o_b = jnp.einsum('gbsh,gbshd->bshd', w_grp.astype(x.dtype), jnp.stack(outs, axis=0))
        o_b = o_b.reshape(B, S, DL_OUT)

        mixed = jax.nn.sigmoid(gate_a) * (o_a @ w_branch_a[l]) + jax.nn.sigmoid(gate_b) * (o_b @ w_branch_b[l])
        x = x + mixed @ w_out[l]

        x = x + hier_moe(rmsnorm(x, norm_ffn[l]), w_group_router[l], w_expert_router[l],
                         w_gate_up[l], w_down[l])
    return x
```

```python
import functools
import math

import jax
import jax.numpy as jnp
import numpy as np
from jax import lax
from jax.experimental import pallas as pl
from jax.experimental.pallas import tpu as pltpu

F32 = jnp.float32
BF16 = jnp.bfloat16

EPS = 1e-6
LOG2E = 1.4426950408889634
NEG_BIG = -1e30

DA_HEADS = 8
DA_QK_DIM = 64
DA_V_DIM = 128
DL_GROUPS = ((128, 1), (512, 4), (2048, 16))
DL_HEADS_PER_GROUP = 4
DL_HEAD_DIM = 128
DL_SPAN = 128
MOE_GROUPS = 4
MOE_EXPERTS_PER_GROUP = 8
MOE_N_EXPERTS = 32
MOE_TOP_K = 2
MOE_BLOCK = 128

LANES = 128
COL_TILE = 512
VMEM_LIMIT = 56 * 1024 * 1024

CT_GATE_A, CT_GATE_B, CT_DA_Q, CT_DA_K, CT_DA_V, CT_DL_Q, CT_DL_K, CT_DL_V, CT_END = 0, 4, 8, 10, 12, 14, 17, 20, 23


def _params(sem, vmem=VMEM_LIMIT):
    return pltpu.CompilerParams(dimension_semantics=sem, vmem_limit_bytes=vmem)


def _dot(a, b):
    return jnp.dot(a, b, preferred_element_type=F32)


def _dot_nt(a, b):
    return lax.dot_general(a, b, (((1,), (1,)), ((), ())), preferred_element_type=F32)


def _inproj_kernel(x_ref, g_ref, w_ref, gain_ref, o_ref, h_scr):
    j = pl.program_id(1)

    @pl.when(j == 0)
    def _():
        x = x_ref[...]
        ms = jnp.mean(x * x, axis=-1, keepdims=True)
        h_scr[...] = (x * lax.rsqrt(ms + EPS) * g_ref[...]).astype(BF16)

    y = _dot(h_scr[...], w_ref[...])
    gain = gain_ref[...]
    heads = COL_TILE // LANES

    is64 = (j >= CT_DA_Q) & (j < CT_DA_V)
    is128 = (j >= CT_DL_Q) & (j < CT_DL_V)

    @pl.when(is64)
    def _():
        for h in range(heads):
            sl = slice(h * LANES, (h + 1) * LANES)
            yh = y[:, sl]
            sq = yh * yh
            lo = lax.broadcasted_iota(jnp.int32, yh.shape, 1) < DA_QK_DIM
            s_lo = jnp.sum(jnp.where(lo, sq, 0.0), axis=-1, keepdims=True)
            s_hi = jnp.sum(jnp.where(lo, 0.0, sq), axis=-1, keepdims=True)
            r = jnp.where(lo, lax.rsqrt(s_lo * (1.0 / DA_QK_DIM) + EPS),
                          lax.rsqrt(s_hi * (1.0 / DA_QK_DIM) + EPS))
            o_ref[:, sl] = (yh * r * gain[:, sl]).astype(o_ref.dtype)

    @pl.when(is128)
    def _():
        for h in range(heads):
            sl = slice(h * LANES, (h + 1) * LANES)
            yh = y[:, sl]
            ss = jnp.sum(yh * yh, axis=-1, keepdims=True)
            r = lax.rsqrt(ss * (1.0 / DL_HEAD_DIM) + EPS)
            o_ref[:, sl] = (yh * r * gain[:, sl]).astype(o_ref.dtype)

    @pl.when(jnp.logical_not(is64 | is128))
    def _():
        o_ref[...] = y.astype(o_ref.dtype)


def _inproj(x2, gain_mix, w_bf, gain_tab, tm=1024):
    T, D = x2.shape
    N = w_bf.shape[1]
    return pl.pallas_call(
        _inproj_kernel,
        grid=(T // tm, N // COL_TILE),
        in_specs=[
            pl.BlockSpec((tm, D), lambda i, j: (i, 0)),
            pl.BlockSpec((1, D), lambda i, j: (0, 0)),
            pl.BlockSpec((D, COL_TILE), lambda i, j: (0, j)),
            pl.BlockSpec((None, 1, COL_TILE), lambda i, j: (j, 0, 0)),
        ],
        out_specs=pl.BlockSpec((tm, COL_TILE), lambda i, j: (i, j)),
        out_shape=jax.ShapeDtypeStruct((T, N), BF16),
        scratch_shapes=[pltpu.VMEM((tm, D), BF16)],
        compiler_params=_params(("parallel", "arbitrary")),
    )(x2, gain_mix, w_bf, gain_tab)


def _da_kernel(q_ref, k_ref, v_ref, lq_ref, lk_ref, sg_ref, o_ref, m_scr, l_scr, acc_scr,
               *, tq, lam_init):
    h = pl.program_id(1)
    qi = pl.program_id(2)
    slope2 = jnp.exp2(-(h + 1).astype(F32)) * LOG2E

    q = q_ref[...]
    lane = lax.broadcasted_iota(jnp.int32, q.shape, 1)
    qs = (jnp.where(lane < DA_QK_DIM, q, jnp.zeros_like(q)),
          jnp.where(lane >= DA_QK_DIM, q, jnp.zeros_like(q)))

    row = lax.broadcasted_iota(jnp.int32, (tq, tq), 0)
    col = lax.broadcasted_iota(jnp.int32, (tq, tq), 1)
    rel = col - row
    bias = slope2 * rel.astype(F32)

    m_scr[...] = jnp.full(m_scr.shape, NEG_BIG, F32)
    l_scr[...] = jnp.zeros(l_scr.shape, F32)
    acc_scr[...] = jnp.zeros(acc_scr.shape, F32)

    def step(ki, masked):
        off = pl.multiple_of(ki * tq, tq)
        k = k_ref[pl.ds(off, tq), :]
        v = v_ref[pl.ds(off, tq), :]
        c = slope2 * ((ki - qi) * tq).astype(F32)
        for mi in range(2):
            s = _dot_nt(qs[mi], k) + bias
            if masked:
                s = jnp.where(rel <= 0, s, NEG_BIG)
            m_prev = m_scr[mi]
            m_new = jnp.maximum(m_prev, jnp.max(s, axis=-1, keepdims=True) + c)
            alpha = jnp.exp2(m_prev - m_new)
            p = jnp.exp2(s - (m_new - c))
            l_scr[mi] = alpha * l_scr[mi] + jnp.sum(p, axis=-1, keepdims=True)
            acc_scr[mi] = alpha * acc_scr[mi] + _dot(p.astype(BF16), v)
            m_scr[mi] = m_new

    def body(ki, carry):
        step(ki, False)
        return carry

    lax.fori_loop(0, qi, body, 0)
    step(qi, True)

    lam_e = jnp.exp(jnp.sum(lq_ref[...] * lk_ref[...], axis=-1, keepdims=True))
    lam = lam_e[0:1, :] - lam_e[1:2, :] + lam_init
    o = acc_scr[0] / l_scr[0] - lam * (acc_scr[1] / l_scr[1])
    ms = jnp.mean(o * o, axis=-1, keepdims=True)
    o = o * lax.rsqrt(ms + EPS) * sg_ref[...] * (1.0 - lam_init)
    o_ref[...] = o.astype(o_ref.dtype)


def _diff_attention(proj, lam_q, lam_k, sub_gain, B, S, lam_init, tq=512):
    T = proj.shape[0]
    nq = S // tq
    lb = LANES
    q_blk0, k_blk0, v_blk0 = (CT_DA_Q * COL_TILE) // lb, (CT_DA_K * COL_TILE) // lb, (CT_DA_V * COL_TILE) // lb
    return pl.pallas_call(
        functools.partial(_da_kernel, tq=tq, lam_init=lam_init),
        grid=(B, DA_HEADS, nq),
        in_specs=[
            pl.BlockSpec((tq, lb), lambda b, h, i: (b * nq + i, q_blk0 + h)),
            pl.BlockSpec((S, lb), lambda b, h, i: (b, k_blk0 + h)),
            pl.BlockSpec((S, lb), lambda b, h, i: (b, v_blk0 + h)),
            pl.BlockSpec((2, DA_QK_DIM), lambda b, h, i: (0, 0)),
            pl.BlockSpec((2, DA_QK_DIM), lambda b, h, i: (0, 0)),
            pl.BlockSpec((1, DA_V_DIM), lambda b, h, i: (0, 0)),
        ],
        out_specs=pl.BlockSpec((tq, lb), lambda b, h, i: (b * nq + i, h)),
        out_shape=jax.ShapeDtypeStruct((T, DA_HEADS * DA_V_DIM), BF16),
        scratch_shapes=[pltpu.VMEM((2, tq, 1), F32), pltpu.VMEM((2, tq, 1), F32),
                        pltpu.VMEM((2, tq, DA_V_DIM), F32)],
        compiler_params=_params(("parallel", "parallel", "arbitrary")),
    )(proj, proj, proj, lam_q, lam_k, sub_gain)


def _dl_kernel(q_ref, kc_ref, kp_ref, vc_ref, vp_ref, o_ref, lse_ref, *, slopes2, tq):
    n = pl.program_id(2)
    sp = DL_SPAN
    row = lax.broadcasted_iota(jnp.int32, (sp, sp), 0)
    col = lax.broadcasted_iota(jnp.int32, (sp, sp), 1)
    dcur = row - col
    cur_ok = dcur >= 0
    prev_ok = dcur <= 0
    dcur_f = dcur.astype(F32)
    for hh in range(DL_HEADS_PER_GROUP):
        hs = slice(hh * LANES, (hh + 1) * LANES)
        bias_c = -slopes2[hh] * dcur_f
        bias_p = -slopes2[hh] * (dcur_f + float(sp))
        for j in range(tq // sp):
            rs = slice(j * sp, (j + 1) * sp)
            q = q_ref[rs, hs]
            kc = kc_ref[rs, hs]
            vc = vc_ref[rs, hs]
            if j == 0:
                kp = kp_ref[:, hs]
                vp = vp_ref[:, hs]
                p_ok = prev_ok & (n > 0)
            else:
                ps = slice((j - 1) * sp, j * sp)
                kp = kc_ref[ps, hs]
                vp = vc_ref[ps, hs]
                p_ok = prev_ok
            s_c = jnp.where(cur_ok, _dot_nt(q, kc) + bias_c, NEG_BIG)
            s_p = jnp.where(p_ok, _dot_nt(q, kp) + bias_p, NEG_BIG)
            m = jnp.maximum(jnp.max(s_c, axis=-1, keepdims=True), jnp.max(s_p, axis=-1, keepdims=True))
            p_c = jnp.exp2(s_c - m)
            p_p = jnp.exp2(s_p - m)
            den = jnp.sum(p_c, axis=-1, keepdims=True) + jnp.sum(p_p, axis=-1, keepdims=True)
            acc = _dot(p_c.astype(BF16), vc) + _dot(p_p.astype(BF16), vp)
            o_ref[rs, hs] = acc / den
            lse_ref[rs, hs] = jnp.broadcast_to(m + jnp.log2(den), (sp, LANES))


def _dilated_group(proj, g, B, S):
    window, d = DL_GROUPS[g]
    assert window // d == DL_SPAN
    L = S // d
    tq = min(512, L)
    assert L % tq == 0 and tq % DL_SPAN == 0
    n_in = proj.shape[1]
    ct = n_in // COL_TILE
    pv = proj.reshape(B, L, d * n_in)
    nh = DL_HEADS_PER_GROUP * DL_N_SLOPES_DIV
    slopes2 = tuple(2.0 ** (-8.0 * (g * DL_HEADS_PER_GROUP + hh + 1) / nh) * d * LOG2E
                    for hh in range(DL_HEADS_PER_GROUP))
    spb = tq // DL_SPAN
    qc, kc, vc = CT_DL_Q + g, CT_DL_K + g, CT_DL_V + g
    cur = lambda c: pl.BlockSpec((None, tq, COL_TILE), lambda b, r, n: (b, n, r * ct + c))
    prev = lambda c: pl.BlockSpec((None, DL_SPAN, COL_TILE),
                                  lambda b, r, n: (b, jnp.maximum(n * spb - 1, 0), r * ct + c))
    out_spec = pl.BlockSpec((None, tq, COL_TILE), lambda b, r, n: (b, n, r))
    o, lse = pl.pallas_call(
        functools.partial(_dl_kernel, slopes2=slopes2, tq=tq),
        grid=(B, d, L // tq),
        in_specs=[cur(qc), cur(kc), prev(kc), cur(vc), prev(vc)],
        out_specs=[out_spec, out_spec],
        out_shape=[jax.ShapeDtypeStruct((B, L, d * COL_TILE), F32)] * 2,
        compiler_params=_params(("parallel", "parallel", "arbitrary")),
    )(pv, pv, pv, pv, pv)
    return o.reshape(B * S, COL_TILE), lse.reshape(B * S, COL_TILE)


DL_N_SLOPES_DIV = len(DL_GROUPS)


def _route(logits):
    lane = lax.broadcasted_iota(jnp.int32, logits.shape, 1)
    big = jnp.int32(1 << 20)
    is_g = (lane >= MOE_N_EXPERTS) & (lane < MOE_N_EXPERTS + MOE_GROUPS)
    lg = jnp.where(is_g, logits, -jnp.inf)
    gmax = jnp.max(lg, axis=-1, keepdims=True)
    gsum = jnp.sum(jnp.exp(lg - gmax), axis=-1, keepdims=True)
    g_w = 1.0 / gsum
    g_idx = jnp.min(jnp.where(lg == gmax, lane - MOE_N_EXPERTS, big), axis=-1, keepdims=True)
    in_grp = (lane < MOE_N_EXPERTS) & ((lane // MOE_EXPERTS_PER_GROUP) == g_idx)
    le = jnp.where(in_grp, logits, -jnp.inf)
    t1 = jnp.max(le, axis=-1, keepdims=True)
    e1 = jnp.min(jnp.where(le == t1, lane, big), axis=-1, keepdims=True)
    le2 = jnp.where(lane == e1, -jnp.inf, le)
    t2 = jnp.max(le2, axis=-1, keepdims=True)
    e2 = jnp.min(jnp.where(le2 == t2, lane, big), axis=-1, keepdims=True)
    r = jnp.exp(t2 - t1)
    w1 = g_w / (1.0 + r)
    w2 = w1 * r
    out = jnp.where(lane == 0, e1.astype(F32),
                    jnp.where(lane == 1, e2.astype(F32),
                              jnp.where(lane == 2, w1, jnp.where(lane == 3, w2, 0.0))))
    return out


def _outproj_kernel(x_ref, oa_ref, ga_ref, gb_ref, o0_ref, o1_ref, o2_ref, l0_ref, l1_ref, l2_ref,
                    wa_ref, wb_ref, wo_ref, gf_ref, rh_ref, rl_ref,
                    x1_ref, hn_ref, rt_ref):
    l0, l1, l2 = l0_ref[...], l1_ref[...], l2_ref[...]
    lm = jnp.maximum(jnp.maximum(l0, l1), l2)
    e0, e1, e2 = jnp.exp2(l0 - lm), jnp.exp2(l1 - lm), jnp.exp2(l2 - lm)
    ob = (e0 * o0_ref[...] + e1 * o1_ref[...] + e2 * o2_ref[...]) / (e0 + e1 + e2)
    a = _dot(oa_ref[...], wa_ref[...])
    b = _dot(ob.astype(BF16), wb_ref[...])
    mixed = jax.nn.sigmoid(ga_ref[...].astype(F32)) * a + jax.nn.sigmoid(gb_ref[...].astype(F32)) * b
    x1 = x_ref[...] + _dot(mixed.astype(BF16), wo_ref[...])
    x1_ref[...] = x1
    ms = jnp.mean(x1 * x1, axis=-1, keepdims=True)
    hn = x1 * lax.rsqrt(ms + EPS) * gf_ref[...]
    hn_ref[...] = hn
    hn_hi = hn.astype(BF16)
    hn_lo = (hn - hn_hi.astype(F32)).astype(BF16)
    logits = _dot(hn_hi, rh_ref[...]) + (_dot(hn_lo, rh_ref[...]) + _dot(hn_hi, rl_ref[...]))
    rt_ref[...] = _route(logits)


def _outproj(x2, o_a, proj, dl_o, dl_lse, wa, wb, wo, gain_ffn, r_hi, r_lo, tm=256):
    T, D = x2.shape
    row = lambda w: pl.BlockSpec((tm, w), lambda i: (i, 0))
    full = lambda s: pl.BlockSpec(s, lambda i: (0, 0), pipeline_mode=pl.Buffered(1))
    return pl.pallas_call(
        _outproj_kernel,
        grid=(T // tm,),
        in_specs=[
            row(D), row(o_a.shape[1]),
            pl.BlockSpec((tm, D), lambda i: (i, (CT_GATE_A * COL_TILE) // D)),
            pl.BlockSpec((tm, D), lambda i: (i, (CT_GATE_B * COL_TILE) // D)),
            row(COL_TILE), row(COL_TILE), row(COL_TILE), row(COL_TILE), row(COL_TILE), row(COL_TILE),
            full(wa.shape), full(wb.shape), full(wo.shape), full((1, D)), full(r_hi.shape), full(r_lo.shape),
        ],
        out_specs=[row(D), row(D), row(LANES)],
        out_shape=[jax.ShapeDtypeStruct((T, D), F32), jax.ShapeDtypeStruct((T, D), F32),
                   jax.ShapeDtypeStruct((T, LANES), F32)],
        compiler_params=_params(("parallel",)),
    )(x2, o_a, proj, proj, dl_o[0], dl_o[1], dl_o[2], dl_lse[0], dl_lse[1], dl_lse[2],
      wa, wb, wo, gain_ffn, r_hi, r_lo)


def _gather_kernel(idx_ref, src_ref, dst_ref, sem, *, rows):
    i = pl.program_id(0)
    base = i * rows

    def copy(r):
        return pltpu.make_async_copy(src_ref.at[idx_ref[0, r]], dst_ref.at[base + r], sem)

    def start(r, c):
        copy(r).start()
        return c

    def wait(r, c):
        copy(r).wait()
        return c

    lax.fori_loop(0, rows, start, 0)
    lax.fori_loop(0, rows, wait, 0)


def _gather_rows(src, idx, rows=512):
    T, D = src.shape
    P = idx.shape[0]
    return pl.pallas_call(
        functools.partial(_gather_kernel, rows=rows),
        grid=(P // rows,),
        in_specs=[
            pl.BlockSpec((None, 1, rows), lambda i: (i, 0, 0), memory_space=pltpu.SMEM),
            pl.BlockSpec(memory_space=pl.ANY),
        ],
        out_specs=pl.BlockSpec(memory_space=pl.ANY),
        out_shape=jax.ShapeDtypeStruct((P, D), src.dtype),
        scratch_shapes=[pltpu.SemaphoreType.DMA(())],
        compiler_params=_params(("arbitrary",)),
    )(idx.reshape(P // rows, 1, rows), src)


def _expert_kernel(be_ref, nu_ref, x_ref, wgu_ref, wd_ref, y_ref, *, d_ff):
    i = pl.program_id(0)

    @pl.when(i < nu_ref[0])
    def _():
        h = _dot(x_ref[...].astype(BF16), wgu_ref[...])
        gate = h[:, :d_ff]
        up = h[:, d_ff:]
        act = gate * jax.nn.sigmoid(gate) * up
        y_ref[...] = _dot(act.astype(BF16), wd_ref[...])

    @pl.when(i >= nu_ref[0])
    def _():
        y_ref[...] = jnp.zeros(y_ref.shape, y_ref.dtype)


def _experts(x_pad, blk_expert, n_used, wgu, wd):
    P, D = x_pad.shape
    nb = P // MOE_BLOCK
    d_ff = wd.shape[1]

    def live(i, nu):
        return jnp.minimum(i, nu[0] - 1)

    grid_spec = pltpu.PrefetchScalarGridSpec(
        num_scalar_prefetch=2,
        grid=(nb,),
        in_specs=[
            pl.BlockSpec((MOE_BLOCK, D), lambda i, be, nu: (live(i, nu), 0)),
            pl.BlockSpec((None, D, 2 * d_ff), lambda i, be, nu: (be[live(i, nu)], 0, 0)),
            pl.BlockSpec((None, d_ff, D), lambda i, be, nu: (be[live(i, nu)], 0, 0)),
        ],
        out_specs=pl.BlockSpec((MOE_BLOCK, D), lambda i, be, nu: (i, 0)),
    )
    return pl.pallas_call(
        functools.partial(_expert_kernel, d_ff=d_ff),
        grid_spec=grid_spec,
        out_shape=jax.ShapeDtypeStruct((P, D), F32),
        compiler_params=_params(("arbitrary",)),
    )(blk_expert, n_used, x_pad, wgu, wd)


def _combine_kernel(pc_ref, pn_ref, x1_ref, rt_ref, y_ref, o_ref, buf, sem, *, tc):
    i = pl.program_id(0)
    n = pl.num_programs(0)
    slot = i % 2

    def copy(pos_ref, r, s):
        return pltpu.make_async_copy(y_ref.at[pos_ref[0, r]], buf.at[s, r], sem.at[s])

    def issue(pos_ref, s):
        def body(r, c):
            copy(pos_ref, r, s).start()
            return c
        lax.fori_loop(0, 2 * tc, body, 0)

    @pl.when(i == 0)
    def _():
        issue(pc_ref, 0)

    @pl.when(i + 1 < n)
    def _():
        issue(pn_ref, 1 - slot)

    def wait(r, c):
        copy(pc_ref, r, slot).wait()
        return c
    lax.fori_loop(0, 2 * tc, wait, 0)

    rt = rt_ref[...]
    w1 = rt[:, 2:3]
    w2 = rt[:, 3:4]
    y1 = buf[slot, pl.ds(0, tc), :]
    y2 = buf[slot, pl.ds(tc, tc), :]
    o_ref[...] = x1_ref[...] + (w1 * y1 + w2 * y2)


def _combine(x1, route, y_pad, slot_pos, tc=128):
    T, D = x1.shape
    nt = T // tc
    pos = slot_pos.reshape(nt, tc, 2).transpose(0, 2, 1).reshape(nt, 1, 2 * tc)
    smem = lambda f: pl.BlockSpec((None, 1, 2 * tc), f, memory_space=pltpu.SMEM)
    return pl.pallas_call(
        functools.partial(_combine_kernel, tc=tc),
        grid=(nt,),
        in_specs=[
            smem(lambda i: (i, 0, 0)),
            smem(lambda i: (jnp.minimum(i + 1, nt - 1), 0, 0)),
            pl.BlockSpec((tc, D), lambda i: (i, 0)),
            pl.BlockSpec((tc, LANES), lambda i: (i, 0)),
            pl.BlockSpec(memory_space=pl.ANY),
        ],
        out_specs=pl.BlockSpec((tc, D), lambda i: (i, 0)),
        out_shape=jax.ShapeDtypeStruct((T, D), F32),
        scratch_shapes=[pltpu.VMEM((2, 2 * tc, D), F32), pltpu.SemaphoreType.DMA((2,))],
        compiler_params=_params(("arbitrary",)),
    )(pos, pos, x1, route, y_pad)


def _dispatch_tables(route, T):
    TK = T * MOE_TOP_K
    flat_e = route[:, :MOE_TOP_K].astype(jnp.int32).reshape(-1)
    order = jnp.argsort(flat_e)
    sorted_e = flat_e[order]
    sizes = jnp.bincount(flat_e, length=MOE_N_EXPERTS).astype(jnp.int32)
    start = jnp.cumsum(sizes) - sizes
    padded = ((sizes + MOE_BLOCK - 1) // MOE_BLOCK) * MOE_BLOCK
    pad_end = jnp.cumsum(padded)
    pad_start = pad_end - padded
    dest = (pad_start[sorted_e] + (jnp.arange(TK, dtype=jnp.int32) - start[sorted_e])).astype(jnp.int32)
    n_blocks = TK // MOE_BLOCK + MOE_N_EXPERTS
    src_tok = jnp.zeros((n_blocks * MOE_BLOCK,), jnp.int32).at[dest].set((order // MOE_TOP_K).astype(jnp.int32))
    slot_pos = jnp.zeros((TK,), jnp.int32).at[order].set(dest)
    blk_expert = jnp.minimum(
        jnp.searchsorted(pad_end, jnp.arange(n_blocks, dtype=jnp.int32) * MOE_BLOCK, side='right'),
        MOE_N_EXPERTS - 1).astype(jnp.int32)
    n_used = (pad_end[-1:] // MOE_BLOCK).astype(jnp.int32)
    return src_tok, slot_pos, blk_expert, n_used


def _permute_w_in(w):
    da = DA_HEADS * 2 * DA_QK_DIM
    dav = DA_HEADS * DA_V_DIM
    dl = len(DL_GROUPS) * DL_HEADS_PER_GROUP * DL_HEAD_DIM
    o = np.cumsum([0, da, da, dav, dl, dl, dl])
    gates = w[:, int(o[6]):]
    return jnp.concatenate([gates, w[:, :int(o[6])]], axis=1)


def _gain_table(da_q_norm, da_k_norm, dl_q_norm, dl_k_norm):
    ones = jnp.ones((COL_TILE,), F32)
    daq = jnp.tile(da_q_norm, COL_TILE // DA_QK_DIM) * (DA_QK_DIM ** -0.5 * LOG2E)
    dak = jnp.tile(da_k_norm, COL_TILE // DA_QK_DIM)
    dlq = jnp.tile(dl_q_norm, COL_TILE // DL_HEAD_DIM) * (DL_HEAD_DIM ** -0.5 * LOG2E)
    dlk = jnp.tile(dl_k_norm, COL_TILE // DL_HEAD_DIM)
    rows = []
    for j in range(CT_END):
        if CT_DA_Q <= j < CT_DA_K:
            rows.append(daq)
        elif CT_DA_K <= j < CT_DA_V:
            rows.append(dak)
        elif CT_DL_Q <= j < CT_DL_K:
            rows.append(dlq)
        elif CT_DL_K <= j < CT_DL_V:
            rows.append(dlk)
        else:
            rows.append(ones)
    return jnp.stack(rows, axis=0).reshape(CT_END, 1, COL_TILE)


def kernel(x, norm_mix, w_in, da_q_norm, da_k_norm, da_lambda_q, da_lambda_k, da_sub_norm,
           dl_q_norm, dl_k_norm, w_branch_a, w_branch_b, w_out, norm_ffn,
           w_group_router, w_expert_router, w_gate_up, w_down):
    B, S, D = x.shape
    T = B * S
    depth = w_in.shape[0]
    x2 = x.reshape(T, D)
    for l in range(depth):
        lam_init = 0.8 - 0.6 * math.exp(-0.3 * l)
        w_in_bf = _permute_w_in(w_in[l]).astype(BF16)
        gain_tab = _gain_table(da_q_norm[l], da_k_norm[l], dl_q_norm[l], dl_k_norm[l])
        proj = _inproj(x2, norm_mix[l].reshape(1, D), w_in_bf, gain_tab)

        o_a = _diff_attention(proj, da_lambda_q[l], da_lambda_k[l], da_sub_norm[l].reshape(1, DA_V_DIM),
                              B, S, lam_init)
        dl = [_dilated_group(proj, g, B, S) for g in range(len(DL_GROUPS))]

        w_r = jnp.concatenate([w_expert_router[l], w_group_router[l]], axis=1)
        w_r = jnp.pad(w_r, ((0, 0), (0, LANES - w_r.shape[1])))
        r_hi = w_r.astype(BF16)
        r_lo = (w_r - r_hi.astype(F32)).astype(BF16)
        x1, hn, route = _outproj(
            x2, o_a, proj, [t[0] for t in dl], [t[1] for t in dl],
            w_branch_a[l].astype(BF16), w_branch_b[l].astype(BF16), w_out[l].astype(BF16),
            norm_ffn[l].reshape(1, D), r_hi, r_lo)

        src_tok, slot_pos, blk_expert, n_used = _dispatch_tables(route, T)
        x_pad = _gather_rows(hn, src_tok)
        y_pad = _experts(x_pad, blk_expert, n_used, w_gate_up[l].astype(BF16), w_down[l].astype(BF16))
        x2 = _combine(x1, route, y_pad, slot_pos)
    return x2.reshape(B, S, D)
```

```python
import functools
import math

import jax
import jax.numpy as jnp
import numpy as np
from jax import lax
from jax.experimental import pallas as pl
from jax.experimental.pallas import tpu as pltpu

F32 = jnp.float32
BF16 = jnp.bfloat16

EPS = 1e-6
LOG2E = 1.4426950408889634
NEG_BIG = -1e30

DA_HEADS = 8
DA_QK_DIM = 64
DA_V_DIM = 128
DL_GROUPS = ((128, 1), (512, 4), (2048, 16))
DL_HEADS_PER_GROUP = 4
DL_HEAD_DIM = 128
DL_SPAN = 128
MOE_GROUPS = 4
MOE_EXPERTS_PER_GROUP = 8
MOE_N_EXPERTS = 32
MOE_TOP_K = 2
MOE_BLOCK = 128

LANES = 128
COL_TILE = 512
VMEM_LIMIT = 56 * 1024 * 1024

CT_GATE_A, CT_GATE_B, CT_DA_Q, CT_DA_K, CT_DA_V, CT_DL_Q, CT_DL_K, CT_DL_V, CT_END = 0, 4, 8, 10, 12, 14, 17, 20, 23


def _params(sem, vmem=VMEM_LIMIT):
    return pltpu.CompilerParams(dimension_semantics=sem, vmem_limit_bytes=vmem)


def _dot(a, b):
    return jnp.dot(a, b, preferred_element_type=F32)


def _dot_nt(a, b):
    return lax.dot_general(a, b, (((1,), (1,)), ((), ())), preferred_element_type=F32)


def _inproj_kernel(x_ref, g_ref, w_ref, gain_ref, o_ref, h_scr):
    j = pl.program_id(1)

    @pl.when(j == 0)
    def _():
        x = x_ref[...]
        ms = jnp.mean(x * x, axis=-1, keepdims=True)
        h_scr[...] = (x * lax.rsqrt(ms + EPS) * g_ref[...]).astype(BF16)

    y = _dot(h_scr[...], w_ref[...])
    gain = gain_ref[...]
    heads = COL_TILE // LANES

    is64 = (j >= CT_DA_Q) & (j < CT_DA_V)
    is128 = (j >= CT_DL_Q) & (j < CT_DL_V)

    @pl.when(is64)
    def _():
        for h in range(heads):
            sl = slice(h * LANES, (h + 1) * LANES)
            yh = y[:, sl]
            sq = yh * yh
            lo = lax.broadcasted_iota(jnp.int32, yh.shape, 1) < DA_QK_DIM
            s_lo = jnp.sum(jnp.where(lo, sq, 0.0), axis=-1, keepdims=True)
            s_hi = jnp.sum(jnp.where(lo, 0.0, sq), axis=-1, keepdims=True)
            r = jnp.where(lo, lax.rsqrt(s_lo * (1.0 / DA_QK_DIM) + EPS),
                          lax.rsqrt(s_hi * (1.0 / DA_QK_DIM) + EPS))
            o_ref[:, sl] = (yh * r * gain[:, sl]).astype(o_ref.dtype)

    @pl.when(is128)
    def _():
        for h in range(heads):
            sl = slice(h * LANES, (h + 1) * LANES)
            yh = y[:, sl]
            ss = jnp.sum(yh * yh, axis=-1, keepdims=True)
            r = lax.rsqrt(ss * (1.0 / DL_HEAD_DIM) + EPS)
            o_ref[:, sl] = (yh * r * gain[:, sl]).astype(o_ref.dtype)

    @pl.when(jnp.logical_not(is64 | is128))
    def _():
        o_ref[...] = y.astype(o_ref.dtype)


def _inproj(x2, gain_mix, w_bf, gain_tab, tm=1024):
    T, D = x2.shape
    N = w_bf.shape[1]
    return pl.pallas_call(
        _inproj_kernel,
        grid=(T // tm, N // COL_TILE),
        in_specs=[
            pl.BlockSpec((tm, D), lambda i, j: (i, 0)),
            pl.BlockSpec((1, D), lambda i, j: (0, 0)),
            pl.BlockSpec((D, COL_TILE), lambda i, j: (0, j)),
            pl.BlockSpec((None, 1, COL_TILE), lambda i, j: (j, 0, 0)),
        ],
        out_specs=pl.BlockSpec((tm, COL_TILE), lambda i, j: (i, j)),
        out_shape=jax.ShapeDtypeStruct((T, N), BF16),
        scratch_shapes=[pltpu.VMEM((tm, D), BF16)],
        compiler_params=_params(("parallel", "arbitrary")),
    )(x2, gain_mix, w_bf, gain_tab)


def _da_kernel(q_ref, k_ref, v_ref, lq_ref, lk_ref, sg_ref, o_ref, m_scr, l_scr, acc_scr,
               *, tq, lam_init):
    h = pl.program_id(1)
    qi = pl.program_id(2)
    slope2 = jnp.exp2(-(h + 1).astype(F32)) * LOG2E

    q = q_ref[...]
    lane = lax.broadcasted_iota(jnp.int32, q.shape, 1)
    qs = (jnp.where(lane < DA_QK_DIM, q, jnp.zeros_like(q)),
          jnp.where(lane >= DA_QK_DIM, q, jnp.zeros_like(q)))

    row = lax.broadcasted_iota(jnp.int32, (tq, tq), 0)
    col = lax.broadcasted_iota(jnp.int32, (tq, tq), 1)
    rel = col - row
    bias = slope2 * rel.astype(F32)

    m_scr[...] = jnp.full(m_scr.shape, NEG_BIG, F32)
    l_scr[...] = jnp.zeros(l_scr.shape, F32)
    acc_scr[...] = jnp.zeros(acc_scr.shape, F32)

    def step(ki, masked):
        off = pl.multiple_of(ki * tq, tq)
        k = k_ref[pl.ds(off, tq), :]
        v = v_ref[pl.ds(off, tq), :]
        c = slope2 * ((ki - qi) * tq).astype(F32)
        for mi in range(2):
            s = _dot_nt(qs[mi], k) + bias
            if masked:
                s = jnp.where(rel <= 0, s, NEG_BIG)
            m_prev = m_scr[mi]
            m_new = jnp.maximum(m_prev, jnp.max(s, axis=-1, keepdims=True) + c)
            alpha = jnp.exp2(m_prev - m_new)
            p = jnp.exp2(s - (m_new - c))
            l_scr[mi] = alpha * l_scr[mi] + jnp.sum(p, axis=-1, keepdims=True)
            acc_scr[mi] = alpha * acc_scr[mi] + _dot(p.astype(BF16), v)
            m_scr[mi] = m_new

    def body(ki, carry):
        step(ki, False)
        return carry

    lax.fori_loop(0, qi, body, 0)
    step(qi, True)

    lam_e = jnp.exp(jnp.sum(lq_ref[...] * lk_ref[...], axis=-1, keepdims=True))
    lam = lam_e[0:1, :] - lam_e[1:2, :] + lam_init
    o = acc_scr[0] / l_scr[0] - lam * (acc_scr[1] / l_scr[1])
    ms = jnp.mean(o * o, axis=-1, keepdims=True)
    o = o * lax.rsqrt(ms + EPS) * sg_ref[...] * (1.0 - lam_init)
    o_ref[...] = o.astype(o_ref.dtype)


def _diff_attention(proj, lam_q, lam_k, sub_gain, B, S, lam_init, tq=512):
    T = proj.shape[0]
    nq = S // tq
    lb = LANES
    q_blk0, k_blk0, v_blk0 = (CT_DA_Q * COL_TILE) // lb, (CT_DA_K * COL_TILE) // lb, (CT_DA_V * COL_TILE) // lb
    return pl.pallas_call(
        functools.partial(_da_kernel, tq=tq, lam_init=lam_init),
        grid=(B, DA_HEADS, nq),
        in_specs=[
            pl.BlockSpec((tq, lb), lambda b, h, i: (b * nq + i, q_blk0 + h)),
            pl.BlockSpec((S, lb), lambda b, h, i: (b, k_blk0 + h)),
            pl.BlockSpec((S, lb), lambda b, h, i: (b, v_blk0 + h)),
            pl.BlockSpec((2, DA_QK_DIM), lambda b, h, i: (0, 0)),
            pl.BlockSpec((2, DA_QK_DIM), lambda b, h, i: (0, 0)),
            pl.BlockSpec((1, DA_V_DIM), lambda b, h, i: (0, 0)),
        ],
        out_specs=pl.BlockSpec((tq, lb), lambda b, h, i: (b * nq + i, h)),
        out_shape=jax.ShapeDtypeStruct((T, DA_HEADS * DA_V_DIM), BF16),
        scratch_shapes=[pltpu.VMEM((2, tq, 1), F32), pltpu.VMEM((2, tq, 1), F32),
                        pltpu.VMEM((2, tq, DA_V_DIM), F32)],
        compiler_params=_params(("parallel", "parallel", "arbitrary")),
    )(proj, proj, proj, lam_q, lam_k, sub_gain)


def _dl_kernel(q_ref, kc_ref, kp_ref, vc_ref, vp_ref, o_ref, lse_ref, *, slopes2, tq):
    n = pl.program_id(2)
    sp = DL_SPAN
    row = lax.broadcasted_iota(jnp.int32, (sp, sp), 0)
    col = lax.broadcasted_iota(jnp.int32, (sp, sp), 1)
    dcur = row - col
    cur_ok = dcur >= 0
    prev_ok = dcur <= 0
    dcur_f = dcur.astype(F32)
    for hh in range(DL_HEADS_PER_GROUP):
        hs = slice(hh * LANES, (hh + 1) * LANES)
        bias_c = -slopes2[hh] * dcur_f
        bias_p = -slopes2[hh] * (dcur_f + float(sp))
        for j in range(tq // sp):
            rs = slice(j * sp, (j + 1) * sp)
            q = q_ref[rs, hs]
            kc = kc_ref[rs, hs]
            vc = vc_ref[rs, hs]
            if j == 0:
                kp = kp_ref[:, hs]
                vp = vp_ref[:, hs]
                p_ok = prev_ok & (n > 0)
            else:
                ps = slice((j - 1) * sp, j * sp)
                kp = kc_ref[ps, hs]
                vp = vc_ref[ps, hs]
                p_ok = prev_ok
            s_c = jnp.where(cur_ok, _dot_nt(q, kc) + bias_c, NEG_BIG)
            s_p = jnp.where(p_ok, _dot_nt(q, kp) + bias_p, NEG_BIG)
            m = jnp.maximum(jnp.max(s_c, axis=-1, keepdims=True), jnp.max(s_p, axis=-1, keepdims=True))
            p_c = jnp.exp2(s_c - m)
            p_p = jnp.exp2(s_p - m)
            den = jnp.sum(p_c, axis=-1, keepdims=True) + jnp.sum(p_p, axis=-1, keepdims=True)
            acc = _dot(p_c.astype(BF16), vc) + _dot(p_p.astype(BF16), vp)
            o_ref[rs, hs] = acc / den
            lse_ref[rs, hs] = jnp.broadcast_to(m + jnp.log2(den), (sp, LANES))


def _dilated_group(proj, g, B, S):
    window, d = DL_GROUPS[g]
    assert window // d == DL_SPAN
    L = S // d
    tq = min(512, L)
    assert L % tq == 0 and tq % DL_SPAN == 0
    n_in = proj.shape[1]
    ct = n_in // COL_TILE
    pv = proj.reshape(B, L, d * n_in)
    nh = DL_HEADS_PER_GROUP * DL_N_SLOPES_DIV
    slopes2 = tuple(2.0 ** (-8.0 * (g * DL_HEADS_PER_GROUP + hh + 1) / nh) * d * LOG2E
                    for hh in range(DL_HEADS_PER_GROUP))
    spb = tq // DL_SPAN
    qc, kc, vc = CT_DL_Q + g, CT_DL_K + g, CT_DL_V + g
    cur = lambda c: pl.BlockSpec((None, tq, COL_TILE), lambda b, r, n: (b, n, r * ct + c))
    prev = lambda c: pl.BlockSpec((None, DL_SPAN, COL_TILE),
                                  lambda b, r, n: (b, jnp.maximum(n * spb - 1, 0), r * ct + c))
    out_spec = pl.BlockSpec((None, tq, COL_TILE), lambda b, r, n: (b, n, r))
    o, lse = pl.pallas_call(
        functools.partial(_dl_kernel, slopes2=slopes2, tq=tq),
        grid=(B, d, L // tq),
        in_specs=[cur(qc), cur(kc), prev(kc), cur(vc), prev(vc)],
        out_specs=[out_spec, out_spec],
        out_shape=[jax.ShapeDtypeStruct((B, L, d * COL_TILE), F32)] * 2,
        compiler_params=_params(("parallel", "parallel", "arbitrary")),
    )(pv, pv, pv, pv, pv)
    return o.reshape(B * S, COL_TILE), lse.reshape(B * S, COL_TILE)


DL_N_SLOPES_DIV = len(DL_GROUPS)


def _route(logits):
    lane = lax.broadcasted_iota(jnp.int32, logits.shape, 1)
    big = jnp.int32(1 << 20)
    is_g = (lane >= MOE_N_EXPERTS) & (lane < MOE_N_EXPERTS + MOE_GROUPS)
    lg = jnp.where(is_g, logits, -jnp.inf)
    gmax = jnp.max(lg, axis=-1, keepdims=True)
    gsum = jnp.sum(jnp.exp(lg - gmax), axis=-1, keepdims=True)
    g_w = 1.0 / gsum
    g_idx = jnp.min(jnp.where(lg == gmax, lane - MOE_N_EXPERTS, big), axis=-1, keepdims=True)
    in_grp = (lane < MOE_N_EXPERTS) & ((lane // MOE_EXPERTS_PER_GROUP) == g_idx)
    le = jnp.where(in_grp, logits, -jnp.inf)
    t1 = jnp.max(le, axis=-1, keepdims=True)
    e1 = jnp.min(jnp.where(le == t1, lane, big), axis=-1, keepdims=True)
    le2 = jnp.where(lane == e1, -jnp.inf, le)
    t2 = jnp.max(le2, axis=-1, keepdims=True)
    e2 = jnp.min(jnp.where(le2 == t2, lane, big), axis=-1, keepdims=True)
    r = jnp.exp(t2 - t1)
    w1 = g_w / (1.0 + r)
    w2 = w1 * r
    out = jnp.where(lane == 0, e1.astype(F32),
                    jnp.where(lane == 1, e2.astype(F32),
                              jnp.where(lane == 2, w1, jnp.where(lane == 3, w2, 0.0))))
    return out


def _outproj_kernel(x_ref, oa_ref, ga_ref, gb_ref, o0_ref, o1_ref, o2_ref, l0_ref, l1_ref, l2_ref,
                    wa_ref, wb_ref, wo_ref, gf_ref, rh_ref, rl_ref,
                    x1_ref, hn_ref, rt_ref):
    l0, l1, l2 = l0_ref[...], l1_ref[...], l2_ref[...]
    lm = jnp.maximum(jnp.maximum(l0, l1), l2)
    e0, e1, e2 = jnp.exp2(l0 - lm), jnp.exp2(l1 - lm), jnp.exp2(l2 - lm)
    ob = (e0 * o0_ref[...] + e1 * o1_ref[...] + e2 * o2_ref[...]) / (e0 + e1 + e2)
    a = _dot(oa_ref[...], wa_ref[...])
    b = _dot(ob.astype(BF16), wb_ref[...])
    mixed = jax.nn.sigmoid(ga_ref[...].astype(F32)) * a + jax.nn.sigmoid(gb_ref[...].astype(F32)) * b
    x1 = x_ref[...] + _dot(mixed.astype(BF16), wo_ref[...])
    x1_ref[...] = x1
    ms = jnp.mean(x1 * x1, axis=-1, keepdims=True)
    hn = x1 * lax.rsqrt(ms + EPS) * gf_ref[...]
    hn_ref[...] = hn
    hn_hi = hn.astype(BF16)
    hn_lo = (hn - hn_hi.astype(F32)).astype(BF16)
    logits = _dot(hn_hi, rh_ref[...]) + (_dot(hn_lo, rh_ref[...]) + _dot(hn_hi, rl_ref[...]))
    rt_ref[...] = _route(logits)


def _outproj(x2, o_a, proj, dl_o, dl_lse, wa, wb, wo, gain_ffn, r_hi, r_lo, tm=256):
    T, D = x2.shape
    row = lambda w: pl.BlockSpec((tm, w), lambda i: (i, 0))
    full = lambda s: pl.BlockSpec(s, lambda i: (0, 0), pipeline_mode=pl.Buffered(1))
    return pl.pallas_call(
        _outproj_kernel,
        grid=(T // tm,),
        in_specs=[
            row(D), row(o_a.shape[1]),
            pl.BlockSpec((tm, D), lambda i: (i, (CT_GATE_A * COL_TILE) // D)),
            pl.BlockSpec((tm, D), lambda i: (i, (CT_GATE_B * COL_TILE) // D)),
            row(COL_TILE), row(COL_TILE), row(COL_TILE), row(COL_TILE), row(COL_TILE), row(COL_TILE),
            full(wa.shape), full(wb.shape), full(wo.shape), full((1, D)), full(r_hi.shape), full(r_lo.shape),
        ],
        out_specs=[row(D), row(D), row(LANES)],
        out_shape=[jax.ShapeDtypeStruct((T, D), F32), jax.ShapeDtypeStruct((T, D), F32),
                   jax.ShapeDtypeStruct((T, LANES), F32)],
        compiler_params=_params(("parallel",)),
    )(x2, o_a, proj, proj, dl_o[0], dl_o[1], dl_o[2], dl_lse[0], dl_lse[1], dl_lse[2],
      wa, wb, wo, gain_ffn, r_hi, r_lo)


def _expert_kernel(be_ref, nu_ref, tc_ref, tn_ref, hn_ref, wgu_ref, wd_ref, y_ref, xbuf, sem, *, d_ff):
    i = pl.program_id(0)
    nu = nu_ref[0]
    slot = i % 2
    rows = xbuf.shape[1]

    def copy(tok_ref, r, s):
        return pltpu.make_async_copy(hn_ref.at[tok_ref[0, r]], xbuf.at[s, r], sem.at[s])

    def issue(tok_ref, s):
        def body(r, c):
            copy(tok_ref, r, s).start()
            return c
        lax.fori_loop(0, rows, body, 0)

    @pl.when((i == 0) & (nu > 0))
    def _():
        issue(tc_ref, 0)

    @pl.when(i + 1 < nu)
    def _():
        issue(tn_ref, 1 - slot)

    @pl.when(i < nu)
    def _():
        def wait(r, c):
            copy(tc_ref, r, slot).wait()
            return c
        lax.fori_loop(0, rows, wait, 0)
        h = _dot(xbuf[slot].astype(BF16), wgu_ref[...])
        gate = h[:, :d_ff]
        up = h[:, d_ff:]
        act = gate * jax.nn.sigmoid(gate) * up
        y_ref[...] = _dot(act.astype(BF16), wd_ref[...])

    @pl.when(i >= nu)
    def _():
        y_ref[...] = jnp.zeros(y_ref.shape, y_ref.dtype)


def _experts(hn, src_tok, blk_expert, n_used, wgu, wd):
    T, D = hn.shape
    nb = src_tok.shape[0] // MOE_BLOCK
    d_ff = wd.shape[1]
    tok = src_tok.reshape(nb, 1, MOE_BLOCK)

    def live(i, nu):
        return jnp.minimum(i, nu[0] - 1)

    smem = lambda f: pl.BlockSpec((None, 1, MOE_BLOCK), f, memory_space=pltpu.SMEM)
    grid_spec = pltpu.PrefetchScalarGridSpec(
        num_scalar_prefetch=2,
        grid=(nb,),
        in_specs=[
            smem(lambda i, be, nu: (i, 0, 0)),
            smem(lambda i, be, nu: (jnp.minimum(i + 1, nb - 1), 0, 0)),
            pl.BlockSpec(memory_space=pl.ANY),
            pl.BlockSpec((None, D, 2 * d_ff), lambda i, be, nu: (be[live(i, nu)], 0, 0)),
            pl.BlockSpec((None, d_ff, D), lambda i, be, nu: (be[live(i, nu)], 0, 0)),
        ],
        out_specs=pl.BlockSpec((MOE_BLOCK, D), lambda i, be, nu: (i, 0)),
        scratch_shapes=[pltpu.VMEM((2, MOE_BLOCK, D), F32), pltpu.SemaphoreType.DMA((2,))],
    )
    return pl.pallas_call(
        functools.partial(_expert_kernel, d_ff=d_ff),
        grid_spec=grid_spec,
        out_shape=jax.ShapeDtypeStruct((nb * MOE_BLOCK, D), F32),
        compiler_params=_params(("arbitrary",)),
    )(blk_expert, n_used, tok, tok, hn, wgu, wd)


def _combine_kernel(pc_ref, pn_ref, x1_ref, rt_ref, y_ref, o_ref, buf, sem, *, tc):
    i = pl.program_id(0)
    n = pl.num_programs(0)
    slot = i % 2

    def copy(pos_ref, r, s):
        return pltpu.make_async_copy(y_ref.at[pos_ref[0, r]], buf.at[s, r], sem.at[s])

    def issue(pos_ref, s):
        def body(r, c):
            copy(pos_ref, r, s).start()
            return c
        lax.fori_loop(0, 2 * tc, body, 0)

    @pl.when(i == 0)
    def _():
        issue(pc_ref, 0)

    @pl.when(i + 1 < n)
    def _():
        issue(pn_ref, 1 - slot)

    def wait(r, c):
        copy(pc_ref, r, slot).wait()
        return c
    lax.fori_loop(0, 2 * tc, wait, 0)

    rt = rt_ref[...]
    w1 = rt[:, 2:3]
    w2 = rt[:, 3:4]
    y1 = buf[slot, pl.ds(0, tc), :]
    y2 = buf[slot, pl.ds(tc, tc), :]
    o_ref[...] = x1_ref[...] + (w1 * y1 + w2 * y2)


def _combine(x1, route, y_pad, slot_pos, tc=128):
    T, D = x1.shape
    nt = T // tc
    pos = slot_pos.reshape(nt, tc, 2).transpose(0, 2, 1).reshape(nt, 1, 2 * tc)
    smem = lambda f: pl.BlockSpec((None, 1, 2 * tc), f, memory_space=pltpu.SMEM)
    return pl.pallas_call(
        functools.partial(_combine_kernel, tc=tc),
        grid=(nt,),
        in_specs=[
            smem(lambda i: (i, 0, 0)),
            smem(lambda i: (jnp.minimum(i + 1, nt - 1), 0, 0)),
            pl.BlockSpec((tc, D), lambda i: (i, 0)),
            pl.BlockSpec((tc, LANES), lambda i: (i, 0)),
            pl.BlockSpec(memory_space=pl.ANY),
        ],
        out_specs=pl.BlockSpec((tc, D), lambda i: (i, 0)),
        out_shape=jax.ShapeDtypeStruct((T, D), F32),
        scratch_shapes=[pltpu.VMEM((2, 2 * tc, D), F32), pltpu.SemaphoreType.DMA((2,))],
        compiler_params=_params(("arbitrary",)),
    )(pos, pos, x1, route, y_pad)


def _dispatch_tables(route, T):
    TK = T * MOE_TOP_K
    flat_e = route[:, :MOE_TOP_K].astype(jnp.int32).reshape(-1)
    order = jnp.argsort(flat_e)
    sorted_e = flat_e[order]
    sizes = jnp.bincount(flat_e, length=MOE_N_EXPERTS).astype(jnp.int32)
    start = jnp.cumsum(sizes) - sizes
    padded = ((sizes + MOE_BLOCK - 1) // MOE_BLOCK) * MOE_BLOCK
    pad_end = jnp.cumsum(padded)
    pad_start = pad_end - padded
    dest = (pad_start[sorted_e] + (jnp.arange(TK, dtype=jnp.int32) - start[sorted_e])).astype(jnp.int32)
    n_blocks = TK // MOE_BLOCK + MOE_N_EXPERTS
    src_tok = jnp.zeros((n_blocks * MOE_BLOCK,), jnp.int32).at[dest].set((order // MOE_TOP_K).astype(jnp.int32))
    slot_pos = jnp.zeros((TK,), jnp.int32).at[order].set(dest)
    blk_expert = jnp.minimum(
        jnp.searchsorted(pad_end, jnp.arange(n_blocks, dtype=jnp.int32) * MOE_BLOCK, side='right'),
        MOE_N_EXPERTS - 1).astype(jnp.int32)
    n_used = (pad_end[-1:] // MOE_BLOCK).astype(jnp.int32)
    return src_tok, slot_pos, blk_expert, n_used


def _permute_w_in(w):
    da = DA_HEADS * 2 * DA_QK_DIM
    dav = DA_HEADS * DA_V_DIM
    dl = len(DL_GROUPS) * DL_HEADS_PER_GROUP * DL_HEAD_DIM
    o = np.cumsum([0, da, da, dav, dl, dl, dl])
    gates = w[:, int(o[6]):]
    return jnp.concatenate([gates, w[:, :int(o[6])]], axis=1)


def _gain_table(da_q_norm, da_k_norm, dl_q_norm, dl_k_norm):
    ones = jnp.ones((COL_TILE,), F32)
    daq = jnp.tile(da_q_norm, COL_TILE // DA_QK_DIM) * (DA_QK_DIM ** -0.5 * LOG2E)
    dak = jnp.tile(da_k_norm, COL_TILE // DA_QK_DIM)
    dlq = jnp.tile(dl_q_norm, COL_TILE // DL_HEAD_DIM) * (DL_HEAD_DIM ** -0.5 * LOG2E)
    dlk = jnp.tile(dl_k_norm, COL_TILE // DL_HEAD_DIM)
    rows = []
    for j in range(CT_END):
        if CT_DA_Q <= j < CT_DA_K:
            rows.append(daq)
        elif CT_DA_K <= j < CT_DA_V:
            rows.append(dak)
        elif CT_DL_Q <= j < CT_DL_K:
            rows.append(dlq)
        elif CT_DL_K <= j < CT_DL_V:
            rows.append(dlk)
        else:
            rows.append(ones)
    return jnp.stack(rows, axis=0).reshape(CT_END, 1, COL_TILE)


def kernel(x, norm_mix, w_in, da_q_norm, da_k_norm, da_lambda_q, da_lambda_k, da_sub_norm,
           dl_q_norm, dl_k_norm, w_branch_a, w_branch_b, w_out, norm_ffn,
           w_group_router, w_expert_router, w_gate_up, w_down):
    B, S, D = x.shape
    T = B * S
    depth = w_in.shape[0]
    x2 = x.reshape(T, D)
    for l in range(depth):
        lam_init = 0.8 - 0.6 * math.exp(-0.3 * l)
        w_in_bf = _permute_w_in(w_in[l]).astype(BF16)
        gain_tab = _gain_table(da_q_norm[l], da_k_norm[l], dl_q_norm[l], dl_k_norm[l])
        proj = _inproj(x2, norm_mix[l].reshape(1, D), w_in_bf, gain_tab)

        o_a = _diff_attention(proj, da_lambda_q[l], da_lambda_k[l], da_sub_norm[l].reshape(1, DA_V_DIM),
                              B, S, lam_init)
        dl = [_dilated_group(proj, g, B, S) for g in range(len(DL_GROUPS))]

        w_r = jnp.concatenate([w_expert_router[l], w_group_router[l]], axis=1)
        w_r = jnp.pad(w_r, ((0, 0), (0, LANES - w_r.shape[1])))
        r_hi = w_r.astype(BF16)
        r_lo = (w_r - r_hi.astype(F32)).astype(BF16)
        x1, hn, route = _outproj(
            x2, o_a, proj, [t[0] for t in dl], [t[1] for t in dl],
            w_branch_a[l].astype(BF16), w_branch_b[l].astype(BF16), w_out[l].astype(BF16),
            norm_ffn[l].reshape(1, D), r_hi, r_lo)

        src_tok, slot_pos, blk_expert, n_used = _dispatch_tables(route, T)
        y_pad = _experts(hn, src_tok, blk_expert, n_used, w_gate_up[l].astype(BF16), w_down[l].astype(BF16))
        x2 = _combine(x1, route, y_pad, slot_pos)
    return x2.reshape(B, S, D)
```

```python
import functools
import math

import jax
import jax.numpy as jnp
import numpy as np
from jax import lax
from jax.experimental import pallas as pl
from jax.experimental.pallas import tpu as pltpu

F32 = jnp.float32
BF16 = jnp.bfloat16

EPS = 1e-6
LOG2E = 1.4426950408889634
NEG_BIG = -1e30

DA_HEADS = 8
DA_QK_DIM = 64
DA_V_DIM = 128
DL_GROUPS = ((128, 1), (512, 4), (2048, 16))
DL_HEADS_PER_GROUP = 4
DL_HEAD_DIM = 128
DL_SPAN = 128
MOE_GROUPS = 4
MOE_EXPERTS_PER_GROUP = 8
MOE_N_EXPERTS = 32
MOE_TOP_K = 2
MOE_BLOCK = 128

LANES = 128
COL_TILE = 512
VMEM_LIMIT = 56 * 1024 * 1024

CT_GATE_A, CT_GATE_B, CT_DA_Q, CT_DA_K, CT_DA_V, CT_DL, CT_MAIN_END, CT_END = 0, 4, 8, 10, 12, 14, 17, 23


def _params(sem, vmem=VMEM_LIMIT):
    return pltpu.CompilerParams(dimension_semantics=sem, vmem_limit_bytes=vmem)


def _dot(a, b):
    return jnp.dot(a, b, preferred_element_type=F32)


def _dot_nt(a, b):
    return lax.dot_general(a, b, (((1,), (1,)), ((), ())), preferred_element_type=F32)


def _inproj_kernel(x_ref, g_ref, w_ref, gain_ref, o_ref, d1_ref, d2_ref, h_scr, y_scr):
    j = pl.program_id(1)

    @pl.when(j == 0)
    def _():
        x = x_ref[...]
        ms = jnp.mean(x * x, axis=-1, keepdims=True)
        h_scr[...] = (x * lax.rsqrt(ms + EPS) * g_ref[...]).astype(BF16)

    y = _dot(h_scr[...], w_ref[...])
    gain = gain_ref[...]
    heads = COL_TILE // LANES

    is64 = (j >= CT_DA_Q) & (j < CT_DA_V)
    is128 = (j >= CT_DL) & (lax.rem(j - CT_DL, 3) < 2)

    @pl.when(is64)
    def _():
        for h in range(heads):
            sl = slice(h * LANES, (h + 1) * LANES)
            yh = y[:, sl]
            sq = yh * yh
            lo = lax.broadcasted_iota(jnp.int32, yh.shape, 1) < DA_QK_DIM
            s_lo = jnp.sum(jnp.where(lo, sq, 0.0), axis=-1, keepdims=True)
            s_hi = jnp.sum(jnp.where(lo, 0.0, sq), axis=-1, keepdims=True)
            r = jnp.where(lo, lax.rsqrt(s_lo * (1.0 / DA_QK_DIM) + EPS),
                          lax.rsqrt(s_hi * (1.0 / DA_QK_DIM) + EPS))
            y_scr[h] = yh * r * gain[:, sl]

    @pl.when(is128)
    def _():
        for h in range(heads):
            sl = slice(h * LANES, (h + 1) * LANES)
            yh = y[:, sl]
            ss = jnp.sum(yh * yh, axis=-1, keepdims=True)
            r = lax.rsqrt(ss * (1.0 / DL_HEAD_DIM) + EPS)
            y_scr[h] = yh * r * gain[:, sl]

    @pl.when(jnp.logical_not(is64 | is128))
    def _():
        for h in range(heads):
            y_scr[h] = y[:, h * LANES:(h + 1) * LANES]

    @pl.when(j < CT_MAIN_END)
    def _():
        for h in range(heads):
            o_ref[:, h * LANES:(h + 1) * LANES] = y_scr[h].astype(o_ref.dtype)

    def deinterleave(dst_ref):
        d, rows = dst_ref.shape[0], dst_ref.shape[1]
        for r in range(d):
            for h in range(heads):
                dst_ref[r, :, h * LANES:(h + 1) * LANES] = (
                    y_scr[h, pl.ds(r, rows, stride=d), :].astype(dst_ref.dtype))

    @pl.when((j >= CT_MAIN_END) & (j < CT_MAIN_END + 3))
    def _():
        deinterleave(d1_ref)

    @pl.when(j >= CT_MAIN_END + 3)
    def _():
        deinterleave(d2_ref)


def _inproj(x2, gain_mix, w_bf, gain_tab, B, S, tm=1024):
    T, D = x2.shape
    tiles_per_batch = S // tm
    d1, d2 = DL_GROUPS[1][1], DL_GROUPS[2][1]
    part1 = lambda j: jnp.clip(j - CT_MAIN_END, 0, 2)
    part2 = lambda j: jnp.clip(j - CT_MAIN_END - 3, 0, 2)
    return pl.pallas_call(
        _inproj_kernel,
        grid=(T // tm, CT_END),
        in_specs=[
            pl.BlockSpec((tm, D), lambda i, j: (i, 0)),
            pl.BlockSpec((1, D), lambda i, j: (0, 0)),
            pl.BlockSpec((D, COL_TILE), lambda i, j: (0, j)),
            pl.BlockSpec((None, 1, COL_TILE), lambda i, j: (j, 0, 0)),
        ],
        out_specs=[
            pl.BlockSpec((tm, COL_TILE), lambda i, j: (i, jnp.minimum(j, CT_MAIN_END - 1))),
            pl.BlockSpec((d1, tm // d1, COL_TILE),
                         lambda i, j: (i // tiles_per_batch, i % tiles_per_batch, part1(j))),
            pl.BlockSpec((d2, tm // d2, COL_TILE),
                         lambda i, j: (i // tiles_per_batch, i % tiles_per_batch, part2(j))),
        ],
        out_shape=[
            jax.ShapeDtypeStruct((T, CT_MAIN_END * COL_TILE), BF16),
            jax.ShapeDtypeStruct((B * d1, S // d1, 3 * COL_TILE), BF16),
            jax.ShapeDtypeStruct((B * d2, S // d2, 3 * COL_TILE), BF16),
        ],
        scratch_shapes=[pltpu.VMEM((tm, D), BF16), pltpu.VMEM((COL_TILE // LANES, tm, LANES), F32)],
        compiler_params=_params(("parallel", "arbitrary")),
    )(x2, gain_mix, w_bf, gain_tab)


def _da_kernel(q_ref, k_ref, v_ref, lq_ref, lk_ref, sg_ref, o_ref,
               bias_scr, s_scr, p_scr, m_scr, l_scr, a_scr, acc_scr, *, tq, rc, lam_init):
    h = pl.program_id(1)
    qi = pl.program_id(2)
    slope2 = jnp.exp2(-(h + 1).astype(F32)) * LOG2E
    nlb = tq // LANES

    q = q_ref[...]
    lane = lax.broadcasted_iota(jnp.int32, q.shape, 1)
    qs = (jnp.where(lane < DA_QK_DIM, q, jnp.zeros_like(q)),
          jnp.where(lane >= DA_QK_DIM, q, jnp.zeros_like(q)))

    row = lax.broadcasted_iota(jnp.int32, (tq, tq), 0)
    col = lax.broadcasted_iota(jnp.int32, (tq, tq), 1)
    bias_scr[...] = slope2 * (col - row).astype(F32)

    m_scr[...] = jnp.full(m_scr.shape, NEG_BIG, F32)
    l_scr[...] = jnp.zeros(l_scr.shape, F32)
    acc_scr[...] = jnp.zeros(acc_scr.shape, F32)

    def step(ki, masked):
        off = pl.multiple_of(ki * tq, tq)
        k = k_ref[pl.ds(off, tq), :]
        v = v_ref[pl.ds(off, tq), :]
        c = slope2 * ((ki - qi) * tq).astype(F32)
        for mi in range(2):
            s_scr[...] = _dot_nt(qs[mi], k)
            for r in range(tq // rc):
                rows = slice(r * rc, (r + 1) * rc)
                nb = min(nlb, ((r + 1) * rc - 1) // LANES + 1) if masked else nlb
                sb = []
                for j in range(nb):
                    cs = slice(j * LANES, (j + 1) * LANES)
                    s = s_scr[rows, cs] + bias_scr[rows, cs]
                    if masked and (j + 1) * LANES - 1 > r * rc:
                        rr = lax.broadcasted_iota(jnp.int32, (rc, LANES), 0) + r * rc
                        cc = lax.broadcasted_iota(jnp.int32, (rc, LANES), 1) + j * LANES
                        s = jnp.where(cc <= rr, s, NEG_BIG)
                    sb.append(s)
                mx = sb[0]
                for s in sb[1:]:
                    mx = jnp.maximum(mx, s)
                m_prev = m_scr[mi, rows, :]
                m_new = jnp.maximum(m_prev, jnp.max(mx, axis=-1, keepdims=True) + c)
                a_scr[mi, rows, :] = jnp.exp2(m_prev - m_new)
                m_scr[mi, rows, :] = m_new
                mc = m_new - c
                psum = None
                for j in range(nlb):
                    cs = slice(j * LANES, (j + 1) * LANES)
                    if j < nb:
                        p = jnp.exp2(sb[j] - mc)
                        psum = p if psum is None else psum + p
                        p_scr[rows, cs] = p.astype(BF16)
                    else:
                        p_scr[rows, cs] = jnp.zeros((rc, LANES), BF16)
                l_scr[mi, rows, :] = a_scr[mi, rows, :] * l_scr[mi, rows, :] + psum
            acc_scr[mi] = a_scr[mi] * acc_scr[mi] + _dot(p_scr[...], v)

    def body(ki, carry):
        step(ki, False)
        return carry

    lax.fori_loop(0, qi, body, 0)
    step(qi, True)

    lam_e = jnp.exp(jnp.sum(lq_ref[...] * lk_ref[...], axis=-1, keepdims=True))
    lam = lam_e[0:1, :] - lam_e[1:2, :] + lam_init
    l0 = jnp.sum(l_scr[0], axis=-1, keepdims=True)
    l1 = jnp.sum(l_scr[1], axis=-1, keepdims=True)
    o = acc_scr[0] / l0 - lam * (acc_scr[1] / l1)
    ms = jnp.mean(o * o, axis=-1, keepdims=True)
    o = o * lax.rsqrt(ms + EPS) * sg_ref[...] * (1.0 - lam_init)
    o_ref[...] = o.astype(o_ref.dtype)


def _diff_attention(proj, lam_q, lam_k, sub_gain, B, S, lam_init, tq=512, rc=32):
    T = proj.shape[0]
    nq = S // tq
    lb = LANES
    q_blk0, k_blk0, v_blk0 = (CT_DA_Q * COL_TILE) // lb, (CT_DA_K * COL_TILE) // lb, (CT_DA_V * COL_TILE) // lb
    return pl.pallas_call(
        functools.partial(_da_kernel, tq=tq, rc=rc, lam_init=lam_init),
        grid=(B, DA_HEADS, nq),
        in_specs=[
            pl.BlockSpec((tq, lb), lambda b, h, i: (b * nq + i, q_blk0 + h)),
            pl.BlockSpec((S, lb), lambda b, h, i: (b, k_blk0 + h)),
            pl.BlockSpec((S, lb), lambda b, h, i: (b, v_blk0 + h)),
            pl.BlockSpec((2, DA_QK_DIM), lambda b, h, i: (0, 0)),
            pl.BlockSpec((2, DA_QK_DIM), lambda b, h, i: (0, 0)),
            pl.BlockSpec((1, DA_V_DIM), lambda b, h, i: (0, 0)),
        ],
        out_specs=pl.BlockSpec((tq, lb), lambda b, h, i: (b * nq + i, h)),
        out_shape=jax.ShapeDtypeStruct((T, DA_HEADS * DA_V_DIM), BF16),
        scratch_shapes=[pltpu.VMEM((tq, tq), F32), pltpu.VMEM((tq, tq), F32), pltpu.VMEM((tq, tq), BF16),
                        pltpu.VMEM((2, tq, LANES), F32), pltpu.VMEM((2, tq, LANES), F32),
                        pltpu.VMEM((2, tq, LANES), F32), pltpu.VMEM((2, tq, DA_V_DIM), F32)],
        compiler_params=_params(("parallel", "parallel", "arbitrary")),
    )(proj, proj, proj, lam_q, lam_k, sub_gain)


def _dl_kernel(q_ref, kc_ref, kp_ref, vc_ref, vp_ref, o_ref, lse_ref, *, slopes2, d, tq):
    n = pl.program_id(1)
    sp = DL_SPAN
    row = lax.broadcasted_iota(jnp.int32, (sp, sp), 0)
    col = lax.broadcasted_iota(jnp.int32, (sp, sp), 1)
    dcur = row - col
    cur_ok = dcur >= 0
    prev_ok = dcur <= 0
    dcur_f = dcur.astype(F32)

    def residue(r, carry):
        for hh in range(DL_HEADS_PER_GROUP):
            hs = slice(hh * LANES, (hh + 1) * LANES)
            bias_c = -slopes2[hh] * dcur_f
            bias_p = -slopes2[hh] * (dcur_f + float(sp))
            for j in range(tq // sp):
                rs = slice(j * sp, (j + 1) * sp)
                q = q_ref[r, rs, hs]
                kc = kc_ref[r, rs, hs]
                vc = vc_ref[r, rs, hs]
                if j == 0:
                    kp = kp_ref[r, :, hs]
                    vp = vp_ref[r, :, hs]
                    p_ok = prev_ok & (n > 0)
                else:
                    ps = slice((j - 1) * sp, j * sp)
                    kp = kc_ref[r, ps, hs]
                    vp = vc_ref[r, ps, hs]
                    p_ok = prev_ok
                s_c = jnp.where(cur_ok, _dot_nt(q, kc) + bias_c, NEG_BIG)
                s_p = jnp.where(p_ok, _dot_nt(q, kp) + bias_p, NEG_BIG)
                m = jnp.maximum(jnp.max(s_c, axis=-1, keepdims=True), jnp.max(s_p, axis=-1, keepdims=True))
                p_c = jnp.exp2(s_c - m)
                p_p = jnp.exp2(s_p - m)
                den = jnp.sum(p_c, axis=-1, keepdims=True) + jnp.sum(p_p, axis=-1, keepdims=True)
                acc = _dot(p_c.astype(BF16), vc) + _dot(p_p.astype(BF16), vp)
                if d == 1:
                    out_rows = pl.ds(j * sp, sp)
                else:
                    out_rows = pl.ds(r + j * sp * d, sp, stride=d)
                o_ref[hh, out_rows, :] = acc / den
                lse_ref[hh, out_rows, :] = jnp.broadcast_to(m + jnp.log2(den), (sp, LANES))
        return carry

    lax.fori_loop(0, d, residue, 0)


def _dilated_group(src, col0, g, B, S, tok_per_step=2048):
    window, d = DL_GROUPS[g]
    assert window // d == DL_SPAN
    L = S // d
    tq = min(tok_per_step, S) // d
    assert tq % DL_SPAN == 0 and L % tq == 0
    nh = DL_HEADS_PER_GROUP * len(DL_GROUPS)
    slopes2 = tuple(2.0 ** (-8.0 * (g * DL_HEADS_PER_GROUP + hh + 1) / nh) * d * LOG2E
                    for hh in range(DL_HEADS_PER_GROUP))
    spb = tq // DL_SPAN
    nsteps = L // tq
    cur = lambda c: pl.BlockSpec((d, tq, COL_TILE), lambda b, n: (b, n, c))
    prev = lambda c: pl.BlockSpec((d, DL_SPAN, COL_TILE), lambda b, n: (b, jnp.maximum(n * spb - 1, 0), c))
    out_spec = pl.BlockSpec((DL_HEADS_PER_GROUP, d * tq, LANES), lambda b, n: (0, b * nsteps + n, 0))
    return pl.pallas_call(
        functools.partial(_dl_kernel, slopes2=slopes2, d=d, tq=tq),
        grid=(B, nsteps),
        in_specs=[cur(col0), cur(col0 + 1), prev(col0 + 1), cur(col0 + 2), prev(col0 + 2)],
        out_specs=[out_spec, out_spec],
        out_shape=[jax.ShapeDtypeStruct((DL_HEADS_PER_GROUP, B * S, LANES), F32)] * 2,
        compiler_params=_params(("parallel", "arbitrary")),
    )(src, src, src, src, src)


def _route(logits):
    lane = lax.broadcasted_iota(jnp.int32, logits.shape, 1)
    big = jnp.int32(1 << 20)
    is_g = (lane >= MOE_N_EXPERTS) & (lane < MOE_N_EXPERTS + MOE_GROUPS)
    lg = jnp.where(is_g, logits, -jnp.inf)
    gmax = jnp.max(lg, axis=-1, keepdims=True)
    gsum = jnp.sum(jnp.exp(lg - gmax), axis=-1, keepdims=True)
    g_w = 1.0 / gsum
    g_idx = jnp.min(jnp.where(lg == gmax, lane - MOE_N_EXPERTS, big), axis=-1, keepdims=True)
    in_grp = (lane < MOE_N_EXPERTS) & ((lane // MOE_EXPERTS_PER_GROUP) == g_idx)
    le = jnp.where(in_grp, logits, -jnp.inf)
    t1 = jnp.max(le, axis=-1, keepdims=True)
    e1 = jnp.min(jnp.where(le == t1, lane, big), axis=-1, keepdims=True)
    le2 = jnp.where(lane == e1, -jnp.inf, le)
    t2 = jnp.max(le2, axis=-1, keepdims=True)
    e2 = jnp.min(jnp.where(le2 == t2, lane, big), axis=-1, keepdims=True)
    r = jnp.exp(t2 - t1)
    w1 = g_w / (1.0 + r)
    w2 = w1 * r
    out = jnp.where(lane == 0, e1.astype(F32),
                    jnp.where(lane == 1, e2.astype(F32),
                              jnp.where(lane == 2, w1, jnp.where(lane == 3, w2, 0.0))))
    return out


def _outproj_kernel(x_ref, oa_ref, ga_ref, gb_ref, o0_ref, o1_ref, o2_ref, l0_ref, l1_ref, l2_ref,
                    wa_ref, wb_ref, wo_ref, gf_ref, rh_ref, rl_ref,
                    x1_ref, hn_ref, rt_ref):
    obs = []
    for hh in range(DL_HEADS_PER_GROUP):
        l0, l1, l2 = l0_ref[hh], l1_ref[hh], l2_ref[hh]
        lm = jnp.maximum(jnp.maximum(l0, l1), l2)
        e0, e1, e2 = jnp.exp2(l0 - lm), jnp.exp2(l1 - lm), jnp.exp2(l2 - lm)
        obs.append((e0 * o0_ref[hh] + e1 * o1_ref[hh] + e2 * o2_ref[hh]) / (e0 + e1 + e2))
    ob = jnp.concatenate(obs, axis=1)
    a = _dot(oa_ref[...], wa_ref[...])
    b = _dot(ob.astype(BF16), wb_ref[...])
    mixed = jax.nn.sigmoid(ga_ref[...].astype(F32)) * a + jax.nn.sigmoid(gb_ref[...].astype(F32)) * b
    x1 = x_ref[...] + _dot(mixed.astype(BF16), wo_ref[...])
    x1_ref[...] = x1
    ms = jnp.mean(x1 * x1, axis=-1, keepdims=True)
    hn = x1 * lax.rsqrt(ms + EPS) * gf_ref[...]
    hn_ref[...] = hn
    hn_hi = hn.astype(BF16)
    hn_lo = (hn - hn_hi.astype(F32)).astype(BF16)
    logits = _dot(hn_hi, rh_ref[...]) + (_dot(hn_lo, rh_ref[...]) + _dot(hn_hi, rl_ref[...]))
    rt_ref[...] = _route(logits)


def _outproj(x2, o_a, proj, dl_o, dl_lse, wa, wb, wo, gain_ffn, r_hi, r_lo, tm=256):
    T, D = x2.shape
    row = lambda w: pl.BlockSpec((tm, w), lambda i: (i, 0))
    full = lambda s: pl.BlockSpec(s, lambda i: (0, 0), pipeline_mode=pl.Buffered(1))
    hrow = pl.BlockSpec((DL_HEADS_PER_GROUP, tm, LANES), lambda i: (0, i, 0))
    return pl.pallas_call(
        _outproj_kernel,
        grid=(T // tm,),
        in_specs=[
            row(D), row(o_a.shape[1]),
            pl.BlockSpec((tm, D), lambda i: (i, (CT_GATE_A * COL_TILE) // D)),
            pl.BlockSpec((tm, D), lambda i: (i, (CT_GATE_B * COL_TILE) // D)),
            hrow, hrow, hrow, hrow, hrow, hrow,
            full(wa.shape), full(wb.shape), full(wo.shape), full((1, D)), full(r_hi.shape), full(r_lo.shape),
        ],
        out_specs=[row(D), row(D), row(LANES)],
        out_shape=[jax.ShapeDtypeStruct((T, D), F32), jax.ShapeDtypeStruct((T, D), F32),
                   jax.ShapeDtypeStruct((T, LANES), F32)],
        compiler_params=_params(("parallel",)),
    )(x2, o_a, proj, proj, dl_o[0], dl_o[1], dl_o[2], dl_lse[0], dl_lse[1], dl_lse[2],
      wa, wb, wo, gain_ffn, r_hi, r_lo)


def _expert_kernel(be_ref, nu_ref, tc_ref, tn_ref, hn_ref, wgu_ref, wd_ref, y_ref, xbuf, sem, *, d_ff):
    i = pl.program_id(0)
    nu = nu_ref[0]
    slot = i % 2
    rows = xbuf.shape[1]

    def copy(tok_ref, r, s):
        return pltpu.make_async_copy(hn_ref.at[tok_ref[0, r]], xbuf.at[s, r], sem.at[s])

    def issue(tok_ref, s):
        def body(r, c):
            copy(tok_ref, r, s).start()
            return c
        lax.fori_loop(0, rows, body, 0)

    @pl.when((i == 0) & (nu > 0))
    def _():
        issue(tc_ref, 0)

    @pl.when(i + 1 < nu)
    def _():
        issue(tn_ref, 1 - slot)

    @pl.when(i < nu)
    def _():
        def wait(r, c):
            copy(tc_ref, r, slot).wait()
            return c
        lax.fori_loop(0, rows, wait, 0)
        h = _dot(xbuf[slot].astype(BF16), wgu_ref[...])
        gate = h[:, :d_ff]
        up = h[:, d_ff:]
        act = gate * jax.nn.sigmoid(gate) * up
        y_ref[...] = _dot(act.astype(BF16), wd_ref[...])

    @pl.when(i >= nu)
    def _():
        y_ref[...] = jnp.zeros(y_ref.shape, y_ref.dtype)


def _experts(hn, src_tok, blk_expert, n_used, wgu, wd):
    T, D = hn.shape
    nb = src_tok.shape[0] // MOE_BLOCK
    d_ff = wd.shape[1]
    tok = src_tok.reshape(nb, 1, MOE_BLOCK)

    def live(i, nu):
        return jnp.minimum(i, nu[0] - 1)

    smem = lambda f: pl.BlockSpec((None, 1, MOE_BLOCK), f, memory_space=pltpu.SMEM)
    grid_spec = pltpu.PrefetchScalarGridSpec(
        num_scalar_prefetch=2,
        grid=(nb,),
        in_specs=[
            smem(lambda i, be, nu: (i, 0, 0)),
            smem(lambda i, be, nu: (jnp.minimum(i + 1, nb - 1), 0, 0)),
            pl.BlockSpec(memory_space=pl.ANY),
            pl.BlockSpec((None, D, 2 * d_ff), lambda i, be, nu: (be[live(i, nu)], 0, 0)),
            pl.BlockSpec((None, d_ff, D), lambda i, be, nu: (be[live(i, nu)], 0, 0)),
        ],
        out_specs=pl.BlockSpec((MOE_BLOCK, D), lambda i, be, nu: (i, 0)),
        scratch_shapes=[pltpu.VMEM((2, MOE_BLOCK, D), F32), pltpu.SemaphoreType.DMA((2,))],
    )
    return pl.pallas_call(
        functools.partial(_expert_kernel, d_ff=d_ff),
        grid_spec=grid_spec,
        out_shape=jax.ShapeDtypeStruct((nb * MOE_BLOCK, D), F32),
        compiler_params=_params(("arbitrary",)),
    )(blk_expert, n_used, tok, tok, hn, wgu, wd)


def _combine_kernel(pc_ref, pn_ref, x1_ref, rt_ref, y_ref, o_ref, buf, sem, *, tc):
    i = pl.program_id(0)
    n = pl.num_programs(0)
    slot = i % 2

    def copy(pos_ref, r, s):
        return pltpu.make_async_copy(y_ref.at[pos_ref[0, r]], buf.at[s, r], sem.at[s])

    def issue(pos_ref, s):
        def body(r, c):
            copy(pos_ref, r, s).start()
            return c
        lax.fori_loop(0, 2 * tc, body, 0)

    @pl.when(i == 0)
    def _():
        issue(pc_ref, 0)

    @pl.when(i + 1 < n)
    def _():
        issue(pn_ref, 1 - slot)

    def wait(r, c):
        copy(pc_ref, r, slot).wait()
        return c
    lax.fori_loop(0, 2 * tc, wait, 0)

    rt = rt_ref[...]
    w1 = rt[:, 2:3]
    w2 = rt[:, 3:4]
    y1 = buf[slot, pl.ds(0, tc), :]
    y2 = buf[slot, pl.ds(tc, tc), :]
    o_ref[...] = x1_ref[...] + (w1 * y1 + w2 * y2)


def _combine(x1, route, y_pad, slot_pos, tc=128):
    T, D = x1.shape
    nt = T // tc
    pos = slot_pos.reshape(nt, tc, 2).transpose(0, 2, 1).reshape(nt, 1, 2 * tc)
    smem = lambda f: pl.BlockSpec((None, 1, 2 * tc), f, memory_space=pltpu.SMEM)
    return pl.pallas_call(
        functools.partial(_combine_kernel, tc=tc),
        grid=(nt,),
        in_specs=[
            smem(lambda i: (i, 0, 0)),
            smem(lambda i: (jnp.minimum(i + 1, nt - 1), 0, 0)),
            pl.BlockSpec((tc, D), lambda i: (i, 0)),
            pl.BlockSpec((tc, LANES), lambda i: (i, 0)),
            pl.BlockSpec(memory_space=pl.ANY),
        ],
        out_specs=pl.BlockSpec((tc, D), lambda i: (i, 0)),
        out_shape=jax.ShapeDtypeStruct((T, D), F32),
        scratch_shapes=[pltpu.VMEM((2, 2 * tc, D), F32), pltpu.SemaphoreType.DMA((2,))],
        compiler_params=_params(("arbitrary",)),
    )(pos, pos, x1, route, y_pad)


def _dispatch_tables(route, T):
    TK = T * MOE_TOP_K
    flat_e = route[:, :MOE_TOP_K].astype(jnp.int32).reshape(-1)
    order = jnp.argsort(flat_e)
    sorted_e = flat_e[order]
    sizes = jnp.bincount(flat_e, length=MOE_N_EXPERTS).astype(jnp.int32)
    start = jnp.cumsum(sizes) - sizes
    padded = ((sizes + MOE_BLOCK - 1) // MOE_BLOCK) * MOE_BLOCK
    pad_end = jnp.cumsum(padded)
    pad_start = pad_end - padded
    dest = (pad_start[sorted_e] + (jnp.arange(TK, dtype=jnp.int32) - start[sorted_e])).astype(jnp.int32)
    n_blocks = TK // MOE_BLOCK + MOE_N_EXPERTS
    src_tok = jnp.zeros((n_blocks * MOE_BLOCK,), jnp.int32).at[dest].set((order // MOE_TOP_K).astype(jnp.int32))
    slot_pos = jnp.zeros((TK,), jnp.int32).at[order].set(dest)
    blk_expert = jnp.minimum(
        jnp.searchsorted(pad_end, jnp.arange(n_blocks, dtype=jnp.int32) * MOE_BLOCK, side='right'),
        MOE_N_EXPERTS - 1).astype(jnp.int32)
    n_used = (pad_end[-1:] // MOE_BLOCK).astype(jnp.int32)
    return src_tok, slot_pos, blk_expert, n_used


def _permute_w_in(w):
    da = DA_HEADS * 2 * DA_QK_DIM
    dav = DA_HEADS * DA_V_DIM
    dl = len(DL_GROUPS) * DL_HEADS_PER_GROUP * DL_HEAD_DIM
    o = [int(v) for v in np.cumsum([0, da, da, dav, dl, dl, dl])]
    parts = [w[:, o[6]:], w[:, :o[3]]]
    for g in range(len(DL_GROUPS)):
        for p in range(3):
            parts.append(w[:, o[3 + p] + g * COL_TILE: o[3 + p] + (g + 1) * COL_TILE])
    return jnp.concatenate(parts, axis=1)


def _gain_table(da_q_norm, da_k_norm, dl_q_norm, dl_k_norm):
    ones = jnp.ones((COL_TILE,), F32)
    daq = jnp.tile(da_q_norm, COL_TILE // DA_QK_DIM) * (DA_QK_DIM ** -0.5 * LOG2E)
    dak = jnp.tile(da_k_norm, COL_TILE // DA_QK_DIM)
    dlq = jnp.tile(dl_q_norm, COL_TILE // DL_HEAD_DIM) * (DL_HEAD_DIM ** -0.5 * LOG2E)
    dlk = jnp.tile(dl_k_norm, COL_TILE // DL_HEAD_DIM)
    rows = []
    for j in range(CT_END):
        if CT_DA_Q <= j < CT_DA_K:
            rows.append(daq)
        elif CT_DA_K <= j < CT_DA_V:
            rows.append(dak)
        elif j >= CT_DL and (j - CT_DL) % 3 == 0:
            rows.append(dlq)
        elif j >= CT_DL and (j - CT_DL) % 3 == 1:
            rows.append(dlk)
        else:
            rows.append(ones)
    return jnp.stack(rows, axis=0).reshape(CT_END, 1, COL_TILE)


def kernel(x, norm_mix, w_in, da_q_norm, da_k_norm, da_lambda_q, da_lambda_k, da_sub_norm,
           dl_q_norm, dl_k_norm, w_branch_a, w_branch_b, w_out, norm_ffn,
           w_group_router, w_expert_router, w_gate_up, w_down):
    B, S, D = x.shape
    T = B * S
    depth = w_in.shape[0]
    x2 = x.reshape(T, D)
    for l in range(depth):
        lam_init = 0.8 - 0.6 * math.exp(-0.3 * l)
        w_in_bf = _permute_w_in(w_in[l]).astype(BF16)
        gain_tab = _gain_table(da_q_norm[l], da_k_norm[l], dl_q_norm[l], dl_k_norm[l])
        proj, dl1, dl2 = _inproj(x2, norm_mix[l].reshape(1, D), w_in_bf, gain_tab, B, S)

        o_a = _diff_attention(proj, da_lambda_q[l], da_lambda_k[l], da_sub_norm[l].reshape(1, DA_V_DIM),
                              B, S, lam_init)
        dl = [_dilated_group(proj.reshape(B, S, proj.shape[1]), CT_DL, 0, B, S),
              _dilated_group(dl1, 0, 1, B, S), _dilated_group(dl2, 0, 2, B, S)]

        w_r = jnp.concatenate([w_expert_router[l], w_group_router[l]], axis=1)
        w_r = jnp.pad(w_r, ((0, 0), (0, LANES - w_r.shape[1])))
        r_hi = w_r.astype(BF16)
        r_lo = (w_r - r_hi.astype(F32)).astype(BF16)
        x1, hn, route = _outproj(
            x2, o_a, proj, [t[0] for t in dl], [t[1] for t in dl],
            w_branch_a[l].astype(BF16), w_branch_b[l].astype(BF16), w_out[l].astype(BF16),
            norm_ffn[l].reshape(1, D), r_hi, r_lo)

        src_tok, slot_pos, blk_expert, n_used = _dispatch_tables(route, T)
        y_pad = _experts(hn, src_tok, blk_expert, n_used, w_gate_up[l].astype(BF16), w_down[l].astype(BF16))
        x2 = _combine(x1, route, y_pad, slot_pos)
    return x2.reshape(B, S, D)
```

```python
import functools
import math

import jax
import jax.numpy as jnp
import numpy as np
from jax import lax
from jax.experimental import pallas as pl
from jax.experimental.pallas import tpu as pltpu

F32 = jnp.float32
BF16 = jnp.bfloat16

EPS = 1e-6
LOG2E = 1.4426950408889634
NEG_BIG = -1e30

DA_HEADS = 8
DA_QK_DIM = 64
DA_V_DIM = 128
DL_GROUPS = ((128, 1), (512, 4), (2048, 16))
DL_HEADS_PER_GROUP = 4
DL_HEAD_DIM = 128
DL_SPAN = 128
MOE_GROUPS = 4
MOE_EXPERTS_PER_GROUP = 8
MOE_N_EXPERTS = 32
MOE_TOP_K = 2
MOE_BLOCK = 128

LANES = 128
COL_TILE = 512
VMEM_LIMIT = 56 * 1024 * 1024

CT_GATE_A, CT_GATE_B, CT_DA_Q, CT_DA_K, CT_DA_V, CT_DL, CT_MAIN_END, CT_END = 0, 4, 8, 10, 12, 14, 17, 23


def _params(sem, vmem=VMEM_LIMIT):
    return pltpu.CompilerParams(dimension_semantics=sem, vmem_limit_bytes=vmem)


def _dot(a, b):
    return jnp.dot(a, b, preferred_element_type=F32)


def _dot_nt(a, b):
    return lax.dot_general(a, b, (((1,), (1,)), ((), ())), preferred_element_type=F32)


def _inproj_kernel(x_ref, g_ref, w_ref, gain_ref, o_ref, d1_ref, d2_ref, h_scr, y_scr):
    j = pl.program_id(1)

    @pl.when(j == 0)
    def _():
        x = x_ref[...]
        ms = jnp.mean(x * x, axis=-1, keepdims=True)
        h_scr[...] = (x * lax.rsqrt(ms + EPS) * g_ref[...]).astype(BF16)

    y = _dot(h_scr[...], w_ref[...])
    gain = gain_ref[...]
    heads = COL_TILE // LANES

    is64 = (j >= CT_DA_Q) & (j < CT_DA_V)
    is128 = (j >= CT_DL) & (lax.rem(j - CT_DL, 3) < 2)

    @pl.when(is64)
    def _():
        for h in range(heads):
            sl = slice(h * LANES, (h + 1) * LANES)
            yh = y[:, sl]
            sq = yh * yh
            lo = lax.broadcasted_iota(jnp.int32, yh.shape, 1) < DA_QK_DIM
            s_lo = jnp.sum(jnp.where(lo, sq, 0.0), axis=-1, keepdims=True)
            s_hi = jnp.sum(jnp.where(lo, 0.0, sq), axis=-1, keepdims=True)
            r = jnp.where(lo, lax.rsqrt(s_lo * (1.0 / DA_QK_DIM) + EPS),
                          lax.rsqrt(s_hi * (1.0 / DA_QK_DIM) + EPS))
            y_scr[h] = yh * r * gain[:, sl]

    @pl.when(is128)
    def _():
        for h in range(heads):
            sl = slice(h * LANES, (h + 1) * LANES)
            yh = y[:, sl]
            ss = jnp.sum(yh * yh, axis=-1, keepdims=True)
            r = lax.rsqrt(ss * (1.0 / DL_HEAD_DIM) + EPS)
            y_scr[h] = yh * r * gain[:, sl]

    @pl.when(jnp.logical_not(is64 | is128))
    def _():
        for h in range(heads):
            y_scr[h] = y[:, h * LANES:(h + 1) * LANES]

    @pl.when(j < CT_MAIN_END)
    def _():
        for h in range(heads):
            o_ref[:, h * LANES:(h + 1) * LANES] = y_scr[h].astype(o_ref.dtype)

    def deinterleave(dst_ref):
        d, rows = dst_ref.shape[0], dst_ref.shape[1]
        for r in range(d):
            for h in range(heads):
                dst_ref[r, :, h * LANES:(h + 1) * LANES] = (
                    y_scr[h, pl.ds(r, rows, stride=d), :].astype(dst_ref.dtype))

    @pl.when((j >= CT_MAIN_END) & (j < CT_MAIN_END + 3))
    def _():
        deinterleave(d1_ref)

    @pl.when(j >= CT_MAIN_END + 3)
    def _():
        deinterleave(d2_ref)


def _inproj(x2, gain_mix, w_bf, gain_tab, B, S, tm=1024):
    T, D = x2.shape
    tiles_per_batch = S // tm
    d1, d2 = DL_GROUPS[1][1], DL_GROUPS[2][1]
    part1 = lambda j: jnp.clip(j - CT_MAIN_END, 0, 2)
    part2 = lambda j: jnp.clip(j - CT_MAIN_END - 3, 0, 2)
    return pl.pallas_call(
        _inproj_kernel,
        grid=(T // tm, CT_END),
        in_specs=[
            pl.BlockSpec((tm, D), lambda i, j: (i, 0)),
            pl.BlockSpec((1, D), lambda i, j: (0, 0)),
            pl.BlockSpec((D, COL_TILE), lambda i, j: (0, j)),
            pl.BlockSpec((None, 1, COL_TILE), lambda i, j: (j, 0, 0)),
        ],
        out_specs=[
            pl.BlockSpec((tm, COL_TILE), lambda i, j: (i, jnp.minimum(j, CT_MAIN_END - 1))),
            pl.BlockSpec((d1, tm // d1, COL_TILE),
                         lambda i, j: (i // tiles_per_batch, i % tiles_per_batch, part1(j))),
            pl.BlockSpec((d2, tm // d2, COL_TILE),
                         lambda i, j: (i // tiles_per_batch, i % tiles_per_batch, part2(j))),
        ],
        out_shape=[
            jax.ShapeDtypeStruct((T, CT_MAIN_END * COL_TILE), BF16),
            jax.ShapeDtypeStruct((B * d1, S // d1, 3 * COL_TILE), BF16),
            jax.ShapeDtypeStruct((B * d2, S // d2, 3 * COL_TILE), BF16),
        ],
        scratch_shapes=[pltpu.VMEM((tm, D), BF16), pltpu.VMEM((COL_TILE // LANES, tm, LANES), F32)],
        compiler_params=_params(("parallel", "arbitrary")),
    )(x2, gain_mix, w_bf, gain_tab)


def _da_kernel(q_ref, k_ref, v_ref, lq_ref, lk_ref, sg_ref, o_ref,
               bias_scr, s_scr, p_scr, m_scr, l_scr, a_scr, acc_scr, *, tq, rc, lam_init):
    h = pl.program_id(1)
    qi = pl.program_id(2)
    slope2 = jnp.exp2(-(h + 1).astype(F32)) * LOG2E
    nlb = tq // LANES

    q = q_ref[...]
    lane = lax.broadcasted_iota(jnp.int32, q.shape, 1)
    qs = (jnp.where(lane < DA_QK_DIM, q, jnp.zeros_like(q)),
          jnp.where(lane >= DA_QK_DIM, q, jnp.zeros_like(q)))

    row = lax.broadcasted_iota(jnp.int32, (tq, tq), 0)
    col = lax.broadcasted_iota(jnp.int32, (tq, tq), 1)
    bias_scr[...] = slope2 * (col - row).astype(F32)

    m_scr[...] = jnp.full(m_scr.shape, NEG_BIG, F32)
    l_scr[...] = jnp.zeros(l_scr.shape, F32)
    acc_scr[...] = jnp.zeros(acc_scr.shape, F32)

    def step(ki, masked):
        off = pl.multiple_of(ki * tq, tq)
        k = k_ref[pl.ds(off, tq), :]
        v = v_ref[pl.ds(off, tq), :]
        c = slope2 * ((ki - qi) * tq).astype(F32)
        for mi in range(2):
            s_scr[...] = _dot_nt(qs[mi], k)
            for r in range(tq // rc):
                rows = slice(r * rc, (r + 1) * rc)
                nb = min(nlb, ((r + 1) * rc - 1) // LANES + 1) if masked else nlb
                sb = []
                for j in range(nb):
                    cs = slice(j * LANES, (j + 1) * LANES)
                    s = s_scr[rows, cs] + bias_scr[rows, cs]
                    if masked and (j + 1) * LANES - 1 > r * rc:
                        rr = lax.broadcasted_iota(jnp.int32, (rc, LANES), 0) + r * rc
                        cc = lax.broadcasted_iota(jnp.int32, (rc, LANES), 1) + j * LANES
                        s = jnp.where(cc <= rr, s, NEG_BIG)
                    sb.append(s)
                mx = sb[0]
                for s in sb[1:]:
                    mx = jnp.maximum(mx, s)
                m_prev = m_scr[mi, rows, :]
                m_new = jnp.maximum(m_prev, jnp.max(mx, axis=-1, keepdims=True) + c)
                a_scr[mi, rows, :] = jnp.exp2(m_prev - m_new)
                m_scr[mi, rows, :] = m_new
                mc = m_new - c
                psum = None
                for j in range(nlb):
                    cs = slice(j * LANES, (j + 1) * LANES)
                    if j < nb:
                        p = jnp.exp2(sb[j] - mc)
                        psum = p if psum is None else psum + p
                        p_scr[rows, cs] = p.astype(BF16)
                    else:
                        p_scr[rows, cs] = jnp.zeros((rc, LANES), BF16)
                l_scr[mi, rows, :] = a_scr[mi, rows, :] * l_scr[mi, rows, :] + psum
            acc_scr[mi] = a_scr[mi] * acc_scr[mi] + _dot(p_scr[...], v)

    def body(ki, carry):
        step(ki, False)
        return carry

    lax.fori_loop(0, qi, body, 0)
    step(qi, True)

    lam_e = jnp.exp(jnp.sum(lq_ref[...] * lk_ref[...], axis=-1, keepdims=True))
    lam = lam_e[0:1, :] - lam_e[1:2, :] + lam_init
    l0 = jnp.sum(l_scr[0], axis=-1, keepdims=True)
    l1 = jnp.sum(l_scr[1], axis=-1, keepdims=True)
    o = acc_scr[0] / l0 - lam * (acc_scr[1] / l1)
    ms = jnp.mean(o * o, axis=-1, keepdims=True)
    o = o * lax.rsqrt(ms + EPS) * sg_ref[...] * (1.0 - lam_init)
    o_ref[...] = o.astype(o_ref.dtype)


def _diff_attention(proj, lam_q, lam_k, sub_gain, B, S, lam_init, tq=512, rc=32):
    T = proj.shape[0]
    nq = S // tq
    lb = LANES
    q_blk0, k_blk0, v_blk0 = (CT_DA_Q * COL_TILE) // lb, (CT_DA_K * COL_TILE) // lb, (CT_DA_V * COL_TILE) // lb
    return pl.pallas_call(
        functools.partial(_da_kernel, tq=tq, rc=rc, lam_init=lam_init),
        grid=(B, DA_HEADS, nq),
        in_specs=[
            pl.BlockSpec((tq, lb), lambda b, h, i: (b * nq + i, q_blk0 + h)),
            pl.BlockSpec((S, lb), lambda b, h, i: (b, k_blk0 + h)),
            pl.BlockSpec((S, lb), lambda b, h, i: (b, v_blk0 + h)),
            pl.BlockSpec((2, DA_QK_DIM), lambda b, h, i: (0, 0)),
            pl.BlockSpec((2, DA_QK_DIM), lambda b, h, i: (0, 0)),
            pl.BlockSpec((1, DA_V_DIM), lambda b, h, i: (0, 0)),
        ],
        out_specs=pl.BlockSpec((tq, lb), lambda b, h, i: (b * nq + i, h)),
        out_shape=jax.ShapeDtypeStruct((T, DA_HEADS * DA_V_DIM), BF16),
        scratch_shapes=[pltpu.VMEM((tq, tq), F32), pltpu.VMEM((tq, tq), F32), pltpu.VMEM((tq, tq), BF16),
                        pltpu.VMEM((2, tq, LANES), F32), pltpu.VMEM((2, tq, LANES), F32),
                        pltpu.VMEM((2, tq, LANES), F32), pltpu.VMEM((2, tq, DA_V_DIM), F32)],
        compiler_params=_params(("parallel", "parallel", "arbitrary")),
    )(proj, proj, proj, lam_q, lam_k, sub_gain)


def _dl_kernel(q_ref, kc_ref, kp_ref, vc_ref, vp_ref, o_ref, lse_ref, *, slopes2, d, tq):
    n = pl.program_id(1)
    sp = DL_SPAN
    row = lax.broadcasted_iota(jnp.int32, (sp, sp), 0)
    col = lax.broadcasted_iota(jnp.int32, (sp, sp), 1)
    dcur = row - col
    cur_ok = dcur >= 0
    prev_ok = dcur <= 0
    dcur_f = dcur.astype(F32)

    def residue(r, carry):
        for hh in range(DL_HEADS_PER_GROUP):
            hs = slice(hh * LANES, (hh + 1) * LANES)
            bias_c = -slopes2[hh] * dcur_f
            bias_p = -slopes2[hh] * (dcur_f + float(sp))
            for j in range(tq // sp):
                rs = slice(j * sp, (j + 1) * sp)
                q = q_ref[r, rs, hs]
                kc = kc_ref[r, rs, hs]
                vc = vc_ref[r, rs, hs]
                if j == 0:
                    kp = kp_ref[r, :, hs]
                    vp = vp_ref[r, :, hs]
                    p_ok = prev_ok & (n > 0)
                else:
                    ps = slice((j - 1) * sp, j * sp)
                    kp = kc_ref[r, ps, hs]
                    vp = vc_ref[r, ps, hs]
                    p_ok = prev_ok
                s_c = jnp.where(cur_ok, _dot_nt(q, kc) + bias_c, NEG_BIG)
                s_p = jnp.where(p_ok, _dot_nt(q, kp) + bias_p, NEG_BIG)
                m = jnp.maximum(jnp.max(s_c, axis=-1, keepdims=True), jnp.max(s_p, axis=-1, keepdims=True))
                p_c = jnp.exp2(s_c - m)
                p_p = jnp.exp2(s_p - m)
                den = jnp.sum(p_c, axis=-1, keepdims=True) + jnp.sum(p_p, axis=-1, keepdims=True)
                acc = _dot(p_c.astype(BF16), vc) + _dot(p_p.astype(BF16), vp)
                if d == 1:
                    out_rows = pl.ds(j * sp, sp)
                else:
                    out_rows = pl.ds(r + j * sp * d, sp, stride=d)
                o_ref[hh, out_rows, :] = acc / den
                lse_ref[hh, out_rows, :] = jnp.broadcast_to(m + jnp.log2(den), (sp, LANES))
        return carry

    lax.fori_loop(0, d, residue, 0)


def _dilated_group(src, col0, g, B, S, tok_per_step=2048):
    window, d = DL_GROUPS[g]
    assert window // d == DL_SPAN
    L = S // d
    tq = min(tok_per_step, S) // d
    assert tq % DL_SPAN == 0 and L % tq == 0
    nh = DL_HEADS_PER_GROUP * len(DL_GROUPS)
    slopes2 = tuple(2.0 ** (-8.0 * (g * DL_HEADS_PER_GROUP + hh + 1) / nh) * d * LOG2E
                    for hh in range(DL_HEADS_PER_GROUP))
    spb = tq // DL_SPAN
    nsteps = L // tq
    cur = lambda c: pl.BlockSpec((d, tq, COL_TILE), lambda b, n: (b, n, c))
    prev = lambda c: pl.BlockSpec((d, DL_SPAN, COL_TILE), lambda b, n: (b, jnp.maximum(n * spb - 1, 0), c))
    out_spec = pl.BlockSpec((DL_HEADS_PER_GROUP, d * tq, LANES), lambda b, n: (0, b * nsteps + n, 0))
    return pl.pallas_call(
        functools.partial(_dl_kernel, slopes2=slopes2, d=d, tq=tq),
        grid=(B, nsteps),
        in_specs=[cur(col0), cur(col0 + 1), prev(col0 + 1), cur(col0 + 2), prev(col0 + 2)],
        out_specs=[out_spec, out_spec],
        out_shape=[jax.ShapeDtypeStruct((DL_HEADS_PER_GROUP, B * S, LANES), F32)] * 2,
        compiler_params=_params(("parallel", "arbitrary")),
    )(src, src, src, src, src)


def _route(logits):
    lane = lax.broadcasted_iota(jnp.int32, logits.shape, 1)
    big = jnp.int32(1 << 20)
    is_g = (lane >= MOE_N_EXPERTS) & (lane < MOE_N_EXPERTS + MOE_GROUPS)
    lg = jnp.where(is_g, logits, -jnp.inf)
    gmax = jnp.max(lg, axis=-1, keepdims=True)
    gsum = jnp.sum(jnp.exp(lg - gmax), axis=-1, keepdims=True)
    g_w = 1.0 / gsum
    g_idx = jnp.min(jnp.where(lg == gmax, lane - MOE_N_EXPERTS, big), axis=-1, keepdims=True)
    in_grp = (lane < MOE_N_EXPERTS) & ((lane // MOE_EXPERTS_PER_GROUP) == g_idx)
    le = jnp.where(in_grp, logits, -jnp.inf)
    t1 = jnp.max(le, axis=-1, keepdims=True)
    e1 = jnp.min(jnp.where(le == t1, lane, big), axis=-1, keepdims=True)
    le2 = jnp.where(lane == e1, -jnp.inf, le)
    t2 = jnp.max(le2, axis=-1, keepdims=True)
    e2 = jnp.min(jnp.where(le2 == t2, lane, big), axis=-1, keepdims=True)
    r = jnp.exp(t2 - t1)
    w1 = g_w / (1.0 + r)
    w2 = w1 * r
    out = jnp.where(lane == 0, e1.astype(F32),
                    jnp.where(lane == 1, e2.astype(F32),
                              jnp.where(lane == 2, w1, jnp.where(lane == 3, w2, 0.0))))
    return out


def _outproj_kernel(x_ref, oa_ref, ga_ref, gb_ref, o0_ref, o1_ref, o2_ref, l0_ref, l1_ref, l2_ref,
                    wa_ref, wb_ref, wo_ref, gf_ref, rh_ref, rl_ref,
                    x1_ref, hn_ref, rt_ref):
    obs = []
    for hh in range(DL_HEADS_PER_GROUP):
        l0, l1, l2 = l0_ref[hh], l1_ref[hh], l2_ref[hh]
        lm = jnp.maximum(jnp.maximum(l0, l1), l2)
        e0, e1, e2 = jnp.exp2(l0 - lm), jnp.exp2(l1 - lm), jnp.exp2(l2 - lm)
        obs.append((e0 * o0_ref[hh] + e1 * o1_ref[hh] + e2 * o2_ref[hh]) / (e0 + e1 + e2))
    ob = jnp.concatenate(obs, axis=1)
    a = _dot(oa_ref[...], wa_ref[...])
    b = _dot(ob.astype(BF16), wb_ref[...])
    mixed = jax.nn.sigmoid(ga_ref[...].astype(F32)) * a + jax.nn.sigmoid(gb_ref[...].astype(F32)) * b
    x1 = x_ref[...] + _dot(mixed.astype(BF16), wo_ref[...])
    x1_ref[...] = x1
    ms = jnp.mean(x1 * x1, axis=-1, keepdims=True)
    hn = x1 * lax.rsqrt(ms + EPS) * gf_ref[...]
    hn_ref[...] = hn
    hn_hi = hn.astype(BF16)
    hn_lo = (hn - hn_hi.astype(F32)).astype(BF16)
    logits = _dot(hn_hi, rh_ref[...]) + (_dot(hn_lo, rh_ref[...]) + _dot(hn_hi, rl_ref[...]))
    rt_ref[...] = _route(logits)


def _outproj(x2, o_a, proj, dl_o, dl_lse, wa, wb, wo, gain_ffn, r_hi, r_lo, tm=256):
    T, D = x2.shape
    row = lambda w: pl.BlockSpec((tm, w), lambda i: (i, 0))
    full = lambda s: pl.BlockSpec(s, lambda i: (0, 0), pipeline_mode=pl.Buffered(1))
    hrow = pl.BlockSpec((DL_HEADS_PER_GROUP, tm, LANES), lambda i: (0, i, 0))
    return pl.pallas_call(
        _outproj_kernel,
        grid=(T // tm,),
        in_specs=[
            row(D), row(o_a.shape[1]),
            pl.BlockSpec((tm, D), lambda i: (i, (CT_GATE_A * COL_TILE) // D)),
            pl.BlockSpec((tm, D), lambda i: (i, (CT_GATE_B * COL_TILE) // D)),
            hrow, hrow, hrow, hrow, hrow, hrow,
            full(wa.shape), full(wb.shape), full(wo.shape), full((1, D)), full(r_hi.shape), full(r_lo.shape),
        ],
        out_specs=[row(D), row(D), row(LANES)],
        out_shape=[jax.ShapeDtypeStruct((T, D), F32), jax.ShapeDtypeStruct((T, D), F32),
                   jax.ShapeDtypeStruct((T, LANES), F32)],
        compiler_params=_params(("parallel",)),
    )(x2, o_a, proj, proj, dl_o[0], dl_o[1], dl_o[2], dl_lse[0], dl_lse[1], dl_lse[2],
      wa, wb, wo, gain_ffn, r_hi, r_lo)


def _expert_kernel(be_ref, nu_ref, tc_ref, tn_ref, dc_ref, hn_ref, wgu_ref, wd_ref, y_ref,
                   xbuf, ybuf, gsem, ssem, *, d_ff):
    i = pl.program_id(0)
    last = pl.num_programs(0) - 1
    nu = nu_ref[0]
    slot = i % 2
    rows = xbuf.shape[1]

    def gather(tok_ref, r, s):
        return pltpu.make_async_copy(hn_ref.at[tok_ref[0, r]], xbuf.at[s, r], gsem.at[s])

    def scatter(r, s):
        return pltpu.make_async_copy(ybuf.at[s, r], y_ref.at[dc_ref[0, r]], ssem.at[s])

    def wait_gather(s):
        pltpu.make_async_copy(hn_ref.at[pl.ds(0, rows)], xbuf.at[s], gsem.at[s]).wait()

    def wait_scatter(s):
        pltpu.make_async_copy(ybuf.at[s], y_ref.at[pl.ds(0, rows)], ssem.at[s]).wait()

    @pl.when(i == 0)
    def _():
        n_res = y_ref.shape[0] - 2 * rows
        ybuf[...] = jnp.zeros(ybuf.shape, ybuf.dtype)
        for s in range(2):
            pltpu.make_async_copy(ybuf.at[s], y_ref.at[pl.ds(n_res + s * rows, rows)], ssem.at[s]).start()
        for s in range(2):
            pltpu.make_async_copy(ybuf.at[s], y_ref.at[pl.ds(n_res + s * rows, rows)], ssem.at[s]).wait()

    @pl.when((i == 0) & (nu > 0))
    def _():
        def body(r, c):
            gather(tc_ref, r, 0).start()
            return c
        lax.fori_loop(0, rows, body, 0)

    @pl.when((i >= 2) & (i < nu))
    def _():
        wait_scatter(slot)

    @pl.when(i < nu)
    def _():
        wait_gather(slot)
        for r in range(rows):
            gather(tn_ref, r, 1 - slot).start()
        h = _dot(xbuf[slot].astype(BF16), wgu_ref[...])
        gate = h[:, :d_ff]
        up = h[:, d_ff:]
        act = gate * jax.nn.sigmoid(gate) * up
        ybuf[slot] = _dot(act.astype(BF16), wd_ref[...])
        for r in range(rows):
            scatter(r, slot).start()

    @pl.when((i == nu) & (nu > 0))
    def _():
        wait_gather(slot)

    @pl.when(i == last)
    def _():
        @pl.when(nu >= 1)
        def _():
            wait_scatter((nu - 1) % 2)

        @pl.when(nu >= 2)
        def _():
            wait_scatter(nu % 2)


def _experts(hn, src_tok, dst_row, blk_expert, n_used, wgu, wd, n_out_rows):
    T, D = hn.shape
    nblk = src_tok.shape[0] // MOE_BLOCK
    nsteps = nblk - 1
    d_ff = wd.shape[1]
    tok = src_tok.reshape(nblk, 1, MOE_BLOCK)
    dst = dst_row.reshape(nblk, 1, MOE_BLOCK)

    def live(i, nu):
        return jnp.maximum(jnp.minimum(i, nu[0] - 1), 0)

    smem = lambda f: pl.BlockSpec((None, 1, MOE_BLOCK), f, memory_space=pltpu.SMEM)
    grid_spec = pltpu.PrefetchScalarGridSpec(
        num_scalar_prefetch=2,
        grid=(nsteps,),
        in_specs=[
            smem(lambda i, be, nu: (i, 0, 0)),
            smem(lambda i, be, nu: (i + 1, 0, 0)),
            smem(lambda i, be, nu: (i, 0, 0)),
            pl.BlockSpec(memory_space=pl.ANY),
            pl.BlockSpec((None, D, 2 * d_ff), lambda i, be, nu: (be[live(i, nu)], 0, 0)),
            pl.BlockSpec((None, d_ff, D), lambda i, be, nu: (be[live(i, nu)], 0, 0)),
        ],
        out_specs=pl.BlockSpec(memory_space=pl.ANY),
        scratch_shapes=[pltpu.VMEM((2, MOE_BLOCK, D), F32), pltpu.VMEM((2, MOE_BLOCK, D), F32),
                        pltpu.SemaphoreType.DMA((2,)), pltpu.SemaphoreType.DMA((2,))],
    )
    return pl.pallas_call(
        functools.partial(_expert_kernel, d_ff=d_ff),
        grid_spec=grid_spec,
        out_shape=jax.ShapeDtypeStruct((n_out_rows, D), F32),
        compiler_params=_params(("arbitrary",)),
    )(blk_expert, n_used, tok, tok, dst, hn, wgu, wd)


def _combine_kernel(x1_ref, rt_ref, y1_ref, y2_ref, o_ref):
    rt = rt_ref[...]
    o_ref[...] = x1_ref[...] + (rt[:, 2:3] * y1_ref[...] + rt[:, 3:4] * y2_ref[...])


def _combine(x1, route, y, tm=512):
    T, D = x1.shape
    nt = T // tm
    return pl.pallas_call(
        _combine_kernel,
        grid=(nt,),
        in_specs=[
            pl.BlockSpec((tm, D), lambda i: (i, 0)),
            pl.BlockSpec((tm, LANES), lambda i: (i, 0)),
            pl.BlockSpec((tm, D), lambda i: (i, 0)),
            pl.BlockSpec((tm, D), lambda i: (nt + i, 0)),
        ],
        out_specs=pl.BlockSpec((tm, D), lambda i: (i, 0)),
        out_shape=jax.ShapeDtypeStruct((T, D), F32),
        compiler_params=_params(("parallel",)),
    )(x1, route, y, y)


def _dispatch_tables(route, T):
    TK = T * MOE_TOP_K
    flat_e = route[:, :MOE_TOP_K].astype(jnp.int32).reshape(-1)
    order = jnp.argsort(flat_e)
    sorted_e = flat_e[order]
    sizes = jnp.bincount(flat_e, length=MOE_N_EXPERTS).astype(jnp.int32)
    start = jnp.cumsum(sizes) - sizes
    padded = ((sizes + MOE_BLOCK - 1) // MOE_BLOCK) * MOE_BLOCK
    pad_end = jnp.cumsum(padded)
    pad_start = pad_end - padded
    dest = (pad_start[sorted_e] + (jnp.arange(TK, dtype=jnp.int32) - start[sorted_e])).astype(jnp.int32)
    n_blocks = TK // MOE_BLOCK + MOE_N_EXPERTS
    n_pos = (n_blocks + 2) * MOE_BLOCK
    pos = jnp.arange(n_pos, dtype=jnp.int32)
    trash = TK + ((pos // MOE_BLOCK) % 2) * MOE_BLOCK + pos % MOE_BLOCK
    tok_sorted = (order // MOE_TOP_K).astype(jnp.int32)
    row_sorted = ((order % MOE_TOP_K) * T + tok_sorted).astype(jnp.int32)
    src_tok = jnp.zeros((n_pos,), jnp.int32).at[dest].set(tok_sorted)
    dst_row = trash.at[dest].set(row_sorted)
    blk_expert = jnp.minimum(
        jnp.searchsorted(pad_end, jnp.arange(n_blocks + 1, dtype=jnp.int32) * MOE_BLOCK, side='right'),
        MOE_N_EXPERTS - 1).astype(jnp.int32)
    n_used = (pad_end[-1:] // MOE_BLOCK).astype(jnp.int32)
    return src_tok, dst_row, blk_expert, n_used, TK + 2 * MOE_BLOCK


def _permute_w_in(w):
    da = DA_HEADS * 2 * DA_QK_DIM
    dav = DA_HEADS * DA_V_DIM
    dl = len(DL_GROUPS) * DL_HEADS_PER_GROUP * DL_HEAD_DIM
    o = [int(v) for v in np.cumsum([0, da, da, dav, dl, dl, dl])]
    parts = [w[:, o[6]:], w[:, :o[3]]]
    for g in range(len(DL_GROUPS)):
        for p in range(3):
            parts.append(w[:, o[3 + p] + g * COL_TILE: o[3 + p] + (g + 1) * COL_TILE])
    return jnp.concatenate(parts, axis=1)


def _gain_table(da_q_norm, da_k_norm, dl_q_norm, dl_k_norm):
    ones = jnp.ones((COL_TILE,), F32)
    daq = jnp.tile(da_q_norm, COL_TILE // DA_QK_DIM) * (DA_QK_DIM ** -0.5 * LOG2E)
    dak = jnp.tile(da_k_norm, COL_TILE // DA_QK_DIM)
    dlq = jnp.tile(dl_q_norm, COL_TILE // DL_HEAD_DIM) * (DL_HEAD_DIM ** -0.5 * LOG2E)
    dlk = jnp.tile(dl_k_norm, COL_TILE // DL_HEAD_DIM)
    rows = []
    for j in range(CT_END):
        if CT_DA_Q <= j < CT_DA_K:
            rows.append(daq)
        elif CT_DA_K <= j < CT_DA_V:
            rows.append(dak)
        elif j >= CT_DL and (j - CT_DL) % 3 == 0:
            rows.append(dlq)
        elif j >= CT_DL and (j - CT_DL) % 3 == 1:
            rows.append(dlk)
        else:
            rows.append(ones)
    return jnp.stack(rows, axis=0).reshape(CT_END, 1, COL_TILE)


def kernel(x, norm_mix, w_in, da_q_norm, da_k_norm, da_lambda_q, da_lambda_k, da_sub_norm,
           dl_q_norm, dl_k_norm, w_branch_a, w_branch_b, w_out, norm_ffn,
           w_group_router, w_expert_router, w_gate_up, w_down):
    B, S, D = x.shape
    T = B * S
    depth = w_in.shape[0]
    x2 = x.reshape(T, D)
    for l in range(depth):
        lam_init = 0.8 - 0.6 * math.exp(-0.3 * l)
        w_in_bf = _permute_w_in(w_in[l]).astype(BF16)
        gain_tab = _gain_table(da_q_norm[l], da_k_norm[l], dl_q_norm[l], dl_k_norm[l])
        proj, dl1, dl2 = _inproj(x2, norm_mix[l].reshape(1, D), w_in_bf, gain_tab, B, S)

        o_a = _diff_attention(proj, da_lambda_q[l], da_lambda_k[l], da_sub_norm[l].reshape(1, DA_V_DIM),
                              B, S, lam_init)
        dl = [_dilated_group(proj.reshape(B, S, proj.shape[1]), CT_DL, 0, B, S),
              _dilated_group(dl1, 0, 1, B, S), _dilated_group(dl2, 0, 2, B, S)]

        w_r = jnp.concatenate([w_expert_router[l], w_group_router[l]], axis=1)
        w_r = jnp.pad(w_r, ((0, 0), (0, LANES - w_r.shape[1])))
        r_hi = w_r.astype(BF16)
        r_lo = (w_r - r_hi.astype(F32)).astype(BF16)
        x1, hn, route = _outproj(
            x2, o_a, proj, [t[0] for t in dl], [t[1] for t in dl],
            w_branch_a[l].astype(BF16), w_branch_b[l].astype(BF16), w_out[l].astype(BF16),
            norm_ffn[l].reshape(1, D), r_hi, r_lo)

        src_tok, dst_row, blk_expert, n_used, n_rows = _dispatch_tables(route, T)
        y = _experts(hn, src_tok, dst_row, blk_expert, n_used,
                     w_gate_up[l].astype(BF16), w_down[l].astype(BF16), n_rows)
        x2 = _combine(x1, route, y)
    return x2.reshape(B, S, D)
```

```python
import functools
import math

import jax
import jax.numpy as jnp
import numpy as np
from jax import lax
from jax.experimental import pallas as pl
from jax.experimental.pallas import tpu as pltpu

F32 = jnp.float32
BF16 = jnp.bfloat16

EPS = 1e-6
LOG2E = 1.4426950408889634
NEG_BIG = -1e30

DA_HEADS = 8
DA_QK_DIM = 64
DA_V_DIM = 128
DL_GROUPS = ((128, 1), (512, 4), (2048, 16))
DL_HEADS_PER_GROUP = 4
DL_HEAD_DIM = 128
DL_SPAN = 128
MOE_GROUPS = 4
MOE_EXPERTS_PER_GROUP = 8
MOE_N_EXPERTS = 32
MOE_TOP_K = 2
MOE_BLOCK = 128

LANES = 128
COL_TILE = 512
VMEM_LIMIT = 56 * 1024 * 1024

CT_GATE_A, CT_GATE_B, CT_DA_Q, CT_DA_K, CT_DA_V, CT_DL, CT_MAIN_END, CT_END = 0, 4, 8, 10, 12, 14, 17, 23


def _params(sem, vmem=VMEM_LIMIT):
    return pltpu.CompilerParams(dimension_semantics=sem, vmem_limit_bytes=vmem)


def _dot(a, b):
    return jnp.dot(a, b, preferred_element_type=F32)


def _dot_nt(a, b):
    return lax.dot_general(a, b, (((1,), (1,)), ((), ())), preferred_element_type=F32)


def _inproj_kernel(x_ref, g_ref, w_ref, gain_ref, o_ref, d1_ref, d2_ref, h_scr, y_scr):
    j = pl.program_id(1)

    @pl.when(j == 0)
    def _():
        x = x_ref[...]
        ms = jnp.mean(x * x, axis=-1, keepdims=True)
        h_scr[...] = (x * lax.rsqrt(ms + EPS) * g_ref[...]).astype(BF16)

    y = _dot(h_scr[...], w_ref[...])
    gain = gain_ref[...]
    heads = COL_TILE // LANES

    is64 = (j >= CT_DA_Q) & (j < CT_DA_V)
    is128 = (j >= CT_DL) & (lax.rem(j - CT_DL, 3) < 2)
    main = j < CT_MAIN_END

    def norm128(h):
        sl = slice(h * LANES, (h + 1) * LANES)
        yh = y[:, sl]
        ss = jnp.sum(yh * yh, axis=-1, keepdims=True)
        return yh * lax.rsqrt(ss * (1.0 / DL_HEAD_DIM) + EPS) * gain[:, sl]

    @pl.when(is64)
    def _():
        for h in range(heads):
            sl = slice(h * LANES, (h + 1) * LANES)
            yh = y[:, sl]
            sq = yh * yh
            lo = lax.broadcasted_iota(jnp.int32, yh.shape, 1) < DA_QK_DIM
            s_lo = jnp.sum(jnp.where(lo, sq, 0.0), axis=-1, keepdims=True)
            s_hi = jnp.sum(jnp.where(lo, 0.0, sq), axis=-1, keepdims=True)
            r = jnp.where(lo, lax.rsqrt(s_lo * (1.0 / DA_QK_DIM) + EPS),
                          lax.rsqrt(s_hi * (1.0 / DA_QK_DIM) + EPS))
            o_ref[:, sl] = (yh * r * gain[:, sl]).astype(o_ref.dtype)

    @pl.when(is128 & main)
    def _():
        for h in range(heads):
            o_ref[:, h * LANES:(h + 1) * LANES] = norm128(h).astype(o_ref.dtype)

    @pl.when(is128 & jnp.logical_not(main))
    def _():
        for h in range(heads):
            y_scr[h] = norm128(h)

    plain = jnp.logical_not(is64 | is128)

    @pl.when(plain & main)
    def _():
        o_ref[...] = y.astype(o_ref.dtype)

    @pl.when(plain & jnp.logical_not(main))
    def _():
        for h in range(heads):
            y_scr[h] = y[:, h * LANES:(h + 1) * LANES]

    def deinterleave(dst_ref):
        d, rows = dst_ref.shape[0], dst_ref.shape[1]
        for r in range(d):
            for h in range(heads):
                dst_ref[r, :, h * LANES:(h + 1) * LANES] = (
                    y_scr[h, pl.ds(r, rows, stride=d), :].astype(dst_ref.dtype))

    @pl.when((j >= CT_MAIN_END) & (j < CT_MAIN_END + 3))
    def _():
        deinterleave(d1_ref)

    @pl.when(j >= CT_MAIN_END + 3)
    def _():
        deinterleave(d2_ref)


def _inproj(x2, gain_mix, w_bf, gain_tab, B, S, tm=1024):
    T, D = x2.shape
    tiles_per_batch = S // tm
    d1, d2 = DL_GROUPS[1][1], DL_GROUPS[2][1]
    part1 = lambda j: jnp.clip(j - CT_MAIN_END, 0, 2)
    part2 = lambda j: jnp.clip(j - CT_MAIN_END - 3, 0, 2)
    return pl.pallas_call(
        _inproj_kernel,
        grid=(T // tm, CT_END),
        in_specs=[
            pl.BlockSpec((tm, D), lambda i, j: (i, 0)),
            pl.BlockSpec((1, D), lambda i, j: (0, 0)),
            pl.BlockSpec((D, COL_TILE), lambda i, j: (0, j)),
            pl.BlockSpec((None, 1, COL_TILE), lambda i, j: (j, 0, 0)),
        ],
        out_specs=[
            pl.BlockSpec((tm, COL_TILE), lambda i, j: (i, jnp.minimum(j, CT_MAIN_END - 1))),
            pl.BlockSpec((d1, tm // d1, COL_TILE),
                         lambda i, j: (i // tiles_per_batch, i % tiles_per_batch, part1(j))),
            pl.BlockSpec((d2, tm // d2, COL_TILE),
                         lambda i, j: (i // tiles_per_batch, i % tiles_per_batch, part2(j))),
        ],
        out_shape=[
            jax.ShapeDtypeStruct((T, CT_MAIN_END * COL_TILE), BF16),
            jax.ShapeDtypeStruct((B * d1, S // d1, 3 * COL_TILE), BF16),
            jax.ShapeDtypeStruct((B * d2, S // d2, 3 * COL_TILE), BF16),
        ],
        scratch_shapes=[pltpu.VMEM((tm, D), BF16), pltpu.VMEM((COL_TILE // LANES, tm, LANES), F32)],
        compiler_params=_params(("parallel", "arbitrary")),
    )(x2, gain_mix, w_bf, gain_tab)


def _bf16_pieces(x, n=3):
    out = []
    r = np.float64(x)
    for _ in range(n):
        p = np.asarray(np.float32(r)).astype(jnp.bfloat16).astype(np.float64)
        out.append(float(p))
        r = r - p
    return out


def _alibi_features(tk):
    pieces = _bf16_pieces(LOG2E)
    qf = np.zeros((2, LANES), np.float32)
    kf = np.zeros((2, tk, LANES), np.float32)
    j = np.arange(tk)
    hi, lo = (j // 16) * 16, j % 16
    for m in range(2):
        f0 = DA_QK_DIM if m == 0 else 0
        for n, p in enumerate(pieces):
            qf[m, f0 + 2 * n] = p
            qf[m, f0 + 2 * n + 1] = p
            kf[m, :, f0 + 2 * n] = hi
            kf[m, :, f0 + 2 * n + 1] = lo
    return jnp.asarray(qf), jnp.asarray(kf, dtype=BF16)


def _da_kernel(q_ref, k_ref, v_ref, qf_ref, kf_ref, lq_ref, lk_ref, sg_ref, o_ref,
               s00, s01, s10, s11, p00, p01, p10, p11,
               m0_scr, m1_scr, l0_scr, l1_scr, a0_scr, a1_scr, acc0_scr, acc1_scr, *, tq, rc, lam_init):
    h = pl.program_id(1)
    qi = pl.program_id(2)
    nlb = tq // LANES
    pow2 = jnp.exp2(-(h + 1).astype(F32))
    slope2 = pow2 * LOG2E

    q = q_ref[...]
    lane = lax.broadcasted_iota(jnp.int32, (tq, LANES), 1)
    own = (lane < DA_QK_DIM, lane >= DA_QK_DIM)
    qfs = [jnp.where(own[mi], q, jnp.broadcast_to((qf_ref[mi:mi + 1, :] * pow2).astype(BF16), q.shape))
           for mi in range(2)]

    m_scrs, l_scrs, a_scrs, acc_scrs = (m0_scr, m1_scr), (l0_scr, l1_scr), (a0_scr, a1_scr), (acc0_scr, acc1_scr)
    for mi in range(2):
        m_scrs[mi][...] = jnp.full(m_scrs[mi].shape, NEG_BIG, F32)
        l_scrs[mi][...] = jnp.zeros(l_scrs[mi].shape, F32)
        acc_scrs[mi][...] = jnp.zeros(acc_scrs[mi].shape, F32)

    def scores(ki, mi, s_ref):
        k = k_ref[pl.ds(pl.multiple_of(ki * tq, tq), tq), :]
        s_ref[...] = _dot_nt(qfs[mi], jnp.where(own[mi], k, kf_ref[mi]))

    def softmax(ki, mi, s_ref, p_ref, masked):
        m_scr, l_scr, a_scr = m_scrs[mi], l_scrs[mi], a_scrs[mi]
        c = slope2 * ((ki - qi) * tq).astype(F32)
        for r in range(tq // rc):
            rows = slice(r * rc, (r + 1) * rc)
            nb = min(nlb, ((r + 1) * rc - 1) // LANES + 1) if masked else nlb
            sb = []
            for j in range(nb):
                cs = slice(j * LANES, (j + 1) * LANES)
                s = s_ref[rows, cs]
                if masked and (j + 1) * LANES - 1 > r * rc:
                    rr = lax.broadcasted_iota(jnp.int32, (rc, LANES), 0) + r * rc
                    cc = lax.broadcasted_iota(jnp.int32, (rc, LANES), 1) + j * LANES
                    s = jnp.where(cc <= rr, s, NEG_BIG)
                sb.append(s)
            mx = sb[0]
            for s in sb[1:]:
                mx = jnp.maximum(mx, s)
            m_prev = m_scr[rows, :]
            m_new = jnp.maximum(m_prev, jnp.max(mx, axis=-1, keepdims=True) + c)
            alpha = jnp.exp2(m_prev - m_new)
            a_scr[rows, :] = alpha
            m_scr[rows, :] = m_new
            mc = m_new - c
            psum = alpha * l_scr[rows, :]
            for j in range(nlb):
                cs = slice(j * LANES, (j + 1) * LANES)
                if j < nb:
                    p = jnp.exp2(sb[j] - mc)
                    psum = psum + p
                    p_ref[rows, cs] = p.astype(BF16)
                else:
                    p_ref[rows, cs] = jnp.zeros((rc, LANES), BF16)
            l_scr[rows, :] = psum

    def values(ki, mi, p_ref):
        v = v_ref[pl.ds(pl.multiple_of(ki * tq, tq), tq), :]
        acc_scrs[mi][...] = a_scrs[mi][...] * acc_scrs[mi][...] + _dot(p_ref[...], v)

    def single(ki, masked):
        scores(ki, 0, s00)
        scores(ki, 1, s01)
        softmax(ki, 0, s00, p00, masked)
        values(ki, 0, p00)
        softmax(ki, 1, s01, p01, masked)
        values(ki, 1, p01)

    def pair(k0, masked1):
        k1 = k0 + 1
        scores(k0, 0, s00)
        scores(k0, 1, s01)
        softmax(k0, 0, s00, p00, False)
        values(k0, 0, p00)
        scores(k1, 0, s10)
        softmax(k0, 1, s01, p01, False)
        values(k0, 1, p01)
        scores(k1, 1, s11)
        softmax(k1, 0, s10, p10, masked1)
        values(k1, 0, p10)
        softmax(k1, 1, s11, p11, masked1)
        values(k1, 1, p11)

    def body(t, carry):
        pair(2 * t, False)
        return carry

    lax.fori_loop(0, qi // 2, body, 0)

    @pl.when(qi % 2 == 1)
    def _():
        pair(qi - 1, True)

    @pl.when(qi % 2 == 0)
    def _():
        single(qi, True)

    lam_e = jnp.exp(jnp.sum(lq_ref[...] * lk_ref[...], axis=-1, keepdims=True))
    lam = lam_e[0:1, :] - lam_e[1:2, :] + lam_init
    l0 = jnp.sum(l0_scr[...], axis=-1, keepdims=True)
    l1 = jnp.sum(l1_scr[...], axis=-1, keepdims=True)
    o = acc0_scr[...] / l0 - lam * (acc1_scr[...] / l1)
    ms = jnp.mean(o * o, axis=-1, keepdims=True)
    o = o * lax.rsqrt(ms + EPS) * sg_ref[...] * (1.0 - lam_init)
    o_ref[...] = o.astype(o_ref.dtype)


def _diff_attention(proj, lam_q, lam_k, sub_gain, B, S, lam_init, tq=512, rc=32):
    T = proj.shape[0]
    nq = S // tq
    lb = LANES
    q_blk0, k_blk0, v_blk0 = (CT_DA_Q * COL_TILE) // lb, (CT_DA_K * COL_TILE) // lb, (CT_DA_V * COL_TILE) // lb
    qfeat, kfeat = _alibi_features(tq)
    const = lambda shape: pl.BlockSpec(shape, lambda b, h, i: (0,) * len(shape))
    return pl.pallas_call(
        functools.partial(_da_kernel, tq=tq, rc=rc, lam_init=lam_init),
        grid=(B, DA_HEADS, nq),
        in_specs=[
            pl.BlockSpec((tq, lb), lambda b, h, i: (b * nq + i, q_blk0 + h)),
            pl.BlockSpec((S, lb), lambda b, h, i: (b, k_blk0 + h)),
            pl.BlockSpec((S, lb), lambda b, h, i: (b, v_blk0 + h)),
            const((2, LANES)), const((2, tq, LANES)),
            const((2, DA_QK_DIM)), const((2, DA_QK_DIM)), const((1, DA_V_DIM)),
        ],
        out_specs=pl.BlockSpec((tq, lb), lambda b, h, i: (b * nq + i, h)),
        out_shape=jax.ShapeDtypeStruct((T, DA_HEADS * DA_V_DIM), BF16),
        scratch_shapes=[pltpu.VMEM((tq, tq), F32)] * 4 + [pltpu.VMEM((tq, tq), BF16)] * 4
        + [pltpu.VMEM((tq, LANES), F32)] * 6 + [pltpu.VMEM((tq, DA_V_DIM), F32)] * 2,
        compiler_params=_params(("parallel", "parallel", "arbitrary")),
    )(proj, proj, proj, qfeat, kfeat, lam_q, lam_k, sub_gain)


def _dl_kernel(q_ref, kc_ref, kp_ref, vc_ref, vp_ref, o_ref, lse_ref, *, slopes2, d, tq):
    n = pl.program_id(1)
    sp = DL_SPAN
    row = lax.broadcasted_iota(jnp.int32, (sp, sp), 0)
    col = lax.broadcasted_iota(jnp.int32, (sp, sp), 1)
    dcur = row - col
    cur_ok = dcur >= 0
    prev_ok = dcur <= 0
    dcur_f = dcur.astype(F32)

    def residue(r, carry):
        for hh in range(DL_HEADS_PER_GROUP):
            hs = slice(hh * LANES, (hh + 1) * LANES)
            bias_c = -slopes2[hh] * dcur_f
            bias_p = -slopes2[hh] * (dcur_f + float(sp))
            for j in range(tq // sp):
                rs = slice(j * sp, (j + 1) * sp)
                q = q_ref[r, rs, hs]
                kc = kc_ref[r, rs, hs]
                vc = vc_ref[r, rs, hs]
                if j == 0:
                    kp = kp_ref[r, :, hs]
                    vp = vp_ref[r, :, hs]
                    p_ok = prev_ok & (n > 0)
                else:
                    ps = slice((j - 1) * sp, j * sp)
                    kp = kc_ref[r, ps, hs]
                    vp = vc_ref[r, ps, hs]
                    p_ok = prev_ok
                s_c = jnp.where(cur_ok, _dot_nt(q, kc) + bias_c, NEG_BIG)
                s_p = jnp.where(p_ok, _dot_nt(q, kp) + bias_p, NEG_BIG)
                m = jnp.maximum(jnp.max(s_c, axis=-1, keepdims=True), jnp.max(s_p, axis=-1, keepdims=True))
                p_c = jnp.exp2(s_c - m)
                p_p = jnp.exp2(s_p - m)
                den = jnp.sum(p_c, axis=-1, keepdims=True) + jnp.sum(p_p, axis=-1, keepdims=True)
                acc = _dot(p_c.astype(BF16), vc) + _dot(p_p.astype(BF16), vp)
                if d == 1:
                    out_rows = pl.ds(j * sp, sp)
                else:
                    out_rows = pl.ds(r + j * sp * d, sp, stride=d)
                o_ref[hh, out_rows, :] = acc / den
                lse_ref[hh, out_rows, :] = jnp.broadcast_to(m + jnp.log2(den), (sp, LANES))
        return carry

    lax.fori_loop(0, d, residue, 0)


def _dilated_group(src, col0, g, B, S, tok_per_step=2048):
    window, d = DL_GROUPS[g]
    assert window // d == DL_SPAN
    L = S // d
    tq = min(tok_per_step, S) // d
    assert tq % DL_SPAN == 0 and L % tq == 0
    nh = DL_HEADS_PER_GROUP * len(DL_GROUPS)
    slopes2 = tuple(2.0 ** (-8.0 * (g * DL_HEADS_PER_GROUP + hh + 1) / nh) * d * LOG2E
                    for hh in range(DL_HEADS_PER_GROUP))
    spb = tq // DL_SPAN
    nsteps = L // tq
    cur = lambda c: pl.BlockSpec((d, tq, COL_TILE), lambda b, n: (b, n, c))
    prev = lambda c: pl.BlockSpec((d, DL_SPAN, COL_TILE), lambda b, n: (b, jnp.maximum(n * spb - 1, 0), c))
    out_spec = pl.BlockSpec((DL_HEADS_PER_GROUP, d * tq, LANES), lambda b, n: (0, b * nsteps + n, 0))
    return pl.pallas_call(
        functools.partial(_dl_kernel, slopes2=slopes2, d=d, tq=tq),
        grid=(B, nsteps),
        in_specs=[cur(col0), cur(col0 + 1), prev(col0 + 1), cur(col0 + 2), prev(col0 + 2)],
        out_specs=[out_spec, out_spec],
        out_shape=[jax.ShapeDtypeStruct((DL_HEADS_PER_GROUP, B * S, LANES), F32)] * 2,
        compiler_params=_params(("parallel", "arbitrary")),
    )(src, src, src, src, src)


def _route(logits):
    lane = lax.broadcasted_iota(jnp.int32, logits.shape, 1)
    big = jnp.int32(1 << 20)
    is_g = (lane >= MOE_N_EXPERTS) & (lane < MOE_N_EXPERTS + MOE_GROUPS)
    lg = jnp.where(is_g, logits, -jnp.inf)
    gmax = jnp.max(lg, axis=-1, keepdims=True)
    gsum = jnp.sum(jnp.exp(lg - gmax), axis=-1, keepdims=True)
    g_w = 1.0 / gsum
    g_idx = jnp.min(jnp.where(lg == gmax, lane - MOE_N_EXPERTS, big), axis=-1, keepdims=True)
    in_grp = (lane < MOE_N_EXPERTS) & ((lane // MOE_EXPERTS_PER_GROUP) == g_idx)
    le = jnp.where(in_grp, logits, -jnp.inf)
    t1 = jnp.max(le, axis=-1, keepdims=True)
    e1 = jnp.min(jnp.where(le == t1, lane, big), axis=-1, keepdims=True)
    le2 = jnp.where(lane == e1, -jnp.inf, le)
    t2 = jnp.max(le2, axis=-1, keepdims=True)
    e2 = jnp.min(jnp.where(le2 == t2, lane, big), axis=-1, keepdims=True)
    r = jnp.exp(t2 - t1)
    w1 = g_w / (1.0 + r)
    w2 = w1 * r
    out = jnp.where(lane == 0, e1.astype(F32),
                    jnp.where(lane == 1, e2.astype(F32),
                              jnp.where(lane == 2, w1, jnp.where(lane == 3, w2, 0.0))))
    return out


def _outproj_kernel(x_ref, oa_ref, ga_ref, gb_ref, o0_ref, o1_ref, o2_ref, l0_ref, l1_ref, l2_ref,
                    wa_ref, wb_ref, wo_ref, gf_ref, rh_ref, rl_ref,
                    x1_ref, hn_ref, rt_ref):
    obs = []
    for hh in range(DL_HEADS_PER_GROUP):
        l0, l1, l2 = l0_ref[hh], l1_ref[hh], l2_ref[hh]
        lm = jnp.maximum(jnp.maximum(l0, l1), l2)
        e0, e1, e2 = jnp.exp2(l0 - lm), jnp.exp2(l1 - lm), jnp.exp2(l2 - lm)
        obs.append((e0 * o0_ref[hh] + e1 * o1_ref[hh] + e2 * o2_ref[hh]) / (e0 + e1 + e2))
    ob = jnp.concatenate(obs, axis=1)
    a = _dot(oa_ref[...], wa_ref[...])
    b = _dot(ob.astype(BF16), wb_ref[...])
    mixed = jax.nn.sigmoid(ga_ref[...].astype(F32)) * a + jax.nn.sigmoid(gb_ref[...].astype(F32)) * b
    x1 = x_ref[...] + _dot(mixed.astype(BF16), wo_ref[...])
    x1_ref[...] = x1
    ms = jnp.mean(x1 * x1, axis=-1, keepdims=True)
    hn = x1 * lax.rsqrt(ms + EPS) * gf_ref[...]
    hn_ref[...] = hn
    hn_hi = hn.astype(BF16)
    hn_lo = (hn - hn_hi.astype(F32)).astype(BF16)
    logits = _dot(hn_hi, rh_ref[...]) + (_dot(hn_lo, rh_ref[...]) + _dot(hn_hi, rl_ref[...]))
    rt_ref[...] = _route(logits)


def _outproj(x2, o_a, proj, dl_o, dl_lse, wa, wb, wo, gain_ffn, r_hi, r_lo, tm=256):
    T, D = x2.shape
    row = lambda w: pl.BlockSpec((tm, w), lambda i: (i, 0))
    full = lambda s: pl.BlockSpec(s, lambda i: (0, 0), pipeline_mode=pl.Buffered(1))
    hrow = pl.BlockSpec((DL_HEADS_PER_GROUP, tm, LANES), lambda i: (0, i, 0))
    return pl.pallas_call(
        _outproj_kernel,
        grid=(T // tm,),
        in_specs=[
            row(D), row(o_a.shape[1]),
            pl.BlockSpec((tm, D), lambda i: (i, (CT_GATE_A * COL_TILE) // D)),
            pl.BlockSpec((tm, D), lambda i: (i, (CT_GATE_B * COL_TILE) // D)),
            hrow, hrow, hrow, hrow, hrow, hrow,
            full(wa.shape), full(wb.shape), full(wo.shape), full((1, D)), full(r_hi.shape), full(r_lo.shape),
        ],
        out_specs=[row(D), row(D), row(LANES)],
        out_shape=[jax.ShapeDtypeStruct((T, D), F32), jax.ShapeDtypeStruct((T, D), F32),
                   jax.ShapeDtypeStruct((T, LANES), F32)],
        compiler_params=_params(("parallel",)),
    )(x2, o_a, proj, proj, dl_o[0], dl_o[1], dl_o[2], dl_lse[0], dl_lse[1], dl_lse[2],
      wa, wb, wo, gain_ffn, r_hi, r_lo)


def _expert_kernel(be_ref, nu_ref, tc_ref, tn_ref, dc_ref, hn_ref, wgu_ref, wd_ref, y_ref,
                   xbuf, ybuf, gsem, ssem, *, d_ff):
    i = pl.program_id(0)
    last = pl.num_programs(0) - 1
    nu = nu_ref[0]
    slot = i % 2
    rows = xbuf.shape[1]

    def gather(tok_ref, r, s):
        return pltpu.make_async_copy(hn_ref.at[tok_ref[0, r]], xbuf.at[s, r], gsem.at[s])

    def scatter(r, s):
        return pltpu.make_async_copy(ybuf.at[s, r], y_ref.at[dc_ref[0, r]], ssem.at[s])

    def wait_gather(s):
        pltpu.make_async_copy(hn_ref.at[pl.ds(0, rows)], xbuf.at[s], gsem.at[s]).wait()

    def wait_scatter(s):
        pltpu.make_async_copy(ybuf.at[s], y_ref.at[pl.ds(0, rows)], ssem.at[s]).wait()

    @pl.when(i == 0)
    def _():
        n_res = y_ref.shape[0] - 2 * rows
        ybuf[...] = jnp.zeros(ybuf.shape, ybuf.dtype)
        for s in range(2):
            pltpu.make_async_copy(ybuf.at[s], y_ref.at[pl.ds(n_res + s * rows, rows)], ssem.at[s]).start()
        for s in range(2):
            pltpu.make_async_copy(ybuf.at[s], y_ref.at[pl.ds(n_res + s * rows, rows)], ssem.at[s]).wait()

    @pl.when((i == 0) & (nu > 0))
    def _():
        def body(r, c):
            gather(tc_ref, r, 0).start()
            return c
        lax.fori_loop(0, rows, body, 0)

    @pl.when((i >= 2) & (i < nu))
    def _():
        wait_scatter(slot)

    @pl.when(i < nu)
    def _():
        wait_gather(slot)
        for r in range(rows):
            gather(tn_ref, r, 1 - slot).start()
        h = _dot(xbuf[slot].astype(BF16), wgu_ref[...])
        gate = h[:, :d_ff]
        up = h[:, d_ff:]
        act = gate * jax.nn.sigmoid(gate) * up
        ybuf[slot] = _dot(act.astype(BF16), wd_ref[...])
        for r in range(rows):
            scatter(r, slot).start()

    @pl.when((i == nu) & (nu > 0))
    def _():
        wait_gather(slot)

    @pl.when(i == last)
    def _():
        @pl.when(nu >= 1)
        def _():
            wait_scatter((nu - 1) % 2)

        @pl.when(nu >= 2)
        def _():
            wait_scatter(nu % 2)


def _experts(hn, src_tok, dst_row, blk_expert, n_used, wgu, wd, n_out_rows):
    T, D = hn.shape
    nblk = src_tok.shape[0] // MOE_BLOCK
    nsteps = nblk - 1
    d_ff = wd.shape[1]
    tok = src_tok.reshape(nblk, 1, MOE_BLOCK)
    dst = dst_row.reshape(nblk, 1, MOE_BLOCK)

    def live(i, nu):
        return jnp.maximum(jnp.minimum(i, nu[0] - 1), 0)

    smem = lambda f: pl.BlockSpec((None, 1, MOE_BLOCK), f, memory_space=pltpu.SMEM)
    grid_spec = pltpu.PrefetchScalarGridSpec(
        num_scalar_prefetch=2,
        grid=(nsteps,),
        in_specs=[
            smem(lambda i, be, nu: (i, 0, 0)),
            smem(lambda i, be, nu: (i + 1, 0, 0)),
            smem(lambda i, be, nu: (i, 0, 0)),
            pl.BlockSpec(memory_space=pl.ANY),
            pl.BlockSpec((None, D, 2 * d_ff), lambda i, be, nu: (be[live(i, nu)], 0, 0)),
            pl.BlockSpec((None, d_ff, D), lambda i, be, nu: (be[live(i, nu)], 0, 0)),
        ],
        out_specs=pl.BlockSpec(memory_space=pl.ANY),
        scratch_shapes=[pltpu.VMEM((2, MOE_BLOCK, D), F32), pltpu.VMEM((2, MOE_BLOCK, D), F32),
                        pltpu.SemaphoreType.DMA((2,)), pltpu.SemaphoreType.DMA((2,))],
    )
    return pl.pallas_call(
        functools.partial(_expert_kernel, d_ff=d_ff),
        grid_spec=grid_spec,
        out_shape=jax.ShapeDtypeStruct((n_out_rows, D), F32),
        compiler_params=_params(("arbitrary",)),
    )(blk_expert, n_used, tok, tok, dst, hn, wgu, wd)


def _combine_kernel(x1_ref, rt_ref, y1_ref, y2_ref, o_ref):
    rt = rt_ref[...]
    o_ref[...] = x1_ref[...] + (rt[:, 2:3] * y1_ref[...] + rt[:, 3:4] * y2_ref[...])


def _combine(x1, route, y, tm=512):
    T, D = x1.shape
    nt = T // tm
    return pl.pallas_call(
        _combine_kernel,
        grid=(nt,),
        in_specs=[
            pl.BlockSpec((tm, D), lambda i: (i, 0)),
            pl.BlockSpec((tm, LANES), lambda i: (i, 0)),
            pl.BlockSpec((tm, D), lambda i: (i, 0)),
            pl.BlockSpec((tm, D), lambda i: (nt + i, 0)),
        ],
        out_specs=pl.BlockSpec((tm, D), lambda i: (i, 0)),
        out_shape=jax.ShapeDtypeStruct((T, D), F32),
        compiler_params=_params(("parallel",)),
    )(x1, route, y, y)


def _dispatch_tables(route, T):
    TK = T * MOE_TOP_K
    flat_e = route[:, :MOE_TOP_K].astype(jnp.int32).reshape(-1)
    order = jnp.argsort(flat_e).astype(jnp.int32)
    sizes = jnp.sum(flat_e[:, None] == jnp.arange(MOE_N_EXPERTS, dtype=jnp.int32)[None, :], axis=0,
                    dtype=jnp.int32)
    start = jnp.cumsum(sizes) - sizes
    padded = ((sizes + MOE_BLOCK - 1) // MOE_BLOCK) * MOE_BLOCK
    pad_end = jnp.cumsum(padded)
    pad_start = pad_end - padded
    n_blocks = TK // MOE_BLOCK + MOE_N_EXPERTS
    blk = jnp.arange(n_blocks + 2, dtype=jnp.int32)
    blk_expert = jnp.minimum(jnp.sum(pad_end[None, :] <= (blk * MOE_BLOCK)[:, None], axis=1, dtype=jnp.int32),
                             MOE_N_EXPERTS - 1)
    r = jnp.arange(MOE_BLOCK, dtype=jnp.int32)[None, :]
    pos = blk[:, None] * MOE_BLOCK + r
    off = pos - pad_start[blk_expert][:, None]
    live = (off < sizes[blk_expert][:, None]) & (pos < pad_end[-1])
    sorted_idx = jnp.where(live, off + start[blk_expert][:, None], 0)
    slot = order[sorted_idx]
    tok = slot // MOE_TOP_K
    src_tok = jnp.where(live, tok, 0).reshape(-1)
    trash = TK + (blk[:, None] % 2) * MOE_BLOCK + r
    dst_row = jnp.where(live, (slot % MOE_TOP_K) * T + tok, trash).reshape(-1)
    n_used = (pad_end[-1:] // MOE_BLOCK).astype(jnp.int32)
    return src_tok, dst_row, blk_expert[:n_blocks + 1], n_used, TK + 2 * MOE_BLOCK


def _permute_w_in(w):
    da = DA_HEADS * 2 * DA_QK_DIM
    dav = DA_HEADS * DA_V_DIM
    dl = len(DL_GROUPS) * DL_HEADS_PER_GROUP * DL_HEAD_DIM
    o = [int(v) for v in np.cumsum([0, da, da, dav, dl, dl, dl])]
    parts = [w[:, o[6]:], w[:, :o[3]]]
    for g in range(len(DL_GROUPS)):
        for p in range(3):
            parts.append(w[:, o[3 + p] + g * COL_TILE: o[3 + p] + (g + 1) * COL_TILE])
    return jnp.concatenate(parts, axis=1)


def _gain_table(da_q_norm, da_k_norm, dl_q_norm, dl_k_norm):
    ones = jnp.ones((COL_TILE,), F32)
    daq = jnp.tile(da_q_norm, COL_TILE // DA_QK_DIM) * (DA_QK_DIM ** -0.5 * LOG2E)
    dak = jnp.tile(da_k_norm, COL_TILE // DA_QK_DIM)
    dlq = jnp.tile(dl_q_norm, COL_TILE // DL_HEAD_DIM) * (DL_HEAD_DIM ** -0.5 * LOG2E)
    dlk = jnp.tile(dl_k_norm, COL_TILE // DL_HEAD_DIM)
    rows = []
    for j in range(CT_END):
        if CT_DA_Q <= j < CT_DA_K:
            rows.append(daq)
        elif CT_DA_K <= j < CT_DA_V:
            rows.append(dak)
        elif j >= CT_DL and (j - CT_DL) % 3 == 0:
            rows.append(dlq)
        elif j >= CT_DL and (j - CT_DL) % 3 == 1:
            rows.append(dlk)
        else:
            rows.append(ones)
    return jnp.stack(rows, axis=0).reshape(CT_END, 1, COL_TILE)


def kernel(x, norm_mix, w_in, da_q_norm, da_k_norm, da_lambda_q, da_lambda_k, da_sub_norm,
           dl_q_norm, dl_k_norm, w_branch_a, w_branch_b, w_out, norm_ffn,
           w_group_router, w_expert_router, w_gate_up, w_down):
    B, S, D = x.shape
    T = B * S
    depth = w_in.shape[0]
    x2 = x.reshape(T, D)
    for l in range(depth):
        lam_init = 0.8 - 0.6 * math.exp(-0.3 * l)
        w_in_bf = _permute_w_in(w_in[l]).astype(BF16)
        gain_tab = _gain_table(da_q_norm[l], da_k_norm[l], dl_q_norm[l], dl_k_norm[l])
        proj, dl1, dl2 = _inproj(x2, norm_mix[l].reshape(1, D), w_in_bf, gain_tab, B, S)

        o_a = _diff_attention(proj, da_lambda_q[l], da_lambda_k[l], da_sub_norm[l].reshape(1, DA_V_DIM),
                              B, S, lam_init)
        dl = [_dilated_group(proj.reshape(B, S, proj.shape[1]), CT_DL, 0, B, S),
              _dilated_group(dl1, 0, 1, B, S), _dilated_group(dl2, 0, 2, B, S)]

        w_r = jnp.concatenate([w_expert_router[l], w_group_router[l]], axis=1)
        w_r = jnp.pad(w_r, ((0, 0), (0, LANES - w_r.shape[1])))
        r_hi = w_r.astype(BF16)
        r_lo = (w_r - r_hi.astype(F32)).astype(BF16)
        x1, hn, route = _outproj(
            x2, o_a, proj, [t[0] for t in dl], [t[1] for t in dl],
            w_branch_a[l].astype(BF16), w_branch_b[l].astype(BF16), w_out[l].astype(BF16),
            norm_ffn[l].reshape(1, D), r_hi, r_lo)

        src_tok, dst_row, blk_expert, n_used, n_rows = _dispatch_tables(route, T)
        y = _experts(hn, src_tok, dst_row, blk_expert, n_used,
                     w_gate_up[l].astype(BF16), w_down[l].astype(BF16), n_rows)
        x2 = _combine(x1, route, y)
    return x2.reshape(B, S, D)
```

```python
import functools
import math

import jax
import jax.numpy as jnp
import numpy as np
from jax import lax
from jax.experimental import pallas as pl
from jax.experimental.pallas import tpu as pltpu

F32 = jnp.float32
BF16 = jnp.bfloat16

EPS = 1e-6
LOG2E = 1.4426950408889634
NEG_BIG = -1e30

DA_HEADS = 8
DA_QK_DIM = 64
DA_V_DIM = 128
DA_TILE_GROUP = 4
DL_GROUPS = ((128, 1), (512, 4), (2048, 16))
DL_HEADS_PER_GROUP = 4
DL_HEAD_DIM = 128
DL_SPAN = 128
MOE_GROUPS = 4
MOE_EXPERTS_PER_GROUP = 8
MOE_N_EXPERTS = 32
MOE_TOP_K = 2
MOE_BLOCK = 128

LANES = 128
COL_TILE = 512
VMEM_LIMIT = 56 * 1024 * 1024

CT_GATE_A, CT_GATE_B, CT_DA_Q, CT_DA_K, CT_DA_V, CT_DL, CT_MAIN_END, CT_END = 0, 4, 8, 10, 12, 14, 17, 23


def _params(sem, vmem=VMEM_LIMIT):
    return pltpu.CompilerParams(dimension_semantics=sem, vmem_limit_bytes=vmem)


def _dot(a, b):
    return jnp.dot(a, b, preferred_element_type=F32)


def _dot_nt(a, b):
    return lax.dot_general(a, b, (((1,), (1,)), ((), ())), preferred_element_type=F32)


def _inproj_kernel(x_ref, g_ref, w_ref, gain_ref, o_ref, d1_ref, d2_ref, h_scr, y_scr):
    j = pl.program_id(1)

    @pl.when(j == 0)
    def _():
        x = x_ref[...]
        ms = jnp.mean(x * x, axis=-1, keepdims=True)
        h_scr[...] = (x * lax.rsqrt(ms + EPS) * g_ref[...]).astype(BF16)

    y = _dot(h_scr[...], w_ref[...])
    gain = gain_ref[...]
    heads = COL_TILE // LANES

    is64 = (j >= CT_DA_Q) & (j < CT_DA_V)
    is128 = (j >= CT_DL) & (lax.rem(j - CT_DL, 3) < 2)
    main = j < CT_MAIN_END

    def norm128(h):
        sl = slice(h * LANES, (h + 1) * LANES)
        yh = y[:, sl]
        ss = jnp.sum(yh * yh, axis=-1, keepdims=True)
        return yh * lax.rsqrt(ss * (1.0 / DL_HEAD_DIM) + EPS) * gain[:, sl]

    @pl.when(is64)
    def _():
        for h in range(heads):
            sl = slice(h * LANES, (h + 1) * LANES)
            yh = y[:, sl]
            sq = yh * yh
            lo = lax.broadcasted_iota(jnp.int32, yh.shape, 1) < DA_QK_DIM
            s_lo = jnp.sum(jnp.where(lo, sq, 0.0), axis=-1, keepdims=True)
            s_hi = jnp.sum(jnp.where(lo, 0.0, sq), axis=-1, keepdims=True)
            r = jnp.where(lo, lax.rsqrt(s_lo * (1.0 / DA_QK_DIM) + EPS),
                          lax.rsqrt(s_hi * (1.0 / DA_QK_DIM) + EPS))
            o_ref[:, sl] = (yh * r * gain[:, sl]).astype(o_ref.dtype)

    @pl.when(is128 & main)
    def _():
        for h in range(heads):
            o_ref[:, h * LANES:(h + 1) * LANES] = norm128(h).astype(o_ref.dtype)

    @pl.when(is128 & jnp.logical_not(main))
    def _():
        for h in range(heads):
            y_scr[h] = norm128(h)

    plain = jnp.logical_not(is64 | is128)

    @pl.when(plain & main)
    def _():
        o_ref[...] = y.astype(o_ref.dtype)

    @pl.when(plain & jnp.logical_not(main))
    def _():
        for h in range(heads):
            y_scr[h] = y[:, h * LANES:(h + 1) * LANES]

    def deinterleave(dst_ref):
        d, rows = dst_ref.shape[0], dst_ref.shape[1]
        for r in range(d):
            for h in range(heads):
                dst_ref[r, :, h * LANES:(h + 1) * LANES] = (
                    y_scr[h, pl.ds(r, rows, stride=d), :].astype(dst_ref.dtype))

    @pl.when((j >= CT_MAIN_END) & (j < CT_MAIN_END + 3))
    def _():
        deinterleave(d1_ref)

    @pl.when(j >= CT_MAIN_END + 3)
    def _():
        deinterleave(d2_ref)


def _inproj(x2, gain_mix, w_bf, gain_tab, B, S, tm=1024):
    T, D = x2.shape
    tiles_per_batch = S // tm
    d1, d2 = DL_GROUPS[1][1], DL_GROUPS[2][1]
    part1 = lambda j: jnp.clip(j - CT_MAIN_END, 0, 2)
    part2 = lambda j: jnp.clip(j - CT_MAIN_END - 3, 0, 2)
    return pl.pallas_call(
        _inproj_kernel,
        grid=(T // tm, CT_END),
        in_specs=[
            pl.BlockSpec((tm, D), lambda i, j: (i, 0)),
            pl.BlockSpec((1, D), lambda i, j: (0, 0)),
            pl.BlockSpec((D, COL_TILE), lambda i, j: (0, j)),
            pl.BlockSpec((None, 1, COL_TILE), lambda i, j: (j, 0, 0)),
        ],
        out_specs=[
            pl.BlockSpec((tm, COL_TILE), lambda i, j: (i, jnp.minimum(j, CT_MAIN_END - 1))),
            pl.BlockSpec((d1, tm // d1, COL_TILE),
                         lambda i, j: (i // tiles_per_batch, i % tiles_per_batch, part1(j))),
            pl.BlockSpec((d2, tm // d2, COL_TILE),
                         lambda i, j: (i // tiles_per_batch, i % tiles_per_batch, part2(j))),
        ],
        out_shape=[
            jax.ShapeDtypeStruct((T, CT_MAIN_END * COL_TILE), BF16),
            jax.ShapeDtypeStruct((B * d1, S // d1, 3 * COL_TILE), BF16),
            jax.ShapeDtypeStruct((B * d2, S // d2, 3 * COL_TILE), BF16),
        ],
        scratch_shapes=[pltpu.VMEM((tm, D), BF16), pltpu.VMEM((COL_TILE // LANES, tm, LANES), F32)],
        compiler_params=_params(("parallel", "arbitrary")),
    )(x2, gain_mix, w_bf, gain_tab)


def _bf16_pieces(x, n=3):
    out = []
    r = np.float64(x)
    for _ in range(n):
        p = np.asarray(np.float32(r)).astype(jnp.bfloat16).astype(np.float64)
        out.append(float(p))
        r = r - p
    return out


def _alibi_features(tk):
    pieces = _bf16_pieces(LOG2E)
    qf = np.zeros((2, LANES), np.float32)
    kf = np.zeros((2, tk, LANES), np.float32)
    j = np.arange(tk)
    hi, lo = (j // 16) * 16, j % 16
    for m in range(2):
        f0 = DA_QK_DIM if m == 0 else 0
        for n, p in enumerate(pieces):
            qf[m, f0 + 2 * n] = p
            qf[m, f0 + 2 * n + 1] = p
            kf[m, :, f0 + 2 * n] = hi
            kf[m, :, f0 + 2 * n + 1] = lo
    return jnp.asarray(qf), jnp.asarray(kf, dtype=BF16)


def _da_kernel(q_ref, k_ref, v_ref, qf_ref, kf_ref, lq_ref, lk_ref, sg_ref, o_ref,
               s00, s01, s10, s11, p00, p01, p10, p11,
               m0_scr, m1_scr, l0_scr, l1_scr, a0_scr, a1_scr, acc0_scr, acc1_scr, *, tq, rc, lam_init):
    h = pl.program_id(1)
    qi = pl.program_id(2)
    nlb = tq // LANES
    pow2 = jnp.exp2(-(h + 1).astype(F32))
    slope2 = pow2 * LOG2E

    q = q_ref[...]
    lane = lax.broadcasted_iota(jnp.int32, (tq, LANES), 1)
    own = (lane < DA_QK_DIM, lane >= DA_QK_DIM)
    qfs = [jnp.where(own[mi], q, jnp.broadcast_to((qf_ref[mi:mi + 1, :] * pow2).astype(BF16), q.shape))
           for mi in range(2)]

    m_scrs, l_scrs, a_scrs, acc_scrs = (m0_scr, m1_scr), (l0_scr, l1_scr), (a0_scr, a1_scr), (acc0_scr, acc1_scr)
    for mi in range(2):
        m_scrs[mi][...] = jnp.full(m_scrs[mi].shape, NEG_BIG, F32)
        l_scrs[mi][...] = jnp.zeros(l_scrs[mi].shape, F32)
        acc_scrs[mi][...] = jnp.zeros(acc_scrs[mi].shape, F32)

    def scores(ki, mi, s_ref):
        k = k_ref[pl.ds(pl.multiple_of(ki * tq, tq), tq), :]
        s_ref[...] = _dot_nt(qfs[mi], jnp.where(own[mi], k, kf_ref[mi]))

    def softmax(ki, mi, s_ref, p_ref, masked):
        m_scr, l_scr, a_scr = m_scrs[mi], l_scrs[mi], a_scrs[mi]
        c = slope2 * ((ki - qi) * tq).astype(F32)
        for r in range(tq // rc):
            rows = slice(r * rc, (r + 1) * rc)
            nb = min(nlb, ((r + 1) * rc - 1) // LANES + 1) if masked else nlb
            sb = []
            for j in range(nb):
                cs = slice(j * LANES, (j + 1) * LANES)
                s = s_ref[rows, cs]
                if masked and (j + 1) * LANES - 1 > r * rc:
                    rr = lax.broadcasted_iota(jnp.int32, (rc, LANES), 0) + r * rc
                    cc = lax.broadcasted_iota(jnp.int32, (rc, LANES), 1) + j * LANES
                    s = jnp.where(cc <= rr, s, NEG_BIG)
                sb.append(s)
            mx = sb[0]
            for s in sb[1:]:
                mx = jnp.maximum(mx, s)
            m_prev = m_scr[rows, :]
            m_new = jnp.maximum(m_prev, jnp.max(mx, axis=-1, keepdims=True) + c)
            alpha = jnp.exp2(m_prev - m_new)
            a_scr[rows, :] = alpha
            m_scr[rows, :] = m_new
            mc = m_new - c
            psum = alpha * l_scr[rows, :]
            for j in range(nlb):
                cs = slice(j * LANES, (j + 1) * LANES)
                if j < nb:
                    p = jnp.exp2(sb[j] - mc)
                    psum = psum + p
                    p_ref[rows, cs] = p.astype(BF16)
                else:
                    p_ref[rows, cs] = jnp.zeros((rc, LANES), BF16)
            l_scr[rows, :] = psum

    def values(ki, mi, p_ref):
        v = v_ref[pl.ds(pl.multiple_of(ki * tq, tq), tq), :]
        acc_scrs[mi][...] = a_scrs[mi][...] * acc_scrs[mi][...] + _dot(p_ref[...], v)

    s_bufs, p_bufs = ((s00, s01), (s10, s11)), ((p00, p01), (p10, p11))

    def tile_group(k0, n, last_masked):
        for mi in range(2):
            scores(k0, mi, s_bufs[0][mi])
        for i in range(n):
            masked = last_masked and i == n - 1
            for mi in range(2):
                softmax(k0 + i, mi, s_bufs[i % 2][mi], p_bufs[i % 2][mi], masked)
                values(k0 + i, mi, p_bufs[i % 2][mi])
                if i + 1 < n:
                    scores(k0 + i + 1, mi, s_bufs[(i + 1) % 2][mi])

    def body(t, carry):
        tile_group(DA_TILE_GROUP * t, DA_TILE_GROUP, False)
        return carry

    n_full = qi // DA_TILE_GROUP
    lax.fori_loop(0, n_full, body, 0)
    for rem in range(1, DA_TILE_GROUP + 1):
        @pl.when(qi - DA_TILE_GROUP * n_full == rem - 1)
        def _(rem=rem):
            tile_group(qi - (rem - 1), rem, True)

    lam_e = jnp.exp(jnp.sum(lq_ref[...] * lk_ref[...], axis=-1, keepdims=True))
    lam = lam_e[0:1, :] - lam_e[1:2, :] + lam_init
    l0 = jnp.sum(l0_scr[...], axis=-1, keepdims=True)
    l1 = jnp.sum(l1_scr[...], axis=-1, keepdims=True)
    o = acc0_scr[...] / l0 - lam * (acc1_scr[...] / l1)
    ms = jnp.mean(o * o, axis=-1, keepdims=True)
    o = o * lax.rsqrt(ms + EPS) * sg_ref[...] * (1.0 - lam_init)
    o_ref[...] = o.astype(o_ref.dtype)


def _diff_attention(proj, lam_q, lam_k, sub_gain, B, S, lam_init, tq=512, rc=32):
    T = proj.shape[0]
    nq = S // tq
    lb = LANES
    q_blk0, k_blk0, v_blk0 = (CT_DA_Q * COL_TILE) // lb, (CT_DA_K * COL_TILE) // lb, (CT_DA_V * COL_TILE) // lb
    qfeat, kfeat = _alibi_features(tq)
    const = lambda shape: pl.BlockSpec(shape, lambda b, h, i: (0,) * len(shape))
    return pl.pallas_call(
        functools.partial(_da_kernel, tq=tq, rc=rc, lam_init=lam_init),
        grid=(B, DA_HEADS, nq),
        in_specs=[
            pl.BlockSpec((tq, lb), lambda b, h, i: (b * nq + i, q_blk0 + h)),
            pl.BlockSpec((S, lb), lambda b, h, i: (b, k_blk0 + h)),
            pl.BlockSpec((S, lb), lambda b, h, i: (b, v_blk0 + h)),
            const((2, LANES)), const((2, tq, LANES)),
            const((2, DA_QK_DIM)), const((2, DA_QK_DIM)), const((1, DA_V_DIM)),
        ],
        out_specs=pl.BlockSpec((tq, lb), lambda b, h, i: (b * nq + i, h)),
        out_shape=jax.ShapeDtypeStruct((T, DA_HEADS * DA_V_DIM), BF16),
        scratch_shapes=[pltpu.VMEM((tq, tq), F32)] * 4 + [pltpu.VMEM((tq, tq), BF16)] * 4
        + [pltpu.VMEM((tq, LANES), F32)] * 6 + [pltpu.VMEM((tq, DA_V_DIM), F32)] * 2,
        compiler_params=_params(("parallel", "parallel", "arbitrary")),
    )(proj, proj, proj, qfeat, kfeat, lam_q, lam_k, sub_gain)


def _dl_kernel(q_ref, kc_ref, kp_ref, vc_ref, vp_ref, o_ref, lse_ref, *, slopes2, d, tq, ru):
    n = pl.program_id(1)
    sp = DL_SPAN
    row = lax.broadcasted_iota(jnp.int32, (sp, sp), 0)
    col = lax.broadcasted_iota(jnp.int32, (sp, sp), 1)
    dcur = row - col
    cur_ok = dcur >= 0
    prev_ok = dcur <= 0
    dcur_f = dcur.astype(F32)

    def scores(r, hh, j):
        hs = slice(hh * LANES, (hh + 1) * LANES)
        rs = slice(j * sp, (j + 1) * sp)
        q = q_ref[r, rs, hs]
        if j == 0:
            kp, vp, p_ok = kp_ref[r, :, hs], vp_ref[r, :, hs], prev_ok & (n > 0)
        else:
            ps = slice((j - 1) * sp, j * sp)
            kp, vp, p_ok = kc_ref[r, ps, hs], vc_ref[r, ps, hs], prev_ok
        s_c = jnp.where(cur_ok, _dot_nt(q, kc_ref[r, rs, hs]) - slopes2[hh] * dcur_f, NEG_BIG)
        s_p = jnp.where(p_ok, _dot_nt(q, kp) - slopes2[hh] * (dcur_f + float(sp)), NEG_BIG)
        return s_c, s_p, vc_ref[r, rs, hs], vp

    def finish(r, hh, j, s_c, s_p, vc, vp):
        m = jnp.max(jnp.maximum(s_c, s_p), axis=-1, keepdims=True)
        p_c = jnp.exp2(s_c - m)
        p_p = jnp.exp2(s_p - m)
        den = jnp.sum(p_c + p_p, axis=-1, keepdims=True)
        acc = _dot(p_c.astype(BF16), vc) + _dot(p_p.astype(BF16), vp)
        out_rows = pl.ds(j * sp, sp) if d == 1 else pl.ds(r + j * sp * d, sp, stride=d)
        o_ref[hh, out_rows, :] = acc / den
        lse_ref[hh, out_rows, :] = jnp.broadcast_to(m + jnp.log2(den), (sp, LANES))

    def residues(t, carry):
        units = [(t * ru + rr, hh, j) for rr in range(ru) for hh in range(DL_HEADS_PER_GROUP)
                 for j in range(tq // sp)]
        pending = []
        for u in units:
            pending.append((u, scores(*u)))
            if len(pending) > 2:
                u0, vals = pending.pop(0)
                finish(*u0, *vals)
        for u0, vals in pending:
            finish(*u0, *vals)
        return carry

    lax.fori_loop(0, d // ru, residues, 0)


def _dilated_group(src, col0, g, B, S, tok_per_step=2048):
    window, d = DL_GROUPS[g]
    assert window // d == DL_SPAN
    L = S // d
    tq = min(tok_per_step, S) // d
    assert tq % DL_SPAN == 0 and L % tq == 0
    nh = DL_HEADS_PER_GROUP * len(DL_GROUPS)
    slopes2 = tuple(2.0 ** (-8.0 * (g * DL_HEADS_PER_GROUP + hh + 1) / nh) * d * LOG2E
                    for hh in range(DL_HEADS_PER_GROUP))
    spb = tq // DL_SPAN
    nsteps = L // tq
    cur = lambda c: pl.BlockSpec((d, tq, COL_TILE), lambda b, n: (b, n, c))
    prev = lambda c: pl.BlockSpec((d, DL_SPAN, COL_TILE), lambda b, n: (b, jnp.maximum(n * spb - 1, 0), c))
    out_spec = pl.BlockSpec((DL_HEADS_PER_GROUP, d * tq, LANES), lambda b, n: (0, b * nsteps + n, 0))
    return pl.pallas_call(
        functools.partial(_dl_kernel, slopes2=slopes2, d=d, tq=tq, ru=min(d, 4)),
        grid=(B, nsteps),
        in_specs=[cur(col0), cur(col0 + 1), prev(col0 + 1), cur(col0 + 2), prev(col0 + 2)],
        out_specs=[out_spec, out_spec],
        out_shape=[jax.ShapeDtypeStruct((DL_HEADS_PER_GROUP, B * S, LANES), F32)] * 2,
        compiler_params=_params(("parallel", "arbitrary")),
    )(src, src, src, src, src)


def _route(logits):
    lane = lax.broadcasted_iota(jnp.int32, logits.shape, 1)
    big = jnp.int32(1 << 20)
    is_g = (lane >= MOE_N_EXPERTS) & (lane < MOE_N_EXPERTS + MOE_GROUPS)
    lg = jnp.where(is_g, logits, -jnp.inf)
    gmax = jnp.max(lg, axis=-1, keepdims=True)
    gsum = jnp.sum(jnp.exp(lg - gmax), axis=-1, keepdims=True)
    g_w = 1.0 / gsum
    g_idx = jnp.min(jnp.where(lg == gmax, lane - MOE_N_EXPERTS, big), axis=-1, keepdims=True)
    in_grp = (lane < MOE_N_EXPERTS) & ((lane // MOE_EXPERTS_PER_GROUP) == g_idx)
    le = jnp.where(in_grp, logits, -jnp.inf)
    t1 = jnp.max(le, axis=-1, keepdims=True)
    e1 = jnp.min(jnp.where(le == t1, lane, big), axis=-1, keepdims=True)
    le2 = jnp.where(lane == e1, -jnp.inf, le)
    t2 = jnp.max(le2, axis=-1, keepdims=True)
    e2 = jnp.min(jnp.where(le2 == t2, lane, big), axis=-1, keepdims=True)
    r = jnp.exp(t2 - t1)
    w1 = g_w / (1.0 + r)
    w2 = w1 * r
    out = jnp.where(lane == 0, e1.astype(F32),
                    jnp.where(lane == 1, e2.astype(F32),
                              jnp.where(lane == 2, w1, jnp.where(lane == 3, w2, 0.0))))
    return out


def _outproj_kernel(x_ref, oa_ref, ga_ref, gb_ref, o0_ref, o1_ref, o2_ref, l0_ref, l1_ref, l2_ref,
                    wa_ref, wb_ref, wo_ref, gf_ref, rh_ref, rc_ref,
                    x1_ref, hn_ref, rt_ref):
    obs = []
    for hh in range(DL_HEADS_PER_GROUP):
        l0, l1, l2 = l0_ref[hh], l1_ref[hh], l2_ref[hh]
        lm = jnp.maximum(jnp.maximum(l0, l1), l2)
        e0, e1, e2 = jnp.exp2(l0 - lm), jnp.exp2(l1 - lm), jnp.exp2(l2 - lm)
        obs.append((e0 * o0_ref[hh] + e1 * o1_ref[hh] + e2 * o2_ref[hh]) / (e0 + e1 + e2))
    ob = jnp.concatenate(obs, axis=1)
    a = _dot(oa_ref[...], wa_ref[...])
    b = _dot(ob.astype(BF16), wb_ref[...])
    mixed = jax.nn.sigmoid(ga_ref[...].astype(F32)) * a + jax.nn.sigmoid(gb_ref[...].astype(F32)) * b
    x1 = x_ref[...] + _dot(mixed.astype(BF16), wo_ref[...])
    x1_ref[...] = x1
    ms = jnp.mean(x1 * x1, axis=-1, keepdims=True)
    hn = x1 * lax.rsqrt(ms + EPS) * gf_ref[...]
    hn_ref[...] = hn
    hn_hi = hn.astype(BF16)
    hn_lo = (hn - hn_hi.astype(F32)).astype(BF16)
    t = _dot(hn_hi, rc_ref[...])
    logits = t[:, 0:LANES] + (_dot(hn_lo, rh_ref[...]) + t[:, LANES:2 * LANES])
    rt_ref[...] = _route(logits)


def _outproj(x2, o_a, proj, dl_o, dl_lse, wa, wb, wo, gain_ffn, r_hi, r_cat, tm=256):
    T, D = x2.shape
    row = lambda w: pl.BlockSpec((tm, w), lambda i: (i, 0))
    full = lambda s: pl.BlockSpec(s, lambda i: (0, 0), pipeline_mode=pl.Buffered(1))
    hrow = pl.BlockSpec((DL_HEADS_PER_GROUP, tm, LANES), lambda i: (0, i, 0))
    return pl.pallas_call(
        _outproj_kernel,
        grid=(T // tm,),
        in_specs=[
            row(D), row(o_a.shape[1]),
            pl.BlockSpec((tm, D), lambda i: (i, (CT_GATE_A * COL_TILE) // D)),
            pl.BlockSpec((tm, D), lambda i: (i, (CT_GATE_B * COL_TILE) // D)),
            hrow, hrow, hrow, hrow, hrow, hrow,
            full(wa.shape), full(wb.shape), full(wo.shape), full((1, D)), full(r_hi.shape), full(r_cat.shape),
        ],
        out_specs=[row(D), row(D), row(LANES)],
        out_shape=[jax.ShapeDtypeStruct((T, D), F32), jax.ShapeDtypeStruct((T, D), F32),
                   jax.ShapeDtypeStruct((T, LANES), F32)],
        compiler_params=_params(("parallel",)),
    )(x2, o_a, proj, proj, dl_o[0], dl_o[1], dl_o[2], dl_lse[0], dl_lse[1], dl_lse[2],
      wa, wb, wo, gain_ffn, r_hi, r_cat)


def _expert_kernel(be_ref, nu_ref, tc_ref, tn_ref, dc_ref, hn_ref, wgu_ref, wd_ref, y_ref,
                   xbuf, ybuf, gsem, ssem, *, d_ff):
    i = pl.program_id(0)
    last = pl.num_programs(0) - 1
    nu = nu_ref[0]
    slot = i % 2
    rows = xbuf.shape[1]

    def gather(tok_ref, r, s):
        return pltpu.make_async_copy(hn_ref.at[tok_ref[0, r]], xbuf.at[s, r], gsem.at[s])

    def scatter(r, s):
        return pltpu.make_async_copy(ybuf.at[s, r], y_ref.at[dc_ref[0, r]], ssem.at[s])

    def wait_gather(s):
        pltpu.make_async_copy(hn_ref.at[pl.ds(0, rows)], xbuf.at[s], gsem.at[s]).wait()

    def wait_scatter(s):
        pltpu.make_async_copy(ybuf.at[s], y_ref.at[pl.ds(0, rows)], ssem.at[s]).wait()

    @pl.when(i == 0)
    def _():
        n_res = y_ref.shape[0] - 2 * rows
        ybuf[...] = jnp.zeros(ybuf.shape, ybuf.dtype)
        for s in range(2):
            pltpu.make_async_copy(ybuf.at[s], y_ref.at[pl.ds(n_res + s * rows, rows)], ssem.at[s]).start()
        for s in range(2):
            pltpu.make_async_copy(ybuf.at[s], y_ref.at[pl.ds(n_res + s * rows, rows)], ssem.at[s]).wait()

    @pl.when((i == 0) & (nu > 0))
    def _():
        def body(r, c):
            gather(tc_ref, r, 0).start()
            return c
        lax.fori_loop(0, rows, body, 0)

    @pl.when((i >= 2) & (i < nu))
    def _():
        wait_scatter(slot)

    @pl.when(i < nu)
    def _():
        wait_gather(slot)
        for r in range(rows):
            gather(tn_ref, r, 1 - slot).start()
        h = _dot(xbuf[slot].astype(BF16), wgu_ref[...])
        gate = h[:, :d_ff]
        up = h[:, d_ff:]
        act = gate * jax.nn.sigmoid(gate) * up
        ybuf[slot] = _dot(act.astype(BF16), wd_ref[...])
        for r in range(rows):
            scatter(r, slot).start()

    @pl.when((i == nu) & (nu > 0))
    def _():
        wait_gather(slot)

    @pl.when(i == last)
    def _():
        @pl.when(nu >= 1)
        def _():
            wait_scatter((nu - 1) % 2)

        @pl.when(nu >= 2)
        def _():
            wait_scatter(nu % 2)


def _experts(hn, src_tok, dst_row, blk_expert, n_used, wgu, wd, n_out_rows):
    T, D = hn.shape
    nblk = src_tok.shape[0] // MOE_BLOCK
    nsteps = nblk - 1
    d_ff = wd.shape[1]
    tok = src_tok.reshape(nblk, 1, MOE_BLOCK)
    dst = dst_row.reshape(nblk, 1, MOE_BLOCK)

    def live(i, nu):
        return jnp.maximum(jnp.minimum(i, nu[0] - 1), 0)

    smem = lambda f: pl.BlockSpec((None, 1, MOE_BLOCK), f, memory_space=pltpu.SMEM)
    grid_spec = pltpu.PrefetchScalarGridSpec(
        num_scalar_prefetch=2,
        grid=(nsteps,),
        in_specs=[
            smem(lambda i, be, nu: (i, 0, 0)),
            smem(lambda i, be, nu: (i + 1, 0, 0)),
            smem(lambda i, be, nu: (i, 0, 0)),
            pl.BlockSpec(memory_space=pl.ANY),
            pl.BlockSpec((None, D, 2 * d_ff), lambda i, be, nu: (be[live(i, nu)], 0, 0)),
            pl.BlockSpec((None, d_ff, D), lambda i, be, nu: (be[live(i, nu)], 0, 0)),
        ],
        out_specs=pl.BlockSpec(memory_space=pl.ANY),
        scratch_shapes=[pltpu.VMEM((2, MOE_BLOCK, D), F32), pltpu.VMEM((2, MOE_BLOCK, D), F32),
                        pltpu.SemaphoreType.DMA((2,)), pltpu.SemaphoreType.DMA((2,))],
    )
    return pl.pallas_call(
        functools.partial(_expert_kernel, d_ff=d_ff),
        grid_spec=grid_spec,
        out_shape=jax.ShapeDtypeStruct((n_out_rows, D), F32),
        compiler_params=_params(("arbitrary",)),
    )(blk_expert, n_used, tok, tok, dst, hn, wgu, wd)


def _combine_kernel(x1_ref, rt_ref, y1_ref, y2_ref, o_ref):
    rt = rt_ref[...]
    o_ref[...] = x1_ref[...] + (rt[:, 2:3] * y1_ref[...] + rt[:, 3:4] * y2_ref[...])


def _combine(x1, route, y, tm=512):
    T, D = x1.shape
    nt = T // tm
    return pl.pallas_call(
        _combine_kernel,
        grid=(nt,),
        in_specs=[
            pl.BlockSpec((tm, D), lambda i: (i, 0)),
            pl.BlockSpec((tm, LANES), lambda i: (i, 0)),
            pl.BlockSpec((tm, D), lambda i: (i, 0)),
            pl.BlockSpec((tm, D), lambda i: (nt + i, 0)),
        ],
        out_specs=pl.BlockSpec((tm, D), lambda i: (i, 0)),
        out_shape=jax.ShapeDtypeStruct((T, D), F32),
        compiler_params=_params(("parallel",)),
    )(x1, route, y, y)


def _dispatch_tables(route, T):
    TK = T * MOE_TOP_K
    flat_e = route[:, :MOE_TOP_K].astype(jnp.int32).reshape(-1)
    order = jnp.argsort(flat_e).astype(jnp.int32)
    sizes = jnp.sum(flat_e[:, None] == jnp.arange(MOE_N_EXPERTS, dtype=jnp.int32)[None, :], axis=0,
                    dtype=jnp.int32)
    start = jnp.cumsum(sizes) - sizes
    padded = ((sizes + MOE_BLOCK - 1) // MOE_BLOCK) * MOE_BLOCK
    pad_end = jnp.cumsum(padded)
    pad_start = pad_end - padded
    n_blocks = TK // MOE_BLOCK + MOE_N_EXPERTS
    blk = jnp.arange(n_blocks + 2, dtype=jnp.int32)
    blk_expert = jnp.minimum(jnp.sum(pad_end[None, :] <= (blk * MOE_BLOCK)[:, None], axis=1, dtype=jnp.int32),
                             MOE_N_EXPERTS - 1)
    r = jnp.arange(MOE_BLOCK, dtype=jnp.int32)[None, :]
    pos = blk[:, None] * MOE_BLOCK + r
    off = pos - pad_start[blk_expert][:, None]
    live = (off < sizes[blk_expert][:, None]) & (pos < pad_end[-1])
    sorted_idx = jnp.where(live, off + start[blk_expert][:, None], 0)
    slot = order[sorted_idx]
    tok = slot // MOE_TOP_K
    src_tok = jnp.where(live, tok, 0).reshape(-1)
    trash = TK + (blk[:, None] % 2) * MOE_BLOCK + r
    dst_row = jnp.where(live, (slot % MOE_TOP_K) * T + tok, trash).reshape(-1)
    n_used = (pad_end[-1:] // MOE_BLOCK).astype(jnp.int32)
    return src_tok, dst_row, blk_expert[:n_blocks + 1], n_used, TK + 2 * MOE_BLOCK


def _permute_w_in(w):
    da = DA_HEADS * 2 * DA_QK_DIM
    dav = DA_HEADS * DA_V_DIM
    dl = len(DL_GROUPS) * DL_HEADS_PER_GROUP * DL_HEAD_DIM
    o = [int(v) for v in np.cumsum([0, da, da, dav, dl, dl, dl])]
    parts = [w[:, o[6]:], w[:, :o[3]]]
    for g in range(len(DL_GROUPS)):
        for p in range(3):
            parts.append(w[:, o[3 + p] + g * COL_TILE: o[3 + p] + (g + 1) * COL_TILE])
    return jnp.concatenate(parts, axis=1)


def _gain_table(da_q_norm, da_k_norm, dl_q_norm, dl_k_norm):
    ones = jnp.ones((COL_TILE,), F32)
    daq = jnp.tile(da_q_norm, COL_TILE // DA_QK_DIM) * (DA_QK_DIM ** -0.5 * LOG2E)
    dak = jnp.tile(da_k_norm, COL_TILE // DA_QK_DIM)
    dlq = jnp.tile(dl_q_norm, COL_TILE // DL_HEAD_DIM) * (DL_HEAD_DIM ** -0.5 * LOG2E)
    dlk = jnp.tile(dl_k_norm, COL_TILE // DL_HEAD_DIM)
    rows = []
    for j in range(CT_END):
        if CT_DA_Q <= j < CT_DA_K:
            rows.append(daq)
        elif CT_DA_K <= j < CT_DA_V:
            rows.append(dak)
        elif j >= CT_DL and (j - CT_DL) % 3 == 0:
            rows.append(dlq)
        elif j >= CT_DL and (j - CT_DL) % 3 == 1:
            rows.append(dlk)
        else:
            rows.append(ones)
    return jnp.stack(rows, axis=0).reshape(CT_END, 1, COL_TILE)


def kernel(x, norm_mix, w_in, da_q_norm, da_k_norm, da_lambda_q, da_lambda_k, da_sub_norm,
           dl_q_norm, dl_k_norm, w_branch_a, w_branch_b, w_out, norm_ffn,
           w_group_router, w_expert_router, w_gate_up, w_down):
    B, S, D = x.shape
    T = B * S
    depth = w_in.shape[0]
    x2 = x.reshape(T, D)
    for l in range(depth):
        lam_init = 0.8 - 0.6 * math.exp(-0.3 * l)
        w_in_bf = _permute_w_in(w_in[l]).astype(BF16)
        gain_tab = _gain_table(da_q_norm[l], da_k_norm[l], dl_q_norm[l], dl_k_norm[l])
        proj, dl1, dl2 = _inproj(x2, norm_mix[l].reshape(1, D), w_in_bf, gain_tab, B, S)

        o_a = _diff_attention(proj, da_lambda_q[l], da_lambda_k[l], da_sub_norm[l].reshape(1, DA_V_DIM),
                              B, S, lam_init)
        dl = [_dilated_group(proj.reshape(B, S, proj.shape[1]), CT_DL, 0, B, S),
              _dilated_group(dl1, 0, 1, B, S), _dilated_group(dl2, 0, 2, B, S)]

        w_r = jnp.concatenate([w_expert_router[l], w_group_router[l]], axis=1)
        w_r = jnp.pad(w_r, ((0, 0), (0, LANES - w_r.shape[1])))
        r_hi = w_r.astype(BF16)
        r_lo = (w_r - r_hi.astype(F32)).astype(BF16)
        x1, hn, route = _outproj(
            x2, o_a, proj, [t[0] for t in dl], [t[1] for t in dl],
            w_branch_a[l].astype(BF16), w_branch_b[l].astype(BF16), w_out[l].astype(BF16),
            norm_ffn[l].reshape(1, D), r_hi, jnp.concatenate([r_hi, r_lo], axis=1))

        src_tok, dst_row, blk_expert, n_used, n_rows = _dispatch_tables(route, T)
        y = _experts(hn, src_tok, dst_row, blk_expert, n_used,
                     w_gate_up[l].astype(BF16), w_down[l].astype(BF16), n_rows)
        x2 = _combine(x1, route, y)
    return x2.reshape(B, S, D)
```

```python
import functools
import math

import jax
import jax.numpy as jnp
import numpy as np
from jax import lax
from jax.experimental import pallas as pl
from jax.experimental.pallas import tpu as pltpu

F32 = jnp.float32
BF16 = jnp.bfloat16

EPS = 1e-6
LOG2E = 1.4426950408889634
NEG_BIG = -1e30

DA_HEADS = 8
DA_QK_DIM = 64
DA_V_DIM = 128
DA_TILE_GROUP = 4
DL_GROUPS = ((128, 1), (512, 4), (2048, 16))
DL_HEADS_PER_GROUP = 4
DL_HEAD_DIM = 128
DL_SPAN = 128
MOE_GROUPS = 4
MOE_EXPERTS_PER_GROUP = 8
MOE_N_EXPERTS = 32
MOE_TOP_K = 2
MOE_BLOCK = 256

LANES = 128
COL_TILE = 512
VMEM_LIMIT = 56 * 1024 * 1024

CT_GATE_A, CT_GATE_B, CT_DA_Q, CT_DA_K, CT_DA_V, CT_DL, CT_MAIN_END, CT_END = 0, 4, 8, 10, 12, 14, 17, 23


def _params(sem, vmem=VMEM_LIMIT):
    return pltpu.CompilerParams(dimension_semantics=sem, vmem_limit_bytes=vmem)


def _dot(a, b):
    return jnp.dot(a, b, preferred_element_type=F32)


def _dot_nt(a, b):
    return lax.dot_general(a, b, (((1,), (1,)), ((), ())), preferred_element_type=F32)


def _inproj_kernel(x_ref, g_ref, w_ref, gain_ref, o_ref, d1_ref, d2_ref, h_scr, y_scr):
    j = pl.program_id(1)

    @pl.when(j == 0)
    def _():
        x = x_ref[...]
        ms = jnp.mean(x * x, axis=-1, keepdims=True)
        h_scr[...] = (x * lax.rsqrt(ms + EPS) * g_ref[...]).astype(BF16)

    y = _dot(h_scr[...], w_ref[...])
    gain = gain_ref[...]
    heads = COL_TILE // LANES

    is64 = (j >= CT_DA_Q) & (j < CT_DA_V)
    is128 = (j >= CT_DL) & (lax.rem(j - CT_DL, 3) < 2)
    main = j < CT_MAIN_END

    def norm128(h):
        sl = slice(h * LANES, (h + 1) * LANES)
        yh = y[:, sl]
        ss = jnp.sum(yh * yh, axis=-1, keepdims=True)
        return yh * lax.rsqrt(ss * (1.0 / DL_HEAD_DIM) + EPS) * gain[:, sl]

    @pl.when(is64)
    def _():
        for h in range(heads):
            sl = slice(h * LANES, (h + 1) * LANES)
            yh = y[:, sl]
            sq = yh * yh
            lo = lax.broadcasted_iota(jnp.int32, yh.shape, 1) < DA_QK_DIM
            s_lo = jnp.sum(jnp.where(lo, sq, 0.0), axis=-1, keepdims=True)
            s_hi = jnp.sum(jnp.where(lo, 0.0, sq), axis=-1, keepdims=True)
            r = jnp.where(lo, lax.rsqrt(s_lo * (1.0 / DA_QK_DIM) + EPS),
                          lax.rsqrt(s_hi * (1.0 / DA_QK_DIM) + EPS))
            o_ref[:, sl] = (yh * r * gain[:, sl]).astype(o_ref.dtype)

    @pl.when(is128 & main)
    def _():
        for h in range(heads):
            o_ref[:, h * LANES:(h + 1) * LANES] = norm128(h).astype(o_ref.dtype)

    @pl.when(is128 & jnp.logical_not(main))
    def _():
        for h in range(heads):
            y_scr[h] = norm128(h)

    plain = jnp.logical_not(is64 | is128)

    @pl.when(plain & main)
    def _():
        o_ref[...] = y.astype(o_ref.dtype)

    @pl.when(plain & jnp.logical_not(main))
    def _():
        for h in range(heads):
            y_scr[h] = y[:, h * LANES:(h + 1) * LANES]

    def deinterleave(dst_ref):
        d, rows = dst_ref.shape[0], dst_ref.shape[1]
        for r in range(d):
            for h in range(heads):
                dst_ref[r, :, h * LANES:(h + 1) * LANES] = (
                    y_scr[h, pl.ds(r, rows, stride=d), :].astype(dst_ref.dtype))

    @pl.when((j >= CT_MAIN_END) & (j < CT_MAIN_END + 3))
    def _():
        deinterleave(d1_ref)

    @pl.when(j >= CT_MAIN_END + 3)
    def _():
        deinterleave(d2_ref)


def _inproj(x2, gain_mix, w_bf, gain_tab, B, S, tm=1024):
    T, D = x2.shape
    tiles_per_batch = S // tm
    d1, d2 = DL_GROUPS[1][1], DL_GROUPS[2][1]
    part1 = lambda j: jnp.clip(j - CT_MAIN_END, 0, 2)
    part2 = lambda j: jnp.clip(j - CT_MAIN_END - 3, 0, 2)
    return pl.pallas_call(
        _inproj_kernel,
        grid=(T // tm, CT_END),
        in_specs=[
            pl.BlockSpec((tm, D), lambda i, j: (i, 0)),
            pl.BlockSpec((1, D), lambda i, j: (0, 0)),
            pl.BlockSpec((D, COL_TILE), lambda i, j: (0, j)),
            pl.BlockSpec((None, 1, COL_TILE), lambda i, j: (j, 0, 0)),
        ],
        out_specs=[
            pl.BlockSpec((tm, COL_TILE), lambda i, j: (i, jnp.minimum(j, CT_MAIN_END - 1))),
            pl.BlockSpec((d1, tm // d1, COL_TILE),
                         lambda i, j: (i // tiles_per_batch, i % tiles_per_batch, part1(j))),
            pl.BlockSpec((d2, tm // d2, COL_TILE),
                         lambda i, j: (i // tiles_per_batch, i % tiles_per_batch, part2(j))),
        ],
        out_shape=[
            jax.ShapeDtypeStruct((T, CT_MAIN_END * COL_TILE), BF16),
            jax.ShapeDtypeStruct((B * d1, S // d1, 3 * COL_TILE), BF16),
            jax.ShapeDtypeStruct((B * d2, S // d2, 3 * COL_TILE), BF16),
        ],
        scratch_shapes=[pltpu.VMEM((tm, D), BF16), pltpu.VMEM((COL_TILE // LANES, tm, LANES), F32)],
        compiler_params=_params(("parallel", "arbitrary")),
    )(x2, gain_mix, w_bf, gain_tab)


def _bf16_pieces(x, n=3):
    out = []
    r = np.float64(x)
    for _ in range(n):
        p = np.asarray(np.float32(r)).astype(jnp.bfloat16).astype(np.float64)
        out.append(float(p))
        r = r - p
    return out


def _alibi_features(tk):
    pieces = _bf16_pieces(LOG2E)
    qf = np.zeros((2, LANES), np.float32)
    kf = np.zeros((2, tk, LANES), np.float32)
    j = np.arange(tk)
    hi, lo = (j // 16) * 16, j % 16
    for m in range(2):
        f0 = DA_QK_DIM if m == 0 else 0
        for n, p in enumerate(pieces):
            qf[m, f0 + 2 * n] = p
            qf[m, f0 + 2 * n + 1] = p
            kf[m, :, f0 + 2 * n] = hi
            kf[m, :, f0 + 2 * n + 1] = lo
    return jnp.asarray(qf), jnp.asarray(kf, dtype=BF16)


def _da_kernel(q_ref, k_ref, v_ref, qf_ref, kf_ref, lq_ref, lk_ref, sg_ref, o_ref,
               s00, s01, s10, s11, p00, p01, p10, p11,
               m0_scr, m1_scr, l0_scr, l1_scr, a0_scr, a1_scr, acc0_scr, acc1_scr, *, tq, rc, lam_init):
    h = pl.program_id(1)
    qi = pl.program_id(2)
    nlb = tq // LANES
    pow2 = jnp.exp2(-(h + 1).astype(F32))
    slope2 = pow2 * LOG2E

    q = q_ref[...]
    lane = lax.broadcasted_iota(jnp.int32, (tq, LANES), 1)
    own = (lane < DA_QK_DIM, lane >= DA_QK_DIM)
    qfs = [jnp.where(own[mi], q, jnp.broadcast_to((qf_ref[mi:mi + 1, :] * pow2).astype(BF16), q.shape))
           for mi in range(2)]

    m_scrs, l_scrs, a_scrs, acc_scrs = (m0_scr, m1_scr), (l0_scr, l1_scr), (a0_scr, a1_scr), (acc0_scr, acc1_scr)
    for mi in range(2):
        m_scrs[mi][...] = jnp.full(m_scrs[mi].shape, NEG_BIG, F32)
        l_scrs[mi][...] = jnp.zeros(l_scrs[mi].shape, F32)
        acc_scrs[mi][...] = jnp.zeros(acc_scrs[mi].shape, F32)

    def scores(ki, mi, s_ref):
        k = k_ref[pl.ds(pl.multiple_of(ki * tq, tq), tq), :]
        s_ref[...] = _dot_nt(qfs[mi], jnp.where(own[mi], k, kf_ref[mi]))

    def softmax(ki, mi, s_ref, p_ref, masked):
        m_scr, l_scr, a_scr = m_scrs[mi], l_scrs[mi], a_scrs[mi]
        c = slope2 * ((ki - qi) * tq).astype(F32)
        for r in range(tq // rc):
            rows = slice(r * rc, (r + 1) * rc)
            nb = min(nlb, ((r + 1) * rc - 1) // LANES + 1) if masked else nlb
            sb = []
            for j in range(nb):
                cs = slice(j * LANES, (j + 1) * LANES)
                s = s_ref[rows, cs]
                if masked and (j + 1) * LANES - 1 > r * rc:
                    rr = lax.broadcasted_iota(jnp.int32, (rc, LANES), 0) + r * rc
                    cc = lax.broadcasted_iota(jnp.int32, (rc, LANES), 1) + j * LANES
                    s = jnp.where(cc <= rr, s, NEG_BIG)
                sb.append(s)
            mx = sb[0]
            for s in sb[1:]:
                mx = jnp.maximum(mx, s)
            m_prev = m_scr[rows, :]
            m_new = jnp.maximum(m_prev, jnp.max(mx, axis=-1, keepdims=True) + c)
            alpha = jnp.exp2(m_prev - m_new)
            a_scr[rows, :] = alpha
            m_scr[rows, :] = m_new
            mc = m_new - c
            psum = alpha * l_scr[rows, :]
            for j in range(nlb):
                cs = slice(j * LANES, (j + 1) * LANES)
                if j < nb:
                    p = jnp.exp2(sb[j] - mc)
                    psum = psum + p
                    p_ref[rows, cs] = p.astype(BF16)
                else:
                    p_ref[rows, cs] = jnp.zeros((rc, LANES), BF16)
            l_scr[rows, :] = psum

    def values(ki, mi, p_ref):
        v = v_ref[pl.ds(pl.multiple_of(ki * tq, tq), tq), :]
        acc_scrs[mi][...] = a_scrs[mi][...] * acc_scrs[mi][...] + _dot(p_ref[...], v)

    s_bufs, p_bufs = ((s00, s01), (s10, s11)), ((p00, p01), (p10, p11))

    def tile_group(k0, n, last_masked):
        for mi in range(2):
            scores(k0, mi, s_bufs[0][mi])
        for i in range(n):
            masked = last_masked and i == n - 1
            for mi in range(2):
                softmax(k0 + i, mi, s_bufs[i % 2][mi], p_bufs[i % 2][mi], masked)
                values(k0 + i, mi, p_bufs[i % 2][mi])
                if i + 1 < n:
                    scores(k0 + i + 1, mi, s_bufs[(i + 1) % 2][mi])

    def body(t, carry):
        tile_group(DA_TILE_GROUP * t, DA_TILE_GROUP, False)
        return carry

    n_full = qi // DA_TILE_GROUP
    lax.fori_loop(0, n_full, body, 0)
    for rem in range(1, DA_TILE_GROUP + 1):
        @pl.when(qi - DA_TILE_GROUP * n_full == rem - 1)
        def _(rem=rem):
            tile_group(qi - (rem - 1), rem, True)

    lam_e = jnp.exp(jnp.sum(lq_ref[...] * lk_ref[...], axis=-1, keepdims=True))
    lam = lam_e[0:1, :] - lam_e[1:2, :] + lam_init
    l0 = jnp.sum(l0_scr[...], axis=-1, keepdims=True)
    l1 = jnp.sum(l1_scr[...], axis=-1, keepdims=True)
    o = acc0_scr[...] / l0 - lam * (acc1_scr[...] / l1)
    ms = jnp.mean(o * o, axis=-1, keepdims=True)
    o = o * lax.rsqrt(ms + EPS) * sg_ref[...] * (1.0 - lam_init)
    o_ref[...] = o.astype(o_ref.dtype)


def _diff_attention(proj, lam_q, lam_k, sub_gain, B, S, lam_init, tq=512, rc=32):
    T = proj.shape[0]
    nq = S // tq
    lb = LANES
    q_blk0, k_blk0, v_blk0 = (CT_DA_Q * COL_TILE) // lb, (CT_DA_K * COL_TILE) // lb, (CT_DA_V * COL_TILE) // lb
    qfeat, kfeat = _alibi_features(tq)
    const = lambda shape: pl.BlockSpec(shape, lambda b, h, i: (0,) * len(shape))
    return pl.pallas_call(
        functools.partial(_da_kernel, tq=tq, rc=rc, lam_init=lam_init),
        grid=(B, DA_HEADS, nq),
        in_specs=[
            pl.BlockSpec((tq, lb), lambda b, h, i: (b * nq + i, q_blk0 + h)),
            pl.BlockSpec((S, lb), lambda b, h, i: (b, k_blk0 + h)),
            pl.BlockSpec((S, lb), lambda b, h, i: (b, v_blk0 + h)),
            const((2, LANES)), const((2, tq, LANES)),
            const((2, DA_QK_DIM)), const((2, DA_QK_DIM)), const((1, DA_V_DIM)),
        ],
        out_specs=pl.BlockSpec((tq, lb), lambda b, h, i: (b * nq + i, h)),
        out_shape=jax.ShapeDtypeStruct((T, DA_HEADS * DA_V_DIM), BF16),
        scratch_shapes=[pltpu.VMEM((tq, tq), F32)] * 4 + [pltpu.VMEM((tq, tq), BF16)] * 4
        + [pltpu.VMEM((tq, LANES), F32)] * 6 + [pltpu.VMEM((tq, DA_V_DIM), F32)] * 2,
        compiler_params=_params(("parallel", "parallel", "arbitrary")),
    )(proj, proj, proj, qfeat, kfeat, lam_q, lam_k, sub_gain)


def _dl_kernel(q_ref, kc_ref, kp_ref, vc_ref, vp_ref, o_ref, lse_ref, *, slopes2, d, tq, ru):
    n = pl.program_id(1)
    sp = DL_SPAN
    row = lax.broadcasted_iota(jnp.int32, (sp, sp), 0)
    col = lax.broadcasted_iota(jnp.int32, (sp, sp), 1)
    dcur = row - col
    cur_ok = dcur >= 0
    prev_ok = dcur <= 0
    dcur_f = dcur.astype(F32)

    def scores(r, hh, j):
        hs = slice(hh * LANES, (hh + 1) * LANES)
        rs = slice(j * sp, (j + 1) * sp)
        q = q_ref[r, rs, hs]
        if j == 0:
            kp, vp, p_ok = kp_ref[r, :, hs], vp_ref[r, :, hs], prev_ok & (n > 0)
        else:
            ps = slice((j - 1) * sp, j * sp)
            kp, vp, p_ok = kc_ref[r, ps, hs], vc_ref[r, ps, hs], prev_ok
        s_c = jnp.where(cur_ok, _dot_nt(q, kc_ref[r, rs, hs]) - slopes2[hh] * dcur_f, NEG_BIG)
        s_p = jnp.where(p_ok, _dot_nt(q, kp) - slopes2[hh] * (dcur_f + float(sp)), NEG_BIG)
        return s_c, s_p, vc_ref[r, rs, hs], vp

    def finish(r, hh, j, s_c, s_p, vc, vp):
        m = jnp.max(jnp.maximum(s_c, s_p), axis=-1, keepdims=True)
        p_c = jnp.exp2(s_c - m)
        p_p = jnp.exp2(s_p - m)
        den = jnp.sum(p_c + p_p, axis=-1, keepdims=True)
        acc = _dot(p_c.astype(BF16), vc) + _dot(p_p.astype(BF16), vp)
        out_rows = pl.ds(j * sp, sp) if d == 1 else pl.ds(r + j * sp * d, sp, stride=d)
        o_ref[hh, out_rows, :] = acc / den
        lse_ref[hh, out_rows, :] = jnp.broadcast_to(m + jnp.log2(den), (sp, LANES))

    def residues(t, carry):
        units = [(t * ru + rr, hh, j) for rr in range(ru) for hh in range(DL_HEADS_PER_GROUP)
                 for j in range(tq // sp)]
        pending = []
        for u in units:
            pending.append((u, scores(*u)))
            if len(pending) > 2:
                u0, vals = pending.pop(0)
                finish(*u0, *vals)
        for u0, vals in pending:
            finish(*u0, *vals)
        return carry

    lax.fori_loop(0, d // ru, residues, 0)


def _dilated_group(src, col0, g, B, S, tok_per_step=2048):
    window, d = DL_GROUPS[g]
    assert window // d == DL_SPAN
    L = S // d
    tq = min(tok_per_step, S) // d
    assert tq % DL_SPAN == 0 and L % tq == 0
    nh = DL_HEADS_PER_GROUP * len(DL_GROUPS)
    slopes2 = tuple(2.0 ** (-8.0 * (g * DL_HEADS_PER_GROUP + hh + 1) / nh) * d * LOG2E
                    for hh in range(DL_HEADS_PER_GROUP))
    spb = tq // DL_SPAN
    nsteps = L // tq
    cur = lambda c: pl.BlockSpec((d, tq, COL_TILE), lambda b, n: (b, n, c))
    prev = lambda c: pl.BlockSpec((d, DL_SPAN, COL_TILE), lambda b, n: (b, jnp.maximum(n * spb - 1, 0), c))
    out_spec = pl.BlockSpec((DL_HEADS_PER_GROUP, d * tq, LANES), lambda b, n: (0, b * nsteps + n, 0))
    return pl.pallas_call(
        functools.partial(_dl_kernel, slopes2=slopes2, d=d, tq=tq, ru=min(d, 4)),
        grid=(B, nsteps),
        in_specs=[cur(col0), cur(col0 + 1), prev(col0 + 1), cur(col0 + 2), prev(col0 + 2)],
        out_specs=[out_spec, out_spec],
        out_shape=[jax.ShapeDtypeStruct((DL_HEADS_PER_GROUP, B * S, LANES), F32)] * 2,
        compiler_params=_params(("parallel", "arbitrary")),
    )(src, src, src, src, src)


def _route(logits):
    lane = lax.broadcasted_iota(jnp.int32, logits.shape, 1)
    big = jnp.int32(1 << 20)
    is_g = (lane >= MOE_N_EXPERTS) & (lane < MOE_N_EXPERTS + MOE_GROUPS)
    lg = jnp.where(is_g, logits, -jnp.inf)
    gmax = jnp.max(lg, axis=-1, keepdims=True)
    gsum = jnp.sum(jnp.exp(lg - gmax), axis=-1, keepdims=True)
    g_w = 1.0 / gsum
    g_idx = jnp.min(jnp.where(lg == gmax, lane - MOE_N_EXPERTS, big), axis=-1, keepdims=True)
    in_grp = (lane < MOE_N_EXPERTS) & ((lane // MOE_EXPERTS_PER_GROUP) == g_idx)
    le = jnp.where(in_grp, logits, -jnp.inf)
    t1 = jnp.max(le, axis=-1, keepdims=True)
    e1 = jnp.min(jnp.where(le == t1, lane, big), axis=-1, keepdims=True)
    le2 = jnp.where(lane == e1, -jnp.inf, le)
    t2 = jnp.max(le2, axis=-1, keepdims=True)
    e2 = jnp.min(jnp.where(le2 == t2, lane, big), axis=-1, keepdims=True)
    r = jnp.exp(t2 - t1)
    w1 = g_w / (1.0 + r)
    w2 = w1 * r
    out = jnp.where(lane == 0, e1.astype(F32),
                    jnp.where(lane == 1, e2.astype(F32),
                              jnp.where(lane == 2, w1, jnp.where(lane == 3, w2, 0.0))))
    return out


def _outproj_kernel(x_ref, oa_ref, ga_ref, gb_ref, o0_ref, o1_ref, o2_ref, l0_ref, l1_ref, l2_ref,
                    wa_ref, wb_ref, wo_ref, gf_ref, rh_ref, rc_ref,
                    x1_ref, hn_ref, rt_ref):
    obs = []
    for hh in range(DL_HEADS_PER_GROUP):
        l0, l1, l2 = l0_ref[hh], l1_ref[hh], l2_ref[hh]
        lm = jnp.maximum(jnp.maximum(l0, l1), l2)
        e0, e1, e2 = jnp.exp2(l0 - lm), jnp.exp2(l1 - lm), jnp.exp2(l2 - lm)
        obs.append((e0 * o0_ref[hh] + e1 * o1_ref[hh] + e2 * o2_ref[hh]) / (e0 + e1 + e2))
    ob = jnp.concatenate(obs, axis=1)
    a = _dot(oa_ref[...], wa_ref[...])
    b = _dot(ob.astype(BF16), wb_ref[...])
    mixed = jax.nn.sigmoid(ga_ref[...].astype(F32)) * a + jax.nn.sigmoid(gb_ref[...].astype(F32)) * b
    x1 = x_ref[...] + _dot(mixed.astype(BF16), wo_ref[...])
    x1_ref[...] = x1
    ms = jnp.mean(x1 * x1, axis=-1, keepdims=True)
    hn = x1 * lax.rsqrt(ms + EPS) * gf_ref[...]
    hn_ref[...] = hn
    hn_hi = hn.astype(BF16)
    hn_lo = (hn - hn_hi.astype(F32)).astype(BF16)
    t = _dot(hn_hi, rc_ref[...])
    logits = t[:, 0:LANES] + (_dot(hn_lo, rh_ref[...]) + t[:, LANES:2 * LANES])
    rt_ref[...] = _route(logits)


def _outproj(x2, o_a, proj, dl_o, dl_lse, wa, wb, wo, gain_ffn, r_hi, r_cat, tm=256):
    T, D = x2.shape
    row = lambda w: pl.BlockSpec((tm, w), lambda i: (i, 0))
    full = lambda s: pl.BlockSpec(s, lambda i: (0, 0), pipeline_mode=pl.Buffered(1))
    hrow = pl.BlockSpec((DL_HEADS_PER_GROUP, tm, LANES), lambda i: (0, i, 0))
    return pl.pallas_call(
        _outproj_kernel,
        grid=(T // tm,),
        in_specs=[
            row(D), row(o_a.shape[1]),
            pl.BlockSpec((tm, D), lambda i: (i, (CT_GATE_A * COL_TILE) // D)),
            pl.BlockSpec((tm, D), lambda i: (i, (CT_GATE_B * COL_TILE) // D)),
            hrow, hrow, hrow, hrow, hrow, hrow,
            full(wa.shape), full(wb.shape), full(wo.shape), full((1, D)), full(r_hi.shape), full(r_cat.shape),
        ],
        out_specs=[row(D), row(D), row(LANES)],
        out_shape=[jax.ShapeDtypeStruct((T, D), F32), jax.ShapeDtypeStruct((T, D), F32),
                   jax.ShapeDtypeStruct((T, LANES), F32)],
        compiler_params=_params(("parallel",)),
    )(x2, o_a, proj, proj, dl_o[0], dl_o[1], dl_o[2], dl_lse[0], dl_lse[1], dl_lse[2],
      wa, wb, wo, gain_ffn, r_hi, r_cat)


def _cast_rows(src_ref, dst_ref, chunk=256):
    def body(c, carry):
        r0 = pl.multiple_of(c * chunk, chunk)
        dst_ref[pl.ds(r0, chunk), :] = src_ref[pl.ds(r0, chunk), :].astype(dst_ref.dtype)
        return carry
    lax.fori_loop(0, src_ref.shape[0] // chunk, body, 0)


def _expert_changed(be_ref, i):
    return (i == 0) | (be_ref[i] != be_ref[jnp.maximum(i - 1, 0)])


def _moe_up_kernel(be_ref, nu_ref, tc_ref, tn_ref, hn_ref, wgu_ref, act_ref, xa, xb, wbf, gsem, *, d_ff):
    i = pl.program_id(0)
    nu = nu_ref[0]
    rows = xa.shape[0]
    even = i % 2 == 0

    def gather(tok_ref, r, buf, s):
        return pltpu.make_async_copy(hn_ref.at[tok_ref[0, r]], buf.at[r], gsem.at[s])

    def wait_gather(buf, s):
        pltpu.make_async_copy(hn_ref.at[pl.ds(0, rows)], buf, gsem.at[s]).wait()

    @pl.when((i == 0) & (nu > 0))
    def _():
        def body(r, c):
            gather(tc_ref, r, xa, 0).start()
            return c
        lax.fori_loop(0, rows, body, 0)

    @pl.when((i < nu) & _expert_changed(be_ref, i))
    def _():
        _cast_rows(wgu_ref, wbf)

    def live_step(cur, nxt, s):
        wait_gather(cur, s)
        for r in range(rows):
            gather(tn_ref, r, nxt, 1 - s).start()
        h = _dot(cur[...].astype(BF16), wbf[...])
        gate = h[:, :d_ff]
        up = h[:, d_ff:]
        act_ref[...] = (gate * jax.nn.sigmoid(gate) * up).astype(act_ref.dtype)

    @pl.when((i < nu) & even)
    def _():
        live_step(xa, xb, 0)

    @pl.when((i < nu) & jnp.logical_not(even))
    def _():
        live_step(xb, xa, 1)

    @pl.when((i == nu) & (nu > 0) & even)
    def _():
        wait_gather(xa, 0)

    @pl.when((i == nu) & (nu > 0) & jnp.logical_not(even))
    def _():
        wait_gather(xb, 1)

    @pl.when(i >= nu)
    def _():
        act_ref[...] = jnp.zeros(act_ref.shape, act_ref.dtype)


def _moe_down_kernel(be_ref, nu_ref, dp_ref, act_ref, wd_ref, y_ref, ya, yb, wbf, ssem):
    i = pl.program_id(0)
    nu = nu_ref[0]
    rows = ya.shape[0]
    even = i % 2 == 0

    def scatter(r, buf, s):
        return pltpu.make_async_copy(buf.at[r], y_ref.at[dp_ref[0, r]], ssem.at[s])

    def wait_scatter(buf, s):
        pltpu.make_async_copy(buf, y_ref.at[pl.ds(0, rows)], ssem.at[s]).wait()

    @pl.when(i == 0)
    def _():
        n_res = y_ref.shape[0] - 2 * rows
        for s, buf in enumerate((ya, yb)):
            buf[...] = jnp.zeros(buf.shape, buf.dtype)
            pltpu.make_async_copy(buf, y_ref.at[pl.ds(n_res + s * rows, rows)], ssem.at[s]).start()
        for s, buf in enumerate((ya, yb)):
            pltpu.make_async_copy(buf, y_ref.at[pl.ds(n_res + s * rows, rows)], ssem.at[s]).wait()

    @pl.when((i >= 2) & (i < nu + 2) & even)
    def _():
        wait_scatter(ya, 0)

    @pl.when((i >= 2) & (i < nu + 2) & jnp.logical_not(even))
    def _():
        wait_scatter(yb, 1)

    @pl.when((i < nu) & _expert_changed(be_ref, i))
    def _():
        _cast_rows(wd_ref, wbf)

    def step(cur, prv, s, do_scatter, do_compute):
        if do_scatter:
            for r in range(rows):
                scatter(r, prv, 1 - s).start()
        if do_compute:
            cur[...] = _dot(act_ref[...], wbf[...])

    for s, (cur, prv) in enumerate(((ya, yb), (yb, ya))):
        par = even if s == 0 else jnp.logical_not(even)

        @pl.when((i >= 1) & (i < nu) & par)
        def _(cur=cur, prv=prv, s=s):
            step(cur, prv, s, True, True)

        @pl.when((i == 0) & (nu > 0) & par)
        def _(cur=cur, prv=prv, s=s):
            step(cur, prv, s, False, True)

        @pl.when((i == nu) & (nu > 0) & par)
        def _(cur=cur, prv=prv, s=s):
            step(cur, prv, s, True, False)


def _experts(hn, src_tok, dst_row, blk_expert, n_used, wgu, wd, n_out_rows):
    T, D = hn.shape
    nblk = src_tok.shape[0] // MOE_BLOCK
    d_ff = wd.shape[1]
    tok = src_tok.reshape(nblk, 1, MOE_BLOCK)
    dst = dst_row.reshape(nblk, 1, MOE_BLOCK)

    def live(i, nu):
        return jnp.maximum(jnp.minimum(i, nu[0] - 1), 0)

    smem = lambda f: pl.BlockSpec((None, 1, MOE_BLOCK), f, memory_space=pltpu.SMEM)
    act = pl.pallas_call(
        functools.partial(_moe_up_kernel, d_ff=d_ff),
        grid_spec=pltpu.PrefetchScalarGridSpec(
            num_scalar_prefetch=2,
            grid=(nblk - 1,),
            in_specs=[
                smem(lambda i, be, nu: (i, 0, 0)),
                smem(lambda i, be, nu: (i + 1, 0, 0)),
                pl.BlockSpec(memory_space=pl.ANY),
                pl.BlockSpec((None, D, 2 * d_ff), lambda i, be, nu: (be[live(i, nu)], 0, 0)),
            ],
            out_specs=pl.BlockSpec((MOE_BLOCK, d_ff), lambda i, be, nu: (i, 0)),
            scratch_shapes=[pltpu.VMEM((MOE_BLOCK, D), F32), pltpu.VMEM((MOE_BLOCK, D), F32),
                            pltpu.VMEM((D, 2 * d_ff), BF16), pltpu.SemaphoreType.DMA((2,))],
        ),
        out_shape=jax.ShapeDtypeStruct(((nblk - 1) * MOE_BLOCK, d_ff), BF16),
        compiler_params=_params(("arbitrary",)),
    )(blk_expert, n_used, tok, tok, hn, wgu)
    return pl.pallas_call(
        _moe_down_kernel,
        grid_spec=pltpu.PrefetchScalarGridSpec(
            num_scalar_prefetch=2,
            grid=(nblk,),
            in_specs=[
                smem(lambda i, be, nu: (jnp.maximum(i - 1, 0), 0, 0)),
                pl.BlockSpec((MOE_BLOCK, d_ff), lambda i, be, nu: (live(i, nu), 0)),
                pl.BlockSpec((None, d_ff, D), lambda i, be, nu: (be[live(i, nu)], 0, 0)),
            ],
            out_specs=pl.BlockSpec(memory_space=pl.ANY),
            scratch_shapes=[pltpu.VMEM((MOE_BLOCK, D), F32), pltpu.VMEM((MOE_BLOCK, D), F32),
                            pltpu.VMEM((d_ff, D), BF16), pltpu.SemaphoreType.DMA((2,))],
        ),
        out_shape=jax.ShapeDtypeStruct((n_out_rows, D), F32),
        compiler_params=_params(("arbitrary",)),
    )(blk_expert, n_used, dst, act, wd)


def _combine_kernel(x1_ref, rt_ref, y1_ref, y2_ref, o_ref):
    rt = rt_ref[...]
    o_ref[...] = x1_ref[...] + (rt[:, 2:3] * y1_ref[...] + rt[:, 3:4] * y2_ref[...])


def _combine(x1, route, y, tm=512):
    T, D = x1.shape
    nt = T // tm
    return pl.pallas_call(
        _combine_kernel,
        grid=(nt,),
        in_specs=[
            pl.BlockSpec((tm, D), lambda i: (i, 0)),
            pl.BlockSpec((tm, LANES), lambda i: (i, 0)),
            pl.BlockSpec((tm, D), lambda i: (i, 0)),
            pl.BlockSpec((tm, D), lambda i: (nt + i, 0)),
        ],
        out_specs=pl.BlockSpec((tm, D), lambda i: (i, 0)),
        out_shape=jax.ShapeDtypeStruct((T, D), F32),
        compiler_params=_params(("parallel",)),
    )(x1, route, y, y)


def _dispatch_tables(route, T):
    TK = T * MOE_TOP_K
    flat_e = route[:, :MOE_TOP_K].astype(jnp.int32).reshape(-1)
    order = jnp.argsort(flat_e).astype(jnp.int32)
    sizes = jnp.sum(flat_e[:, None] == jnp.arange(MOE_N_EXPERTS, dtype=jnp.int32)[None, :], axis=0,
                    dtype=jnp.int32)
    start = jnp.cumsum(sizes) - sizes
    padded = ((sizes + MOE_BLOCK - 1) // MOE_BLOCK) * MOE_BLOCK
    pad_end = jnp.cumsum(padded)
    pad_start = pad_end - padded
    n_blocks = TK // MOE_BLOCK + MOE_N_EXPERTS
    blk = jnp.arange(n_blocks + 2, dtype=jnp.int32)
    blk_expert = jnp.minimum(jnp.sum(pad_end[None, :] <= (blk * MOE_BLOCK)[:, None], axis=1, dtype=jnp.int32),
                             MOE_N_EXPERTS - 1)
    r = jnp.arange(MOE_BLOCK, dtype=jnp.int32)[None, :]
    pos = blk[:, None] * MOE_BLOCK + r
    off = pos - pad_start[blk_expert][:, None]
    live = (off < sizes[blk_expert][:, None]) & (pos < pad_end[-1])
    sorted_idx = jnp.where(live, off + start[blk_expert][:, None], 0)
    slot = order[sorted_idx]
    tok = slot // MOE_TOP_K
    src_tok = jnp.where(live, tok, 0).reshape(-1)
    trash = TK + (blk[:, None] % 2) * MOE_BLOCK + r
    dst_row = jnp.where(live, (slot % MOE_TOP_K) * T + tok, trash).reshape(-1)
    n_used = (pad_end[-1:] // MOE_BLOCK).astype(jnp.int32)
    return src_tok, dst_row, blk_expert, n_used, TK + 2 * MOE_BLOCK


def _permute_w_in(w):
    da = DA_HEADS * 2 * DA_QK_DIM
    dav = DA_HEADS * DA_V_DIM
    dl = len(DL_GROUPS) * DL_HEADS_PER_GROUP * DL_HEAD_DIM
    o = [int(v) for v in np.cumsum([0, da, da, dav, dl, dl, dl])]
    parts = [w[:, o[6]:], w[:, :o[3]]]
    for g in range(len(DL_GROUPS)):
        for p in range(3):
            parts.append(w[:, o[3 + p] + g * COL_TILE: o[3 + p] + (g + 1) * COL_TILE])
    return jnp.concatenate(parts, axis=1)


def _gain_table(da_q_norm, da_k_norm, dl_q_norm, dl_k_norm):
    ones = jnp.ones((COL_TILE,), F32)
    daq = jnp.tile(da_q_norm, COL_TILE // DA_QK_DIM) * (DA_QK_DIM ** -0.5 * LOG2E)
    dak = jnp.tile(da_k_norm, COL_TILE // DA_QK_DIM)
    dlq = jnp.tile(dl_q_norm, COL_TILE // DL_HEAD_DIM) * (DL_HEAD_DIM ** -0.5 * LOG2E)
    dlk = jnp.tile(dl_k_norm, COL_TILE // DL_HEAD_DIM)
    rows = []
    for j in range(CT_END):
        if CT_DA_Q <= j < CT_DA_K:
            rows.append(daq)
        elif CT_DA_K <= j < CT_DA_V:
            rows.append(dak)
        elif j >= CT_DL and (j - CT_DL) % 3 == 0:
            rows.append(dlq)
        elif j >= CT_DL and (j - CT_DL) % 3 == 1:
            rows.append(dlk)
        else:
            rows.append(ones)
    return jnp.stack(rows, axis=0).reshape(CT_END, 1, COL_TILE)


def kernel(x, norm_mix, w_in, da_q_norm, da_k_norm, da_lambda_q, da_lambda_k, da_sub_norm,
           dl_q_norm, dl_k_norm, w_branch_a, w_branch_b, w_out, norm_ffn,
           w_group_router, w_expert_router, w_gate_up, w_down):
    B, S, D = x.shape
    T = B * S
    depth = w_in.shape[0]
    x2 = x.reshape(T, D)
    for l in range(depth):
        lam_init = 0.8 - 0.6 * math.exp(-0.3 * l)
        w_in_bf = _permute_w_in(w_in[l]).astype(BF16)
        gain_tab = _gain_table(da_q_norm[l], da_k_norm[l], dl_q_norm[l], dl_k_norm[l])
        proj, dl1, dl2 = _inproj(x2, norm_mix[l].reshape(1, D), w_in_bf, gain_tab, B, S)

        o_a = _diff_attention(proj, da_lambda_q[l], da_lambda_k[l], da_sub_norm[l].reshape(1, DA_V_DIM),
                              B, S, lam_init)
        dl = [_dilated_group(proj.reshape(B, S, proj.shape[1]), CT_DL, 0, B, S),
              _dilated_group(dl1, 0, 1, B, S), _dilated_group(dl2, 0, 2, B, S)]

        w_r = jnp.concatenate([w_expert_router[l], w_group_router[l]], axis=1)
        w_r = jnp.pad(w_r, ((0, 0), (0, LANES - w_r.shape[1])))
        r_hi = w_r.astype(BF16)
        r_lo = (w_r - r_hi.astype(F32)).astype(BF16)
        x1, hn, route = _outproj(
            x2, o_a, proj, [t[0] for t in dl], [t[1] for t in dl],
            w_branch_a[l].astype(BF16), w_branch_b[l].astype(BF16), w_out[l].astype(BF16),
            norm_ffn[l].reshape(1, D), r_hi, jnp.concatenate([r_hi, r_lo], axis=1))

        src_tok, dst_row, blk_expert, n_used, n_rows = _dispatch_tables(route, T)
        y = _experts(hn, src_tok, dst_row, blk_expert, n_used, w_gate_up[l], w_down[l], n_rows)
        x2 = _combine(x1, route, y)
    return x2.reshape(B, S, D)
```

```python
import functools
import math

import jax
import jax.numpy as jnp
import numpy as np
from jax import lax
from jax.experimental import pallas as pl
from jax.experimental.pallas import tpu as pltpu

F32 = jnp.float32
BF16 = jnp.bfloat16

EPS = 1e-6
LOG2E = 1.4426950408889634
NEG_BIG = -1e30

DA_HEADS = 8
DA_QK_DIM = 64
DA_V_DIM = 128
DA_TILE_GROUP = 4
DL_GROUPS = ((128, 1), (512, 4), (2048, 16))
DL_HEADS_PER_GROUP = 4
DL_HEAD_DIM = 128
DL_SPAN = 128
MOE_GROUPS = 4
MOE_EXPERTS_PER_GROUP = 8
MOE_N_EXPERTS = 32
MOE_TOP_K = 2
MOE_BLOCK = 256

LANES = 128
SLAB = 16
COL_TILE = 512
VMEM_LIMIT = 56 * 1024 * 1024

CT_GATE_A, CT_GATE_B, CT_DA_Q, CT_DA_K, CT_DA_V, CT_DL, CT_MAIN_END, CT_END = 0, 4, 8, 10, 12, 14, 17, 23


def _params(sem, vmem=VMEM_LIMIT):
    return pltpu.CompilerParams(dimension_semantics=sem, vmem_limit_bytes=vmem)


def _dot(a, b):
    return jnp.dot(a, b, preferred_element_type=F32)


def _dot_nt(a, b):
    return lax.dot_general(a, b, (((1,), (1,)), ((), ())), preferred_element_type=F32)


def _inproj_kernel(x_ref, g_ref, w_ref, gain_ref, o_ref, d1_ref, d2_ref, h_scr, y_scr):
    j = pl.program_id(1)

    @pl.when(j == 0)
    def _():
        x = x_ref[...]
        ms = jnp.mean(x * x, axis=-1, keepdims=True)
        h_scr[...] = (x * lax.rsqrt(ms + EPS) * g_ref[...]).astype(BF16)

    y = _dot(h_scr[...], w_ref[...])
    gain = gain_ref[...]
    heads = COL_TILE // LANES

    is64 = (j >= CT_DA_Q) & (j < CT_DA_V)
    is128 = (j >= CT_DL) & (lax.rem(j - CT_DL, 3) < 2)
    main = j < CT_MAIN_END

    def norm128(h):
        sl = slice(h * LANES, (h + 1) * LANES)
        yh = y[:, sl]
        ss = jnp.sum(yh * yh, axis=-1, keepdims=True)
        return yh * lax.rsqrt(ss * (1.0 / DL_HEAD_DIM) + EPS) * gain[:, sl]

    @pl.when(is64)
    def _():
        for h in range(heads):
            sl = slice(h * LANES, (h + 1) * LANES)
            yh = y[:, sl]
            sq = yh * yh
            lo = lax.broadcasted_iota(jnp.int32, yh.shape, 1) < DA_QK_DIM
            s_lo = jnp.sum(jnp.where(lo, sq, 0.0), axis=-1, keepdims=True)
            s_hi = jnp.sum(jnp.where(lo, 0.0, sq), axis=-1, keepdims=True)
            r = jnp.where(lo, lax.rsqrt(s_lo * (1.0 / DA_QK_DIM) + EPS),
                          lax.rsqrt(s_hi * (1.0 / DA_QK_DIM) + EPS))
            o_ref[:, sl] = (yh * r * gain[:, sl]).astype(o_ref.dtype)

    @pl.when(is128 & main)
    def _():
        for h in range(heads):
            o_ref[:, h * LANES:(h + 1) * LANES] = norm128(h).astype(o_ref.dtype)

    @pl.when(is128 & jnp.logical_not(main))
    def _():
        for h in range(heads):
            y_scr[h] = norm128(h)

    plain = jnp.logical_not(is64 | is128)

    @pl.when(plain & main)
    def _():
        o_ref[...] = y.astype(o_ref.dtype)

    @pl.when(plain & jnp.logical_not(main))
    def _():
        for h in range(heads):
            y_scr[h] = y[:, h * LANES:(h + 1) * LANES]

    def deinterleave(dst_ref):
        d, rows = dst_ref.shape[0], dst_ref.shape[1]
        for r in range(d):
            for h in range(heads):
                dst_ref[r, :, h * LANES:(h + 1) * LANES] = (
                    y_scr[h, pl.ds(r, rows, stride=d), :].astype(dst_ref.dtype))

    @pl.when((j >= CT_MAIN_END) & (j < CT_MAIN_END + 3))
    def _():
        deinterleave(d1_ref)

    @pl.when(j >= CT_MAIN_END + 3)
    def _():
        deinterleave(d2_ref)


def _inproj(x2, gain_mix, w_bf, gain_tab, B, S, tm=1024):
    T, D = x2.shape
    tiles_per_batch = S // tm
    d1, d2 = DL_GROUPS[1][1], DL_GROUPS[2][1]
    part1 = lambda j: jnp.clip(j - CT_MAIN_END, 0, 2)
    part2 = lambda j: jnp.clip(j - CT_MAIN_END - 3, 0, 2)
    return pl.pallas_call(
        _inproj_kernel,
        grid=(T // tm, CT_END),
        in_specs=[
            pl.BlockSpec((tm, D), lambda i, j: (i, 0)),
            pl.BlockSpec((1, D), lambda i, j: (0, 0)),
            pl.BlockSpec((D, COL_TILE), lambda i, j: (0, j)),
            pl.BlockSpec((None, 1, COL_TILE), lambda i, j: (j, 0, 0)),
        ],
        out_specs=[
            pl.BlockSpec((tm, COL_TILE), lambda i, j: (i, jnp.minimum(j, CT_MAIN_END - 1))),
            pl.BlockSpec((d1, tm // d1, COL_TILE),
                         lambda i, j: (i // tiles_per_batch, i % tiles_per_batch, part1(j))),
            pl.BlockSpec((d2, tm // d2, COL_TILE),
                         lambda i, j: (i // tiles_per_batch, i % tiles_per_batch, part2(j))),
        ],
        out_shape=[
            jax.ShapeDtypeStruct((T, CT_MAIN_END * COL_TILE), BF16),
            jax.ShapeDtypeStruct((B * d1, S // d1, 3 * COL_TILE), BF16),
            jax.ShapeDtypeStruct((B * d2, S // d2, 3 * COL_TILE), BF16),
        ],
        scratch_shapes=[pltpu.VMEM((tm, D), BF16), pltpu.VMEM((COL_TILE // LANES, tm, LANES), F32)],
        compiler_params=_params(("parallel", "arbitrary")),
    )(x2, gain_mix, w_bf, gain_tab)


def _bf16_pieces(x, n=3):
    out = []
    r = np.float64(x)
    for _ in range(n):
        p = np.asarray(np.float32(r)).astype(jnp.bfloat16).astype(np.float64)
        out.append(float(p))
        r = r - p
    return out


def _alibi_features(tk):
    pieces = _bf16_pieces(LOG2E)
    qf = np.zeros((2, LANES), np.float32)
    kf = np.zeros((2, tk, LANES), np.float32)
    j = np.arange(tk)
    hi, lo = (j // 16) * 16, j % 16
    for m in range(2):
        f0 = DA_QK_DIM if m == 0 else 0
        for n, p in enumerate(pieces):
            qf[m, f0 + 2 * n] = p
            qf[m, f0 + 2 * n + 1] = p
            kf[m, :, f0 + 2 * n] = hi
            kf[m, :, f0 + 2 * n + 1] = lo
    return jnp.asarray(qf), jnp.asarray(kf, dtype=BF16)


def _da_kernel(q_ref, k_ref, v_ref, qf_ref, kf_ref, lq_ref, lk_ref, sg_ref, o_ref,
               s00, s01, s10, s11, p00, p01, p10, p11,
               m0_scr, m1_scr, l0_scr, l1_scr, a0_scr, a1_scr, acc0_scr, acc1_scr, *, tq, rc, lam_init):
    h = pl.program_id(1)
    qi = pl.program_id(2)
    nlb = tq // LANES
    pow2 = jnp.exp2(-(h + 1).astype(F32))
    slope2 = pow2 * LOG2E

    q = q_ref[...]
    lane = lax.broadcasted_iota(jnp.int32, (tq, LANES), 1)
    own = (lane < DA_QK_DIM, lane >= DA_QK_DIM)
    qfs = [jnp.where(own[mi], q, jnp.broadcast_to((qf_ref[mi:mi + 1, :] * pow2).astype(BF16), q.shape))
           for mi in range(2)]

    m_scrs, l_scrs, a_scrs, acc_scrs = (m0_scr, m1_scr), (l0_scr, l1_scr), (a0_scr, a1_scr), (acc0_scr, acc1_scr)
    for mi in range(2):
        m_scrs[mi][...] = jnp.full(m_scrs[mi].shape, NEG_BIG, F32)
        l_scrs[mi][...] = jnp.zeros(l_scrs[mi].shape, F32)
        acc_scrs[mi][...] = jnp.zeros(acc_scrs[mi].shape, F32)

    def scores(ki, mi, s_ref):
        k = k_ref[pl.ds(pl.multiple_of(ki * tq, tq), tq), :]
        s_ref[...] = _dot_nt(qfs[mi], jnp.where(own[mi], k, kf_ref[mi]))

    def softmax(ki, mi, s_ref, p_ref, masked):
        m_scr, l_scr, a_scr = m_scrs[mi], l_scrs[mi], a_scrs[mi]
        c = slope2 * ((ki - qi) * tq).astype(F32)
        for r in range(tq // rc):
            rows = slice(r * rc, (r + 1) * rc)
            nb = min(nlb, ((r + 1) * rc - 1) // LANES + 1) if masked else nlb
            sb = []
            for j in range(nb):
                cs = slice(j * LANES, (j + 1) * LANES)
                s = s_ref[rows, cs]
                if masked and (j + 1) * LANES - 1 > r * rc:
                    rr = lax.broadcasted_iota(jnp.int32, (rc, LANES), 0) + r * rc
                    cc = lax.broadcasted_iota(jnp.int32, (rc, LANES), 1) + j * LANES
                    s = jnp.where(cc <= rr, s, NEG_BIG)
                sb.append(s)
            mx = sb[0]
            for s in sb[1:]:
                mx = jnp.maximum(mx, s)
            m_prev = m_scr[rows, :]
            m_new = jnp.maximum(m_prev, jnp.max(mx, axis=-1, keepdims=True) + c)
            alpha = jnp.exp2(m_prev - m_new)
            a_scr[rows, :] = alpha
            m_scr[rows, :] = m_new
            mc = m_new - c
            psum = alpha * l_scr[rows, :]
            for j in range(nlb):
                cs = slice(j * LANES, (j + 1) * LANES)
                if j < nb:
                    p = jnp.exp2(sb[j] - mc)
                    psum = psum + p
                    p_ref[rows, cs] = p.astype(BF16)
                else:
                    p_ref[rows, cs] = jnp.zeros((rc, LANES), BF16)
            l_scr[rows, :] = psum

    def values(ki, mi, p_ref):
        v = v_ref[pl.ds(pl.multiple_of(ki * tq, tq), tq), :]
        acc_scrs[mi][...] = a_scrs[mi][...] * acc_scrs[mi][...] + _dot(p_ref[...], v)

    s_bufs, p_bufs = ((s00, s01), (s10, s11)), ((p00, p01), (p10, p11))

    def tile_group(k0, n, last_masked):
        for mi in range(2):
            scores(k0, mi, s_bufs[0][mi])
        for i in range(n):
            masked = last_masked and i == n - 1
            for mi in range(2):
                softmax(k0 + i, mi, s_bufs[i % 2][mi], p_bufs[i % 2][mi], masked)
                values(k0 + i, mi, p_bufs[i % 2][mi])
                if i + 1 < n:
                    scores(k0 + i + 1, mi, s_bufs[(i + 1) % 2][mi])

    def body(t, carry):
        tile_group(DA_TILE_GROUP * t, DA_TILE_GROUP, False)
        return carry

    n_full = qi // DA_TILE_GROUP
    lax.fori_loop(0, n_full, body, 0)
    for rem in range(1, DA_TILE_GROUP + 1):
        @pl.when(qi - DA_TILE_GROUP * n_full == rem - 1)
        def _(rem=rem):
            tile_group(qi - (rem - 1), rem, True)

    lam_e = jnp.exp(jnp.sum(lq_ref[...] * lk_ref[...], axis=-1, keepdims=True))
    lam = lam_e[0:1, :] - lam_e[1:2, :] + lam_init
    l0 = jnp.sum(l0_scr[...], axis=-1, keepdims=True)
    l1 = jnp.sum(l1_scr[...], axis=-1, keepdims=True)
    o = acc0_scr[...] / l0 - lam * (acc1_scr[...] / l1)
    ms = jnp.mean(o * o, axis=-1, keepdims=True)
    o = o * lax.rsqrt(ms + EPS) * sg_ref[...] * (1.0 - lam_init)
    o_ref[...] = o.astype(o_ref.dtype)


def _diff_attention(proj, lam_q, lam_k, sub_gain, B, S, lam_init, tq=512, rc=32):
    T = proj.shape[0]
    nq = S // tq
    lb = LANES
    q_blk0, k_blk0, v_blk0 = (CT_DA_Q * COL_TILE) // lb, (CT_DA_K * COL_TILE) // lb, (CT_DA_V * COL_TILE) // lb
    qfeat, kfeat = _alibi_features(tq)
    const = lambda shape: pl.BlockSpec(shape, lambda b, h, i: (0,) * len(shape))
    return pl.pallas_call(
        functools.partial(_da_kernel, tq=tq, rc=rc, lam_init=lam_init),
        grid=(B, DA_HEADS, nq),
        in_specs=[
            pl.BlockSpec((tq, lb), lambda b, h, i: (b * nq + i, q_blk0 + h)),
            pl.BlockSpec((S, lb), lambda b, h, i: (b, k_blk0 + h)),
            pl.BlockSpec((S, lb), lambda b, h, i: (b, v_blk0 + h)),
            const((2, LANES)), const((2, tq, LANES)),
            const((2, DA_QK_DIM)), const((2, DA_QK_DIM)), const((1, DA_V_DIM)),
        ],
        out_specs=pl.BlockSpec((tq, lb), lambda b, h, i: (b * nq + i, h)),
        out_shape=jax.ShapeDtypeStruct((T, DA_HEADS * DA_V_DIM), BF16),
        scratch_shapes=[pltpu.VMEM((tq, tq), F32)] * 4 + [pltpu.VMEM((tq, tq), BF16)] * 4
        + [pltpu.VMEM((tq, LANES), F32)] * 6 + [pltpu.VMEM((tq, DA_V_DIM), F32)] * 2,
        compiler_params=_params(("parallel", "parallel", "arbitrary")),
    )(proj, proj, proj, qfeat, kfeat, lam_q, lam_k, sub_gain)


def _dl_kernel(q_ref, kc_ref, kp_ref, vc_ref, vp_ref, o_ref, lse_ref, *, slopes2, d, tq, ru):
    n = pl.program_id(1)
    sp = DL_SPAN
    row = lax.broadcasted_iota(jnp.int32, (sp, sp), 0)
    col = lax.broadcasted_iota(jnp.int32, (sp, sp), 1)
    dcur = row - col
    cur_ok = dcur >= 0
    prev_ok = dcur <= 0
    dcur_f = dcur.astype(F32)

    def scores(r, hh, j):
        hs = slice(hh * LANES, (hh + 1) * LANES)
        rs = slice(j * sp, (j + 1) * sp)
        q = q_ref[r, rs, hs]
        if j == 0:
            kp, vp, p_ok = kp_ref[r, :, hs], vp_ref[r, :, hs], prev_ok & (n > 0)
        else:
            ps = slice((j - 1) * sp, j * sp)
            kp, vp, p_ok = kc_ref[r, ps, hs], vc_ref[r, ps, hs], prev_ok
        s_c = jnp.where(cur_ok, _dot_nt(q, kc_ref[r, rs, hs]) - slopes2[hh] * dcur_f, NEG_BIG)
        s_p = jnp.where(p_ok, _dot_nt(q, kp) - slopes2[hh] * (dcur_f + float(sp)), NEG_BIG)
        return s_c, s_p, vc_ref[r, rs, hs], vp

    def finish(r, hh, j, s_c, s_p, vc, vp):
        m = jnp.max(jnp.maximum(s_c, s_p), axis=-1, keepdims=True)
        p_c = jnp.exp2(s_c - m)
        p_p = jnp.exp2(s_p - m)
        den = jnp.sum(p_c + p_p, axis=-1, keepdims=True)
        acc = _dot(p_c.astype(BF16), vc) + _dot(p_p.astype(BF16), vp)
        out_rows = pl.ds(j * sp, sp) if d == 1 else pl.ds(r + j * sp * d, sp, stride=d)
        o_ref[hh, out_rows, :] = acc / den
        lse_ref[hh, out_rows, :] = jnp.broadcast_to(m + jnp.log2(den), (sp, LANES))

    def residues(t, carry):
        units = [(t * ru + rr, hh, j) for rr in range(ru) for hh in range(DL_HEADS_PER_GROUP)
                 for j in range(tq // sp)]
        pending = []
        for u in units:
            pending.append((u, scores(*u)))
            if len(pending) > 2:
                u0, vals = pending.pop(0)
                finish(*u0, *vals)
        for u0, vals in pending:
            finish(*u0, *vals)
        return carry

    lax.fori_loop(0, d // ru, residues, 0)


def _dilated_group(src, col0, g, B, S, tok_per_step=2048):
    window, d = DL_GROUPS[g]
    assert window // d == DL_SPAN
    L = S // d
    tq = min(tok_per_step, S) // d
    assert tq % DL_SPAN == 0 and L % tq == 0
    nh = DL_HEADS_PER_GROUP * len(DL_GROUPS)
    slopes2 = tuple(2.0 ** (-8.0 * (g * DL_HEADS_PER_GROUP + hh + 1) / nh) * d * LOG2E
                    for hh in range(DL_HEADS_PER_GROUP))
    spb = tq // DL_SPAN
    nsteps = L // tq
    cur = lambda c: pl.BlockSpec((d, tq, COL_TILE), lambda b, n: (b, n, c))
    prev = lambda c: pl.BlockSpec((d, DL_SPAN, COL_TILE), lambda b, n: (b, jnp.maximum(n * spb - 1, 0), c))
    out_spec = pl.BlockSpec((DL_HEADS_PER_GROUP, d * tq, LANES), lambda b, n: (0, b * nsteps + n, 0))
    return pl.pallas_call(
        functools.partial(_dl_kernel, slopes2=slopes2, d=d, tq=tq, ru=min(d, 4)),
        grid=(B, nsteps),
        in_specs=[cur(col0), cur(col0 + 1), prev(col0 + 1), cur(col0 + 2), prev(col0 + 2)],
        out_specs=[out_spec, out_spec],
        out_shape=[jax.ShapeDtypeStruct((DL_HEADS_PER_GROUP, B * S, LANES), F32)] * 2,
        compiler_params=_params(("parallel", "arbitrary")),
    )(src, src, src, src, src)


def _route(logits):
    lane = lax.broadcasted_iota(jnp.int32, logits.shape, 1)
    big = jnp.int32(1 << 20)
    is_g = (lane >= MOE_N_EXPERTS) & (lane < MOE_N_EXPERTS + MOE_GROUPS)
    lg = jnp.where(is_g, logits, -jnp.inf)
    gmax = jnp.max(lg, axis=-1, keepdims=True)
    gsum = jnp.sum(jnp.exp(lg - gmax), axis=-1, keepdims=True)
    g_w = 1.0 / gsum
    g_idx = jnp.min(jnp.where(lg == gmax, lane - MOE_N_EXPERTS, big), axis=-1, keepdims=True)
    in_grp = (lane < MOE_N_EXPERTS) & ((lane // MOE_EXPERTS_PER_GROUP) == g_idx)
    le = jnp.where(in_grp, logits, -jnp.inf)
    t1 = jnp.max(le, axis=-1, keepdims=True)
    e1 = jnp.min(jnp.where(le == t1, lane, big), axis=-1, keepdims=True)
    le2 = jnp.where(lane == e1, -jnp.inf, le)
    t2 = jnp.max(le2, axis=-1, keepdims=True)
    e2 = jnp.min(jnp.where(le2 == t2, lane, big), axis=-1, keepdims=True)
    r = jnp.exp(t2 - t1)
    w1 = g_w / (1.0 + r)
    w2 = w1 * r
    out = jnp.where(lane == 0, e1.astype(F32),
                    jnp.where(lane == 1, e2.astype(F32),
                              jnp.where(lane == 2, w1, jnp.where(lane == 3, w2, 0.0))))
    return out


def _outproj_kernel(x_ref, oa_ref, ga_ref, gb_ref, o0_ref, o1_ref, o2_ref, l0_ref, l1_ref, l2_ref,
                    wa_ref, wb_ref, wo_ref, gf_ref, rh_ref, rc_ref,
                    x1_ref, hn_ref, rt_ref):
    obs = []
    for hh in range(DL_HEADS_PER_GROUP):
        l0, l1, l2 = l0_ref[hh], l1_ref[hh], l2_ref[hh]
        lm = jnp.maximum(jnp.maximum(l0, l1), l2)
        e0, e1, e2 = jnp.exp2(l0 - lm), jnp.exp2(l1 - lm), jnp.exp2(l2 - lm)
        obs.append((e0 * o0_ref[hh] + e1 * o1_ref[hh] + e2 * o2_ref[hh]) / (e0 + e1 + e2))
    ob = jnp.concatenate(obs, axis=1)
    a = _dot(oa_ref[...], wa_ref[...])
    b = _dot(ob.astype(BF16), wb_ref[...])
    mixed = jax.nn.sigmoid(ga_ref[...].astype(F32)) * a + jax.nn.sigmoid(gb_ref[...].astype(F32)) * b
    x1 = x_ref[...] + _dot(mixed.astype(BF16), wo_ref[...])
    x1_ref[...] = x1
    ms = jnp.mean(x1 * x1, axis=-1, keepdims=True)
    hn = x1 * lax.rsqrt(ms + EPS) * gf_ref[...]
    for c in range(SLAB):
        hn_ref[pl.ds(c, hn.shape[0], stride=SLAB), :] = hn[:, c * LANES:(c + 1) * LANES]
    hn_hi = hn.astype(BF16)
    hn_lo = (hn - hn_hi.astype(F32)).astype(BF16)
    t = _dot(hn_hi, rc_ref[...])
    logits = t[:, 0:LANES] + (_dot(hn_lo, rh_ref[...]) + t[:, LANES:2 * LANES])
    rt_ref[...] = _route(logits)


def _outproj(x2, o_a, proj, dl_o, dl_lse, wa, wb, wo, gain_ffn, r_hi, r_cat, tm=256):
    T, D = x2.shape
    row = lambda w: pl.BlockSpec((tm, w), lambda i: (i, 0))
    full = lambda s: pl.BlockSpec(s, lambda i: (0, 0), pipeline_mode=pl.Buffered(1))
    hrow = pl.BlockSpec((DL_HEADS_PER_GROUP, tm, LANES), lambda i: (0, i, 0))
    return pl.pallas_call(
        _outproj_kernel,
        grid=(T // tm,),
        in_specs=[
            row(D), row(o_a.shape[1]),
            pl.BlockSpec((tm, D), lambda i: (i, (CT_GATE_A * COL_TILE) // D)),
            pl.BlockSpec((tm, D), lambda i: (i, (CT_GATE_B * COL_TILE) // D)),
            hrow, hrow, hrow, hrow, hrow, hrow,
            full(wa.shape), full(wb.shape), full(wo.shape), full((1, D)), full(r_hi.shape), full(r_cat.shape),
        ],
        out_specs=[row(D), pl.BlockSpec((tm * SLAB, LANES), lambda i: (i, 0)), row(LANES)],
        out_shape=[jax.ShapeDtypeStruct((T, D), F32), jax.ShapeDtypeStruct((T * SLAB, LANES), F32),
                   jax.ShapeDtypeStruct((T, LANES), F32)],
        compiler_params=_params(("parallel",)),
    )(x2, o_a, proj, proj, dl_o[0], dl_o[1], dl_o[2], dl_lse[0], dl_lse[1], dl_lse[2],
      wa, wb, wo, gain_ffn, r_hi, r_cat)


def _cast_rows(src_ref, dst_ref, chunk=256):
    def body(c, carry):
        r0 = pl.multiple_of(c * chunk, chunk)
        dst_ref[pl.ds(r0, chunk), :] = src_ref[pl.ds(r0, chunk), :].astype(dst_ref.dtype)
        return carry
    lax.fori_loop(0, src_ref.shape[0] // chunk, body, 0)


def _expert_changed(be_ref, i):
    return (i == 0) | (be_ref[i] != be_ref[jnp.maximum(i - 1, 0)])


def _moe_up_kernel(be_ref, nu_ref, tc_ref, tn_ref, hn_ref, wgu_ref, act_ref, xa, xb, wbf, gsem, *, d_ff):
    i = pl.program_id(0)
    nu = nu_ref[0]
    rows = xa.shape[0] // SLAB
    even = i % 2 == 0

    def gather(tok_ref, r, buf, s):
        src = hn_ref.at[pl.ds(pl.multiple_of(tok_ref[0, r] * SLAB, SLAB), SLAB)]
        return pltpu.make_async_copy(src, buf.at[pl.ds(r * SLAB, SLAB)], gsem.at[s])

    def wait_gather(buf, s):
        pltpu.make_async_copy(hn_ref.at[pl.ds(0, rows * SLAB)], buf, gsem.at[s]).wait()

    @pl.when((i == 0) & (nu > 0))
    def _():
        def body(r, c):
            gather(tc_ref, r, xa, 0).start()
            return c
        lax.fori_loop(0, rows, body, 0)

    @pl.when((i < nu) & _expert_changed(be_ref, i))
    def _():
        _cast_rows(wgu_ref, wbf)

    def live_step(cur, nxt, s):
        wait_gather(cur, s)
        for r in range(rows):
            gather(tn_ref, r, nxt, 1 - s).start()
        x = jnp.concatenate([cur[pl.ds(c, rows, stride=SLAB), :] for c in range(SLAB)], axis=1)
        h = _dot(x.astype(BF16), wbf[...])
        gate = h[:, :d_ff]
        up = h[:, d_ff:]
        act_ref[...] = (gate * jax.nn.sigmoid(gate) * up).astype(act_ref.dtype)

    @pl.when((i < nu) & even)
    def _():
        live_step(xa, xb, 0)

    @pl.when((i < nu) & jnp.logical_not(even))
    def _():
        live_step(xb, xa, 1)

    @pl.when((i == nu) & (nu > 0) & even)
    def _():
        wait_gather(xa, 0)

    @pl.when((i == nu) & (nu > 0) & jnp.logical_not(even))
    def _():
        wait_gather(xb, 1)

    @pl.when(i >= nu)
    def _():
        act_ref[...] = jnp.zeros(act_ref.shape, act_ref.dtype)


def _moe_down_kernel(be_ref, nu_ref, dp_ref, act_ref, wd_ref, y_ref, ya, yb, wbf, ssem):
    i = pl.program_id(0)
    nu = nu_ref[0]
    rows = ya.shape[0] // SLAB
    even = i % 2 == 0

    def scatter(r, buf, s):
        dst = y_ref.at[pl.ds(pl.multiple_of(dp_ref[0, r] * SLAB, SLAB), SLAB)]
        return pltpu.make_async_copy(buf.at[pl.ds(r * SLAB, SLAB)], dst, ssem.at[s])

    def wait_scatter(buf, s):
        pltpu.make_async_copy(buf, y_ref.at[pl.ds(0, rows * SLAB)], ssem.at[s]).wait()

    @pl.when(i == 0)
    def _():
        n_res = y_ref.shape[0] // SLAB - 2 * rows
        for s, buf in enumerate((ya, yb)):
            buf[...] = jnp.zeros(buf.shape, buf.dtype)
            pltpu.make_async_copy(buf, y_ref.at[pl.ds((n_res + s * rows) * SLAB, rows * SLAB)], ssem.at[s]).start()
        for s, buf in enumerate((ya, yb)):
            pltpu.make_async_copy(buf, y_ref.at[pl.ds((n_res + s * rows) * SLAB, rows * SLAB)], ssem.at[s]).wait()

    @pl.when((i >= 2) & (i < nu + 2) & even)
    def _():
        wait_scatter(ya, 0)

    @pl.when((i >= 2) & (i < nu + 2) & jnp.logical_not(even))
    def _():
        wait_scatter(yb, 1)

    @pl.when((i < nu) & _expert_changed(be_ref, i))
    def _():
        _cast_rows(wd_ref, wbf)

    def step(cur, prv, s, do_scatter, do_compute):
        if do_scatter:
            for r in range(rows):
                scatter(r, prv, 1 - s).start()
        if do_compute:
            yv = _dot(act_ref[...], wbf[...])
            for c in range(SLAB):
                cur[pl.ds(c, rows, stride=SLAB), :] = yv[:, c * LANES:(c + 1) * LANES]

    for s, (cur, prv) in enumerate(((ya, yb), (yb, ya))):
        par = even if s == 0 else jnp.logical_not(even)

        @pl.when((i >= 1) & (i < nu) & par)
        def _(cur=cur, prv=prv, s=s):
            step(cur, prv, s, True, True)

        @pl.when((i == 0) & (nu > 0) & par)
        def _(cur=cur, prv=prv, s=s):
            step(cur, prv, s, False, True)

        @pl.when((i == nu) & (nu > 0) & par)
        def _(cur=cur, prv=prv, s=s):
            step(cur, prv, s, True, False)


def _experts(hn, src_tok, dst_row, blk_expert, n_used, wgu, wd, n_out_rows):
    D = wgu.shape[1]
    nblk = src_tok.shape[0] // MOE_BLOCK
    d_ff = wd.shape[1]
    slab_rows = MOE_BLOCK * SLAB
    tok = src_tok.reshape(nblk, 1, MOE_BLOCK)
    dst = dst_row.reshape(nblk, 1, MOE_BLOCK)

    def live(i, nu):
        return jnp.maximum(jnp.minimum(i, nu[0] - 1), 0)

    smem = lambda f: pl.BlockSpec((None, 1, MOE_BLOCK), f, memory_space=pltpu.SMEM)
    act = pl.pallas_call(
        functools.partial(_moe_up_kernel, d_ff=d_ff),
        grid_spec=pltpu.PrefetchScalarGridSpec(
            num_scalar_prefetch=2,
            grid=(nblk - 1,),
            in_specs=[
                smem(lambda i, be, nu: (i, 0, 0)),
                smem(lambda i, be, nu: (i + 1, 0, 0)),
                pl.BlockSpec(memory_space=pl.ANY),
                pl.BlockSpec((None, D, 2 * d_ff), lambda i, be, nu: (be[live(i, nu)], 0, 0)),
            ],
            out_specs=pl.BlockSpec((MOE_BLOCK, d_ff), lambda i, be, nu: (i, 0)),
            scratch_shapes=[pltpu.VMEM((slab_rows, LANES), F32), pltpu.VMEM((slab_rows, LANES), F32),
                            pltpu.VMEM((D, 2 * d_ff), BF16), pltpu.SemaphoreType.DMA((2,))],
        ),
        out_shape=jax.ShapeDtypeStruct(((nblk - 1) * MOE_BLOCK, d_ff), BF16),
        compiler_params=_params(("arbitrary",)),
    )(blk_expert, n_used, tok, tok, hn, wgu)
    return pl.pallas_call(
        _moe_down_kernel,
        grid_spec=pltpu.PrefetchScalarGridSpec(
            num_scalar_prefetch=2,
            grid=(nblk,),
            in_specs=[
                smem(lambda i, be, nu: (jnp.maximum(i - 1, 0), 0, 0)),
                pl.BlockSpec((MOE_BLOCK, d_ff), lambda i, be, nu: (live(i, nu), 0)),
                pl.BlockSpec((None, d_ff, D), lambda i, be, nu: (be[live(i, nu)], 0, 0)),
            ],
            out_specs=pl.BlockSpec(memory_space=pl.ANY),
            scratch_shapes=[pltpu.VMEM((slab_rows, LANES), F32), pltpu.VMEM((slab_rows, LANES), F32),
                            pltpu.VMEM((d_ff, D), BF16), pltpu.SemaphoreType.DMA((2,))],
        ),
        out_shape=jax.ShapeDtypeStruct((n_out_rows * SLAB, LANES), F32),
        compiler_params=_params(("arbitrary",)),
    )(blk_expert, n_used, dst, act, wd)


def _combine_kernel(x1_ref, rt_ref, y1_ref, y2_ref, o_ref):
    rt = rt_ref[...]
    w1, w2 = rt[:, 2:3], rt[:, 3:4]
    tm = o_ref.shape[0]
    for c in range(SLAB):
        cs = slice(c * LANES, (c + 1) * LANES)
        y1 = y1_ref[pl.ds(c, tm, stride=SLAB), :]
        y2 = y2_ref[pl.ds(c, tm, stride=SLAB), :]
        o_ref[:, cs] = x1_ref[:, cs] + (w1 * y1 + w2 * y2)


def _combine(x1, route, y, tm=512):
    T, D = x1.shape
    nt = T // tm
    return pl.pallas_call(
        _combine_kernel,
        grid=(nt,),
        in_specs=[
            pl.BlockSpec((tm, D), lambda i: (i, 0)),
            pl.BlockSpec((tm, LANES), lambda i: (i, 0)),
            pl.BlockSpec((tm * SLAB, LANES), lambda i: (i, 0)),
            pl.BlockSpec((tm * SLAB, LANES), lambda i: (nt + i, 0)),
        ],
        out_specs=pl.BlockSpec((tm, D), lambda i: (i, 0)),
        out_shape=jax.ShapeDtypeStruct((T, D), F32),
        compiler_params=_params(("parallel",)),
    )(x1, route, y, y)


def _dispatch_tables(route, T):
    TK = T * MOE_TOP_K
    flat_e = route[:, :MOE_TOP_K].astype(jnp.int32).reshape(-1)
    order = jnp.argsort(flat_e).astype(jnp.int32)
    sizes = jnp.sum(flat_e[:, None] == jnp.arange(MOE_N_EXPERTS, dtype=jnp.int32)[None, :], axis=0,
                    dtype=jnp.int32)
    start = jnp.cumsum(sizes) - sizes
    padded = ((sizes + MOE_BLOCK - 1) // MOE_BLOCK) * MOE_BLOCK
    pad_end = jnp.cumsum(padded)
    pad_start = pad_end - padded
    n_blocks = TK // MOE_BLOCK + MOE_N_EXPERTS
    blk = jnp.arange(n_blocks + 2, dtype=jnp.int32)
    blk_expert = jnp.minimum(jnp.sum(pad_end[None, :] <= (blk * MOE_BLOCK)[:, None], axis=1, dtype=jnp.int32),
                             MOE_N_EXPERTS - 1)
    r = jnp.arange(MOE_BLOCK, dtype=jnp.int32)[None, :]
    pos = blk[:, None] * MOE_BLOCK + r
    off = pos - pad_start[blk_expert][:, None]
    live = (off < sizes[blk_expert][:, None]) & (pos < pad_end[-1])
    sorted_idx = jnp.where(live, off + start[blk_expert][:, None], 0)
    slot = order[sorted_idx]
    tok = slot // MOE_TOP_K
    src_tok = jnp.where(live, tok, 0).reshape(-1)
    trash = TK + (blk[:, None] % 2) * MOE_BLOCK + r
    dst_row = jnp.where(live, (slot % MOE_TOP_K) * T + tok, trash).reshape(-1)
    n_used = (pad_end[-1:] // MOE_BLOCK).astype(jnp.int32)
    return src_tok, dst_row, blk_expert, n_used, TK + 2 * MOE_BLOCK


def _permute_w_in(w):
    da = DA_HEADS * 2 * DA_QK_DIM
    dav = DA_HEADS * DA_V_DIM
    dl = len(DL_GROUPS) * DL_HEADS_PER_GROUP * DL_HEAD_DIM
    o = [int(v) for v in np.cumsum([0, da, da, dav, dl, dl, dl])]
    parts = [w[:, o[6]:], w[:, :o[3]]]
    for g in range(len(DL_GROUPS)):
        for p in range(3):
            parts.append(w[:, o[3 + p] + g * COL_TILE: o[3 + p] + (g + 1) * COL_TILE])
    return jnp.concatenate(parts, axis=1)


def _gain_table(da_q_norm, da_k_norm, dl_q_norm, dl_k_norm):
    ones = jnp.ones((COL_TILE,), F32)
    daq = jnp.tile(da_q_norm, COL_TILE // DA_QK_DIM) * (DA_QK_DIM ** -0.5 * LOG2E)
    dak = jnp.tile(da_k_norm, COL_TILE // DA_QK_DIM)
    dlq = jnp.tile(dl_q_norm, COL_TILE // DL_HEAD_DIM) * (DL_HEAD_DIM ** -0.5 * LOG2E)
    dlk = jnp.tile(dl_k_norm, COL_TILE // DL_HEAD_DIM)
    rows = []
    for j in range(CT_END):
        if CT_DA_Q <= j < CT_DA_K:
            rows.append(daq)
        elif CT_DA_K <= j < CT_DA_V:
            rows.append(dak)
        elif j >= CT_DL and (j - CT_DL) % 3 == 0:
            rows.append(dlq)
        elif j >= CT_DL and (j - CT_DL) % 3 == 1:
            rows.append(dlk)
        else:
            rows.append(ones)
    return jnp.stack(rows, axis=0).reshape(CT_END, 1, COL_TILE)


def kernel(x, norm_mix, w_in, da_q_norm, da_k_norm, da_lambda_q, da_lambda_k, da_sub_norm,
           dl_q_norm, dl_k_norm, w_branch_a, w_branch_b, w_out, norm_ffn,
           w_group_router, w_expert_router, w_gate_up, w_down):
    B, S, D = x.shape
    T = B * S
    depth = w_in.shape[0]
    x2 = x.reshape(T, D)
    for l in range(depth):
        lam_init = 0.8 - 0.6 * math.exp(-0.3 * l)
        w_in_bf = _permute_w_in(w_in[l]).astype(BF16)
        gain_tab = _gain_table(da_q_norm[l], da_k_norm[l], dl_q_norm[l], dl_k_norm[l])
        proj, dl1, dl2 = _inproj(x2, norm_mix[l].reshape(1, D), w_in_bf, gain_tab, B, S)

        o_a = _diff_attention(proj, da_lambda_q[l], da_lambda_k[l], da_sub_norm[l].reshape(1, DA_V_DIM),
                              B, S, lam_init)
        dl = [_dilated_group(proj.reshape(B, S, proj.shape[1]), CT_DL, 0, B, S),
              _dilated_group(dl1, 0, 1, B, S), _dilated_group(dl2, 0, 2, B, S)]

        w_r = jnp.concatenate([w_expert_router[l], w_group_router[l]], axis=1)
        w_r = jnp.pad(w_r, ((0, 0), (0, LANES - w_r.shape[1])))
        r_hi = w_r.astype(BF16)
        r_lo = (w_r - r_hi.astype(F32)).astype(BF16)
        x1, hn, route = _outproj(
            x2, o_a, proj, [t[0] for t in dl], [t[1] for t in dl],
            w_branch_a[l].astype(BF16), w_branch_b[l].astype(BF16), w_out[l].astype(BF16),
            norm_ffn[l].reshape(1, D), r_hi, jnp.concatenate([r_hi, r_lo], axis=1))

        src_tok, dst_row, blk_expert, n_used, n_rows = _dispatch_tables(route, T)
        y = _experts(hn, src_tok, dst_row, blk_expert, n_used, w_gate_up[l], w_down[l], n_rows)
        x2 = _combine(x1, route, y)
    return x2.reshape(B, S, D)
```

```python
import functools
import math

import jax
import jax.numpy as jnp
import numpy as np
from jax import lax
from jax.experimental import pallas as pl
from jax.experimental.pallas import tpu as pltpu

F32 = jnp.float32
BF16 = jnp.bfloat16

EPS = 1e-6
LOG2E = 1.4426950408889634
NEG_BIG = -1e30

DA_HEADS = 8
DA_QK_DIM = 64
DA_V_DIM = 128
DA_TILE_GROUP = 4
DL_GROUPS = ((128, 1), (512, 4), (2048, 16))
DL_HEADS_PER_GROUP = 4
DL_HEAD_DIM = 128
DL_SPAN = 128
MOE_GROUPS = 4
MOE_EXPERTS_PER_GROUP = 8
MOE_N_EXPERTS = 32
MOE_TOP_K = 2
MOE_BLOCK = 256
ROW_DMA_PRIORITY = 1

LANES = 128
COL_TILE = 512
VMEM_LIMIT = 56 * 1024 * 1024

CT_GATE_A, CT_GATE_B, CT_DA_Q, CT_DA_K, CT_DA_V, CT_DL, CT_MAIN_END, CT_END = 0, 4, 8, 10, 12, 14, 17, 23


def _params(sem, vmem=VMEM_LIMIT):
    return pltpu.CompilerParams(dimension_semantics=sem, vmem_limit_bytes=vmem)


def _dot(a, b):
    return jnp.dot(a, b, preferred_element_type=F32)


def _dot_nt(a, b):
    return lax.dot_general(a, b, (((1,), (1,)), ((), ())), preferred_element_type=F32)


def _inproj_kernel(x_ref, g_ref, w_ref, gain_ref, o_ref, d1_ref, d2_ref, h_scr, y_scr):
    j = pl.program_id(1)

    @pl.when(j == 0)
    def _():
        x = x_ref[...]
        ms = jnp.mean(x * x, axis=-1, keepdims=True)
        h_scr[...] = (x * lax.rsqrt(ms + EPS) * g_ref[...]).astype(BF16)

    y = _dot(h_scr[...], w_ref[...])
    gain = gain_ref[...]
    heads = COL_TILE // LANES

    is64 = (j >= CT_DA_Q) & (j < CT_DA_V)
    is128 = (j >= CT_DL) & (lax.rem(j - CT_DL, 3) < 2)
    main = j < CT_MAIN_END

    def norm128(h):
        sl = slice(h * LANES, (h + 1) * LANES)
        yh = y[:, sl]
        ss = jnp.sum(yh * yh, axis=-1, keepdims=True)
        return yh * lax.rsqrt(ss * (1.0 / DL_HEAD_DIM) + EPS) * gain[:, sl]

    @pl.when(is64)
    def _():
        for h in range(heads):
            sl = slice(h * LANES, (h + 1) * LANES)
            yh = y[:, sl]
            sq = yh * yh
            lo = lax.broadcasted_iota(jnp.int32, yh.shape, 1) < DA_QK_DIM
            s_lo = jnp.sum(jnp.where(lo, sq, 0.0), axis=-1, keepdims=True)
            s_hi = jnp.sum(jnp.where(lo, 0.0, sq), axis=-1, keepdims=True)
            r = jnp.where(lo, lax.rsqrt(s_lo * (1.0 / DA_QK_DIM) + EPS),
                          lax.rsqrt(s_hi * (1.0 / DA_QK_DIM) + EPS))
            o_ref[:, sl] = (yh * r * gain[:, sl]).astype(o_ref.dtype)

    @pl.when(is128 & main)
    def _():
        for h in range(heads):
            o_ref[:, h * LANES:(h + 1) * LANES] = norm128(h).astype(o_ref.dtype)

    @pl.when(is128 & jnp.logical_not(main))
    def _():
        for h in range(heads):
            y_scr[h] = norm128(h)

    plain = jnp.logical_not(is64 | is128)

    @pl.when(plain & main)
    def _():
        o_ref[...] = y.astype(o_ref.dtype)

    @pl.when(plain & jnp.logical_not(main))
    def _():
        for h in range(heads):
            y_scr[h] = y[:, h * LANES:(h + 1) * LANES]

    def deinterleave(dst_ref):
        d, rows = dst_ref.shape[0], dst_ref.shape[1]
        for r in range(d):
            for h in range(heads):
                dst_ref[r, :, h * LANES:(h + 1) * LANES] = (
                    y_scr[h, pl.ds(r, rows, stride=d), :].astype(dst_ref.dtype))

    @pl.when((j >= CT_MAIN_END) & (j < CT_MAIN_END + 3))
    def _():
        deinterleave(d1_ref)

    @pl.when(j >= CT_MAIN_END + 3)
    def _():
        deinterleave(d2_ref)


def _inproj(x2, gain_mix, w_bf, gain_tab, B, S, tm=1024):
    T, D = x2.shape
    tiles_per_batch = S // tm
    d1, d2 = DL_GROUPS[1][1], DL_GROUPS[2][1]
    part1 = lambda j: jnp.clip(j - CT_MAIN_END, 0, 2)
    part2 = lambda j: jnp.clip(j - CT_MAIN_END - 3, 0, 2)
    return pl.pallas_call(
        _inproj_kernel,
        grid=(T // tm, CT_END),
        in_specs=[
            pl.BlockSpec((tm, D), lambda i, j: (i, 0)),
            pl.BlockSpec((1, D), lambda i, j: (0, 0)),
            pl.BlockSpec((D, COL_TILE), lambda i, j: (0, j)),
            pl.BlockSpec((None, 1, COL_TILE), lambda i, j: (j, 0, 0)),
        ],
        out_specs=[
            pl.BlockSpec((tm, COL_TILE), lambda i, j: (i, jnp.minimum(j, CT_MAIN_END - 1))),
            pl.BlockSpec((d1, tm // d1, COL_TILE),
                         lambda i, j: (i // tiles_per_batch, i % tiles_per_batch, part1(j))),
            pl.BlockSpec((d2, tm // d2, COL_TILE),
                         lambda i, j: (i // tiles_per_batch, i % tiles_per_batch, part2(j))),
        ],
        out_shape=[
            jax.ShapeDtypeStruct((T, CT_MAIN_END * COL_TILE), BF16),
            jax.ShapeDtypeStruct((B * d1, S // d1, 3 * COL_TILE), BF16),
            jax.ShapeDtypeStruct((B * d2, S // d2, 3 * COL_TILE), BF16),
        ],
        scratch_shapes=[pltpu.VMEM((tm, D), BF16), pltpu.VMEM((COL_TILE // LANES, tm, LANES), F32)],
        compiler_params=_params(("parallel", "arbitrary")),
    )(x2, gain_mix, w_bf, gain_tab)


def _bf16_pieces(x, n=3):
    out = []
    r = np.float64(x)
    for _ in range(n):
        p = np.asarray(np.float32(r)).astype(jnp.bfloat16).astype(np.float64)
        out.append(float(p))
        r = r - p
    return out


def _alibi_features(tk):
    pieces = _bf16_pieces(LOG2E)
    qf = np.zeros((2, LANES), np.float32)
    kf = np.zeros((2, tk, LANES), np.float32)
    j = np.arange(tk)
    hi, lo = (j // 16) * 16, j % 16
    for m in range(2):
        f0 = DA_QK_DIM if m == 0 else 0
        for n, p in enumerate(pieces):
            qf[m, f0 + 2 * n] = p
            qf[m, f0 + 2 * n + 1] = p
            kf[m, :, f0 + 2 * n] = hi
            kf[m, :, f0 + 2 * n + 1] = lo
    return jnp.asarray(qf), jnp.asarray(kf, dtype=BF16)


def _da_kernel(q_ref, k_ref, v_ref, qf_ref, kf_ref, lq_ref, lk_ref, sg_ref, o_ref,
               s00, s01, s10, s11, p00, p01, p10, p11, pd0, pd1,
               m0_scr, m1_scr, l0_scr, l1_scr, a0_scr, a1_scr, acc0_scr, acc1_scr, *, tq, rc, lam_init):
    h = pl.program_id(1)
    qi = pl.program_id(2)
    nlb = tq // LANES
    pow2 = jnp.exp2(-(h + 1).astype(F32))
    slope2 = pow2 * LOG2E

    q = q_ref[...]
    lane = lax.broadcasted_iota(jnp.int32, (tq, LANES), 1)
    own = (lane < DA_QK_DIM, lane >= DA_QK_DIM)
    qfs = [jnp.where(own[mi], q, jnp.broadcast_to((qf_ref[mi:mi + 1, :] * pow2).astype(BF16), q.shape))
           for mi in range(2)]

    m_scrs, l_scrs, a_scrs, acc_scrs = (m0_scr, m1_scr), (l0_scr, l1_scr), (a0_scr, a1_scr), (acc0_scr, acc1_scr)
    for mi in range(2):
        m_scrs[mi][...] = jnp.full(m_scrs[mi].shape, NEG_BIG, F32)
        l_scrs[mi][...] = jnp.zeros(l_scrs[mi].shape, F32)
        acc_scrs[mi][...] = jnp.zeros(acc_scrs[mi].shape, F32)

    def scores(ki, mi, s_ref):
        k = k_ref[pl.ds(pl.multiple_of(ki * tq, tq), tq), :]
        s_ref[...] = _dot_nt(qfs[mi], jnp.where(own[mi], k, kf_ref[mi]))

    def softmax(ki, mi, s_ref, p_ref, masked):
        m_scr, l_scr, a_scr = m_scrs[mi], l_scrs[mi], a_scrs[mi]
        c = slope2 * ((ki - qi) * tq).astype(F32)
        for r in range(tq // rc):
            rows = slice(r * rc, (r + 1) * rc)
            nb = min(nlb, ((r + 1) * rc - 1) // LANES + 1) if masked else nlb
            sb = []
            for j in range(nb):
                cs = slice(j * LANES, (j + 1) * LANES)
                s = s_ref[rows, cs]
                if masked and (j + 1) * LANES - 1 > r * rc:
                    rr = lax.broadcasted_iota(jnp.int32, (rc, LANES), 0) + r * rc
                    cc = lax.broadcasted_iota(jnp.int32, (rc, LANES), 1) + j * LANES
                    s = jnp.where(cc <= rr, s, NEG_BIG)
                sb.append(s)
            mx = sb[0]
            for s in sb[1:]:
                mx = jnp.maximum(mx, s)
            m_prev = m_scr[rows, :]
            m_new = jnp.maximum(m_prev, jnp.max(mx, axis=-1, keepdims=True) + c)
            alpha = jnp.exp2(m_prev - m_new)
            a_scr[rows, :] = alpha
            m_scr[rows, :] = m_new
            mc = m_new - c
            psum = alpha * l_scr[rows, :]
            for j in range(nlb):
                cs = slice(j * LANES, (j + 1) * LANES)
                if j < nb:
                    p = jnp.exp2(sb[j] - mc)
                    psum = psum + p
                    p_ref[rows, cs] = p.astype(BF16)
                else:
                    p_ref[rows, cs] = jnp.zeros((rc, LANES), BF16)
            l_scr[rows, :] = psum

    def values(ki, mi, p_ref):
        v = v_ref[pl.ds(pl.multiple_of(ki * tq, tq), tq), :]
        acc_scrs[mi][...] = a_scrs[mi][...] * acc_scrs[mi][...] + _dot(p_ref[...], v)

    s_bufs, p_bufs = ((s00, s01), (s10, s11)), ((p00, p01), (p10, p11))

    def tile_group(k0, n, last_masked):
        for mi in range(2):
            scores(k0, mi, s_bufs[0][mi])
        for i in range(n):
            masked = last_masked and i == n - 1
            for mi in range(2):
                p_ref = (pd0, pd1)[mi] if masked else p_bufs[i % 2][mi]
                softmax(k0 + i, mi, s_bufs[i % 2][mi], p_ref, masked)
                values(k0 + i, mi, p_ref)
                if i + 1 < n:
                    scores(k0 + i + 1, mi, s_bufs[(i + 1) % 2][mi])

    def body(t, carry):
        tile_group(DA_TILE_GROUP * t, DA_TILE_GROUP, False)
        return carry

    n_full = qi // DA_TILE_GROUP
    lax.fori_loop(0, n_full, body, 0)
    for rem in range(1, DA_TILE_GROUP + 1):
        @pl.when(qi - DA_TILE_GROUP * n_full == rem - 1)
        def _(rem=rem):
            tile_group(qi - (rem - 1), rem, True)

    lam_e = jnp.exp(jnp.sum(lq_ref[...] * lk_ref[...], axis=-1, keepdims=True))
    lam = lam_e[0:1, :] - lam_e[1:2, :] + lam_init
    l0 = jnp.sum(l0_scr[...], axis=-1, keepdims=True)
    l1 = jnp.sum(l1_scr[...], axis=-1, keepdims=True)
    o = acc0_scr[...] / l0 - lam * (acc1_scr[...] / l1)
    ms = jnp.mean(o * o, axis=-1, keepdims=True)
    o = o * lax.rsqrt(ms + EPS) * sg_ref[...] * (1.0 - lam_init)
    o_ref[...] = o.astype(o_ref.dtype)


def _diff_attention(proj, lam_q, lam_k, sub_gain, B, S, lam_init, tq=512, rc=32):
    T = proj.shape[0]
    nq = S // tq
    lb = LANES
    q_blk0, k_blk0, v_blk0 = (CT_DA_Q * COL_TILE) // lb, (CT_DA_K * COL_TILE) // lb, (CT_DA_V * COL_TILE) // lb
    qfeat, kfeat = _alibi_features(tq)
    const = lambda shape: pl.BlockSpec(shape, lambda b, h, i: (0,) * len(shape))
    return pl.pallas_call(
        functools.partial(_da_kernel, tq=tq, rc=rc, lam_init=lam_init),
        grid=(B, DA_HEADS, nq),
        in_specs=[
            pl.BlockSpec((tq, lb), lambda b, h, i: (b * nq + i, q_blk0 + h)),
            pl.BlockSpec((S, lb), lambda b, h, i: (b, k_blk0 + h)),
            pl.BlockSpec((S, lb), lambda b, h, i: (b, v_blk0 + h)),
            const((2, LANES)), const((2, tq, LANES)),
            const((2, DA_QK_DIM)), const((2, DA_QK_DIM)), const((1, DA_V_DIM)),
        ],
        out_specs=pl.BlockSpec((tq, lb), lambda b, h, i: (b * nq + i, h)),
        out_shape=jax.ShapeDtypeStruct((T, DA_HEADS * DA_V_DIM), BF16),
        scratch_shapes=[pltpu.VMEM((tq, tq), F32)] * 4 + [pltpu.VMEM((tq, tq), BF16)] * 6
        + [pltpu.VMEM((tq, LANES), F32)] * 6 + [pltpu.VMEM((tq, DA_V_DIM), F32)] * 2,
        compiler_params=_params(("parallel", "parallel", "arbitrary")),
    )(proj, proj, proj, qfeat, kfeat, lam_q, lam_k, sub_gain)


def _dl_kernel(q_ref, kc_ref, kp_ref, vc_ref, vp_ref, o_ref, lse_ref, *, slopes2, d, tq, ru):
    n = pl.program_id(1)
    sp = DL_SPAN
    row = lax.broadcasted_iota(jnp.int32, (sp, sp), 0)
    col = lax.broadcasted_iota(jnp.int32, (sp, sp), 1)
    dcur = row - col
    cur_ok = dcur >= 0
    prev_ok = dcur <= 0
    dcur_f = dcur.astype(F32)

    def scores(r, hh, j):
        hs = slice(hh * LANES, (hh + 1) * LANES)
        rs = slice(j * sp, (j + 1) * sp)
        q = q_ref[r, rs, hs]
        if j == 0:
            kp, vp, p_ok = kp_ref[r, :, hs], vp_ref[r, :, hs], prev_ok & (n > 0)
        else:
            ps = slice((j - 1) * sp, j * sp)
            kp, vp, p_ok = kc_ref[r, ps, hs], vc_ref[r, ps, hs], prev_ok
        s_c = jnp.where(cur_ok, _dot_nt(q, kc_ref[r, rs, hs]) - slopes2[hh] * dcur_f, NEG_BIG)
        s_p = jnp.where(p_ok, _dot_nt(q, kp) - slopes2[hh] * (dcur_f + float(sp)), NEG_BIG)
        return s_c, s_p, vc_ref[r, rs, hs], vp

    def finish(r, hh, j, s_c, s_p, vc, vp):
        m = jnp.max(jnp.maximum(s_c, s_p), axis=-1, keepdims=True)
        p_c = jnp.exp2(s_c - m)
        p_p = jnp.exp2(s_p - m)
        den = jnp.sum(p_c + p_p, axis=-1, keepdims=True)
        acc = _dot(p_c.astype(BF16), vc) + _dot(p_p.astype(BF16), vp)
        out_rows = pl.ds(j * sp, sp) if d == 1 else pl.ds(r + j * sp * d, sp, stride=d)
        o_ref[hh, out_rows, :] = acc / den
        lse_ref[hh, out_rows, :] = jnp.broadcast_to(m + jnp.log2(den), (sp, LANES))

    def residues(t, carry):
        units = [(t * ru + rr, hh, j) for rr in range(ru) for hh in range(DL_HEADS_PER_GROUP)
                 for j in range(tq // sp)]
        pending = []
        for u in units:
            pending.append((u, scores(*u)))
            if len(pending) > 2:
                u0, vals = pending.pop(0)
                finish(*u0, *vals)
        for u0, vals in pending:
            finish(*u0, *vals)
        return carry

    lax.fori_loop(0, d // ru, residues, 0)


def _dilated_group(src, col0, g, B, S, tok_per_step=2048):
    window, d = DL_GROUPS[g]
    assert window // d == DL_SPAN
    L = S // d
    tq = min(tok_per_step, S) // d
    assert tq % DL_SPAN == 0 and L % tq == 0
    nh = DL_HEADS_PER_GROUP * len(DL_GROUPS)
    slopes2 = tuple(2.0 ** (-8.0 * (g * DL_HEADS_PER_GROUP + hh + 1) / nh) * d * LOG2E
                    for hh in range(DL_HEADS_PER_GROUP))
    spb = tq // DL_SPAN
    nsteps = L // tq
    cur = lambda c: pl.BlockSpec((d, tq, COL_TILE), lambda b, n: (b, n, c))
    prev = lambda c: pl.BlockSpec((d, DL_SPAN, COL_TILE), lambda b, n: (b, jnp.maximum(n * spb - 1, 0), c))
    out_spec = pl.BlockSpec((DL_HEADS_PER_GROUP, d * tq, LANES), lambda b, n: (0, b * nsteps + n, 0))
    return pl.pallas_call(
        functools.partial(_dl_kernel, slopes2=slopes2, d=d, tq=tq, ru=min(d, 4)),
        grid=(B, nsteps),
        in_specs=[cur(col0), cur(col0 + 1), prev(col0 + 1), cur(col0 + 2), prev(col0 + 2)],
        out_specs=[out_spec, out_spec],
        out_shape=[jax.ShapeDtypeStruct((DL_HEADS_PER_GROUP, B * S, LANES), F32)] * 2,
        compiler_params=_params(("parallel", "arbitrary")),
    )(src, src, src, src, src)


def _route(logits):
    lane = lax.broadcasted_iota(jnp.int32, logits.shape, 1)
    big = jnp.int32(1 << 20)
    is_g = (lane >= MOE_N_EXPERTS) & (lane < MOE_N_EXPERTS + MOE_GROUPS)
    lg = jnp.where(is_g, logits, -jnp.inf)
    gmax = jnp.max(lg, axis=-1, keepdims=True)
    gsum = jnp.sum(jnp.exp(lg - gmax), axis=-1, keepdims=True)
    g_w = 1.0 / gsum
    g_idx = jnp.min(jnp.where(lg == gmax, lane - MOE_N_EXPERTS, big), axis=-1, keepdims=True)
    in_grp = (lane < MOE_N_EXPERTS) & ((lane // MOE_EXPERTS_PER_GROUP) == g_idx)
    le = jnp.where(in_grp, logits, -jnp.inf)
    t1 = jnp.max(le, axis=-1, keepdims=True)
    e1 = jnp.min(jnp.where(le == t1, lane, big), axis=-1, keepdims=True)
    le2 = jnp.where(lane == e1, -jnp.inf, le)
    t2 = jnp.max(le2, axis=-1, keepdims=True)
    e2 = jnp.min(jnp.where(le2 == t2, lane, big), axis=-1, keepdims=True)
    r = jnp.exp(t2 - t1)
    w1 = g_w / (1.0 + r)
    w2 = w1 * r
    out = jnp.where(lane == 0, e1.astype(F32),
                    jnp.where(lane == 1, e2.astype(F32),
                              jnp.where(lane == 2, w1, jnp.where(lane == 3, w2, 0.0))))
    return out


def _outproj_kernel(x_ref, oa_ref, ga_ref, gb_ref, o0_ref, o1_ref, o2_ref, l0_ref, l1_ref, l2_ref,
                    wa_ref, wb_ref, wo_ref, gf_ref, rh_ref, rl_ref,
                    x1_ref, hn_ref, rt_ref):
    obs = []
    for hh in range(DL_HEADS_PER_GROUP):
        l0, l1, l2 = l0_ref[hh], l1_ref[hh], l2_ref[hh]
        lm = jnp.maximum(jnp.maximum(l0, l1), l2)
        e0, e1, e2 = jnp.exp2(l0 - lm), jnp.exp2(l1 - lm), jnp.exp2(l2 - lm)
        obs.append((e0 * o0_ref[hh] + e1 * o1_ref[hh] + e2 * o2_ref[hh]) / (e0 + e1 + e2))
    ob = jnp.concatenate(obs, axis=1)
    a = _dot(oa_ref[...], wa_ref[...])
    b = _dot(ob.astype(BF16), wb_ref[...])
    mixed = jax.nn.sigmoid(ga_ref[...].astype(F32)) * a + jax.nn.sigmoid(gb_ref[...].astype(F32)) * b
    x1 = x_ref[...] + _dot(mixed.astype(BF16), wo_ref[...])
    x1_ref[...] = x1
    ms = jnp.mean(x1 * x1, axis=-1, keepdims=True)
    hn = x1 * lax.rsqrt(ms + EPS) * gf_ref[...]
    hn_ref[...] = hn
    hn_hi = hn.astype(BF16)
    hn_lo = (hn - hn_hi.astype(F32)).astype(BF16)
    logits = _dot(hn_hi, rh_ref[...]) + (_dot(hn_lo, rh_ref[...]) + _dot(hn_hi, rl_ref[...]))
    rt_ref[...] = _route(logits)


def _outproj(x2, o_a, proj, dl_o, dl_lse, wa, wb, wo, gain_ffn, r_hi, r_lo, tm=256):
    T, D = x2.shape
    row = lambda w: pl.BlockSpec((tm, w), lambda i: (i, 0))
    full = lambda s: pl.BlockSpec(s, lambda i: (0, 0), pipeline_mode=pl.Buffered(1))
    hrow = pl.BlockSpec((DL_HEADS_PER_GROUP, tm, LANES), lambda i: (0, i, 0))
    return pl.pallas_call(
        _outproj_kernel,
        grid=(T // tm,),
        in_specs=[
            row(D), row(o_a.shape[1]),
            pl.BlockSpec((tm, D), lambda i: (i, (CT_GATE_A * COL_TILE) // D)),
            pl.BlockSpec((tm, D), lambda i: (i, (CT_GATE_B * COL_TILE) // D)),
            hrow, hrow, hrow, hrow, hrow, hrow,
            full(wa.shape), full(wb.shape), full(wo.shape), full((1, D)), full(r_hi.shape), full(r_lo.shape),
        ],
        out_specs=[row(D), row(D), row(LANES)],
        out_shape=[jax.ShapeDtypeStruct((T, D), F32), jax.ShapeDtypeStruct((T, D), F32),
                   jax.ShapeDtypeStruct((T, LANES), F32)],
        compiler_params=_params(("parallel",)),
    )(x2, o_a, proj, proj, dl_o[0], dl_o[1], dl_o[2], dl_lse[0], dl_lse[1], dl_lse[2],
      wa, wb, wo, gain_ffn, r_hi, r_lo)


def _cast_rows(src_ref, dst_ref, chunk=256):
    def body(c, carry):
        r0 = pl.multiple_of(c * chunk, chunk)
        dst_ref[pl.ds(r0, chunk), :] = src_ref[pl.ds(r0, chunk), :].astype(dst_ref.dtype)
        return carry
    lax.fori_loop(0, src_ref.shape[0] // chunk, body, 0)


def _expert_changed(be_ref, i):
    return (i == 0) | (be_ref[i] != be_ref[jnp.maximum(i - 1, 0)])


def _moe_up_kernel(be_ref, nu_ref, tc_ref, tn_ref, hn_ref, wgu_ref, act_ref, xa, xb, wbf, gsem, *, d_ff):
    i = pl.program_id(0)
    nu = nu_ref[0]
    rows = xa.shape[0]
    even = i % 2 == 0

    def gather(tok_ref, r, buf, s):
        return pltpu.make_async_copy(hn_ref.at[tok_ref[0, r]], buf.at[r], gsem.at[s])

    def wait_gather(buf, s):
        pltpu.make_async_copy(hn_ref.at[pl.ds(0, rows)], buf, gsem.at[s]).wait()

    @pl.when((i == 0) & (nu > 0))
    def _():
        def body(r, c):
            gather(tc_ref, r, xa, 0).start(priority=ROW_DMA_PRIORITY)
            return c
        lax.fori_loop(0, rows, body, 0)

    @pl.when((i < nu) & _expert_changed(be_ref, i))
    def _():
        _cast_rows(wgu_ref, wbf)

    def live_step(cur, nxt, s):
        wait_gather(cur, s)
        for r in range(rows):
            gather(tn_ref, r, nxt, 1 - s).start(priority=ROW_DMA_PRIORITY)
        h = _dot(cur[...].astype(BF16), wbf[...])
        gate = h[:, :d_ff]
        up = h[:, d_ff:]
        act_ref[...] = (gate * jax.nn.sigmoid(gate) * up).astype(act_ref.dtype)

    @pl.when((i < nu) & even)
    def _():
        live_step(xa, xb, 0)

    @pl.when((i < nu) & jnp.logical_not(even))
    def _():
        live_step(xb, xa, 1)

    @pl.when((i == nu) & (nu > 0) & even)
    def _():
        wait_gather(xa, 0)

    @pl.when((i == nu) & (nu > 0) & jnp.logical_not(even))
    def _():
        wait_gather(xb, 1)

    @pl.when(i >= nu)
    def _():
        act_ref[...] = jnp.zeros(act_ref.shape, act_ref.dtype)


def _moe_down_kernel(be_ref, nu_ref, dp_ref, act_ref, wd_ref, y_ref, ya, yb, wbf, ssem):
    i = pl.program_id(0)
    nu = nu_ref[0]
    rows = ya.shape[0]
    even = i % 2 == 0

    def scatter(r, buf, s):
        return pltpu.make_async_copy(buf.at[r], y_ref.at[dp_ref[0, r]], ssem.at[s])

    def wait_scatter(buf, s):
        pltpu.make_async_copy(buf, y_ref.at[pl.ds(0, rows)], ssem.at[s]).wait()

    @pl.when(i == 0)
    def _():
        n_res = y_ref.shape[0] - 2 * rows
        for s, buf in enumerate((ya, yb)):
            buf[...] = jnp.zeros(buf.shape, buf.dtype)
            pltpu.make_async_copy(buf, y_ref.at[pl.ds(n_res + s * rows, rows)], ssem.at[s]).start()
        for s, buf in enumerate((ya, yb)):
            pltpu.make_async_copy(buf, y_ref.at[pl.ds(n_res + s * rows, rows)], ssem.at[s]).wait()

    @pl.when((i >= 2) & (i < nu + 2) & even)
    def _():
        wait_scatter(ya, 0)

    @pl.when((i >= 2) & (i < nu + 2) & jnp.logical_not(even))
    def _():
        wait_scatter(yb, 1)

    @pl.when((i < nu) & _expert_changed(be_ref, i))
    def _():
        _cast_rows(wd_ref, wbf)

    def step(cur, prv, s, do_scatter, do_compute):
        if do_scatter:
            for r in range(rows):
                scatter(r, prv, 1 - s).start(priority=ROW_DMA_PRIORITY)
        if do_compute:
            cur[...] = _dot(act_ref[...], wbf[...])

    for s, (cur, prv) in enumerate(((ya, yb), (yb, ya))):
        par = even if s == 0 else jnp.logical_not(even)

        @pl.when((i >= 1) & (i < nu) & par)
        def _(cur=cur, prv=prv, s=s):
            step(cur, prv, s, True, True)

        @pl.when((i == 0) & (nu > 0) & par)
        def _(cur=cur, prv=prv, s=s):
            step(cur, prv, s, False, True)

        @pl.when((i == nu) & (nu > 0) & par)
        def _(cur=cur, prv=prv, s=s):
            step(cur, prv, s, True, False)


def _experts(hn, src_tok, dst_row, blk_expert, n_used, wgu, wd, n_out_rows):
    T, D = hn.shape
    nblk = src_tok.shape[0] // MOE_BLOCK
    d_ff = wd.shape[1]
    tok = src_tok.reshape(nblk, 1, MOE_BLOCK)
    dst = dst_row.reshape(nblk, 1, MOE_BLOCK)

    def live(i, nu):
        return jnp.maximum(jnp.minimum(i, nu[0] - 1), 0)

    smem = lambda f: pl.BlockSpec((None, 1, MOE_BLOCK), f, memory_space=pltpu.SMEM)
    act = pl.pallas_call(
        functools.partial(_moe_up_kernel, d_ff=d_ff),
        grid_spec=pltpu.PrefetchScalarGridSpec(
            num_scalar_prefetch=2,
            grid=(nblk - 1,),
            in_specs=[
                smem(lambda i, be, nu: (i, 0, 0)),
                smem(lambda i, be, nu: (i + 1, 0, 0)),
                pl.BlockSpec(memory_space=pl.ANY),
                pl.BlockSpec((None, D, 2 * d_ff), lambda i, be, nu: (be[live(i, nu)], 0, 0)),
            ],
            out_specs=pl.BlockSpec((MOE_BLOCK, d_ff), lambda i, be, nu: (i, 0)),
            scratch_shapes=[pltpu.VMEM((MOE_BLOCK, D), F32), pltpu.VMEM((MOE_BLOCK, D), F32),
                            pltpu.VMEM((D, 2 * d_ff), BF16), pltpu.SemaphoreType.DMA((2,))],
        ),
        out_shape=jax.ShapeDtypeStruct(((nblk - 1) * MOE_BLOCK, d_ff), BF16),
        compiler_params=_params(("arbitrary",)),
    )(blk_expert, n_used, tok, tok, hn, wgu)
    return pl.pallas_call(
        _moe_down_kernel,
        grid_spec=pltpu.PrefetchScalarGridSpec(
            num_scalar_prefetch=2,
            grid=(nblk,),
            in_specs=[
                smem(lambda i, be, nu: (jnp.maximum(i - 1, 0), 0, 0)),
                pl.BlockSpec((MOE_BLOCK, d_ff), lambda i, be, nu: (live(i, nu), 0)),
                pl.BlockSpec((None, d_ff, D), lambda i, be, nu: (be[live(i, nu)], 0, 0)),
            ],
            out_specs=pl.BlockSpec(memory_space=pl.ANY),
            scratch_shapes=[pltpu.VMEM((MOE_BLOCK, D), F32), pltpu.VMEM((MOE_BLOCK, D), F32),
                            pltpu.VMEM((d_ff, D), BF16), pltpu.SemaphoreType.DMA((2,))],
        ),
        out_shape=jax.ShapeDtypeStruct((n_out_rows, D), F32),
        compiler_params=_params(("arbitrary",)),
    )(blk_expert, n_used, dst, act, wd)


def _combine_kernel(x1_ref, rt_ref, y1_ref, y2_ref, o_ref):
    rt = rt_ref[...]
    o_ref[...] = x1_ref[...] + (rt[:, 2:3] * y1_ref[...] + rt[:, 3:4] * y2_ref[...])


def _combine(x1, route, y, tm=512):
    T, D = x1.shape
    nt = T // tm
    return pl.pallas_call(
        _combine_kernel,
        grid=(nt,),
        in_specs=[
            pl.BlockSpec((tm, D), lambda i: (i, 0)),
            pl.BlockSpec((tm, LANES), lambda i: (i, 0)),
            pl.BlockSpec((tm, D), lambda i: (i, 0)),
            pl.BlockSpec((tm, D), lambda i: (nt + i, 0)),
        ],
        out_specs=pl.BlockSpec((tm, D), lambda i: (i, 0)),
        out_shape=jax.ShapeDtypeStruct((T, D), F32),
        compiler_params=_params(("parallel",)),
    )(x1, route, y, y)


def _dispatch_tables(route, T):
    TK = T * MOE_TOP_K
    flat_e = route[:, :MOE_TOP_K].astype(jnp.int32).reshape(-1)
    order = jnp.argsort(flat_e).astype(jnp.int32)
    sizes = jnp.sum(flat_e[:, None] == jnp.arange(MOE_N_EXPERTS, dtype=jnp.int32)[None, :], axis=0,
                    dtype=jnp.int32)
    start = jnp.cumsum(sizes) - sizes
    padded = ((sizes + MOE_BLOCK - 1) // MOE_BLOCK) * MOE_BLOCK
    pad_end = jnp.cumsum(padded)
    pad_start = pad_end - padded
    n_blocks = TK // MOE_BLOCK + MOE_N_EXPERTS
    blk = jnp.arange(n_blocks + 2, dtype=jnp.int32)
    blk_expert = jnp.minimum(jnp.sum(pad_end[None, :] <= (blk * MOE_BLOCK)[:, None], axis=1, dtype=jnp.int32),
                             MOE_N_EXPERTS - 1)
    r = jnp.arange(MOE_BLOCK, dtype=jnp.int32)[None, :]
    pos = blk[:, None] * MOE_BLOCK + r
    off = pos - pad_start[blk_expert][:, None]
    live = (off < sizes[blk_expert][:, None]) & (pos < pad_end[-1])
    sorted_idx = jnp.where(live, off + start[blk_expert][:, None], 0)
    slot = order[sorted_idx]
    tok = slot // MOE_TOP_K
    src_tok = jnp.where(live, tok, 0).reshape(-1)
    trash = TK + (blk[:, None] % 2) * MOE_BLOCK + r
    dst_row = jnp.where(live, (slot % MOE_TOP_K) * T + tok, trash).reshape(-1)
    n_used = (pad_end[-1:] // MOE_BLOCK).astype(jnp.int32)
    return src_tok, dst_row, blk_expert, n_used, TK + 2 * MOE_BLOCK


def _permute_w_in(w):
    da = DA_HEADS * 2 * DA_QK_DIM
    dav = DA_HEADS * DA_V_DIM
    dl = len(DL_GROUPS) * DL_HEADS_PER_GROUP * DL_HEAD_DIM
    o = [int(v) for v in np.cumsum([0, da, da, dav, dl, dl, dl])]
    parts = [w[:, o[6]:], w[:, :o[3]]]
    for g in range(len(DL_GROUPS)):
        for p in range(3):
            parts.append(w[:, o[3 + p] + g * COL_TILE: o[3 + p] + (g + 1) * COL_TILE])
    return jnp.concatenate(parts, axis=1)


def _gain_table(da_q_norm, da_k_norm, dl_q_norm, dl_k_norm):
    ones = jnp.ones((COL_TILE,), F32)
    daq = jnp.tile(da_q_norm, COL_TILE // DA_QK_DIM) * (DA_QK_DIM ** -0.5 * LOG2E)
    dak = jnp.tile(da_k_norm, COL_TILE // DA_QK_DIM)
    dlq = jnp.tile(dl_q_norm, COL_TILE // DL_HEAD_DIM) * (DL_HEAD_DIM ** -0.5 * LOG2E)
    dlk = jnp.tile(dl_k_norm, COL_TILE // DL_HEAD_DIM)
    rows = []
    for j in range(CT_END):
        if CT_DA_Q <= j < CT_DA_K:
            rows.append(daq)
        elif CT_DA_K <= j < CT_DA_V:
            rows.append(dak)
        elif j >= CT_DL and (j - CT_DL) % 3 == 0:
            rows.append(dlq)
        elif j >= CT_DL and (j - CT_DL) % 3 == 1:
            rows.append(dlk)
        else:
            rows.append(ones)
    return jnp.stack(rows, axis=0).reshape(CT_END, 1, COL_TILE)


def kernel(x, norm_mix, w_in, da_q_norm, da_k_norm, da_lambda_q, da_lambda_k, da_sub_norm,
           dl_q_norm, dl_k_norm, w_branch_a, w_branch_b, w_out, norm_ffn,
           w_group_router, w_expert_router, w_gate_up, w_down):
    B, S, D = x.shape
    T = B * S
    depth = w_in.shape[0]
    x2 = x.reshape(T, D)
    for l in range(depth):
        lam_init = 0.8 - 0.6 * math.exp(-0.3 * l)
        w_in_bf = _permute_w_in(w_in[l]).astype(BF16)
        gain_tab = _gain_table(da_q_norm[l], da_k_norm[l], dl_q_norm[l], dl_k_norm[l])
        proj, dl1, dl2 = _inproj(x2, norm_mix[l].reshape(1, D), w_in_bf, gain_tab, B, S)

        o_a = _diff_attention(proj, da_lambda_q[l], da_lambda_k[l], da_sub_norm[l].reshape(1, DA_V_DIM),
                              B, S, lam_init)
        dl = [_dilated_group(proj.reshape(B, S, proj.shape[1]), CT_DL, 0, B, S),
              _dilated_group(dl1, 0, 1, B, S), _dilated_group(dl2, 0, 2, B, S)]

        w_r = jnp.concatenate([w_expert_router[l], w_group_router[l]], axis=1)
        w_r = jnp.pad(w_r, ((0, 0), (0, LANES - w_r.shape[1])))
        r_hi = w_r.astype(BF16)
        r_lo = (w_r - r_hi.astype(F32)).astype(BF16)
        x1, hn, route = _outproj(
            x2, o_a, proj, [t[0] for t in dl], [t[1] for t in dl],
            w_branch_a[l].astype(BF16), w_branch_b[l].astype(BF16), w_out[l].astype(BF16),
            norm_ffn[l].reshape(1, D), r_hi, r_lo)

        src_tok, dst_row, blk_expert, n_used, n_rows = _dispatch_tables(route, T)
        y = _experts(hn, src_tok, dst_row, blk_expert, n_used, w_gate_up[l], w_down[l], n_rows)
        x2 = _combine(x1, route, y)
    return x2.reshape(B, S, D)
```

```python
import functools
import math

import jax
import jax.numpy as jnp
import numpy as np
from jax import lax
from jax.experimental import pallas as pl
from jax.experimental.pallas import tpu as pltpu

F32 = jnp.float32
BF16 = jnp.bfloat16

EPS = 1e-6
LOG2E = 1.4426950408889634
NEG_BIG = -1e30

DA_HEADS = 8
DA_QK_DIM = 64
DA_V_DIM = 128
DA_TILE_GROUP = 4
DL_GROUPS = ((128, 1), (512, 4), (2048, 16))
DL_HEADS_PER_GROUP = 4
DL_HEAD_DIM = 128
DL_SPAN = 128
MOE_GROUPS = 4
MOE_EXPERTS_PER_GROUP = 8
MOE_N_EXPERTS = 32
MOE_TOP_K = 2
MOE_BLOCK = 256

LANES = 128
COL_TILE = 512
VMEM_LIMIT = 56 * 1024 * 1024

CT_GATE_A, CT_GATE_B, CT_DA_Q, CT_DA_K, CT_DA_V, CT_DL, CT_MAIN_END, CT_END = 0, 4, 8, 10, 12, 14, 17, 23


def _params(sem, vmem=VMEM_LIMIT):
    return pltpu.CompilerParams(dimension_semantics=sem, vmem_limit_bytes=vmem)


def _dot(a, b):
    return jnp.dot(a, b, preferred_element_type=F32)


def _dot_nt(a, b):
    return lax.dot_general(a, b, (((1,), (1,)), ((), ())), preferred_element_type=F32)


def _inproj_kernel(x_ref, g_ref, w_ref, gain_ref, o_ref, d1_ref, d2_ref, h_scr, y_scr):
    j = pl.program_id(1)

    @pl.when(j == 0)
    def _():
        x = x_ref[...]
        ms = jnp.mean(x * x, axis=-1, keepdims=True)
        h_scr[...] = (x * lax.rsqrt(ms + EPS) * g_ref[...]).astype(BF16)

    y = _dot(h_scr[...], w_ref[...])
    gain = gain_ref[...]
    heads = COL_TILE // LANES

    is64 = (j >= CT_DA_Q) & (j < CT_DA_V)
    is128 = (j >= CT_DL) & (lax.rem(j - CT_DL, 3) < 2)
    main = j < CT_MAIN_END

    def norm128(h):
        sl = slice(h * LANES, (h + 1) * LANES)
        yh = y[:, sl]
        ss = jnp.sum(yh * yh, axis=-1, keepdims=True)
        return yh * lax.rsqrt(ss * (1.0 / DL_HEAD_DIM) + EPS) * gain[:, sl]

    @pl.when(is64)
    def _():
        for h in range(heads):
            sl = slice(h * LANES, (h + 1) * LANES)
            yh = y[:, sl]
            sq = yh * yh
            lo = lax.broadcasted_iota(jnp.int32, yh.shape, 1) < DA_QK_DIM
            s_lo = jnp.sum(jnp.where(lo, sq, 0.0), axis=-1, keepdims=True)
            s_hi = jnp.sum(jnp.where(lo, 0.0, sq), axis=-1, keepdims=True)
            r = jnp.where(lo, lax.rsqrt(s_lo * (1.0 / DA_QK_DIM) + EPS),
                          lax.rsqrt(s_hi * (1.0 / DA_QK_DIM) + EPS))
            o_ref[:, sl] = (yh * r * gain[:, sl]).astype(o_ref.dtype)

    @pl.when(is128 & main)
    def _():
        for h in range(heads):
            o_ref[:, h * LANES:(h + 1) * LANES] = norm128(h).astype(o_ref.dtype)

    @pl.when(is128 & jnp.logical_not(main))
    def _():
        for h in range(heads):
            y_scr[h] = norm128(h)

    plain = jnp.logical_not(is64 | is128)

    @pl.when(plain & main)
    def _():
        o_ref[...] = y.astype(o_ref.dtype)

    @pl.when(plain & jnp.logical_not(main))
    def _():
        for h in range(heads):
            y_scr[h] = y[:, h * LANES:(h + 1) * LANES]

    def deinterleave(dst_ref):
        d, rows = dst_ref.shape[0], dst_ref.shape[1]
        for r in range(d):
            for h in range(heads):
                dst_ref[r, :, h * LANES:(h + 1) * LANES] = (
                    y_scr[h, pl.ds(r, rows, stride=d), :].astype(dst_ref.dtype))

    @pl.when((j >= CT_MAIN_END) & (j < CT_MAIN_END + 3))
    def _():
        deinterleave(d1_ref)

    @pl.when(j >= CT_MAIN_END + 3)
    def _():
        deinterleave(d2_ref)


def _inproj(x2, gain_mix, w_bf, gain_tab, B, S, tm=1024):
    T, D = x2.shape
    tiles_per_batch = S // tm
    d1, d2 = DL_GROUPS[1][1], DL_GROUPS[2][1]
    part1 = lambda j: jnp.clip(j - CT_MAIN_END, 0, 2)
    part2 = lambda j: jnp.clip(j - CT_MAIN_END - 3, 0, 2)
    return pl.pallas_call(
        _inproj_kernel,
        grid=(T // tm, CT_END),
        in_specs=[
            pl.BlockSpec((tm, D), lambda i, j: (i, 0)),
            pl.BlockSpec((1, D), lambda i, j: (0, 0)),
            pl.BlockSpec((D, COL_TILE), lambda i, j: (0, j)),
            pl.BlockSpec((None, 1, COL_TILE), lambda i, j: (j, 0, 0)),
        ],
        out_specs=[
            pl.BlockSpec((tm, COL_TILE), lambda i, j: (i, jnp.minimum(j, CT_MAIN_END - 1))),
            pl.BlockSpec((d1, tm // d1, COL_TILE),
                         lambda i, j: (i // tiles_per_batch, i % tiles_per_batch, part1(j))),
            pl.BlockSpec((d2, tm // d2, COL_TILE),
                         lambda i, j: (i // tiles_per_batch, i % tiles_per_batch, part2(j))),
        ],
        out_shape=[
            jax.ShapeDtypeStruct((T, CT_MAIN_END * COL_TILE), BF16),
            jax.ShapeDtypeStruct((B * d1, S // d1, 3 * COL_TILE), BF16),
            jax.ShapeDtypeStruct((B * d2, S // d2, 3 * COL_TILE), BF16),
        ],
        scratch_shapes=[pltpu.VMEM((tm, D), BF16), pltpu.VMEM((COL_TILE // LANES, tm, LANES), F32)],
        compiler_params=_params(("parallel", "arbitrary")),
    )(x2, gain_mix, w_bf, gain_tab)


def _bf16_pieces(x, n=3):
    out = []
    r = np.float64(x)
    for _ in range(n):
        p = np.asarray(np.float32(r)).astype(jnp.bfloat16).astype(np.float64)
        out.append(float(p))
        r = r - p
    return out


def _alibi_features(tk):
    pieces = _bf16_pieces(LOG2E)
    qf = np.zeros((2, LANES), np.float32)
    kf = np.zeros((2, tk, LANES), np.float32)
    j = np.arange(tk)
    hi, lo = (j // 16) * 16, j % 16
    for m in range(2):
        f0 = DA_QK_DIM if m == 0 else 0
        for n, p in enumerate(pieces):
            qf[m, f0 + 2 * n] = p
            qf[m, f0 + 2 * n + 1] = p
            kf[m, :, f0 + 2 * n] = hi
            kf[m, :, f0 + 2 * n + 1] = lo
    return jnp.asarray(qf), jnp.asarray(kf, dtype=BF16)


def _da_kernel(q_ref, k_ref, v_ref, qf_ref, kf_ref, lq_ref, lk_ref, sg_ref, o_ref,
               s00, s01, s10, s11, p00, p01, p10, p11, pd0, pd1,
               m0_scr, m1_scr, l0_scr, l1_scr, a0_scr, a1_scr, acc0_scr, acc1_scr, *, tq, rc, lam_init):
    h = pl.program_id(1)
    qi = pl.program_id(2)
    nlb = tq // LANES
    pow2 = jnp.exp2(-(h + 1).astype(F32))
    slope2 = pow2 * LOG2E

    q = q_ref[...]
    lane = lax.broadcasted_iota(jnp.int32, (tq, LANES), 1)
    own = (lane < DA_QK_DIM, lane >= DA_QK_DIM)
    qfs = [jnp.where(own[mi], q, jnp.broadcast_to((qf_ref[mi:mi + 1, :] * pow2).astype(BF16), q.shape))
           for mi in range(2)]

    m_scrs, l_scrs, a_scrs, acc_scrs = (m0_scr, m1_scr), (l0_scr, l1_scr), (a0_scr, a1_scr), (acc0_scr, acc1_scr)
    for mi in range(2):
        m_scrs[mi][...] = jnp.full(m_scrs[mi].shape, NEG_BIG, F32)
        l_scrs[mi][...] = jnp.zeros(l_scrs[mi].shape, F32)
        acc_scrs[mi][...] = jnp.zeros(acc_scrs[mi].shape, F32)

    def scores(ki, mi, s_ref):
        k = k_ref[pl.ds(pl.multiple_of(ki * tq, tq), tq), :]
        s_ref[...] = _dot_nt(qfs[mi], jnp.where(own[mi], k, kf_ref[mi]))

    def softmax(ki, mi, s_ref, p_ref, masked):
        m_scr, l_scr, a_scr = m_scrs[mi], l_scrs[mi], a_scrs[mi]
        c = slope2 * ((ki - qi) * tq).astype(F32)
        for r in range(tq // rc):
            rows = slice(r * rc, (r + 1) * rc)
            nb = min(nlb, ((r + 1) * rc - 1) // LANES + 1) if masked else nlb
            sb = []
            for j in range(nb):
                cs = slice(j * LANES, (j + 1) * LANES)
                s = s_ref[rows, cs]
                if masked and (j + 1) * LANES - 1 > r * rc:
                    rr = lax.broadcasted_iota(jnp.int32, (rc, LANES), 0) + r * rc
                    cc = lax.broadcasted_iota(jnp.int32, (rc, LANES), 1) + j * LANES
                    s = jnp.where(cc <= rr, s, NEG_BIG)
                sb.append(s)
            mx = sb[0]
            for s in sb[1:]:
                mx = jnp.maximum(mx, s)
            m_prev = m_scr[rows, :]
            m_new = jnp.maximum(m_prev, jnp.max(mx, axis=-1, keepdims=True) + c)
            alpha = jnp.exp2(m_prev - m_new)
            a_scr[rows, :] = alpha
            m_scr[rows, :] = m_new
            mc = m_new - c
            psum = alpha * l_scr[rows, :]
            for j in range(nlb):
                cs = slice(j * LANES, (j + 1) * LANES)
                if j < nb:
                    p = jnp.exp2(sb[j] - mc)
                    psum = psum + p
                    p_ref[rows, cs] = p.astype(BF16)
                else:
                    p_ref[rows, cs] = jnp.zeros((rc, LANES), BF16)
            l_scr[rows, :] = psum

    def values(ki, mi, p_ref):
        v = v_ref[pl.ds(pl.multiple_of(ki * tq, tq), tq), :]
        acc_scrs[mi][...] = a_scrs[mi][...] * acc_scrs[mi][...] + _dot(p_ref[...], v)

    s_bufs, p_bufs = ((s00, s01), (s10, s11)), ((p00, p01), (p10, p11))

    def tile_group(k0, n, last_masked):
        for mi in range(2):
            scores(k0, mi, s_bufs[0][mi])
        for i in range(n):
            masked = last_masked and i == n - 1
            for mi in range(2):
                p_ref = (pd0, pd1)[mi] if masked else p_bufs[i % 2][mi]
                softmax(k0 + i, mi, s_bufs[i % 2][mi], p_ref, masked)
                values(k0 + i, mi, p_ref)
                if i + 1 < n:
                    scores(k0 + i + 1, mi, s_bufs[(i + 1) % 2][mi])

    def body(t, carry):
        tile_group(DA_TILE_GROUP * t, DA_TILE_GROUP, False)
        return carry

    n_full = qi // DA_TILE_GROUP
    lax.fori_loop(0, n_full, body, 0)
    for rem in range(1, DA_TILE_GROUP + 1):
        @pl.when(qi - DA_TILE_GROUP * n_full == rem - 1)
        def _(rem=rem):
            tile_group(qi - (rem - 1), rem, True)

    lam_e = jnp.exp(jnp.sum(lq_ref[...] * lk_ref[...], axis=-1, keepdims=True))
    lam = lam_e[0:1, :] - lam_e[1:2, :] + lam_init
    l0 = jnp.sum(l0_scr[...], axis=-1, keepdims=True)
    l1 = jnp.sum(l1_scr[...], axis=-1, keepdims=True)
    o = acc0_scr[...] / l0 - lam * (acc1_scr[...] / l1)
    ms = jnp.mean(o * o, axis=-1, keepdims=True)
    o = o * lax.rsqrt(ms + EPS) * sg_ref[...] * (1.0 - lam_init)
    o_ref[...] = o.astype(o_ref.dtype)


def _diff_attention(proj, lam_q, lam_k, sub_gain, B, S, lam_init, tq=512, rc=32):
    T = proj.shape[0]
    nq = S // tq
    lb = LANES
    q_blk0, k_blk0, v_blk0 = (CT_DA_Q * COL_TILE) // lb, (CT_DA_K * COL_TILE) // lb, (CT_DA_V * COL_TILE) // lb
    qfeat, kfeat = _alibi_features(tq)
    const = lambda shape: pl.BlockSpec(shape, lambda b, h, i: (0,) * len(shape))
    return pl.pallas_call(
        functools.partial(_da_kernel, tq=tq, rc=rc, lam_init=lam_init),
        grid=(B, DA_HEADS, nq),
        in_specs=[
            pl.BlockSpec((tq, lb), lambda b, h, i: (b * nq + i, q_blk0 + h)),
            pl.BlockSpec((S, lb), lambda b, h, i: (b, k_blk0 + h)),
            pl.BlockSpec((S, lb), lambda b, h, i: (b, v_blk0 + h)),
            const((2, LANES)), const((2, tq, LANES)),
            const((2, DA_QK_DIM)), const((2, DA_QK_DIM)), const((1, DA_V_DIM)),
        ],
        out_specs=pl.BlockSpec((tq, lb), lambda b, h, i: (b * nq + i, h)),
        out_shape=jax.ShapeDtypeStruct((T, DA_HEADS * DA_V_DIM), BF16),
        scratch_shapes=[pltpu.VMEM((tq, tq), F32)] * 4 + [pltpu.VMEM((tq, tq), BF16)] * 6
        + [pltpu.VMEM((tq, LANES), F32)] * 6 + [pltpu.VMEM((tq, DA_V_DIM), F32)] * 2,
        compiler_params=_params(("parallel", "parallel", "arbitrary")),
    )(proj, proj, proj, qfeat, kfeat, lam_q, lam_k, sub_gain)


def _dl_kernel(q_ref, kc_ref, kp_ref, vc_ref, vp_ref, o_ref, lse_ref, *, slopes2, d, tq, ru):
    n = pl.program_id(1)
    sp = DL_SPAN
    row = lax.broadcasted_iota(jnp.int32, (sp, sp), 0)
    col = lax.broadcasted_iota(jnp.int32, (sp, sp), 1)
    dcur = row - col
    cur_ok = dcur >= 0
    prev_ok = dcur <= 0
    dcur_f = dcur.astype(F32)

    def scores(r, hh, j):
        hs = slice(hh * LANES, (hh + 1) * LANES)
        rs = slice(j * sp, (j + 1) * sp)
        q = q_ref[r, rs, hs]
        if j == 0:
            kp, vp, p_ok = kp_ref[r, :, hs], vp_ref[r, :, hs], prev_ok & (n > 0)
        else:
            ps = slice((j - 1) * sp, j * sp)
            kp, vp, p_ok = kc_ref[r, ps, hs], vc_ref[r, ps, hs], prev_ok
        s_c = jnp.where(cur_ok, _dot_nt(q, kc_ref[r, rs, hs]) - slopes2[hh] * dcur_f, NEG_BIG)
        s_p = jnp.where(p_ok, _dot_nt(q, kp) - slopes2[hh] * (dcur_f + float(sp)), NEG_BIG)
        return s_c, s_p, vc_ref[r, rs, hs], vp

    def finish(r, hh, j, s_c, s_p, vc, vp):
        m = jnp.max(jnp.maximum(s_c, s_p), axis=-1, keepdims=True)
        p_c = jnp.exp2(s_c - m)
        p_p = jnp.exp2(s_p - m)
        den = jnp.sum(p_c + p_p, axis=-1, keepdims=True)
        acc = _dot(p_c.astype(BF16), vc) + _dot(p_p.astype(BF16), vp)
        out_rows = pl.ds(j * sp, sp) if d == 1 else pl.ds(r + j * sp * d, sp, stride=d)
        o_ref[hh, out_rows, :] = acc / den
        lse_ref[hh, out_rows, :] = jnp.broadcast_to(m + jnp.log2(den), (sp, LANES))

    def residues(t, carry):
        units = [(t * ru + rr, hh, j) for rr in range(ru) for hh in range(DL_HEADS_PER_GROUP)
                 for j in range(tq // sp)]
        pending = []
        for u in units:
            pending.append((u, scores(*u)))
            if len(pending) > 2:
                u0, vals = pending.pop(0)
                finish(*u0, *vals)
        for u0, vals in pending:
            finish(*u0, *vals)
        return carry

    lax.fori_loop(0, d // ru, residues, 0)


def _dilated_group(src, col0, g, B, S, tok_per_step=2048):
    window, d = DL_GROUPS[g]
    assert window // d == DL_SPAN
    L = S // d
    tq = min(tok_per_step, S) // d
    assert tq % DL_SPAN == 0 and L % tq == 0
    nh = DL_HEADS_PER_GROUP * len(DL_GROUPS)
    slopes2 = tuple(2.0 ** (-8.0 * (g * DL_HEADS_PER_GROUP + hh + 1) / nh) * d * LOG2E
                    for hh in range(DL_HEADS_PER_GROUP))
    spb = tq // DL_SPAN
    nsteps = L // tq
    cur = lambda c: pl.BlockSpec((d, tq, COL_TILE), lambda b, n: (b, n, c))
    prev = lambda c: pl.BlockSpec((d, DL_SPAN, COL_TILE), lambda b, n: (b, jnp.maximum(n * spb - 1, 0), c))
    out_spec = pl.BlockSpec((DL_HEADS_PER_GROUP, d * tq, LANES), lambda b, n: (0, b * nsteps + n, 0))
    return pl.pallas_call(
        functools.partial(_dl_kernel, slopes2=slopes2, d=d, tq=tq, ru=min(d, 4)),
        grid=(B, nsteps),
        in_specs=[cur(col0), cur(col0 + 1), prev(col0 + 1), cur(col0 + 2), prev(col0 + 2)],
        out_specs=[out_spec, out_spec],
        out_shape=[jax.ShapeDtypeStruct((DL_HEADS_PER_GROUP, B * S, LANES), F32)] * 2,
        compiler_params=_params(("parallel", "arbitrary")),
    )(src, src, src, src, src)


def _route(logits):
    lane = lax.broadcasted_iota(jnp.int32, logits.shape, 1)
    big = jnp.int32(1 << 20)
    is_g = (lane >= MOE_N_EXPERTS) & (lane < MOE_N_EXPERTS + MOE_GROUPS)
    lg = jnp.where(is_g, logits, -jnp.inf)
    gmax = jnp.max(lg, axis=-1, keepdims=True)
    gsum = jnp.sum(jnp.exp(lg - gmax), axis=-1, keepdims=True)
    g_w = 1.0 / gsum
    g_idx = jnp.min(jnp.where(lg == gmax, lane - MOE_N_EXPERTS, big), axis=-1, keepdims=True)
    in_grp = (lane < MOE_N_EXPERTS) & ((lane // MOE_EXPERTS_PER_GROUP) == g_idx)
    le = jnp.where(in_grp, logits, -jnp.inf)
    t1 = jnp.max(le, axis=-1, keepdims=True)
    e1 = jnp.min(jnp.where(le == t1, lane, big), axis=-1, keepdims=True)
    le2 = jnp.where(lane == e1, -jnp.inf, le)
    t2 = jnp.max(le2, axis=-1, keepdims=True)
    e2 = jnp.min(jnp.where(le2 == t2, lane, big), axis=-1, keepdims=True)
    r = jnp.exp(t2 - t1)
    w1 = g_w / (1.0 + r)
    w2 = w1 * r
    out = jnp.where(lane == 0, e1.astype(F32),
                    jnp.where(lane == 1, e2.astype(F32),
                              jnp.where(lane == 2, w1, jnp.where(lane == 3, w2, 0.0))))
    return out


def _outproj_kernel(x_ref, oa_ref, ga_ref, gb_ref, o0_ref, o1_ref, o2_ref, l0_ref, l1_ref, l2_ref,
                    wa_ref, wb_ref, wo_ref, gf_ref, rh_ref, rc_ref,
                    x1_ref, hn_ref, rt_ref):
    obs = []
    for hh in range(DL_HEADS_PER_GROUP):
        l0, l1, l2 = l0_ref[hh], l1_ref[hh], l2_ref[hh]
        lm = jnp.maximum(jnp.maximum(l0, l1), l2)
        e0, e1, e2 = jnp.exp2(l0 - lm), jnp.exp2(l1 - lm), jnp.exp2(l2 - lm)
        obs.append((e0 * o0_ref[hh] + e1 * o1_ref[hh] + e2 * o2_ref[hh]) / (e0 + e1 + e2))
    ob = jnp.concatenate(obs, axis=1)
    a = _dot(oa_ref[...], wa_ref[...])
    b = _dot(ob.astype(BF16), wb_ref[...])
    mixed = jax.nn.sigmoid(ga_ref[...].astype(F32)) * a + jax.nn.sigmoid(gb_ref[...].astype(F32)) * b
    x1 = x_ref[...] + _dot(mixed.astype(BF16), wo_ref[...])
    x1_ref[...] = x1
    ms = jnp.mean(x1 * x1, axis=-1, keepdims=True)
    hn = x1 * lax.rsqrt(ms + EPS) * gf_ref[...]
    hn_ref[...] = hn
    hn_hi = hn.astype(BF16)
    hn_lo = (hn - hn_hi.astype(F32)).astype(BF16)
    t = _dot(hn_hi, rc_ref[...])
    logits = t[:, 0:LANES] + (_dot(hn_lo, rh_ref[...]) + t[:, LANES:2 * LANES])
    rt_ref[...] = _route(logits)


def _outproj(x2, o_a, proj, dl_o, dl_lse, wa, wb, wo, gain_ffn, r_hi, r_cat, tm=256):
    T, D = x2.shape
    row = lambda w: pl.BlockSpec((tm, w), lambda i: (i, 0))
    full = lambda s: pl.BlockSpec(s, lambda i: (0, 0), pipeline_mode=pl.Buffered(1))
    hrow = pl.BlockSpec((DL_HEADS_PER_GROUP, tm, LANES), lambda i: (0, i, 0))
    return pl.pallas_call(
        _outproj_kernel,
        grid=(T // tm,),
        in_specs=[
            row(D), row(o_a.shape[1]),
            pl.BlockSpec((tm, D), lambda i: (i, (CT_GATE_A * COL_TILE) // D)),
            pl.BlockSpec((tm, D), lambda i: (i, (CT_GATE_B * COL_TILE) // D)),
            hrow, hrow, hrow, hrow, hrow, hrow,
            full(wa.shape), full(wb.shape), full(wo.shape), full((1, D)), full(r_hi.shape), full(r_cat.shape),
        ],
        out_specs=[row(D), row(D), row(LANES)],
        out_shape=[jax.ShapeDtypeStruct((T, D), F32), jax.ShapeDtypeStruct((T, D), F32),
                   jax.ShapeDtypeStruct((T, LANES), F32)],
        compiler_params=_params(("parallel",)),
    )(x2, o_a, proj, proj, dl_o[0], dl_o[1], dl_o[2], dl_lse[0], dl_lse[1], dl_lse[2],
      wa, wb, wo, gain_ffn, r_hi, r_cat)


def _cast_rows(src_ref, dst_ref, chunk=256):
    def body(c, carry):
        r0 = pl.multiple_of(c * chunk, chunk)
        dst_ref[pl.ds(r0, chunk), :] = src_ref[pl.ds(r0, chunk), :].astype(dst_ref.dtype)
        return carry
    lax.fori_loop(0, src_ref.shape[0] // chunk, body, 0)


def _expert_changed(be_ref, i):
    return (i == 0) | (be_ref[i] != be_ref[jnp.maximum(i - 1, 0)])


def _moe_up_kernel(be_ref, nu_ref, tc_ref, tn_ref, hn_ref, wgu_ref, act_ref, xa, xb, wbf, gsem, *, d_ff):
    i = pl.program_id(0)
    nu = nu_ref[0]
    rows = xa.shape[0]
    even = i % 2 == 0

    def gather(tok_ref, r, buf, s):
        return pltpu.make_async_copy(hn_ref.at[tok_ref[0, r]], buf.at[r], gsem.at[s])

    def wait_gather(buf, s):
        pltpu.make_async_copy(hn_ref.at[pl.ds(0, rows)], buf, gsem.at[s]).wait()

    @pl.when((i == 0) & (nu > 0))
    def _():
        def body(r, c):
            gather(tc_ref, r, xa, 0).start()
            return c
        lax.fori_loop(0, rows, body, 0)

    @pl.when((i < nu) & _expert_changed(be_ref, i))
    def _():
        _cast_rows(wgu_ref, wbf)

    def live_step(cur, nxt, s):
        wait_gather(cur, s)
        for r in range(rows):
            gather(tn_ref, r, nxt, 1 - s).start()
        h = _dot(cur[...].astype(BF16), wbf[...])
        gate = h[:, :d_ff]
        up = h[:, d_ff:]
        act_ref[...] = (gate * jax.nn.sigmoid(gate) * up).astype(act_ref.dtype)

    @pl.when((i < nu) & even)
    def _():
        live_step(xa, xb, 0)

    @pl.when((i < nu) & jnp.logical_not(even))
    def _():
        live_step(xb, xa, 1)

    @pl.when((i == nu) & (nu > 0) & even)
    def _():
        wait_gather(xa, 0)

    @pl.when((i == nu) & (nu > 0) & jnp.logical_not(even))
    def _():
        wait_gather(xb, 1)

    @pl.when(i >= nu)
    def _():
        act_ref[...] = jnp.zeros(act_ref.shape, act_ref.dtype)


def _moe_down_kernel(be_ref, nu_ref, dp_ref, act_ref, wd_ref, y_ref, ya, yb, wbf, ssem):
    i = pl.program_id(0)
    nu = nu_ref[0]
    rows = ya.shape[0]
    even = i % 2 == 0

    def scatter(r, buf, s):
        return pltpu.make_async_copy(buf.at[r], y_ref.at[dp_ref[0, r]], ssem.at[s])

    def wait_scatter(buf, s):
        pltpu.make_async_copy(buf, y_ref.at[pl.ds(0, rows)], ssem.at[s]).wait()

    @pl.when(i == 0)
    def _():
        n_res = y_ref.shape[0] - 2 * rows
        for s, buf in enumerate((ya, yb)):
            buf[...] = jnp.zeros(buf.shape, buf.dtype)
            pltpu.make_async_copy(buf, y_ref.at[pl.ds(n_res + s * rows, rows)], ssem.at[s]).start()
        for s, buf in enumerate((ya, yb)):
            pltpu.make_async_copy(buf, y_ref.at[pl.ds(n_res + s * rows, rows)], ssem.at[s]).wait()

    @pl.when((i >= 2) & (i < nu + 2) & even)
    def _():
        wait_scatter(ya, 0)

    @pl.when((i >= 2) & (i < nu + 2) & jnp.logical_not(even))
    def _():
        wait_scatter(yb, 1)

    @pl.when((i < nu) & _expert_changed(be_ref, i))
    def _():
        _cast_rows(wd_ref, wbf)

    def step(cur, prv, s, do_scatter, do_compute):
        if do_scatter:
            for r in range(rows):
                scatter(r, prv, 1 - s).start()
        if do_compute:
            cur[...] = _dot(act_ref[...], wbf[...])

    for s, (cur, prv) in enumerate(((ya, yb), (yb, ya))):
        par = even if s == 0 else jnp.logical_not(even)

        @pl.when((i >= 1) & (i < nu) & par)
        def _(cur=cur, prv=prv, s=s):
            step(cur, prv, s, True, True)

        @pl.when((i == 0) & (nu > 0) & par)
        def _(cur=cur, prv=prv, s=s):
            step(cur, prv, s, False, True)

        @pl.when((i == nu) & (nu > 0) & par)
        def _(cur=cur, prv=prv, s=s):
            step(cur, prv, s, True, False)


def _experts(hn, src_tok, dst_row, blk_expert, n_used, wgu, wd, n_out_rows):
    T, D = hn.shape
    nblk = src_tok.shape[0] // MOE_BLOCK
    d_ff = wd.shape[1]
    tok = src_tok.reshape(nblk, 1, MOE_BLOCK)
    dst = dst_row.reshape(nblk, 1, MOE_BLOCK)

    def live(i, nu):
        return jnp.maximum(jnp.minimum(i, nu[0] - 1), 0)

    smem = lambda f: pl.BlockSpec((None, 1, MOE_BLOCK), f, memory_space=pltpu.SMEM)
    act = pl.pallas_call(
        functools.partial(_moe_up_kernel, d_ff=d_ff),
        grid_spec=pltpu.PrefetchScalarGridSpec(
            num_scalar_prefetch=2,
            grid=(nblk - 1,),
            in_specs=[
                smem(lambda i, be, nu: (i, 0, 0)),
                smem(lambda i, be, nu: (i + 1, 0, 0)),
                pl.BlockSpec(memory_space=pl.ANY),
                pl.BlockSpec((None, D, 2 * d_ff), lambda i, be, nu: (be[live(i, nu)], 0, 0)),
            ],
            out_specs=pl.BlockSpec((MOE_BLOCK, d_ff), lambda i, be, nu: (i, 0)),
            scratch_shapes=[pltpu.VMEM((MOE_BLOCK, D), F32), pltpu.VMEM((MOE_BLOCK, D), F32),
                            pltpu.VMEM((D, 2 * d_ff), BF16), pltpu.SemaphoreType.DMA((2,))],
        ),
        out_shape=jax.ShapeDtypeStruct(((nblk - 1) * MOE_BLOCK, d_ff), BF16),
        compiler_params=_params(("arbitrary",)),
    )(blk_expert, n_used, tok, tok, hn, wgu)
    return pl.pallas_call(
        _moe_down_kernel,
        grid_spec=pltpu.PrefetchScalarGridSpec(
            num_scalar_prefetch=2,
            grid=(nblk,),
            in_specs=[
                smem(lambda i, be, nu: (jnp.maximum(i - 1, 0), 0, 0)),
                pl.BlockSpec((MOE_BLOCK, d_ff), lambda i, be, nu: (live(i, nu), 0)),
                pl.BlockSpec((None, d_ff, D), lambda i, be, nu: (be[live(i, nu)], 0, 0)),
            ],
            out_specs=pl.BlockSpec(memory_space=pl.ANY),
            scratch_shapes=[pltpu.VMEM((MOE_BLOCK, D), F32), pltpu.VMEM((MOE_BLOCK, D), F32),
                            pltpu.VMEM((d_ff, D), BF16), pltpu.SemaphoreType.DMA((2,))],
        ),
        out_shape=jax.ShapeDtypeStruct((n_out_rows, D), F32),
        compiler_params=_params(("arbitrary",)),
    )(blk_expert, n_used, dst, act, wd)


def _combine_kernel(x1_ref, rt_ref, y1_ref, y2_ref, o_ref):
    rt = rt_ref[...]
    o_ref[...] = x1_ref[...] + (rt[:, 2:3] * y1_ref[...] + rt[:, 3:4] * y2_ref[...])


def _combine(x1, route, y, tm=512):
    T, D = x1.shape
    nt = T // tm
    return pl.pallas_call(
        _combine_kernel,
        grid=(nt,),
        in_specs=[
            pl.BlockSpec((tm, D), lambda i: (i, 0)),
            pl.BlockSpec((tm, LANES), lambda i: (i, 0)),
            pl.BlockSpec((tm, D), lambda i: (i, 0)),
            pl.BlockSpec((tm, D), lambda i: (nt + i, 0)),
        ],
        out_specs=pl.BlockSpec((tm, D), lambda i: (i, 0)),
        out_shape=jax.ShapeDtypeStruct((T, D), F32),
        compiler_params=_params(("parallel",)),
    )(x1, route, y, y)


def _dispatch_tables(route, T):
    TK = T * MOE_TOP_K
    flat_e = route[:, :MOE_TOP_K].astype(jnp.int32).reshape(-1)
    order = jnp.argsort(flat_e).astype(jnp.int32)
    sizes = jnp.sum(flat_e[:, None] == jnp.arange(MOE_N_EXPERTS, dtype=jnp.int32)[None, :], axis=0,
                    dtype=jnp.int32)
    start = jnp.cumsum(sizes) - sizes
    padded = ((sizes + MOE_BLOCK - 1) // MOE_BLOCK) * MOE_BLOCK
    pad_end = jnp.cumsum(padded)
    pad_start = pad_end - padded
    n_blocks = TK // MOE_BLOCK + MOE_N_EXPERTS
    blk = jnp.arange(n_blocks + 2, dtype=jnp.int32)
    blk_expert = jnp.minimum(jnp.sum(pad_end[None, :] <= (blk * MOE_BLOCK)[:, None], axis=1, dtype=jnp.int32),
                             MOE_N_EXPERTS - 1)
    r = jnp.arange(MOE_BLOCK, dtype=jnp.int32)[None, :]
    pos = blk[:, None] * MOE_BLOCK + r
    off = pos - pad_start[blk_expert][:, None]
    live = (off < sizes[blk_expert][:, None]) & (pos < pad_end[-1])
    sorted_idx = jnp.where(live, off + start[blk_expert][:, None], 0)
    slot = order[sorted_idx]
    tok = slot // MOE_TOP_K
    src_tok = jnp.where(live, tok, 0).reshape(-1)
    trash = TK + (blk[:, None] % 2) * MOE_BLOCK + r
    dst_row = jnp.where(live, (slot % MOE_TOP_K) * T + tok, trash).reshape(-1)
    n_used = (pad_end[-1:] // MOE_BLOCK).astype(jnp.int32)
    return src_tok, dst_row, blk_expert, n_used, TK + 2 * MOE_BLOCK


def _permute_w_in(w):
    da = DA_HEADS * 2 * DA_QK_DIM
    dav = DA_HEADS * DA_V_DIM
    dl = len(DL_GROUPS) * DL_HEADS_PER_GROUP * DL_HEAD_DIM
    o = [int(v) for v in np.cumsum([0, da, da, dav, dl, dl, dl])]
    parts = [w[:, o[6]:], w[:, :o[3]]]
    for g in range(len(DL_GROUPS)):
        for p in range(3):
            parts.append(w[:, o[3 + p] + g * COL_TILE: o[3 + p] + (g + 1) * COL_TILE])
    return jnp.concatenate(parts, axis=1)


def _gain_table(da_q_norm, da_k_norm, dl_q_norm, dl_k_norm):
    ones = jnp.ones((COL_TILE,), F32)
    daq = jnp.tile(da_q_norm, COL_TILE // DA_QK_DIM) * (DA_QK_DIM ** -0.5 * LOG2E)
    dak = jnp.tile(da_k_norm, COL_TILE // DA_QK_DIM)
    dlq = jnp.tile(dl_q_norm, COL_TILE // DL_HEAD_DIM) * (DL_HEAD_DIM ** -0.5 * LOG2E)
    dlk = jnp.tile(dl_k_norm, COL_TILE // DL_HEAD_DIM)
    rows = []
    for j in range(CT_END):
        if CT_DA_Q <= j < CT_DA_K:
            rows.append(daq)
        elif CT_DA_K <= j < CT_DA_V:
            rows.append(dak)
        elif j >= CT_DL and (j - CT_DL) % 3 == 0:
            rows.append(dlq)
        elif j >= CT_DL and (j - CT_DL) % 3 == 1:
            rows.append(dlk)
        else:
            rows.append(ones)
    return jnp.stack(rows, axis=0).reshape(CT_END, 1, COL_TILE)


def kernel(x, norm_mix, w_in, da_q_norm, da_k_norm, da_lambda_q, da_lambda_k, da_sub_norm,
           dl_q_norm, dl_k_norm, w_branch_a, w_branch_b, w_out, norm_ffn,
           w_group_router, w_expert_router, w_gate_up, w_down):
    B, S, D = x.shape
    T = B * S
    depth = w_in.shape[0]
    x2 = x.reshape(T, D)
    for l in range(depth):
        lam_init = 0.8 - 0.6 * math.exp(-0.3 * l)
        w_in_bf = _permute_w_in(w_in[l]).astype(BF16)
        gain_tab = _gain_table(da_q_norm[l], da_k_norm[l], dl_q_norm[l], dl_k_norm[l])
        proj, dl1, dl2 = _inproj(x2, norm_mix[l].reshape(1, D), w_in_bf, gain_tab, B, S)

        o_a = _diff_attention(proj, da_lambda_q[l], da_lambda_k[l], da_sub_norm[l].reshape(1, DA_V_DIM),
                              B, S, lam_init)
        dl = [_dilated_group(proj.reshape(B, S, proj.shape[1]), CT_DL, 0, B, S),
              _dilated_group(dl1, 0, 1, B, S), _dilated_group(dl2, 0, 2, B, S)]

        w_r = jnp.concatenate([w_expert_router[l], w_group_router[l]], axis=1)
        w_r = jnp.pad(w_r, ((0, 0), (0, LANES - w_r.shape[1])))
        r_hi = w_r.astype(BF16)
        r_lo = (w_r - r_hi.astype(F32)).astype(BF16)
        x1, hn, route = _outproj(
            x2, o_a, proj, [t[0] for t in dl], [t[1] for t in dl],
            w_branch_a[l].astype(BF16), w_branch_b[l].astype(BF16), w_out[l].astype(BF16),
            norm_ffn[l].reshape(1, D), r_hi, jnp.concatenate([r_hi, r_lo], axis=1))

        src_tok, dst_row, blk_expert, n_used, n_rows = _dispatch_tables(route, T)
        y = _experts(hn, src_tok, dst_row, blk_expert, n_used, w_gate_up[l], w_down[l], n_rows)
        x2 = _combine(x1, route, y)
    return x2.reshape(B, S, D)
```

```python
import functools
import math

import jax
import jax.numpy as jnp
import numpy as np
from jax import lax
from jax.experimental import pallas as pl
from jax.experimental.pallas import tpu as pltpu

F32 = jnp.float32
BF16 = jnp.bfloat16

EPS = 1e-6
LOG2E = 1.4426950408889634
NEG_BIG = -1e30

DA_HEADS = 8
DA_QK_DIM = 64
DA_V_DIM = 128
DA_TILE_GROUP = 4
DL_GROUPS = ((128, 1), (512, 4), (2048, 16))
DL_HEADS_PER_GROUP = 4
DL_HEAD_DIM = 128
DL_SPAN = 128
MOE_GROUPS = 4
MOE_EXPERTS_PER_GROUP = 8
MOE_N_EXPERTS = 32
MOE_TOP_K = 2
MOE_BLOCK = 256
WEIGHT_DMA_PRIORITY = 1

LANES = 128
COL_TILE = 512
VMEM_LIMIT = 56 * 1024 * 1024

CT_GATE_A, CT_GATE_B, CT_DA_Q, CT_DA_K, CT_DA_V, CT_DL, CT_MAIN_END, CT_END = 0, 4, 8, 10, 12, 14, 17, 23


def _params(sem, vmem=VMEM_LIMIT):
    return pltpu.CompilerParams(dimension_semantics=sem, vmem_limit_bytes=vmem)


def _dot(a, b):
    return jnp.dot(a, b, preferred_element_type=F32)


def _dot_nt(a, b):
    return lax.dot_general(a, b, (((1,), (1,)), ((), ())), preferred_element_type=F32)


def _inproj_kernel(x_ref, g_ref, w_ref, gain_ref, o_ref, d1_ref, d2_ref, h_scr, y_scr):
    j = pl.program_id(1)

    @pl.when(j == 0)
    def _():
        x = x_ref[...]
        ms = jnp.mean(x * x, axis=-1, keepdims=True)
        h_scr[...] = (x * lax.rsqrt(ms + EPS) * g_ref[...]).astype(BF16)

    y = _dot(h_scr[...], w_ref[...])
    gain = gain_ref[...]
    heads = COL_TILE // LANES

    is64 = (j >= CT_DA_Q) & (j < CT_DA_V)
    is128 = (j >= CT_DL) & (lax.rem(j - CT_DL, 3) < 2)
    main = j < CT_MAIN_END

    def norm128(h):
        sl = slice(h * LANES, (h + 1) * LANES)
        yh = y[:, sl]
        ss = jnp.sum(yh * yh, axis=-1, keepdims=True)
        return yh * lax.rsqrt(ss * (1.0 / DL_HEAD_DIM) + EPS) * gain[:, sl]

    @pl.when(is64)
    def _():
        for h in range(heads):
            sl = slice(h * LANES, (h + 1) * LANES)
            yh = y[:, sl]
            sq = yh * yh
            lo = lax.broadcasted_iota(jnp.int32, yh.shape, 1) < DA_QK_DIM
            s_lo = jnp.sum(jnp.where(lo, sq, 0.0), axis=-1, keepdims=True)
            s_hi = jnp.sum(jnp.where(lo, 0.0, sq), axis=-1, keepdims=True)
            r = jnp.where(lo, lax.rsqrt(s_lo * (1.0 / DA_QK_DIM) + EPS),
                          lax.rsqrt(s_hi * (1.0 / DA_QK_DIM) + EPS))
            o_ref[:, sl] = (yh * r * gain[:, sl]).astype(o_ref.dtype)

    @pl.when(is128 & main)
    def _():
        for h in range(heads):
            o_ref[:, h * LANES:(h + 1) * LANES] = norm128(h).astype(o_ref.dtype)

    @pl.when(is128 & jnp.logical_not(main))
    def _():
        for h in range(heads):
            y_scr[h] = norm128(h)

    plain = jnp.logical_not(is64 | is128)

    @pl.when(plain & main)
    def _():
        o_ref[...] = y.astype(o_ref.dtype)

    @pl.when(plain & jnp.logical_not(main))
    def _():
        for h in range(heads):
            y_scr[h] = y[:, h * LANES:(h + 1) * LANES]

    def deinterleave(dst_ref):
        d, rows = dst_ref.shape[0], dst_ref.shape[1]
        for r in range(d):
            for h in range(heads):
                dst_ref[r, :, h * LANES:(h + 1) * LANES] = (
                    y_scr[h, pl.ds(r, rows, stride=d), :].astype(dst_ref.dtype))

    @pl.when((j >= CT_MAIN_END) & (j < CT_MAIN_END + 3))
    def _():
        deinterleave(d1_ref)

    @pl.when(j >= CT_MAIN_END + 3)
    def _():
        deinterleave(d2_ref)


def _inproj(x2, gain_mix, w_bf, gain_tab, B, S, tm=1024):
    T, D = x2.shape
    tiles_per_batch = S // tm
    d1, d2 = DL_GROUPS[1][1], DL_GROUPS[2][1]
    part1 = lambda j: jnp.clip(j - CT_MAIN_END, 0, 2)
    part2 = lambda j: jnp.clip(j - CT_MAIN_END - 3, 0, 2)
    return pl.pallas_call(
        _inproj_kernel,
        grid=(T // tm, CT_END),
        in_specs=[
            pl.BlockSpec((tm, D), lambda i, j: (i, 0)),
            pl.BlockSpec((1, D), lambda i, j: (0, 0)),
            pl.BlockSpec((D, COL_TILE), lambda i, j: (0, j)),
            pl.BlockSpec((None, 1, COL_TILE), lambda i, j: (j, 0, 0)),
        ],
        out_specs=[
            pl.BlockSpec((tm, COL_TILE), lambda i, j: (i, jnp.minimum(j, CT_MAIN_END - 1))),
            pl.BlockSpec((d1, tm // d1, COL_TILE),
                         lambda i, j: (i // tiles_per_batch, i % tiles_per_batch, part1(j))),
            pl.BlockSpec((d2, tm // d2, COL_TILE),
                         lambda i, j: (i // tiles_per_batch, i % tiles_per_batch, part2(j))),
        ],
        out_shape=[
            jax.ShapeDtypeStruct((T, CT_MAIN_END * COL_TILE), BF16),
            jax.ShapeDtypeStruct((B * d1, S // d1, 3 * COL_TILE), BF16),
            jax.ShapeDtypeStruct((B * d2, S // d2, 3 * COL_TILE), BF16),
        ],
        scratch_shapes=[pltpu.VMEM((tm, D), BF16), pltpu.VMEM((COL_TILE // LANES, tm, LANES), F32)],
        compiler_params=_params(("parallel", "arbitrary")),
    )(x2, gain_mix, w_bf, gain_tab)


def _bf16_pieces(x, n=3):
    out = []
    r = np.float64(x)
    for _ in range(n):
        p = np.asarray(np.float32(r)).astype(jnp.bfloat16).astype(np.float64)
        out.append(float(p))
        r = r - p
    return out


def _alibi_features(tk):
    pieces = _bf16_pieces(LOG2E)
    qf = np.zeros((2, LANES), np.float32)
    kf = np.zeros((2, tk, LANES), np.float32)
    j = np.arange(tk)
    hi, lo = (j // 16) * 16, j % 16
    for m in range(2):
        f0 = DA_QK_DIM if m == 0 else 0
        for n, p in enumerate(pieces):
            qf[m, f0 + 2 * n] = p
            qf[m, f0 + 2 * n + 1] = p
            kf[m, :, f0 + 2 * n] = hi
            kf[m, :, f0 + 2 * n + 1] = lo
    return jnp.asarray(qf), jnp.asarray(kf, dtype=BF16)


def _da_kernel(q_ref, k_ref, v_ref, qf_ref, kf_ref, lq_ref, lk_ref, sg_ref, o_ref,
               s00, s01, s10, s11, p00, p01, p10, p11, pd0, pd1,
               m0_scr, m1_scr, l0_scr, l1_scr, a0_scr, a1_scr, acc0_scr, acc1_scr, *, tq, rc, lam_init):
    h = pl.program_id(1)
    qi = pl.program_id(2)
    nlb = tq // LANES
    pow2 = jnp.exp2(-(h + 1).astype(F32))
    slope2 = pow2 * LOG2E

    q = q_ref[...]
    lane = lax.broadcasted_iota(jnp.int32, (tq, LANES), 1)
    own = (lane < DA_QK_DIM, lane >= DA_QK_DIM)
    qfs = [jnp.where(own[mi], q, jnp.broadcast_to((qf_ref[mi:mi + 1, :] * pow2).astype(BF16), q.shape))
           for mi in range(2)]

    m_scrs, l_scrs, a_scrs, acc_scrs = (m0_scr, m1_scr), (l0_scr, l1_scr), (a0_scr, a1_scr), (acc0_scr, acc1_scr)
    for mi in range(2):
        m_scrs[mi][...] = jnp.full(m_scrs[mi].shape, NEG_BIG, F32)
        l_scrs[mi][...] = jnp.zeros(l_scrs[mi].shape, F32)
        acc_scrs[mi][...] = jnp.zeros(acc_scrs[mi].shape, F32)

    def scores(ki, mi, s_ref):
        k = k_ref[pl.ds(pl.multiple_of(ki * tq, tq), tq), :]
        s_ref[...] = _dot_nt(qfs[mi], jnp.where(own[mi], k, kf_ref[mi]))

    def softmax(ki, mi, s_ref, p_ref, masked):
        m_scr, l_scr, a_scr = m_scrs[mi], l_scrs[mi], a_scrs[mi]
        c = slope2 * ((ki - qi) * tq).astype(F32)
        for r in range(tq // rc):
            rows = slice(r * rc, (r + 1) * rc)
            nb = min(nlb, ((r + 1) * rc - 1) // LANES + 1) if masked else nlb
            sb = []
            for j in range(nb):
                cs = slice(j * LANES, (j + 1) * LANES)
                s = s_ref[rows, cs]
                if masked and (j + 1) * LANES - 1 > r * rc:
                    rr = lax.broadcasted_iota(jnp.int32, (rc, LANES), 0) + r * rc
                    cc = lax.broadcasted_iota(jnp.int32, (rc, LANES), 1) + j * LANES
                    s = jnp.where(cc <= rr, s, NEG_BIG)
                sb.append(s)
            mx = sb[0]
            for s in sb[1:]:
                mx = jnp.maximum(mx, s)
            m_prev = m_scr[rows, :]
            m_new = jnp.maximum(m_prev, jnp.max(mx, axis=-1, keepdims=True) + c)
            alpha = jnp.exp2(m_prev - m_new)
            a_scr[rows, :] = alpha
            m_scr[rows, :] = m_new
            mc = m_new - c
            psum = alpha * l_scr[rows, :]
            for j in range(nlb):
                cs = slice(j * LANES, (j + 1) * LANES)
                if j < nb:
                    p = jnp.exp2(sb[j] - mc)
                    psum = psum + p
                    p_ref[rows, cs] = p.astype(BF16)
                else:
                    p_ref[rows, cs] = jnp.zeros((rc, LANES), BF16)
            l_scr[rows, :] = psum

    def values(ki, mi, p_ref):
        v = v_ref[pl.ds(pl.multiple_of(ki * tq, tq), tq), :]
        acc_scrs[mi][...] = a_scrs[mi][...] * acc_scrs[mi][...] + _dot(p_ref[...], v)

    s_bufs, p_bufs = ((s00, s01), (s10, s11)), ((p00, p01), (p10, p11))

    def tile_group(k0, n, last_masked):
        for mi in range(2):
            scores(k0, mi, s_bufs[0][mi])
        for i in range(n):
            masked = last_masked and i == n - 1
            for mi in range(2):
                p_ref = (pd0, pd1)[mi] if masked else p_bufs[i % 2][mi]
                softmax(k0 + i, mi, s_bufs[i % 2][mi], p_ref, masked)
                values(k0 + i, mi, p_ref)
                if i + 1 < n:
                    scores(k0 + i + 1, mi, s_bufs[(i + 1) % 2][mi])

    def body(t, carry):
        tile_group(DA_TILE_GROUP * t, DA_TILE_GROUP, False)
        return carry

    n_full = qi // DA_TILE_GROUP
    lax.fori_loop(0, n_full, body, 0)
    for rem in range(1, DA_TILE_GROUP + 1):
        @pl.when(qi - DA_TILE_GROUP * n_full == rem - 1)
        def _(rem=rem):
            tile_group(qi - (rem - 1), rem, True)

    lam_e = jnp.exp(jnp.sum(lq_ref[...] * lk_ref[...], axis=-1, keepdims=True))
    lam = lam_e[0:1, :] - lam_e[1:2, :] + lam_init
    l0 = jnp.sum(l0_scr[...], axis=-1, keepdims=True)
    l1 = jnp.sum(l1_scr[...], axis=-1, keepdims=True)
    o = acc0_scr[...] / l0 - lam * (acc1_scr[...] / l1)
    ms = jnp.mean(o * o, axis=-1, keepdims=True)
    o = o * lax.rsqrt(ms + EPS) * sg_ref[...] * (1.0 - lam_init)
    o_ref[...] = o.astype(o_ref.dtype)


def _diff_attention(proj, lam_q, lam_k, sub_gain, B, S, lam_init, tq=512, rc=32):
    T = proj.shape[0]
    nq = S // tq
    lb = LANES
    q_blk0, k_blk0, v_blk0 = (CT_DA_Q * COL_TILE) // lb, (CT_DA_K * COL_TILE) // lb, (CT_DA_V * COL_TILE) // lb
    qfeat, kfeat = _alibi_features(tq)
    const = lambda shape: pl.BlockSpec(shape, lambda b, h, i: (0,) * len(shape))
    return pl.pallas_call(
        functools.partial(_da_kernel, tq=tq, rc=rc, lam_init=lam_init),
        grid=(B, DA_HEADS, nq),
        in_specs=[
            pl.BlockSpec((tq, lb), lambda b, h, i: (b * nq + i, q_blk0 + h)),
            pl.BlockSpec((S, lb), lambda b, h, i: (b, k_blk0 + h)),
            pl.BlockSpec((S, lb), lambda b, h, i: (b, v_blk0 + h)),
            const((2, LANES)), const((2, tq, LANES)),
            const((2, DA_QK_DIM)), const((2, DA_QK_DIM)), const((1, DA_V_DIM)),
        ],
        out_specs=pl.BlockSpec((tq, lb), lambda b, h, i: (b * nq + i, h)),
        out_shape=jax.ShapeDtypeStruct((T, DA_HEADS * DA_V_DIM), BF16),
        scratch_shapes=[pltpu.VMEM((tq, tq), F32)] * 4 + [pltpu.VMEM((tq, tq), BF16)] * 6
        + [pltpu.VMEM((tq, LANES), F32)] * 6 + [pltpu.VMEM((tq, DA_V_DIM), F32)] * 2,
        compiler_params=_params(("parallel", "parallel", "arbitrary")),
    )(proj, proj, proj, qfeat, kfeat, lam_q, lam_k, sub_gain)


def _dl_kernel(q_ref, kc_ref, kp_ref, vc_ref, vp_ref, o_ref, lse_ref, *, slopes2, d, tq, ru):
    n = pl.program_id(1)
    sp = DL_SPAN
    row = lax.broadcasted_iota(jnp.int32, (sp, sp), 0)
    col = lax.broadcasted_iota(jnp.int32, (sp, sp), 1)
    dcur = row - col
    cur_ok = dcur >= 0
    prev_ok = dcur <= 0
    dcur_f = dcur.astype(F32)

    def scores(r, hh, j):
        hs = slice(hh * LANES, (hh + 1) * LANES)
        rs = slice(j * sp, (j + 1) * sp)
        q = q_ref[r, rs, hs]
        if j == 0:
            kp, vp, p_ok = kp_ref[r, :, hs], vp_ref[r, :, hs], prev_ok & (n > 0)
        else:
            ps = slice((j - 1) * sp, j * sp)
            kp, vp, p_ok = kc_ref[r, ps, hs], vc_ref[r, ps, hs], prev_ok
        s_c = jnp.where(cur_ok, _dot_nt(q, kc_ref[r, rs, hs]) - slopes2[hh] * dcur_f, NEG_BIG)
        s_p = jnp.where(p_ok, _dot_nt(q, kp) - slopes2[hh] * (dcur_f + float(sp)), NEG_BIG)
        return s_c, s_p, vc_ref[r, rs, hs], vp

    def finish(r, hh, j, s_c, s_p, vc, vp):
        m = jnp.max(jnp.maximum(s_c, s_p), axis=-1, keepdims=True)
        p_c = jnp.exp2(s_c - m)
        p_p = jnp.exp2(s_p - m)
        den = jnp.sum(p_c + p_p, axis=-1, keepdims=True)
        acc = _dot(p_c.astype(BF16), vc) + _dot(p_p.astype(BF16), vp)
        out_rows = pl.ds(j * sp, sp) if d == 1 else pl.ds(r + j * sp * d, sp, stride=d)
        o_ref[hh, out_rows, :] = acc / den
        lse_ref[hh, out_rows, :] = jnp.broadcast_to(m + jnp.log2(den), (sp, LANES))

    def residues(t, carry):
        units = [(t * ru + rr, hh, j) for rr in range(ru) for hh in range(DL_HEADS_PER_GROUP)
                 for j in range(tq // sp)]
        pending = []
        for u in units:
            pending.append((u, scores(*u)))
            if len(pending) > 2:
                u0, vals = pending.pop(0)
                finish(*u0, *vals)
        for u0, vals in pending:
            finish(*u0, *vals)
        return carry

    lax.fori_loop(0, d // ru, residues, 0)


def _dilated_group(src, col0, g, B, S, tok_per_step=2048):
    window, d = DL_GROUPS[g]
    assert window // d == DL_SPAN
    L = S // d
    tq = min(tok_per_step, S) // d
    assert tq % DL_SPAN == 0 and L % tq == 0
    nh = DL_HEADS_PER_GROUP * len(DL_GROUPS)
    slopes2 = tuple(2.0 ** (-8.0 * (g * DL_HEADS_PER_GROUP + hh + 1) / nh) * d * LOG2E
                    for hh in range(DL_HEADS_PER_GROUP))
    spb = tq // DL_SPAN
    nsteps = L // tq
    cur = lambda c: pl.BlockSpec((d, tq, COL_TILE), lambda b, n: (b, n, c))
    prev = lambda c: pl.BlockSpec((d, DL_SPAN, COL_TILE), lambda b, n: (b, jnp.maximum(n * spb - 1, 0), c))
    out_spec = pl.BlockSpec((DL_HEADS_PER_GROUP, d * tq, LANES), lambda b, n: (0, b * nsteps + n, 0))
    return pl.pallas_call(
        functools.partial(_dl_kernel, slopes2=slopes2, d=d, tq=tq, ru=min(d, 4)),
        grid=(B, nsteps),
        in_specs=[cur(col0), cur(col0 + 1), prev(col0 + 1), cur(col0 + 2), prev(col0 + 2)],
        out_specs=[out_spec, out_spec],
        out_shape=[jax.ShapeDtypeStruct((DL_HEADS_PER_GROUP, B * S, LANES), F32)] * 2,
        compiler_params=_params(("parallel", "arbitrary")),
    )(src, src, src, src, src)


def _route(logits):
    lane = lax.broadcasted_iota(jnp.int32, logits.shape, 1)
    big = jnp.int32(1 << 20)
    is_g = (lane >= MOE_N_EXPERTS) & (lane < MOE_N_EXPERTS + MOE_GROUPS)
    lg = jnp.where(is_g, logits, -jnp.inf)
    gmax = jnp.max(lg, axis=-1, keepdims=True)
    gsum = jnp.sum(jnp.exp(lg - gmax), axis=-1, keepdims=True)
    g_w = 1.0 / gsum
    g_idx = jnp.min(jnp.where(lg == gmax, lane - MOE_N_EXPERTS, big), axis=-1, keepdims=True)
    in_grp = (lane < MOE_N_EXPERTS) & ((lane // MOE_EXPERTS_PER_GROUP) == g_idx)
    le = jnp.where(in_grp, logits, -jnp.inf)
    t1 = jnp.max(le, axis=-1, keepdims=True)
    e1 = jnp.min(jnp.where(le == t1, lane, big), axis=-1, keepdims=True)
    le2 = jnp.where(lane == e1, -jnp.inf, le)
    t2 = jnp.max(le2, axis=-1, keepdims=True)
    e2 = jnp.min(jnp.where(le2 == t2, lane, big), axis=-1, keepdims=True)
    r = jnp.exp(t2 - t1)
    w1 = g_w / (1.0 + r)
    w2 = w1 * r
    out = jnp.where(lane == 0, e1.astype(F32),
                    jnp.where(lane == 1, e2.astype(F32),
                              jnp.where(lane == 2, w1, jnp.where(lane == 3, w2, 0.0))))
    return out


def _outproj_kernel(x_ref, oa_ref, ga_ref, gb_ref, o0_ref, o1_ref, o2_ref, l0_ref, l1_ref, l2_ref,
                    wa_ref, wb_ref, wo_ref, gf_ref, rh_ref, rc_ref,
                    x1_ref, hn_ref, rt_ref):
    obs = []
    for hh in range(DL_HEADS_PER_GROUP):
        l0, l1, l2 = l0_ref[hh], l1_ref[hh], l2_ref[hh]
        lm = jnp.maximum(jnp.maximum(l0, l1), l2)
        e0, e1, e2 = jnp.exp2(l0 - lm), jnp.exp2(l1 - lm), jnp.exp2(l2 - lm)
        obs.append((e0 * o0_ref[hh] + e1 * o1_ref[hh] + e2 * o2_ref[hh]) / (e0 + e1 + e2))
    ob = jnp.concatenate(obs, axis=1)
    a = _dot(oa_ref[...], wa_ref[...])
    b = _dot(ob.astype(BF16), wb_ref[...])
    mixed = jax.nn.sigmoid(ga_ref[...].astype(F32)) * a + jax.nn.sigmoid(gb_ref[...].astype(F32)) * b
    x1 = x_ref[...] + _dot(mixed.astype(BF16), wo_ref[...])
    x1_ref[...] = x1
    ms = jnp.mean(x1 * x1, axis=-1, keepdims=True)
    hn = x1 * lax.rsqrt(ms + EPS) * gf_ref[...]
    hn_ref[...] = hn
    hn_hi = hn.astype(BF16)
    hn_lo = (hn - hn_hi.astype(F32)).astype(BF16)
    t = _dot(hn_hi, rc_ref[...])
    logits = t[:, 0:LANES] + (_dot(hn_lo, rh_ref[...]) + t[:, LANES:2 * LANES])
    rt_ref[...] = _route(logits)


def _outproj(x2, o_a, proj, dl_o, dl_lse, wa, wb, wo, gain_ffn, r_hi, r_cat, tm=256):
    T, D = x2.shape
    row = lambda w: pl.BlockSpec((tm, w), lambda i: (i, 0))
    full = lambda s: pl.BlockSpec(s, lambda i: (0, 0), pipeline_mode=pl.Buffered(1))
    hrow = pl.BlockSpec((DL_HEADS_PER_GROUP, tm, LANES), lambda i: (0, i, 0))
    return pl.pallas_call(
        _outproj_kernel,
        grid=(T // tm,),
        in_specs=[
            row(D), row(o_a.shape[1]),
            pl.BlockSpec((tm, D), lambda i: (i, (CT_GATE_A * COL_TILE) // D)),
            pl.BlockSpec((tm, D), lambda i: (i, (CT_GATE_B * COL_TILE) // D)),
            hrow, hrow, hrow, hrow, hrow, hrow,
            full(wa.shape), full(wb.shape), full(wo.shape), full((1, D)), full(r_hi.shape), full(r_cat.shape),
        ],
        out_specs=[row(D), row(D), row(LANES)],
        out_shape=[jax.ShapeDtypeStruct((T, D), F32), jax.ShapeDtypeStruct((T, D), F32),
                   jax.ShapeDtypeStruct((T, LANES), F32)],
        compiler_params=_params(("parallel",)),
    )(x2, o_a, proj, proj, dl_o[0], dl_o[1], dl_o[2], dl_lse[0], dl_lse[1], dl_lse[2],
      wa, wb, wo, gain_ffn, r_hi, r_cat)


def _cast_rows(src_ref, dst_ref, chunk=256):
    def body(c, carry):
        r0 = pl.multiple_of(c * chunk, chunk)
        dst_ref[pl.ds(r0, chunk), :] = src_ref[pl.ds(r0, chunk), :].astype(dst_ref.dtype)
        return carry
    lax.fori_loop(0, src_ref.shape[0] // chunk, body, 0)


def _expert_changed(be_ref, i):
    return (i == 0) | (be_ref[i] != be_ref[jnp.maximum(i - 1, 0)])


def _stream_expert_weights(i, live, be_ref, ne_ref, w_hbm, stage, wbf, wsem):
    changed = live & _expert_changed(be_ref, i)

    def copy(e):
        return pltpu.make_async_copy(w_hbm.at[e], stage, wsem)

    @pl.when(changed & (i == 0))
    def _():
        copy(be_ref[0]).start(priority=WEIGHT_DMA_PRIORITY)

    @pl.when(changed)
    def _():
        copy(be_ref[i]).wait()
        _cast_rows(stage, wbf)

    @pl.when(changed & (ne_ref[i] >= 0))
    def _():
        copy(ne_ref[i]).start(priority=WEIGHT_DMA_PRIORITY)


def _moe_up_kernel(be_ref, ne_ref, nu_ref, tc_ref, tn_ref, hn_ref, wgu_ref, act_ref,
                   xa, xb, wstage, wbf, gsem, wsem, *, d_ff):
    i = pl.program_id(0)
    nu = nu_ref[0]
    rows = xa.shape[0]
    even = i % 2 == 0

    def gather(tok_ref, r, buf, s):
        return pltpu.make_async_copy(hn_ref.at[tok_ref[0, r]], buf.at[r], gsem.at[s])

    def wait_gather(buf, s):
        pltpu.make_async_copy(hn_ref.at[pl.ds(0, rows)], buf, gsem.at[s]).wait()

    @pl.when((i == 0) & (nu > 0))
    def _():
        def body(r, c):
            gather(tc_ref, r, xa, 0).start()
            return c
        lax.fori_loop(0, rows, body, 0)

    _stream_expert_weights(i, i < nu, be_ref, ne_ref, wgu_ref, wstage, wbf, wsem)

    def live_step(cur, nxt, s):
        wait_gather(cur, s)
        for r in range(rows):
            gather(tn_ref, r, nxt, 1 - s).start()
        h = _dot(cur[...].astype(BF16), wbf[...])
        gate = h[:, :d_ff]
        up = h[:, d_ff:]
        act_ref[...] = (gate * jax.nn.sigmoid(gate) * up).astype(act_ref.dtype)

    @pl.when((i < nu) & even)
    def _():
        live_step(xa, xb, 0)

    @pl.when((i < nu) & jnp.logical_not(even))
    def _():
        live_step(xb, xa, 1)

    @pl.when((i == nu) & (nu > 0) & even)
    def _():
        wait_gather(xa, 0)

    @pl.when((i == nu) & (nu > 0) & jnp.logical_not(even))
    def _():
        wait_gather(xb, 1)

    @pl.when(i >= nu)
    def _():
        act_ref[...] = jnp.zeros(act_ref.shape, act_ref.dtype)


def _moe_down_kernel(be_ref, ne_ref, nu_ref, dp_ref, act_ref, wd_ref, y_ref, ya, yb, wstage, wbf, ssem, wsem):
    i = pl.program_id(0)
    nu = nu_ref[0]
    rows = ya.shape[0]
    even = i % 2 == 0

    def scatter(r, buf, s):
        return pltpu.make_async_copy(buf.at[r], y_ref.at[dp_ref[0, r]], ssem.at[s])

    def wait_scatter(buf, s):
        pltpu.make_async_copy(buf, y_ref.at[pl.ds(0, rows)], ssem.at[s]).wait()

    @pl.when(i == 0)
    def _():
        n_res = y_ref.shape[0] - 2 * rows
        for s, buf in enumerate((ya, yb)):
            buf[...] = jnp.zeros(buf.shape, buf.dtype)
            pltpu.make_async_copy(buf, y_ref.at[pl.ds(n_res + s * rows, rows)], ssem.at[s]).start()
        for s, buf in enumerate((ya, yb)):
            pltpu.make_async_copy(buf, y_ref.at[pl.ds(n_res + s * rows, rows)], ssem.at[s]).wait()

    @pl.when((i >= 2) & (i < nu + 2) & even)
    def _():
        wait_scatter(ya, 0)

    @pl.when((i >= 2) & (i < nu + 2) & jnp.logical_not(even))
    def _():
        wait_scatter(yb, 1)

    _stream_expert_weights(i, i < nu, be_ref, ne_ref, wd_ref, wstage, wbf, wsem)

    def step(cur, prv, s, do_scatter, do_compute):
        if do_scatter:
            for r in range(rows):
                scatter(r, prv, 1 - s).start()
        if do_compute:
            cur[...] = _dot(act_ref[...], wbf[...])

    for s, (cur, prv) in enumerate(((ya, yb), (yb, ya))):
        par = even if s == 0 else jnp.logical_not(even)

        @pl.when((i >= 1) & (i < nu) & par)
        def _(cur=cur, prv=prv, s=s):
            step(cur, prv, s, True, True)

        @pl.when((i == 0) & (nu > 0) & par)
        def _(cur=cur, prv=prv, s=s):
            step(cur, prv, s, False, True)

        @pl.when((i == nu) & (nu > 0) & par)
        def _(cur=cur, prv=prv, s=s):
            step(cur, prv, s, True, False)


def _experts(hn, src_tok, dst_row, blk_expert, nxt_expert, n_used, wgu, wd, n_out_rows):
    T, D = hn.shape
    nblk = src_tok.shape[0] // MOE_BLOCK
    d_ff = wd.shape[1]
    tok = src_tok.reshape(nblk, 1, MOE_BLOCK)
    dst = dst_row.reshape(nblk, 1, MOE_BLOCK)

    def live(i, nu):
        return jnp.maximum(jnp.minimum(i, nu[0] - 1), 0)

    smem = lambda f: pl.BlockSpec((None, 1, MOE_BLOCK), f, memory_space=pltpu.SMEM)
    act = pl.pallas_call(
        functools.partial(_moe_up_kernel, d_ff=d_ff),
        grid_spec=pltpu.PrefetchScalarGridSpec(
            num_scalar_prefetch=3,
            grid=(nblk - 1,),
            in_specs=[
                smem(lambda i, be, ne, nu: (i, 0, 0)),
                smem(lambda i, be, ne, nu: (i + 1, 0, 0)),
                pl.BlockSpec(memory_space=pl.ANY),
                pl.BlockSpec(memory_space=pl.ANY),
            ],
            out_specs=pl.BlockSpec((MOE_BLOCK, d_ff), lambda i, be, ne, nu: (i, 0)),
            scratch_shapes=[pltpu.VMEM((MOE_BLOCK, D), F32), pltpu.VMEM((MOE_BLOCK, D), F32),
                            pltpu.VMEM((D, 2 * d_ff), F32), pltpu.VMEM((D, 2 * d_ff), BF16),
                            pltpu.SemaphoreType.DMA((2,)), pltpu.SemaphoreType.DMA(())],
        ),
        out_shape=jax.ShapeDtypeStruct(((nblk - 1) * MOE_BLOCK, d_ff), BF16),
        compiler_params=_params(("arbitrary",)),
    )(blk_expert, nxt_expert, n_used, tok, tok, hn, wgu)
    return pl.pallas_call(
        _moe_down_kernel,
        grid_spec=pltpu.PrefetchScalarGridSpec(
            num_scalar_prefetch=3,
            grid=(nblk,),
            in_specs=[
                smem(lambda i, be, ne, nu: (jnp.maximum(i - 1, 0), 0, 0)),
                pl.BlockSpec((MOE_BLOCK, d_ff), lambda i, be, ne, nu: (live(i, nu), 0)),
                pl.BlockSpec(memory_space=pl.ANY),
            ],
            out_specs=pl.BlockSpec(memory_space=pl.ANY),
            scratch_shapes=[pltpu.VMEM((MOE_BLOCK, D), F32), pltpu.VMEM((MOE_BLOCK, D), F32),
                            pltpu.VMEM((d_ff, D), F32), pltpu.VMEM((d_ff, D), BF16),
                            pltpu.SemaphoreType.DMA((2,)), pltpu.SemaphoreType.DMA(())],
        ),
        out_shape=jax.ShapeDtypeStruct((n_out_rows, D), F32),
        compiler_params=_params(("arbitrary",)),
    )(blk_expert, nxt_expert, n_used, dst, act, wd)


def _combine_kernel(x1_ref, rt_ref, y1_ref, y2_ref, o_ref):
    rt = rt_ref[...]
    o_ref[...] = x1_ref[...] + (rt[:, 2:3] * y1_ref[...] + rt[:, 3:4] * y2_ref[...])


def _combine(x1, route, y, tm=512):
    T, D = x1.shape
    nt = T // tm
    return pl.pallas_call(
        _combine_kernel,
        grid=(nt,),
        in_specs=[
            pl.BlockSpec((tm, D), lambda i: (i, 0)),
            pl.BlockSpec((tm, LANES), lambda i: (i, 0)),
            pl.BlockSpec((tm, D), lambda i: (i, 0)),
            pl.BlockSpec((tm, D), lambda i: (nt + i, 0)),
        ],
        out_specs=pl.BlockSpec((tm, D), lambda i: (i, 0)),
        out_shape=jax.ShapeDtypeStruct((T, D), F32),
        compiler_params=_params(("parallel",)),
    )(x1, route, y, y)


def _dispatch_tables(route, T):
    TK = T * MOE_TOP_K
    flat_e = route[:, :MOE_TOP_K].astype(jnp.int32).reshape(-1)
    order = jnp.argsort(flat_e).astype(jnp.int32)
    sizes = jnp.sum(flat_e[:, None] == jnp.arange(MOE_N_EXPERTS, dtype=jnp.int32)[None, :], axis=0,
                    dtype=jnp.int32)
    start = jnp.cumsum(sizes) - sizes
    padded = ((sizes + MOE_BLOCK - 1) // MOE_BLOCK) * MOE_BLOCK
    pad_end = jnp.cumsum(padded)
    pad_start = pad_end - padded
    n_blocks = TK // MOE_BLOCK + MOE_N_EXPERTS
    blk = jnp.arange(n_blocks + 2, dtype=jnp.int32)
    blk_expert = jnp.minimum(jnp.sum(pad_end[None, :] <= (blk * MOE_BLOCK)[:, None], axis=1, dtype=jnp.int32),
                             MOE_N_EXPERTS - 1)
    r = jnp.arange(MOE_BLOCK, dtype=jnp.int32)[None, :]
    pos = blk[:, None] * MOE_BLOCK + r
    off = pos - pad_start[blk_expert][:, None]
    live = (off < sizes[blk_expert][:, None]) & (pos < pad_end[-1])
    sorted_idx = jnp.where(live, off + start[blk_expert][:, None], 0)
    slot = order[sorted_idx]
    tok = slot // MOE_TOP_K
    src_tok = jnp.where(live, tok, 0).reshape(-1)
    trash = TK + (blk[:, None] % 2) * MOE_BLOCK + r
    dst_row = jnp.where(live, (slot % MOE_TOP_K) * T + tok, trash).reshape(-1)
    n_used = (pad_end[-1:] // MOE_BLOCK).astype(jnp.int32)
    eid = jnp.arange(MOE_N_EXPERTS, dtype=jnp.int32)
    later = (eid[None, :] > eid[:, None]) & (sizes[None, :] > 0)
    nxt = jnp.min(jnp.where(later, eid[None, :], MOE_N_EXPERTS), axis=1)
    nxt_expert = jnp.where(nxt < MOE_N_EXPERTS, nxt, -1).astype(jnp.int32)[blk_expert]
    return src_tok, dst_row, blk_expert, nxt_expert, n_used, TK + 2 * MOE_BLOCK


def _permute_w_in(w):
    da = DA_HEADS * 2 * DA_QK_DIM
    dav = DA_HEADS * DA_V_DIM
    dl = len(DL_GROUPS) * DL_HEADS_PER_GROUP * DL_HEAD_DIM
    o = [int(v) for v in np.cumsum([0, da, da, dav, dl, dl, dl])]
    parts = [w[:, o[6]:], w[:, :o[3]]]
    for g in range(len(DL_GROUPS)):
        for p in range(3):
            parts.append(w[:, o[3 + p] + g * COL_TILE: o[3 + p] + (g + 1) * COL_TILE])
    return jnp.concatenate(parts, axis=1)


def _gain_table(da_q_norm, da_k_norm, dl_q_norm, dl_k_norm):
    ones = jnp.ones((COL_TILE,), F32)
    daq = jnp.tile(da_q_norm, COL_TILE // DA_QK_DIM) * (DA_QK_DIM ** -0.5 * LOG2E)
    dak = jnp.tile(da_k_norm, COL_TILE // DA_QK_DIM)
    dlq = jnp.tile(dl_q_norm, COL_TILE // DL_HEAD_DIM) * (DL_HEAD_DIM ** -0.5 * LOG2E)
    dlk = jnp.tile(dl_k_norm, COL_TILE // DL_HEAD_DIM)
    rows = []
    for j in range(CT_END):
        if CT_DA_Q <= j < CT_DA_K:
            rows.append(daq)
        elif CT_DA_K <= j < CT_DA_V:
            rows.append(dak)
        elif j >= CT_DL and (j - CT_DL) % 3 == 0:
            rows.append(dlq)
        elif j >= CT_DL and (j - CT_DL) % 3 == 1:
            rows.append(dlk)
        else:
            rows.append(ones)
    return jnp.stack(rows, axis=0).reshape(CT_END, 1, COL_TILE)


def kernel(x, norm_mix, w_in, da_q_norm, da_k_norm, da_lambda_q, da_lambda_k, da_sub_norm,
           dl_q_norm, dl_k_norm, w_branch_a, w_branch_b, w_out, norm_ffn,
           w_group_router, w_expert_router, w_gate_up, w_down):
    B, S, D = x.shape
    T = B * S
    depth = w_in.shape[0]
    x2 = x.reshape(T, D)
    for l in range(depth):
        lam_init = 0.8 - 0.6 * math.exp(-0.3 * l)
        w_in_bf = _permute_w_in(w_in[l]).astype(BF16)
        gain_tab = _gain_table(da_q_norm[l], da_k_norm[l], dl_q_norm[l], dl_k_norm[l])
        proj, dl1, dl2 = _inproj(x2, norm_mix[l].reshape(1, D), w_in_bf, gain_tab, B, S)

        o_a = _diff_attention(proj, da_lambda_q[l], da_lambda_k[l], da_sub_norm[l].reshape(1, DA_V_DIM),
                              B, S, lam_init)
        dl = [_dilated_group(proj.reshape(B, S, proj.shape[1]), CT_DL, 0, B, S),
              _dilated_group(dl1, 0, 1, B, S), _dilated_group(dl2, 0, 2, B, S)]

        w_r = jnp.concatenate([w_expert_router[l], w_group_router[l]], axis=1)
        w_r = jnp.pad(w_r, ((0, 0), (0, LANES - w_r.shape[1])))
        r_hi = w_r.astype(BF16)
        r_lo = (w_r - r_hi.astype(F32)).astype(BF16)
        x1, hn, route = _outproj(
            x2, o_a, proj, [t[0] for t in dl], [t[1] for t in dl],
            w_branch_a[l].astype(BF16), w_branch_b[l].astype(BF16), w_out[l].astype(BF16),
            norm_ffn[l].reshape(1, D), r_hi, jnp.concatenate([r_hi, r_lo], axis=1))

        src_tok, dst_row, blk_expert, nxt_expert, n_used, n_rows = _dispatch_tables(route, T)
        y = _experts(hn, src_tok, dst_row, blk_expert, nxt_expert, n_used, w_gate_up[l], w_down[l], n_rows)
        x2 = _combine(x1, route, y)
    return x2.reshape(B, S, D)
```

```python
import functools
import math

import jax
import jax.numpy as jnp
import numpy as np
from jax import lax
from jax.experimental import pallas as pl
from jax.experimental.pallas import tpu as pltpu

F32 = jnp.float32
BF16 = jnp.bfloat16

EPS = 1e-6
LOG2E = 1.4426950408889634
NEG_BIG = -1e30

DA_HEADS = 8
DA_QK_DIM = 64
DA_V_DIM = 128
DA_TILE_GROUP = 4
DL_GROUPS = ((128, 1), (512, 4), (2048, 16))
DL_HEADS_PER_GROUP = 4
DL_HEAD_DIM = 128
DL_SPAN = 128
MOE_GROUPS = 4
MOE_EXPERTS_PER_GROUP = 8
MOE_N_EXPERTS = 32
MOE_TOP_K = 2
MOE_BLOCK = 256
WEIGHT_DMA_PRIORITY = 1

LANES = 128
COL_TILE = 512
VMEM_LIMIT = 56 * 1024 * 1024

CT_GATE_A, CT_GATE_B, CT_DA_Q, CT_DA_K, CT_DA_V, CT_DL, CT_MAIN_END, CT_END = 0, 4, 8, 10, 12, 14, 17, 23


def _params(sem, vmem=VMEM_LIMIT):
    return pltpu.CompilerParams(dimension_semantics=sem, vmem_limit_bytes=vmem)


def _dot(a, b):
    return jnp.dot(a, b, preferred_element_type=F32)


def _dot_nt(a, b):
    return lax.dot_general(a, b, (((1,), (1,)), ((), ())), preferred_element_type=F32)


def _inproj_kernel(x_ref, g_ref, w_ref, gain_ref, o_ref, d1_ref, d2_ref, h_scr, y_scr):
    j = pl.program_id(1)

    @pl.when(j == 0)
    def _():
        x = x_ref[...]
        ms = jnp.mean(x * x, axis=-1, keepdims=True)
        h_scr[...] = (x * lax.rsqrt(ms + EPS) * g_ref[...]).astype(BF16)

    y = _dot(h_scr[...], w_ref[...])
    gain = gain_ref[...]
    heads = COL_TILE // LANES

    is64 = (j >= CT_DA_Q) & (j < CT_DA_V)
    is128 = (j >= CT_DL) & (lax.rem(j - CT_DL, 3) < 2)
    main = j < CT_MAIN_END

    def norm128(h):
        sl = slice(h * LANES, (h + 1) * LANES)
        yh = y[:, sl]
        ss = jnp.sum(yh * yh, axis=-1, keepdims=True)
        return yh * lax.rsqrt(ss * (1.0 / DL_HEAD_DIM) + EPS) * gain[:, sl]

    @pl.when(is64)
    def _():
        for h in range(heads):
            sl = slice(h * LANES, (h + 1) * LANES)
            yh = y[:, sl]
            sq = yh * yh
            lo = lax.broadcasted_iota(jnp.int32, yh.shape, 1) < DA_QK_DIM
            s_lo = jnp.sum(jnp.where(lo, sq, 0.0), axis=-1, keepdims=True)
            s_hi = jnp.sum(jnp.where(lo, 0.0, sq), axis=-1, keepdims=True)
            r = jnp.where(lo, lax.rsqrt(s_lo * (1.0 / DA_QK_DIM) + EPS),
                          lax.rsqrt(s_hi * (1.0 / DA_QK_DIM) + EPS))
            o_ref[:, sl] = (yh * r * gain[:, sl]).astype(o_ref.dtype)

    @pl.when(is128 & main)
    def _():
        for h in range(heads):
            o_ref[:, h * LANES:(h + 1) * LANES] = norm128(h).astype(o_ref.dtype)

    @pl.when(is128 & jnp.logical_not(main))
    def _():
        for h in range(heads):
            y_scr[h] = norm128(h)

    plain = jnp.logical_not(is64 | is128)

    @pl.when(plain & main)
    def _():
        o_ref[...] = y.astype(o_ref.dtype)

    @pl.when(plain & jnp.logical_not(main))
    def _():
        for h in range(heads):
            y_scr[h] = y[:, h * LANES:(h + 1) * LANES]

    def deinterleave(dst_ref):
        d, rows = dst_ref.shape[0], dst_ref.shape[1]
        for r in range(d):
            for h in range(heads):
                dst_ref[r, :, h * LANES:(h + 1) * LANES] = (
                    y_scr[h, pl.ds(r, rows, stride=d), :].astype(dst_ref.dtype))

    @pl.when((j >= CT_MAIN_END) & (j < CT_MAIN_END + 3))
    def _():
        deinterleave(d1_ref)

    @pl.when(j >= CT_MAIN_END + 3)
    def _():
        deinterleave(d2_ref)


def _inproj(x2, gain_mix, w_bf, gain_tab, B, S, tm=1024):
    T, D = x2.shape
    tiles_per_batch = S // tm
    d1, d2 = DL_GROUPS[1][1], DL_GROUPS[2][1]
    part1 = lambda j: jnp.clip(j - CT_MAIN_END, 0, 2)
    part2 = lambda j: jnp.clip(j - CT_MAIN_END - 3, 0, 2)
    return pl.pallas_call(
        _inproj_kernel,
        grid=(T // tm, CT_END),
        in_specs=[
            pl.BlockSpec((tm, D), lambda i, j: (i, 0)),
            pl.BlockSpec((1, D), lambda i, j: (0, 0)),
            pl.BlockSpec((D, COL_TILE), lambda i, j: (0, j)),
            pl.BlockSpec((None, 1, COL_TILE), lambda i, j: (j, 0, 0)),
        ],
        out_specs=[
            pl.BlockSpec((tm, COL_TILE), lambda i, j: (i, jnp.minimum(j, CT_MAIN_END - 1))),
            pl.BlockSpec((d1, tm // d1, COL_TILE),
                         lambda i, j: (i // tiles_per_batch, i % tiles_per_batch, part1(j))),
            pl.BlockSpec((d2, tm // d2, COL_TILE),
                         lambda i, j: (i // tiles_per_batch, i % tiles_per_batch, part2(j))),
        ],
        out_shape=[
            jax.ShapeDtypeStruct((T, CT_MAIN_END * COL_TILE), BF16),
            jax.ShapeDtypeStruct((B * d1, S // d1, 3 * COL_TILE), BF16),
            jax.ShapeDtypeStruct((B * d2, S // d2, 3 * COL_TILE), BF16),
        ],
        scratch_shapes=[pltpu.VMEM((tm, D), BF16), pltpu.VMEM((COL_TILE // LANES, tm, LANES), F32)],
        compiler_params=_params(("parallel", "arbitrary")),
    )(x2, gain_mix, w_bf, gain_tab)


def _bf16_pieces(x, n=3):
    out = []
    r = np.float64(x)
    for _ in range(n):
        p = np.asarray(np.float32(r)).astype(jnp.bfloat16).astype(np.float64)
        out.append(float(p))
        r = r - p
    return out


def _alibi_features(tk):
    pieces = _bf16_pieces(LOG2E)
    qf = np.zeros((2, LANES), np.float32)
    kf = np.zeros((2, tk, LANES), np.float32)
    j = np.arange(tk)
    hi, lo = (j // 16) * 16, j % 16
    for m in range(2):
        f0 = DA_QK_DIM if m == 0 else 0
        for n, p in enumerate(pieces):
            qf[m, f0 + 2 * n] = p
            qf[m, f0 + 2 * n + 1] = p
            kf[m, :, f0 + 2 * n] = hi
            kf[m, :, f0 + 2 * n + 1] = lo
    return jnp.asarray(qf), jnp.asarray(kf, dtype=BF16)


def _da_kernel(q_ref, k_ref, v_ref, qf_ref, kf_ref, lq_ref, lk_ref, sg_ref, o_ref,
               s00, s01, s10, s11, p00, p01, p10, p11, pd0, pd1,
               m0_scr, m1_scr, l0_scr, l1_scr, a0_scr, a1_scr, acc0_scr, acc1_scr, *, tq, rc, lam_init):
    h = pl.program_id(1)
    qi = pl.program_id(2)
    nlb = tq // LANES
    pow2 = jnp.exp2(-(h + 1).astype(F32))
    slope2 = pow2 * LOG2E

    q = q_ref[...]
    lane = lax.broadcasted_iota(jnp.int32, (tq, LANES), 1)
    own = (lane < DA_QK_DIM, lane >= DA_QK_DIM)
    qfs = [jnp.where(own[mi], q, jnp.broadcast_to((qf_ref[mi:mi + 1, :] * pow2).astype(BF16), q.shape))
           for mi in range(2)]

    m_scrs, l_scrs, a_scrs, acc_scrs = (m0_scr, m1_scr), (l0_scr, l1_scr), (a0_scr, a1_scr), (acc0_scr, acc1_scr)
    for mi in range(2):
        m_scrs[mi][...] = jnp.full(m_scrs[mi].shape, NEG_BIG, F32)
        l_scrs[mi][...] = jnp.zeros(l_scrs[mi].shape, F32)
        acc_scrs[mi][...] = jnp.zeros(acc_scrs[mi].shape, F32)

    def scores(ki, mi, s_ref):
        k = k_ref[pl.ds(pl.multiple_of(ki * tq, tq), tq), :]
        s_ref[...] = _dot_nt(qfs[mi], jnp.where(own[mi], k, kf_ref[mi]))

    def softmax(ki, mi, s_ref, p_ref, masked):
        m_scr, l_scr, a_scr = m_scrs[mi], l_scrs[mi], a_scrs[mi]
        c = slope2 * ((ki - qi) * tq).astype(F32)
        for r in range(tq // rc):
            rows = slice(r * rc, (r + 1) * rc)
            nb = min(nlb, ((r + 1) * rc - 1) // LANES + 1) if masked else nlb
            sb = []
            for j in range(nb):
                cs = slice(j * LANES, (j + 1) * LANES)
                s = s_ref[rows, cs]
                if masked and (j + 1) * LANES - 1 > r * rc:
                    rr = lax.broadcasted_iota(jnp.int32, (rc, LANES), 0) + r * rc
                    cc = lax.broadcasted_iota(jnp.int32, (rc, LANES), 1) + j * LANES
                    s = jnp.where(cc <= rr, s, NEG_BIG)
                sb.append(s)
            mx = sb[0]
            for s in sb[1:]:
                mx = jnp.maximum(mx, s)
            m_prev = m_scr[rows, :]
            m_new = jnp.maximum(m_prev, jnp.max(mx, axis=-1, keepdims=True) + c)
            alpha = jnp.exp2(m_prev - m_new)
            a_scr[rows, :] = alpha
            m_scr[rows, :] = m_new
            mc = m_new - c
            psum = alpha * l_scr[rows, :]
            for j in range(nlb):
                cs = slice(j * LANES, (j + 1) * LANES)
                if j < nb:
                    p = jnp.exp2(sb[j] - mc)
                    psum = psum + p
                    p_ref[rows, cs] = p.astype(BF16)
                else:
                    p_ref[rows, cs] = jnp.zeros((rc, LANES), BF16)
            l_scr[rows, :] = psum

    def values(ki, mi, p_ref):
        v = v_ref[pl.ds(pl.multiple_of(ki * tq, tq), tq), :]
        acc_scrs[mi][...] = a_scrs[mi][...] * acc_scrs[mi][...] + _dot(p_ref[...], v)

    s_bufs, p_bufs = ((s00, s01), (s10, s11)), ((p00, p01), (p10, p11))

    def tile_group(k0, n, last_masked):
        for mi in range(2):
            scores(k0, mi, s_bufs[0][mi])
        for i in range(n):
            masked = last_masked and i == n - 1
            for mi in range(2):
                p_ref = (pd0, pd1)[mi] if masked else p_bufs[i % 2][mi]
                softmax(k0 + i, mi, s_bufs[i % 2][mi], p_ref, masked)
                values(k0 + i, mi, p_ref)
                if i + 1 < n:
                    scores(k0 + i + 1, mi, s_bufs[(i + 1) % 2][mi])

    def body(t, carry):
        tile_group(DA_TILE_GROUP * t, DA_TILE_GROUP, False)
        return carry

    n_full = qi // DA_TILE_GROUP
    lax.fori_loop(0, n_full, body, 0)
    for rem in range(1, DA_TILE_GROUP + 1):
        @pl.when(qi - DA_TILE_GROUP * n_full == rem - 1)
        def _(rem=rem):
            tile_group(qi - (rem - 1), rem, True)

    lam_e = jnp.exp(jnp.sum(lq_ref[...] * lk_ref[...], axis=-1, keepdims=True))
    lam = lam_e[0:1, :] - lam_e[1:2, :] + lam_init
    l0 = jnp.sum(l0_scr[...], axis=-1, keepdims=True)
    l1 = jnp.sum(l1_scr[...], axis=-1, keepdims=True)
    o = acc0_scr[...] / l0 - lam * (acc1_scr[...] / l1)
    ms = jnp.mean(o * o, axis=-1, keepdims=True)
    o = o * lax.rsqrt(ms + EPS) * sg_ref[...] * (1.0 - lam_init)
    o_ref[...] = o.astype(o_ref.dtype)


def _diff_attention(proj, lam_q, lam_k, sub_gain, B, S, lam_init, tq=512, rc=32):
    T = proj.shape[0]
    nq = S // tq
    lb = LANES
    q_blk0, k_blk0, v_blk0 = (CT_DA_Q * COL_TILE) // lb, (CT_DA_K * COL_TILE) // lb, (CT_DA_V * COL_TILE) // lb
    qfeat, kfeat = _alibi_features(tq)
    const = lambda shape: pl.BlockSpec(shape, lambda b, h, i: (0,) * len(shape))
    return pl.pallas_call(
        functools.partial(_da_kernel, tq=tq, rc=rc, lam_init=lam_init),
        grid=(B, DA_HEADS, nq),
        in_specs=[
            pl.BlockSpec((tq, lb), lambda b, h, i: (b * nq + i, q_blk0 + h)),
            pl.BlockSpec((S, lb), lambda b, h, i: (b, k_blk0 + h)),
            pl.BlockSpec((S, lb), lambda b, h, i: (b, v_blk0 + h)),
            const((2, LANES)), const((2, tq, LANES)),
            const((2, DA_QK_DIM)), const((2, DA_QK_DIM)), const((1, DA_V_DIM)),
        ],
        out_specs=pl.BlockSpec((tq, lb), lambda b, h, i: (b * nq + i, h)),
        out_shape=jax.ShapeDtypeStruct((T, DA_HEADS * DA_V_DIM), BF16),
        scratch_shapes=[pltpu.VMEM((tq, tq), F32)] * 4 + [pltpu.VMEM((tq, tq), BF16)] * 6
        + [pltpu.VMEM((tq, LANES), F32)] * 6 + [pltpu.VMEM((tq, DA_V_DIM), F32)] * 2,
        compiler_params=_params(("parallel", "parallel", "arbitrary")),
    )(proj, proj, proj, qfeat, kfeat, lam_q, lam_k, sub_gain)


def _dl_kernel(q_ref, kc_ref, kp_ref, vc_ref, vp_ref, o_ref, lse_ref, *, slopes2, d, tq, ru):
    n = pl.program_id(1)
    sp = DL_SPAN
    row = lax.broadcasted_iota(jnp.int32, (sp, sp), 0)
    col = lax.broadcasted_iota(jnp.int32, (sp, sp), 1)
    dcur = row - col
    cur_ok = dcur >= 0
    prev_ok = dcur <= 0
    dcur_f = dcur.astype(F32)

    def scores(r, hh, j):
        hs = slice(hh * LANES, (hh + 1) * LANES)
        rs = slice(j * sp, (j + 1) * sp)
        q = q_ref[r, rs, hs]
        if j == 0:
            kp, vp, p_ok = kp_ref[r, :, hs], vp_ref[r, :, hs], prev_ok & (n > 0)
        else:
            ps = slice((j - 1) * sp, j * sp)
            kp, vp, p_ok = kc_ref[r, ps, hs], vc_ref[r, ps, hs], prev_ok
        s_c = jnp.where(cur_ok, _dot_nt(q, kc_ref[r, rs, hs]) - slopes2[hh] * dcur_f, NEG_BIG)
        s_p = jnp.where(p_ok, _dot_nt(q, kp) - slopes2[hh] * (dcur_f + float(sp)), NEG_BIG)
        return s_c, s_p, vc_ref[r, rs, hs], vp

    def finish(r, hh, j, s_c, s_p, vc, vp):
        m = jnp.max(jnp.maximum(s_c, s_p), axis=-1, keepdims=True)
        p_c = jnp.exp2(s_c - m)
        p_p = jnp.exp2(s_p - m)
        den = jnp.sum(p_c + p_p, axis=-1, keepdims=True)
        acc = _dot(p_c.astype(BF16), vc) + _dot(p_p.astype(BF16), vp)
        out_rows = pl.ds(j * sp, sp) if d == 1 else pl.ds(r + j * sp * d, sp, stride=d)
        o_ref[hh, out_rows, :] = acc / den
        lse_ref[hh, out_rows, :] = jnp.broadcast_to(m + jnp.log2(den), (sp, LANES))

    def residues(t, carry):
        units = [(t * ru + rr, hh, j) for rr in range(ru) for hh in range(DL_HEADS_PER_GROUP)
                 for j in range(tq // sp)]
        pending = []
        for u in units:
            pending.append((u, scores(*u)))
            if len(pending) > 2:
                u0, vals = pending.pop(0)
                finish(*u0, *vals)
        for u0, vals in pending:
            finish(*u0, *vals)
        return carry

    lax.fori_loop(0, d // ru, residues, 0)


def _dilated_group(src, col0, g, B, S, tok_per_step=2048):
    window, d = DL_GROUPS[g]
    assert window // d == DL_SPAN
    L = S // d
    tq = min(tok_per_step, S) // d
    assert tq % DL_SPAN == 0 and L % tq == 0
    nh = DL_HEADS_PER_GROUP * len(DL_GROUPS)
    slopes2 = tuple(2.0 ** (-8.0 * (g * DL_HEADS_PER_GROUP + hh + 1) / nh) * d * LOG2E
                    for hh in range(DL_HEADS_PER_GROUP))
    spb = tq // DL_SPAN
    nsteps = L // tq
    cur = lambda c: pl.BlockSpec((d, tq, COL_TILE), lambda b, n: (b, n, c))
    prev = lambda c: pl.BlockSpec((d, DL_SPAN, COL_TILE), lambda b, n: (b, jnp.maximum(n * spb - 1, 0), c))
    out_spec = pl.BlockSpec((DL_HEADS_PER_GROUP, d * tq, LANES), lambda b, n: (0, b * nsteps + n, 0))
    return pl.pallas_call(
        functools.partial(_dl_kernel, slopes2=slopes2, d=d, tq=tq, ru=min(d, 4)),
        grid=(B, nsteps),
        in_specs=[cur(col0), cur(col0 + 1), prev(col0 + 1), cur(col0 + 2), prev(col0 + 2)],
        out_specs=[out_spec, out_spec],
        out_shape=[jax.ShapeDtypeStruct((DL_HEADS_PER_GROUP, B * S, LANES), F32)] * 2,
        compiler_params=_params(("parallel", "arbitrary")),
    )(src, src, src, src, src)


def _route(logits):
    lane = lax.broadcasted_iota(jnp.int32, logits.shape, 1)
    big = jnp.int32(1 << 20)
    is_g = (lane >= MOE_N_EXPERTS) & (lane < MOE_N_EXPERTS + MOE_GROUPS)
    lg = jnp.where(is_g, logits, -jnp.inf)
    gmax = jnp.max(lg, axis=-1, keepdims=True)
    gsum = jnp.sum(jnp.exp(lg - gmax), axis=-1, keepdims=True)
    g_w = 1.0 / gsum
    g_idx = jnp.min(jnp.where(lg == gmax, lane - MOE_N_EXPERTS, big), axis=-1, keepdims=True)
    in_grp = (lane < MOE_N_EXPERTS) & ((lane // MOE_EXPERTS_PER_GROUP) == g_idx)
    le = jnp.where(in_grp, logits, -jnp.inf)
    t1 = jnp.max(le, axis=-1, keepdims=True)
    e1 = jnp.min(jnp.where(le == t1, lane, big), axis=-1, keepdims=True)
    le2 = jnp.where(lane == e1, -jnp.inf, le)
    t2 = jnp.max(le2, axis=-1, keepdims=True)
    e2 = jnp.min(jnp.where(le2 == t2, lane, big), axis=-1, keepdims=True)
    r = jnp.exp(t2 - t1)
    w1 = g_w / (1.0 + r)
    w2 = w1 * r
    out = jnp.where(lane == 0, e1.astype(F32),
                    jnp.where(lane == 1, e2.astype(F32),
                              jnp.where(lane == 2, w1, jnp.where(lane == 3, w2, 0.0))))
    return out


def _outproj_kernel(x_ref, oa_ref, ga_ref, gb_ref, o0_ref, o1_ref, o2_ref, l0_ref, l1_ref, l2_ref,
                    wa_ref, wb_ref, wo_ref, gf_ref, rh_ref, rc_ref,
                    x1_ref, hn_ref, rt_ref):
    obs = []
    for hh in range(DL_HEADS_PER_GROUP):
        l0, l1, l2 = l0_ref[hh], l1_ref[hh], l2_ref[hh]
        lm = jnp.maximum(jnp.maximum(l0, l1), l2)
        e0, e1, e2 = jnp.exp2(l0 - lm), jnp.exp2(l1 - lm), jnp.exp2(l2 - lm)
        obs.append((e0 * o0_ref[hh] + e1 * o1_ref[hh] + e2 * o2_ref[hh]) / (e0 + e1 + e2))
    ob = jnp.concatenate(obs, axis=1)
    a = _dot(oa_ref[...], wa_ref[...])
    b = _dot(ob.astype(BF16), wb_ref[...])
    mixed = jax.nn.sigmoid(ga_ref[...].astype(F32)) * a + jax.nn.sigmoid(gb_ref[...].astype(F32)) * b
    x1 = x_ref[...] + _dot(mixed.astype(BF16), wo_ref[...])
    x1_ref[...] = x1
    ms = jnp.mean(x1 * x1, axis=-1, keepdims=True)
    hn = x1 * lax.rsqrt(ms + EPS) * gf_ref[...]
    hn_ref[...] = hn
    hn_hi = hn.astype(BF16)
    hn_lo = (hn - hn_hi.astype(F32)).astype(BF16)
    t = _dot(hn_hi, rc_ref[...])
    logits = t[:, 0:LANES] + (_dot(hn_lo, rh_ref[...]) + t[:, LANES:2 * LANES])
    rt_ref[...] = _route(logits)


def _outproj(x2, o_a, proj, dl_o, dl_lse, wa, wb, wo, gain_ffn, r_hi, r_cat, tm=256):
    T, D = x2.shape
    row = lambda w: pl.BlockSpec((tm, w), lambda i: (i, 0))
    full = lambda s: pl.BlockSpec(s, lambda i: (0, 0), pipeline_mode=pl.Buffered(1))
    hrow = pl.BlockSpec((DL_HEADS_PER_GROUP, tm, LANES), lambda i: (0, i, 0))
    return pl.pallas_call(
        _outproj_kernel,
        grid=(T // tm,),
        in_specs=[
            row(D), row(o_a.shape[1]),
            pl.BlockSpec((tm, D), lambda i: (i, (CT_GATE_A * COL_TILE) // D)),
            pl.BlockSpec((tm, D), lambda i: (i, (CT_GATE_B * COL_TILE) // D)),
            hrow, hrow, hrow, hrow, hrow, hrow,
            full(wa.shape), full(wb.shape), full(wo.shape), full((1, D)), full(r_hi.shape), full(r_cat.shape),
        ],
        out_specs=[row(D), row(D), row(LANES)],
        out_shape=[jax.ShapeDtypeStruct((T, D), F32), jax.ShapeDtypeStruct((T, D), F32),
                   jax.ShapeDtypeStruct((T, LANES), F32)],
        compiler_params=_params(("parallel",)),
    )(x2, o_a, proj, proj, dl_o[0], dl_o[1], dl_o[2], dl_lse[0], dl_lse[1], dl_lse[2],
      wa, wb, wo, gain_ffn, r_hi, r_cat)


def _cast_rows(src_ref, dst_ref, chunk=256):
    def body(c, carry):
        r0 = pl.multiple_of(c * chunk, chunk)
        dst_ref[pl.ds(r0, chunk), :] = src_ref[pl.ds(r0, chunk), :].astype(dst_ref.dtype)
        return carry
    lax.fori_loop(0, src_ref.shape[0] // chunk, body, 0)


def _expert_changed(be_ref, i):
    return (i == 0) | (be_ref[i] != be_ref[jnp.maximum(i - 1, 0)])


def _stream_expert_weights(i, live, be_ref, ne_ref, w_hbm, stage, wbf, wsem):
    changed = live & _expert_changed(be_ref, i)

    def copy(e):
        return pltpu.make_async_copy(w_hbm.at[e], stage, wsem)

    @pl.when(changed & (i == 0))
    def _():
        copy(be_ref[0]).start(priority=WEIGHT_DMA_PRIORITY)

    @pl.when(changed)
    def _():
        copy(be_ref[i]).wait()
        _cast_rows(stage, wbf)

    @pl.when(changed & (ne_ref[i] >= 0))
    def _():
        copy(ne_ref[i]).start(priority=WEIGHT_DMA_PRIORITY)


def _moe_up_kernel(be_ref, ne_ref, nu_ref, tc_ref, tn_ref, hn_ref, wgu_ref, act_ref,
                   xa, xb, wstage, wbf, gsem, wsem, *, d_ff):
    i = pl.program_id(0)
    nu = nu_ref[0]
    rows = xa.shape[0]
    even = i % 2 == 0

    def gather(tok_ref, r, buf, s):
        return pltpu.make_async_copy(hn_ref.at[tok_ref[0, r]], buf.at[r], gsem.at[s])

    def wait_gather(buf, s):
        pltpu.make_async_copy(hn_ref.at[pl.ds(0, rows)], buf, gsem.at[s]).wait()

    @pl.when((i == 0) & (nu > 0))
    def _():
        def body(r, c):
            gather(tc_ref, r, xa, 0).start()
            return c
        lax.fori_loop(0, rows, body, 0)

    _stream_expert_weights(i, i < nu, be_ref, ne_ref, wgu_ref, wstage, wbf, wsem)

    def live_step(cur, nxt, s):
        wait_gather(cur, s)
        for r in range(rows):
            gather(tn_ref, r, nxt, 1 - s).start(priority=r % 2)
        h = _dot(cur[...].astype(BF16), wbf[...])
        gate = h[:, :d_ff]
        up = h[:, d_ff:]
        act_ref[...] = (gate * jax.nn.sigmoid(gate) * up).astype(act_ref.dtype)

    @pl.when((i < nu) & even)
    def _():
        live_step(xa, xb, 0)

    @pl.when((i < nu) & jnp.logical_not(even))
    def _():
        live_step(xb, xa, 1)

    @pl.when((i == nu) & (nu > 0) & even)
    def _():
        wait_gather(xa, 0)

    @pl.when((i == nu) & (nu > 0) & jnp.logical_not(even))
    def _():
        wait_gather(xb, 1)

    @pl.when(i >= nu)
    def _():
        act_ref[...] = jnp.zeros(act_ref.shape, act_ref.dtype)


def _moe_down_kernel(be_ref, ne_ref, nu_ref, dp_ref, act_ref, wd_ref, y_ref, ya, yb, wstage, wbf, ssem, wsem):
    i = pl.program_id(0)
    nu = nu_ref[0]
    rows = ya.shape[0]
    even = i % 2 == 0

    def scatter(r, buf, s):
        return pltpu.make_async_copy(buf.at[r], y_ref.at[dp_ref[0, r]], ssem.at[s])

    def wait_scatter(buf, s):
        pltpu.make_async_copy(buf, y_ref.at[pl.ds(0, rows)], ssem.at[s]).wait()

    @pl.when(i == 0)
    def _():
        n_res = y_ref.shape[0] - 2 * rows
        for s, buf in enumerate((ya, yb)):
            buf[...] = jnp.zeros(buf.shape, buf.dtype)
            pltpu.make_async_copy(buf, y_ref.at[pl.ds(n_res + s * rows, rows)], ssem.at[s]).start()
        for s, buf in enumerate((ya, yb)):
            pltpu.make_async_copy(buf, y_ref.at[pl.ds(n_res + s * rows, rows)], ssem.at[s]).wait()

    @pl.when((i >= 2) & (i < nu + 2) & even)
    def _():
        wait_scatter(ya, 0)

    @pl.when((i >= 2) & (i < nu + 2) & jnp.logical_not(even))
    def _():
        wait_scatter(yb, 1)

    _stream_expert_weights(i, i < nu, be_ref, ne_ref, wd_ref, wstage, wbf, wsem)

    def step(cur, prv, s, do_scatter, do_compute):
        if do_scatter:
            for r in range(rows):
                scatter(r, prv, 1 - s).start(priority=r % 2)
        if do_compute:
            cur[...] = _dot(act_ref[...], wbf[...])

    for s, (cur, prv) in enumerate(((ya, yb), (yb, ya))):
        par = even if s == 0 else jnp.logical_not(even)

        @pl.when((i >= 1) & (i < nu) & par)
        def _(cur=cur, prv=prv, s=s):
            step(cur, prv, s, True, True)

        @pl.when((i == 0) & (nu > 0) & par)
        def _(cur=cur, prv=prv, s=s):
            step(cur, prv, s, False, True)

        @pl.when((i == nu) & (nu > 0) & par)
        def _(cur=cur, prv=prv, s=s):
            step(cur, prv, s, True, False)


def _experts(hn, src_tok, dst_row, blk_expert, nxt_expert, n_used, wgu, wd, n_out_rows):
    T, D = hn.shape
    nblk = src_tok.shape[0] // MOE_BLOCK
    d_ff = wd.shape[1]
    tok = src_tok.reshape(nblk, 1, MOE_BLOCK)
    dst = dst_row.reshape(nblk, 1, MOE_BLOCK)

    def live(i, nu):
        return jnp.maximum(jnp.minimum(i, nu[0] - 1), 0)

    smem = lambda f: pl.BlockSpec((None, 1, MOE_BLOCK), f, memory_space=pltpu.SMEM)
    act = pl.pallas_call(
        functools.partial(_moe_up_kernel, d_ff=d_ff),
        grid_spec=pltpu.PrefetchScalarGridSpec(
            num_scalar_prefetch=3,
            grid=(nblk - 1,),
            in_specs=[
                smem(lambda i, be, ne, nu: (i, 0, 0)),
                smem(lambda i, be, ne, nu: (i + 1, 0, 0)),
                pl.BlockSpec(memory_space=pl.ANY),
                pl.BlockSpec(memory_space=pl.ANY),
            ],
            out_specs=pl.BlockSpec((MOE_BLOCK, d_ff), lambda i, be, ne, nu: (i, 0)),
            scratch_shapes=[pltpu.VMEM((MOE_BLOCK, D), F32), pltpu.VMEM((MOE_BLOCK, D), F32),
                            pltpu.VMEM((D, 2 * d_ff), F32), pltpu.VMEM((D, 2 * d_ff), BF16),
                            pltpu.SemaphoreType.DMA((2,)), pltpu.SemaphoreType.DMA(())],
        ),
        out_shape=jax.ShapeDtypeStruct(((nblk - 1) * MOE_BLOCK, d_ff), BF16),
        compiler_params=_params(("arbitrary",)),
    )(blk_expert, nxt_expert, n_used, tok, tok, hn, wgu)
    return pl.pallas_call(
        _moe_down_kernel,
        grid_spec=pltpu.PrefetchScalarGridSpec(
            num_scalar_prefetch=3,
            grid=(nblk,),
            in_specs=[
                smem(lambda i, be, ne, nu: (jnp.maximum(i - 1, 0), 0, 0)),
                pl.BlockSpec((MOE_BLOCK, d_ff), lambda i, be, ne, nu: (live(i, nu), 0)),
                pl.BlockSpec(memory_space=pl.ANY),
            ],
            out_specs=pl.BlockSpec(memory_space=pl.ANY),
            scratch_shapes=[pltpu.VMEM((MOE_BLOCK, D), F32), pltpu.VMEM((MOE_BLOCK, D), F32),
                            pltpu.VMEM((d_ff, D), F32), pltpu.VMEM((d_ff, D), BF16),
                            pltpu.SemaphoreType.DMA((2,)), pltpu.SemaphoreType.DMA(())],
        ),
        out_shape=jax.ShapeDtypeStruct((n_out_rows, D), F32),
        compiler_params=_params(("arbitrary",)),
    )(blk_expert, nxt_expert, n_used, dst, act, wd)


def _combine_kernel(x1_ref, rt_ref, y1_ref, y2_ref, o_ref):
    rt = rt_ref[...]
    o_ref[...] = x1_ref[...] + (rt[:, 2:3] * y1_ref[...] + rt[:, 3:4] * y2_ref[...])


def _combine(x1, route, y, tm=512):
    T, D = x1.shape
    nt = T // tm
    return pl.pallas_call(
        _combine_kernel,
        grid=(nt,),
        in_specs=[
            pl.BlockSpec((tm, D), lambda i: (i, 0)),
            pl.BlockSpec((tm, LANES), lambda i: (i, 0)),
            pl.BlockSpec((tm, D), lambda i: (i, 0)),
            pl.BlockSpec((tm, D), lambda i: (nt + i, 0)),
        ],
        out_specs=pl.BlockSpec((tm, D), lambda i: (i, 0)),
        out_shape=jax.ShapeDtypeStruct((T, D), F32),
        compiler_params=_params(("parallel",)),
    )(x1, route, y, y)


def _dispatch_tables(route, T):
    TK = T * MOE_TOP_K
    flat_e = route[:, :MOE_TOP_K].astype(jnp.int32).reshape(-1)
    order = jnp.argsort(flat_e).astype(jnp.int32)
    sizes = jnp.sum(flat_e[:, None] == jnp.arange(MOE_N_EXPERTS, dtype=jnp.int32)[None, :], axis=0,
                    dtype=jnp.int32)
    start = jnp.cumsum(sizes) - sizes
    padded = ((sizes + MOE_BLOCK - 1) // MOE_BLOCK) * MOE_BLOCK
    pad_end = jnp.cumsum(padded)
    pad_start = pad_end - padded
    n_blocks = TK // MOE_BLOCK + MOE_N_EXPERTS
    blk = jnp.arange(n_blocks + 2, dtype=jnp.int32)
    blk_expert = jnp.minimum(jnp.sum(pad_end[None, :] <= (blk * MOE_BLOCK)[:, None], axis=1, dtype=jnp.int32),
                             MOE_N_EXPERTS - 1)
    r = jnp.arange(MOE_BLOCK, dtype=jnp.int32)[None, :]
    pos = blk[:, None] * MOE_BLOCK + r
    off = pos - pad_start[blk_expert][:, None]
    live = (off < sizes[blk_expert][:, None]) & (pos < pad_end[-1])
    sorted_idx = jnp.where(live, off + start[blk_expert][:, None], 0)
    slot = order[sorted_idx]
    tok = slot // MOE_TOP_K
    src_tok = jnp.where(live, tok, 0).reshape(-1)
    trash = TK + (blk[:, None] % 2) * MOE_BLOCK + r
    dst_row = jnp.where(live, (slot % MOE_TOP_K) * T + tok, trash).reshape(-1)
    n_used = (pad_end[-1:] // MOE_BLOCK).astype(jnp.int32)
    eid = jnp.arange(MOE_N_EXPERTS, dtype=jnp.int32)
    later = (eid[None, :] > eid[:, None]) & (sizes[None, :] > 0)
    nxt = jnp.min(jnp.where(later, eid[None, :], MOE_N_EXPERTS), axis=1)
    nxt_expert = jnp.where(nxt < MOE_N_EXPERTS, nxt, -1).astype(jnp.int32)[blk_expert]
    return src_tok, dst_row, blk_expert, nxt_expert, n_used, TK + 2 * MOE_BLOCK


def _permute_w_in(w):
    da = DA_HEADS * 2 * DA_QK_DIM
    dav = DA_HEADS * DA_V_DIM
    dl = len(DL_GROUPS) * DL_HEADS_PER_GROUP * DL_HEAD_DIM
    o = [int(v) for v in np.cumsum([0, da, da, dav, dl, dl, dl])]
    parts = [w[:, o[6]:], w[:, :o[3]]]
    for g in range(len(DL_GROUPS)):
        for p in range(3):
            parts.append(w[:, o[3 + p] + g * COL_TILE: o[3 + p] + (g + 1) * COL_TILE])
    return jnp.concatenate(parts, axis=1)


def _gain_table(da_q_norm, da_k_norm, dl_q_norm, dl_k_norm):
    ones = jnp.ones((COL_TILE,), F32)
    daq = jnp.tile(da_q_norm, COL_TILE // DA_QK_DIM) * (DA_QK_DIM ** -0.5 * LOG2E)
    dak = jnp.tile(da_k_norm, COL_TILE // DA_QK_DIM)
    dlq = jnp.tile(dl_q_norm, COL_TILE // DL_HEAD_DIM) * (DL_HEAD_DIM ** -0.5 * LOG2E)
    dlk = jnp.tile(dl_k_norm, COL_TILE // DL_HEAD_DIM)
    rows = []
    for j in range(CT_END):
        if CT_DA_Q <= j < CT_DA_K:
            rows.append(daq)
        elif CT_DA_K <= j < CT_DA_V:
            rows.append(dak)
        elif j >= CT_DL and (j - CT_DL) % 3 == 0:
            rows.append(dlq)
        elif j >= CT_DL and (j - CT_DL) % 3 == 1:
            rows.append(dlk)
        else:
            rows.append(ones)
    return jnp.stack(rows, axis=0).reshape(CT_END, 1, COL_TILE)


def kernel(x, norm_mix, w_in, da_q_norm, da_k_norm, da_lambda_q, da_lambda_k, da_sub_norm,
           dl_q_norm, dl_k_norm, w_branch_a, w_branch_b, w_out, norm_ffn,
           w_group_router, w_expert_router, w_gate_up, w_down):
    B, S, D = x.shape
    T = B * S
    depth = w_in.shape[0]
    x2 = x.reshape(T, D)
    for l in range(depth):
        lam_init = 0.8 - 0.6 * math.exp(-0.3 * l)
        w_in_bf = _permute_w_in(w_in[l]).astype(BF16)
        gain_tab = _gain_table(da_q_norm[l], da_k_norm[l], dl_q_norm[l], dl_k_norm[l])
        proj, dl1, dl2 = _inproj(x2, norm_mix[l].reshape(1, D), w_in_bf, gain_tab, B, S)

        o_a = _diff_attention(proj, da_lambda_q[l], da_lambda_k[l], da_sub_norm[l].reshape(1, DA_V_DIM),
                              B, S, lam_init)
        dl = [_dilated_group(proj.reshape(B, S, proj.shape[1]), CT_DL, 0, B, S),
              _dilated_group(dl1, 0, 1, B, S), _dilated_group(dl2, 0, 2, B, S)]

        w_r = jnp.concatenate([w_expert_router[l], w_group_router[l]], axis=1)
        w_r = jnp.pad(w_r, ((0, 0), (0, LANES - w_r.shape[1])))
        r_hi = w_r.astype(BF16)
        r_lo = (w_r - r_hi.astype(F32)).astype(BF16)
        x1, hn, route = _outproj(
            x2, o_a, proj, [t[0] for t in dl], [t[1] for t in dl],
            w_branch_a[l].astype(BF16), w_branch_b[l].astype(BF16), w_out[l].astype(BF16),
            norm_ffn[l].reshape(1, D), r_hi, jnp.concatenate([r_hi, r_lo], axis=1))

        src_tok, dst_row, blk_expert, nxt_expert, n_used, n_rows = _dispatch_tables(route, T)
        y = _experts(hn, src_tok, dst_row, blk_expert, nxt_expert, n_used, w_gate_up[l], w_down[l], n_rows)
        x2 = _combine(x1, route, y)
    return x2.reshape(B, S, D)
```

```python
import functools
import math

import jax
import jax.numpy as jnp
import numpy as np
from jax import lax
from jax.experimental import pallas as pl
from jax.experimental.pallas import tpu as pltpu

F32 = jnp.float32
BF16 = jnp.bfloat16

EPS = 1e-6
LOG2E = 1.4426950408889634
NEG_BIG = -1e30

DA_HEADS = 8
DA_QK_DIM = 64
DA_V_DIM = 128
DA_TILE_GROUP = 4
DL_GROUPS = ((128, 1), (512, 4), (2048, 16))
DL_HEADS_PER_GROUP = 4
DL_HEAD_DIM = 128
DL_SPAN = 128
MOE_GROUPS = 4
MOE_EXPERTS_PER_GROUP = 8
MOE_N_EXPERTS = 32
MOE_TOP_K = 2
MOE_BLOCK = 256
WEIGHT_DMA_PRIORITY = 1

LANES = 128
COL_TILE = 512
VMEM_LIMIT = 56 * 1024 * 1024

CT_GATE_A, CT_GATE_B, CT_DA_Q, CT_DA_K, CT_DA_V, CT_DL, CT_MAIN_END, CT_END = 0, 4, 8, 10, 12, 14, 17, 23


def _params(sem, vmem=VMEM_LIMIT):
    return pltpu.CompilerParams(dimension_semantics=sem, vmem_limit_bytes=vmem)


def _dot(a, b):
    return jnp.dot(a, b, preferred_element_type=F32)


def _dot_nt(a, b):
    return lax.dot_general(a, b, (((1,), (1,)), ((), ())), preferred_element_type=F32)


def _inproj_kernel(perm_ref, x_ref, g_ref, w_ref, gain_ref, o_ref, d1_ref, d2_ref, h_scr, y_scr):
    j = pl.program_id(1)

    @pl.when(j == 0)
    def _():
        x = x_ref[...]
        ms = jnp.mean(x * x, axis=-1, keepdims=True)
        h_scr[...] = (x * lax.rsqrt(ms + EPS) * g_ref[...]).astype(BF16)

    gain = gain_ref[...]
    half = COL_TILE // 2

    def head_slices():
        ys = [_dot(h_scr[...], w_ref[:, hf * half:(hf + 1) * half]) for hf in range(2)]
        for hf in range(2):
            for hh in range(half // LANES):
                yield hf * (half // LANES) + hh, ys[hf][:, hh * LANES:(hh + 1) * LANES]

    is64 = (j >= CT_DA_Q) & (j < CT_DA_V)
    is128 = (j >= CT_DL) & (lax.rem(j - CT_DL, 3) < 2)
    main = j < CT_MAIN_END
    plain = jnp.logical_not(is64 | is128)

    def norm64(h, yh):
        sq = yh * yh
        lo = lax.broadcasted_iota(jnp.int32, yh.shape, 1) < DA_QK_DIM
        s_lo = jnp.sum(jnp.where(lo, sq, 0.0), axis=-1, keepdims=True)
        s_hi = jnp.sum(jnp.where(lo, 0.0, sq), axis=-1, keepdims=True)
        r = jnp.where(lo, lax.rsqrt(s_lo * (1.0 / DA_QK_DIM) + EPS), lax.rsqrt(s_hi * (1.0 / DA_QK_DIM) + EPS))
        return yh * r * gain[:, h * LANES:(h + 1) * LANES]

    def norm128(h, yh):
        ss = jnp.sum(yh * yh, axis=-1, keepdims=True)
        return yh * lax.rsqrt(ss * (1.0 / DL_HEAD_DIM) + EPS) * gain[:, h * LANES:(h + 1) * LANES]

    def emit(cond, fn, to_main):
        @pl.when(cond)
        def _():
            for h, yh in head_slices():
                if to_main:
                    o_ref[:, h * LANES:(h + 1) * LANES] = fn(h, yh).astype(o_ref.dtype)
                else:
                    y_scr[h] = fn(h, yh)

    emit(is64, norm64, True)
    emit(is128 & main, norm128, True)
    emit(is128 & jnp.logical_not(main), norm128, False)
    emit(plain & main, lambda h, yh: yh, True)
    emit(plain & jnp.logical_not(main), lambda h, yh: yh, False)

    def deinterleave(dst_ref):
        d, rows = dst_ref.shape[0], dst_ref.shape[1]
        for r in range(d):
            for h in range(COL_TILE // LANES):
                dst_ref[r, :, h * LANES:(h + 1) * LANES] = (
                    y_scr[h, pl.ds(r, rows, stride=d), :].astype(dst_ref.dtype))

    @pl.when((j >= CT_MAIN_END) & (j < CT_MAIN_END + 3))
    def _():
        deinterleave(d1_ref)

    @pl.when(j >= CT_MAIN_END + 3)
    def _():
        deinterleave(d2_ref)


def _inproj(x2, gain_mix, w_bf, tile_perm, gain_tab, B, S, tm=1024):
    T, D = x2.shape
    tiles_per_batch = S // tm
    d1, d2 = DL_GROUPS[1][1], DL_GROUPS[2][1]
    part1 = lambda j: jnp.clip(j - CT_MAIN_END, 0, 2)
    part2 = lambda j: jnp.clip(j - CT_MAIN_END - 3, 0, 2)
    grid_spec = pltpu.PrefetchScalarGridSpec(
        num_scalar_prefetch=1,
        grid=(T // tm, CT_END),
        in_specs=[
            pl.BlockSpec((tm, D), lambda i, j, perm: (i, 0)),
            pl.BlockSpec((1, D), lambda i, j, perm: (0, 0)),
            pl.BlockSpec((D, COL_TILE), lambda i, j, perm: (0, perm[j])),
            pl.BlockSpec((None, 1, COL_TILE), lambda i, j, perm: (j, 0, 0)),
        ],
        out_specs=[
            pl.BlockSpec((tm, COL_TILE), lambda i, j, perm: (i, jnp.minimum(j, CT_MAIN_END - 1))),
            pl.BlockSpec((d1, tm // d1, COL_TILE),
                         lambda i, j, perm: (i // tiles_per_batch, i % tiles_per_batch, part1(j))),
            pl.BlockSpec((d2, tm // d2, COL_TILE),
                         lambda i, j, perm: (i // tiles_per_batch, i % tiles_per_batch, part2(j))),
        ],
        scratch_shapes=[pltpu.VMEM((tm, D), BF16), pltpu.VMEM((COL_TILE // LANES, tm, LANES), F32)],
    )
    return pl.pallas_call(
        _inproj_kernel,
        grid_spec=grid_spec,
        out_shape=[
            jax.ShapeDtypeStruct((T, CT_MAIN_END * COL_TILE), BF16),
            jax.ShapeDtypeStruct((B * d1, S // d1, 3 * COL_TILE), BF16),
            jax.ShapeDtypeStruct((B * d2, S // d2, 3 * COL_TILE), BF16),
        ],
        compiler_params=_params(("parallel", "arbitrary")),
    )(tile_perm, x2, gain_mix, w_bf, gain_tab)


def _bf16_pieces(x, n=3):
    out = []
    r = np.float64(x)
    for _ in range(n):
        p = np.asarray(np.float32(r)).astype(jnp.bfloat16).astype(np.float64)
        out.append(float(p))
        r = r - p
    return out


def _alibi_features(tk):
    pieces = _bf16_pieces(LOG2E)
    qf = np.zeros((2, LANES), np.float32)
    kf = np.zeros((2, tk, LANES), np.float32)
    j = np.arange(tk)
    hi, lo = (j // 16) * 16, j % 16
    for m in range(2):
        f0 = DA_QK_DIM if m == 0 else 0
        for n, p in enumerate(pieces):
            qf[m, f0 + 2 * n] = p
            qf[m, f0 + 2 * n + 1] = p
            kf[m, :, f0 + 2 * n] = hi
            kf[m, :, f0 + 2 * n + 1] = lo
    return jnp.asarray(qf), jnp.asarray(kf, dtype=BF16)


def _da_kernel(q_ref, k_ref, v_ref, qf_ref, kf_ref, lq_ref, lk_ref, sg_ref, o_ref,
               s00, s01, s10, s11, p00, p01, p10, p11, pd0, pd1,
               m0_scr, m1_scr, l0_scr, l1_scr, a0_scr, a1_scr, acc0_scr, acc1_scr, *, tq, rc, lam_init):
    h = pl.program_id(1)
    qi = pl.program_id(2)
    nlb = tq // LANES
    pow2 = jnp.exp2(-(h + 1).astype(F32))
    slope2 = pow2 * LOG2E

    q = q_ref[...]
    lane = lax.broadcasted_iota(jnp.int32, (tq, LANES), 1)
    own = (lane < DA_QK_DIM, lane >= DA_QK_DIM)
    qfs = [jnp.where(own[mi], q, jnp.broadcast_to((qf_ref[mi:mi + 1, :] * pow2).astype(BF16), q.shape))
           for mi in range(2)]

    m_scrs, l_scrs, a_scrs, acc_scrs = (m0_scr, m1_scr), (l0_scr, l1_scr), (a0_scr, a1_scr), (acc0_scr, acc1_scr)
    for mi in range(2):
        m_scrs[mi][...] = jnp.full(m_scrs[mi].shape, NEG_BIG, F32)
        l_scrs[mi][...] = jnp.zeros(l_scrs[mi].shape, F32)
        acc_scrs[mi][...] = jnp.zeros(acc_scrs[mi].shape, F32)

    def scores(ki, mi, s_ref):
        k = k_ref[pl.ds(pl.multiple_of(ki * tq, tq), tq), :]
        s_ref[...] = _dot_nt(qfs[mi], jnp.where(own[mi], k, kf_ref[mi]))

    def softmax(ki, mi, s_ref, p_ref, masked):
        m_scr, l_scr, a_scr = m_scrs[mi], l_scrs[mi], a_scrs[mi]
        c = slope2 * ((ki - qi) * tq).astype(F32)
        for r in range(tq // rc):
            rows = slice(r * rc, (r + 1) * rc)
            nb = min(nlb, ((r + 1) * rc - 1) // LANES + 1) if masked else nlb
            sb = []
            for j in range(nb):
                cs = slice(j * LANES, (j + 1) * LANES)
                s = s_ref[rows, cs]
                if masked and (j + 1) * LANES - 1 > r * rc:
                    rr = lax.broadcasted_iota(jnp.int32, (rc, LANES), 0) + r * rc
                    cc = lax.broadcasted_iota(jnp.int32, (rc, LANES), 1) + j * LANES
                    s = jnp.where(cc <= rr, s, NEG_BIG)
                sb.append(s)
            mx = sb[0]
            for s in sb[1:]:
                mx = jnp.maximum(mx, s)
            m_prev = m_scr[rows, :]
            m_new = jnp.maximum(m_prev, jnp.max(mx, axis=-1, keepdims=True) + c)
            alpha = jnp.exp2(m_prev - m_new)
            a_scr[rows, :] = alpha
            m_scr[rows, :] = m_new
            mc = m_new - c
            psum = alpha * l_scr[rows, :]
            for j in range(nlb):
                cs = slice(j * LANES, (j + 1) * LANES)
                if j < nb:
                    p = jnp.exp2(sb[j] - mc)
                    psum = psum + p
                    p_ref[rows, cs] = p.astype(BF16)
                else:
                    p_ref[rows, cs] = jnp.zeros((rc, LANES), BF16)
            l_scr[rows, :] = psum

    def values(ki, mi, p_ref):
        v = v_ref[pl.ds(pl.multiple_of(ki * tq, tq), tq), :]
        acc_scrs[mi][...] = a_scrs[mi][...] * acc_scrs[mi][...] + _dot(p_ref[...], v)

    s_bufs, p_bufs = ((s00, s01), (s10, s11)), ((p00, p01), (p10, p11))

    def tile_group(k0, n, last_masked):
        for mi in range(2):
            scores(k0, mi, s_bufs[0][mi])
        for i in range(n):
            masked = last_masked and i == n - 1
            for mi in range(2):
                p_ref = (pd0, pd1)[mi] if masked else p_bufs[i % 2][mi]
                softmax(k0 + i, mi, s_bufs[i % 2][mi], p_ref, masked)
                values(k0 + i, mi, p_ref)
                if i + 1 < n:
                    scores(k0 + i + 1, mi, s_bufs[(i + 1) % 2][mi])

    def body(t, carry):
        tile_group(DA_TILE_GROUP * t, DA_TILE_GROUP, False)
        return carry

    n_full = qi // DA_TILE_GROUP
    lax.fori_loop(0, n_full, body, 0)
    for rem in range(1, DA_TILE_GROUP + 1):
        @pl.when(qi - DA_TILE_GROUP * n_full == rem - 1)
        def _(rem=rem):
            tile_group(qi - (rem - 1), rem, True)

    lam_e = jnp.exp(jnp.sum(lq_ref[...] * lk_ref[...], axis=-1, keepdims=True))
    lam = lam_e[0:1, :] - lam_e[1:2, :] + lam_init
    l0 = jnp.sum(l0_scr[...], axis=-1, keepdims=True)
    l1 = jnp.sum(l1_scr[...], axis=-1, keepdims=True)
    o = acc0_scr[...] / l0 - lam * (acc1_scr[...] / l1)
    ms = jnp.mean(o * o, axis=-1, keepdims=True)
    o = o * lax.rsqrt(ms + EPS) * sg_ref[...] * (1.0 - lam_init)
    o_ref[...] = o.astype(o_ref.dtype)


def _diff_attention(proj, lam_q, lam_k, sub_gain, B, S, lam_init, tq=512, rc=32):
    T = proj.shape[0]
    nq = S // tq
    lb = LANES
    q_blk0, k_blk0, v_blk0 = (CT_DA_Q * COL_TILE) // lb, (CT_DA_K * COL_TILE) // lb, (CT_DA_V * COL_TILE) // lb
    qfeat, kfeat = _alibi_features(tq)
    const = lambda shape: pl.BlockSpec(shape, lambda b, h, i: (0,) * len(shape))
    return pl.pallas_call(
        functools.partial(_da_kernel, tq=tq, rc=rc, lam_init=lam_init),
        grid=(B, DA_HEADS, nq),
        in_specs=[
            pl.BlockSpec((tq, lb), lambda b, h, i: (b * nq + i, q_blk0 + h)),
            pl.BlockSpec((S, lb), lambda b, h, i: (b, k_blk0 + h)),
            pl.BlockSpec((S, lb), lambda b, h, i: (b, v_blk0 + h)),
            const((2, LANES)), const((2, tq, LANES)),
            const((2, DA_QK_DIM)), const((2, DA_QK_DIM)), const((1, DA_V_DIM)),
        ],
        out_specs=pl.BlockSpec((tq, lb), lambda b, h, i: (b * nq + i, h)),
        out_shape=jax.ShapeDtypeStruct((T, DA_HEADS * DA_V_DIM), BF16),
        scratch_shapes=[pltpu.VMEM((tq, tq), F32)] * 4 + [pltpu.VMEM((tq, tq), BF16)] * 6
        + [pltpu.VMEM((tq, LANES), F32)] * 6 + [pltpu.VMEM((tq, DA_V_DIM), F32)] * 2,
        compiler_params=_params(("parallel", "parallel", "arbitrary")),
    )(proj, proj, proj, qfeat, kfeat, lam_q, lam_k, sub_gain)


def _dl_kernel(q_ref, kc_ref, kp_ref, vc_ref, vp_ref, o_ref, lse_ref, *, slopes2, d, tq, ru):
    n = pl.program_id(1)
    sp = DL_SPAN
    row = lax.broadcasted_iota(jnp.int32, (sp, sp), 0)
    col = lax.broadcasted_iota(jnp.int32, (sp, sp), 1)
    dcur = row - col
    cur_ok = dcur >= 0
    prev_ok = dcur <= 0
    dcur_f = dcur.astype(F32)

    def scores(r, hh, j):
        hs = slice(hh * LANES, (hh + 1) * LANES)
        rs = slice(j * sp, (j + 1) * sp)
        q = q_ref[r, rs, hs]
        if j == 0:
            kp, vp, p_ok = kp_ref[r, :, hs], vp_ref[r, :, hs], prev_ok & (n > 0)
        else:
            ps = slice((j - 1) * sp, j * sp)
            kp, vp, p_ok = kc_ref[r, ps, hs], vc_ref[r, ps, hs], prev_ok
        s_c = jnp.where(cur_ok, _dot_nt(q, kc_ref[r, rs, hs]) - slopes2[hh] * dcur_f, NEG_BIG)
        s_p = jnp.where(p_ok, _dot_nt(q, kp) - slopes2[hh] * (dcur_f + float(sp)), NEG_BIG)
        return s_c, s_p, vc_ref[r, rs, hs], vp

    def finish(r, hh, j, s_c, s_p, vc, vp):
        m = jnp.max(jnp.maximum(s_c, s_p), axis=-1, keepdims=True)
        p_c = jnp.exp2(s_c - m)
        p_p = jnp.exp2(s_p - m)
        den = jnp.sum(p_c + p_p, axis=-1, keepdims=True)
        acc = _dot(p_c.astype(BF16), vc) + _dot(p_p.astype(BF16), vp)
        out_rows = pl.ds(j * sp, sp) if d == 1 else pl.ds(r + j * sp * d, sp, stride=d)
        o_ref[hh, out_rows, :] = acc / den
        lse_ref[hh, out_rows, :] = jnp.broadcast_to(m + jnp.log2(den), (sp, LANES))

    def residues(t, carry):
        units = [(t * ru + rr, hh, j) for rr in range(ru) for hh in range(DL_HEADS_PER_GROUP)
                 for j in range(tq // sp)]
        pending = []
        for u in units:
            pending.append((u, scores(*u)))
            if len(pending) > 2:
                u0, vals = pending.pop(0)
                finish(*u0, *vals)
        for u0, vals in pending:
            finish(*u0, *vals)
        return carry

    lax.fori_loop(0, d // ru, residues, 0)


def _dilated_group(src, col0, g, B, S, tok_per_step=2048):
    window, d = DL_GROUPS[g]
    assert window // d == DL_SPAN
    L = S // d
    tq = min(tok_per_step, S) // d
    assert tq % DL_SPAN == 0 and L % tq == 0
    nh = DL_HEADS_PER_GROUP * len(DL_GROUPS)
    slopes2 = tuple(2.0 ** (-8.0 * (g * DL_HEADS_PER_GROUP + hh + 1) / nh) * d * LOG2E
                    for hh in range(DL_HEADS_PER_GROUP))
    spb = tq // DL_SPAN
    nsteps = L // tq
    cur = lambda c: pl.BlockSpec((d, tq, COL_TILE), lambda b, n: (b, n, c))
    prev = lambda c: pl.BlockSpec((d, DL_SPAN, COL_TILE), lambda b, n: (b, jnp.maximum(n * spb - 1, 0), c))
    out_spec = pl.BlockSpec((DL_HEADS_PER_GROUP, d * tq, LANES), lambda b, n: (0, b * nsteps + n, 0))
    return pl.pallas_call(
        functools.partial(_dl_kernel, slopes2=slopes2, d=d, tq=tq, ru=min(d, 4)),
        grid=(B, nsteps),
        in_specs=[cur(col0), cur(col0 + 1), prev(col0 + 1), cur(col0 + 2), prev(col0 + 2)],
        out_specs=[out_spec, out_spec],
        out_shape=[jax.ShapeDtypeStruct((DL_HEADS_PER_GROUP, B * S, LANES), F32)] * 2,
        compiler_params=_params(("parallel", "arbitrary")),
    )(src, src, src, src, src)


def _route(logits):
    lane = lax.broadcasted_iota(jnp.int32, logits.shape, 1)
    big = jnp.int32(1 << 20)
    is_g = (lane >= MOE_N_EXPERTS) & (lane < MOE_N_EXPERTS + MOE_GROUPS)
    lg = jnp.where(is_g, logits, -jnp.inf)
    gmax = jnp.max(lg, axis=-1, keepdims=True)
    gsum = jnp.sum(jnp.exp(lg - gmax), axis=-1, keepdims=True)
    g_w = 1.0 / gsum
    g_idx = jnp.min(jnp.where(lg == gmax, lane - MOE_N_EXPERTS, big), axis=-1, keepdims=True)
    in_grp = (lane < MOE_N_EXPERTS) & ((lane // MOE_EXPERTS_PER_GROUP) == g_idx)
    le = jnp.where(in_grp, logits, -jnp.inf)
    t1 = jnp.max(le, axis=-1, keepdims=True)
    e1 = jnp.min(jnp.where(le == t1, lane, big), axis=-1, keepdims=True)
    le2 = jnp.where(lane == e1, -jnp.inf, le)
    t2 = jnp.max(le2, axis=-1, keepdims=True)
    e2 = jnp.min(jnp.where(le2 == t2, lane, big), axis=-1, keepdims=True)
    r = jnp.exp(t2 - t1)
    w1 = g_w / (1.0 + r)
    w2 = w1 * r
    out = jnp.where(lane == 0, e1.astype(F32),
                    jnp.where(lane == 1, e2.astype(F32),
                              jnp.where(lane == 2, w1, jnp.where(lane == 3, w2, 0.0))))
    return out


def _outproj_kernel(x_ref, oa_ref, ga_ref, gb_ref, o0_ref, o1_ref, o2_ref, l0_ref, l1_ref, l2_ref,
                    wa_ref, wb_ref, wo_ref, gf_ref, rh_ref, rc_ref,
                    x1_ref, hn_ref, rt_ref):
    obs = []
    for hh in range(DL_HEADS_PER_GROUP):
        l0, l1, l2 = l0_ref[hh], l1_ref[hh], l2_ref[hh]
        lm = jnp.maximum(jnp.maximum(l0, l1), l2)
        e0, e1, e2 = jnp.exp2(l0 - lm), jnp.exp2(l1 - lm), jnp.exp2(l2 - lm)
        obs.append((e0 * o0_ref[hh] + e1 * o1_ref[hh] + e2 * o2_ref[hh]) / (e0 + e1 + e2))
    ob = jnp.concatenate(obs, axis=1)
    a = _dot(oa_ref[...], wa_ref[...])
    b = _dot(ob.astype(BF16), wb_ref[...])
    mixed = jax.nn.sigmoid(ga_ref[...].astype(F32)) * a + jax.nn.sigmoid(gb_ref[...].astype(F32)) * b
    x1 = x_ref[...] + _dot(mixed.astype(BF16), wo_ref[...])
    x1_ref[...] = x1
    ms = jnp.mean(x1 * x1, axis=-1, keepdims=True)
    hn = x1 * lax.rsqrt(ms + EPS) * gf_ref[...]
    hn_ref[...] = hn
    hn_hi = hn.astype(BF16)
    hn_lo = (hn - hn_hi.astype(F32)).astype(BF16)
    t = _dot(hn_hi, rc_ref[...])
    logits = t[:, 0:LANES] + (_dot(hn_lo, rh_ref[...]) + t[:, LANES:2 * LANES])
    rt_ref[...] = _route(logits)


def _outproj(x2, o_a, proj, dl_o, dl_lse, wa, wb, wo, gain_ffn, r_hi, r_cat, tm=256):
    T, D = x2.shape
    row = lambda w: pl.BlockSpec((tm, w), lambda i: (i, 0))
    full = lambda s: pl.BlockSpec(s, lambda i: (0, 0), pipeline_mode=pl.Buffered(1))
    hrow = pl.BlockSpec((DL_HEADS_PER_GROUP, tm, LANES), lambda i: (0, i, 0))
    return pl.pallas_call(
        _outproj_kernel,
        grid=(T // tm,),
        in_specs=[
            row(D), row(o_a.shape[1]),
            pl.BlockSpec((tm, D), lambda i: (i, (CT_GATE_A * COL_TILE) // D)),
            pl.BlockSpec((tm, D), lambda i: (i, (CT_GATE_B * COL_TILE) // D)),
            hrow, hrow, hrow, hrow, hrow, hrow,
            full(wa.shape), full(wb.shape), full(wo.shape), full((1, D)), full(r_hi.shape), full(r_cat.shape),
        ],
        out_specs=[row(D), row(D), row(LANES)],
        out_shape=[jax.ShapeDtypeStruct((T, D), F32), jax.ShapeDtypeStruct((T, D), F32),
                   jax.ShapeDtypeStruct((T, LANES), F32)],
        compiler_params=_params(("parallel",)),
    )(x2, o_a, proj, proj, dl_o[0], dl_o[1], dl_o[2], dl_lse[0], dl_lse[1], dl_lse[2],
      wa, wb, wo, gain_ffn, r_hi, r_cat)


def _cast_rows(src_ref, dst_ref, chunk=256):
    def body(c, carry):
        r0 = pl.multiple_of(c * chunk, chunk)
        dst_ref[pl.ds(r0, chunk), :] = src_ref[pl.ds(r0, chunk), :].astype(dst_ref.dtype)
        return carry
    lax.fori_loop(0, src_ref.shape[0] // chunk, body, 0)


def _expert_changed(be_ref, i):
    return (i == 0) | (be_ref[i] != be_ref[jnp.maximum(i - 1, 0)])


def _stream_expert_weights(i, live, be_ref, ne_ref, w_hbm, stage, wbf, wsem):
    changed = live & _expert_changed(be_ref, i)

    def copy(e):
        return pltpu.make_async_copy(w_hbm.at[e], stage, wsem)

    @pl.when(changed & (i == 0))
    def _():
        copy(be_ref[0]).start(priority=WEIGHT_DMA_PRIORITY)

    @pl.when(changed)
    def _():
        copy(be_ref[i]).wait()
        _cast_rows(stage, wbf)

    @pl.when(changed & (ne_ref[i] >= 0))
    def _():
        copy(ne_ref[i]).start(priority=WEIGHT_DMA_PRIORITY)


def _moe_up_kernel(be_ref, ne_ref, nu_ref, tc_ref, tn_ref, hn_ref, wgu_ref, act_ref,
                   xa, xb, wstage, wbf, gsem, wsem, *, d_ff):
    i = pl.program_id(0)
    nu = nu_ref[0]
    rows = xa.shape[0]
    even = i % 2 == 0

    def gather(tok_ref, r, buf, s):
        return pltpu.make_async_copy(hn_ref.at[tok_ref[0, r]], buf.at[r], gsem.at[s])

    def wait_gather(buf, s):
        pltpu.make_async_copy(hn_ref.at[pl.ds(0, rows)], buf, gsem.at[s]).wait()

    @pl.when((i == 0) & (nu > 0))
    def _():
        def body(r, c):
            gather(tc_ref, r, xa, 0).start()
            return c
        lax.fori_loop(0, rows, body, 0)

    _stream_expert_weights(i, i < nu, be_ref, ne_ref, wgu_ref, wstage, wbf, wsem)

    def live_step(cur, nxt, s):
        wait_gather(cur, s)
        for r in range(rows):
            gather(tn_ref, r, nxt, 1 - s).start()
        h = _dot(cur[...].astype(BF16), wbf[...])
        gate = h[:, :d_ff]
        up = h[:, d_ff:]
        act_ref[...] = (gate * jax.nn.sigmoid(gate) * up).astype(act_ref.dtype)

    @pl.when((i < nu) & even)
    def _():
        live_step(xa, xb, 0)

    @pl.when((i < nu) & jnp.logical_not(even))
    def _():
        live_step(xb, xa, 1)

    @pl.when((i == nu) & (nu > 0) & even)
    def _():
        wait_gather(xa, 0)

    @pl.when((i == nu) & (nu > 0) & jnp.logical_not(even))
    def _():
        wait_gather(xb, 1)

    @pl.when(i >= nu)
    def _():
        act_ref[...] = jnp.zeros(act_ref.shape, act_ref.dtype)


def _moe_down_kernel(be_ref, ne_ref, nu_ref, dp_ref, act_ref, wd_ref, y_ref, ya, yb, wstage, wbf, ssem, wsem):
    i = pl.program_id(0)
    nu = nu_ref[0]
    rows = ya.shape[0]
    even = i % 2 == 0

    def scatter(r, buf, s):
        return pltpu.make_async_copy(buf.at[r], y_ref.at[dp_ref[0, r]], ssem.at[s])

    def wait_scatter(buf, s):
        pltpu.make_async_copy(buf, y_ref.at[pl.ds(0, rows)], ssem.at[s]).wait()

    @pl.when(i == 0)
    def _():
        n_res = y_ref.shape[0] - 2 * rows
        for s, buf in enumerate((ya, yb)):
            buf[...] = jnp.zeros(buf.shape, buf.dtype)
            pltpu.make_async_copy(buf, y_ref.at[pl.ds(n_res + s * rows, rows)], ssem.at[s]).start()
        for s, buf in enumerate((ya, yb)):
            pltpu.make_async_copy(buf, y_ref.at[pl.ds(n_res + s * rows, rows)], ssem.at[s]).wait()

    @pl.when((i >= 2) & (i < nu + 2) & even)
    def _():
        wait_scatter(ya, 0)

    @pl.when((i >= 2) & (i < nu + 2) & jnp.logical_not(even))
    def _():
        wait_scatter(yb, 1)

    _stream_expert_weights(i, i < nu, be_ref, ne_ref, wd_ref, wstage, wbf, wsem)

    def step(cur, prv, s, do_scatter, do_compute):
        if do_scatter:
            for r in range(rows):
                scatter(r, prv, 1 - s).start()
        if do_compute:
            cur[...] = _dot(act_ref[...], wbf[...])

    for s, (cur, prv) in enumerate(((ya, yb), (yb, ya))):
        par = even if s == 0 else jnp.logical_not(even)

        @pl.when((i >= 1) & (i < nu) & par)
        def _(cur=cur, prv=prv, s=s):
            step(cur, prv, s, True, True)

        @pl.when((i == 0) & (nu > 0) & par)
        def _(cur=cur, prv=prv, s=s):
            step(cur, prv, s, False, True)

        @pl.when((i == nu) & (nu > 0) & par)
        def _(cur=cur, prv=prv, s=s):
            step(cur, prv, s, True, False)


def _experts(hn, src_tok, dst_row, blk_expert, nxt_expert, n_used, wgu, wd, n_out_rows):
    T, D = hn.shape
    nblk = src_tok.shape[0] // MOE_BLOCK
    d_ff = wd.shape[1]
    tok = src_tok.reshape(nblk, 1, MOE_BLOCK)
    dst = dst_row.reshape(nblk, 1, MOE_BLOCK)

    def live(i, nu):
        return jnp.maximum(jnp.minimum(i, nu[0] - 1), 0)

    smem = lambda f: pl.BlockSpec((None, 1, MOE_BLOCK), f, memory_space=pltpu.SMEM)
    act = pl.pallas_call(
        functools.partial(_moe_up_kernel, d_ff=d_ff),
        grid_spec=pltpu.PrefetchScalarGridSpec(
            num_scalar_prefetch=3,
            grid=(nblk - 1,),
            in_specs=[
                smem(lambda i, be, ne, nu: (i, 0, 0)),
                smem(lambda i, be, ne, nu: (i + 1, 0, 0)),
                pl.BlockSpec(memory_space=pl.ANY),
                pl.BlockSpec(memory_space=pl.ANY),
            ],
            out_specs=pl.BlockSpec((MOE_BLOCK, d_ff), lambda i, be, ne, nu: (i, 0)),
            scratch_shapes=[pltpu.VMEM((MOE_BLOCK, D), F32), pltpu.VMEM((MOE_BLOCK, D), F32),
                            pltpu.VMEM((D, 2 * d_ff), F32), pltpu.VMEM((D, 2 * d_ff), BF16),
                            pltpu.SemaphoreType.DMA((2,)), pltpu.SemaphoreType.DMA(())],
        ),
        out_shape=jax.ShapeDtypeStruct(((nblk - 1) * MOE_BLOCK, d_ff), BF16),
        compiler_params=_params(("arbitrary",)),
    )(blk_expert, nxt_expert, n_used, tok, tok, hn, wgu)
    return pl.pallas_call(
        _moe_down_kernel,
        grid_spec=pltpu.PrefetchScalarGridSpec(
            num_scalar_prefetch=3,
            grid=(nblk,),
            in_specs=[
                smem(lambda i, be, ne, nu: (jnp.maximum(i - 1, 0), 0, 0)),
                pl.BlockSpec((MOE_BLOCK, d_ff), lambda i, be, ne, nu: (live(i, nu), 0)),
                pl.BlockSpec(memory_space=pl.ANY),
            ],
            out_specs=pl.BlockSpec(memory_space=pl.ANY),
            scratch_shapes=[pltpu.VMEM((MOE_BLOCK, D), F32), pltpu.VMEM((MOE_BLOCK, D), F32),
                            pltpu.VMEM((d_ff, D), F32), pltpu.VMEM((d_ff, D), BF16),
                            pltpu.SemaphoreType.DMA((2,)), pltpu.SemaphoreType.DMA(())],
        ),
        out_shape=jax.ShapeDtypeStruct((n_out_rows, D), F32),
        compiler_params=_params(("arbitrary",)),
    )(blk_expert, nxt_expert, n_used, dst, act, wd)


def _combine_kernel(x1_ref, rt_ref, y1_ref, y2_ref, o_ref):
    rt = rt_ref[...]
    o_ref[...] = x1_ref[...] + (rt[:, 2:3] * y1_ref[...] + rt[:, 3:4] * y2_ref[...])


def _combine(x1, route, y, tm=512):
    T, D = x1.shape
    nt = T // tm
    return pl.pallas_call(
        _combine_kernel,
        grid=(nt,),
        in_specs=[
            pl.BlockSpec((tm, D), lambda i: (i, 0)),
            pl.BlockSpec((tm, LANES), lambda i: (i, 0)),
            pl.BlockSpec((tm, D), lambda i: (i, 0)),
            pl.BlockSpec((tm, D), lambda i: (nt + i, 0)),
        ],
        out_specs=pl.BlockSpec((tm, D), lambda i: (i, 0)),
        out_shape=jax.ShapeDtypeStruct((T, D), F32),
        compiler_params=_params(("parallel",)),
    )(x1, route, y, y)


def _dispatch_tables(route, T):
    TK = T * MOE_TOP_K
    flat_e = route[:, :MOE_TOP_K].astype(jnp.int32).reshape(-1)
    order = jnp.argsort(flat_e).astype(jnp.int32)
    sizes = jnp.sum(flat_e[:, None] == jnp.arange(MOE_N_EXPERTS, dtype=jnp.int32)[None, :], axis=0,
                    dtype=jnp.int32)
    start = jnp.cumsum(sizes) - sizes
    padded = ((sizes + MOE_BLOCK - 1) // MOE_BLOCK) * MOE_BLOCK
    pad_end = jnp.cumsum(padded)
    pad_start = pad_end - padded
    n_blocks = TK // MOE_BLOCK + MOE_N_EXPERTS
    blk = jnp.arange(n_blocks + 2, dtype=jnp.int32)
    blk_expert = jnp.minimum(jnp.sum(pad_end[None, :] <= (blk * MOE_BLOCK)[:, None], axis=1, dtype=jnp.int32),
                             MOE_N_EXPERTS - 1)
    r = jnp.arange(MOE_BLOCK, dtype=jnp.int32)[None, :]
    pos = blk[:, None] * MOE_BLOCK + r
    off = pos - pad_start[blk_expert][:, None]
    live = (off < sizes[blk_expert][:, None]) & (pos < pad_end[-1])
    sorted_idx = jnp.where(live, off + start[blk_expert][:, None], 0)
    slot = order[sorted_idx]
    tok = slot // MOE_TOP_K
    src_tok = jnp.where(live, tok, 0).reshape(-1)
    trash = TK + (blk[:, None] % 2) * MOE_BLOCK + r
    dst_row = jnp.where(live, (slot % MOE_TOP_K) * T + tok, trash).reshape(-1)
    n_used = (pad_end[-1:] // MOE_BLOCK).astype(jnp.int32)
    eid = jnp.arange(MOE_N_EXPERTS, dtype=jnp.int32)
    later = (eid[None, :] > eid[:, None]) & (sizes[None, :] > 0)
    nxt = jnp.min(jnp.where(later, eid[None, :], MOE_N_EXPERTS), axis=1)
    nxt_expert = jnp.where(nxt < MOE_N_EXPERTS, nxt, -1).astype(jnp.int32)[blk_expert]
    return src_tok, dst_row, blk_expert, nxt_expert, n_used, TK + 2 * MOE_BLOCK


def _w_in_tile_order(d_gate):
    da = DA_HEADS * 2 * DA_QK_DIM
    dav = DA_HEADS * DA_V_DIM
    dl = len(DL_GROUPS) * DL_HEADS_PER_GROUP * DL_HEAD_DIM
    o = [int(v) // COL_TILE for v in np.cumsum([0, da, da, dav, dl, dl, dl, d_gate, d_gate])]
    order = list(range(o[6], o[8])) + list(range(o[0], o[3]))
    for g in range(len(DL_GROUPS)):
        order += [o[3] + g, o[4] + g, o[5] + g]
    assert sorted(order) == list(range(CT_END))
    return jnp.asarray(order, jnp.int32)


def _gain_table(da_q_norm, da_k_norm, dl_q_norm, dl_k_norm):
    ones = jnp.ones((COL_TILE,), F32)
    daq = jnp.tile(da_q_norm, COL_TILE // DA_QK_DIM) * (DA_QK_DIM ** -0.5 * LOG2E)
    dak = jnp.tile(da_k_norm, COL_TILE // DA_QK_DIM)
    dlq = jnp.tile(dl_q_norm, COL_TILE // DL_HEAD_DIM) * (DL_HEAD_DIM ** -0.5 * LOG2E)
    dlk = jnp.tile(dl_k_norm, COL_TILE // DL_HEAD_DIM)
    rows = []
    for j in range(CT_END):
        if CT_DA_Q <= j < CT_DA_K:
            rows.append(daq)
        elif CT_DA_K <= j < CT_DA_V:
            rows.append(dak)
        elif j >= CT_DL and (j - CT_DL) % 3 == 0:
            rows.append(dlq)
        elif j >= CT_DL and (j - CT_DL) % 3 == 1:
            rows.append(dlk)
        else:
            rows.append(ones)
    return jnp.stack(rows, axis=0).reshape(CT_END, 1, COL_TILE)


def kernel(x, norm_mix, w_in, da_q_norm, da_k_norm, da_lambda_q, da_lambda_k, da_sub_norm,
           dl_q_norm, dl_k_norm, w_branch_a, w_branch_b, w_out, norm_ffn,
           w_group_router, w_expert_router, w_gate_up, w_down):
    B, S, D = x.shape
    T = B * S
    depth = w_in.shape[0]
    x2 = x.reshape(T, D)
    for l in range(depth):
        lam_init = 0.8 - 0.6 * math.exp(-0.3 * l)
        gain_tab = _gain_table(da_q_norm[l], da_k_norm[l], dl_q_norm[l], dl_k_norm[l])
        proj, dl1, dl2 = _inproj(x2, norm_mix[l].reshape(1, D), w_in[l].astype(BF16), _w_in_tile_order(D),
                                 gain_tab, B, S)

        o_a = _diff_attention(proj, da_lambda_q[l], da_lambda_k[l], da_sub_norm[l].reshape(1, DA_V_DIM),
                              B, S, lam_init)
        dl = [_dilated_group(proj.reshape(B, S, proj.shape[1]), CT_DL, 0, B, S),
              _dilated_group(dl1, 0, 1, B, S), _dilated_group(dl2, 0, 2, B, S)]

        w_r = jnp.concatenate([w_expert_router[l], w_group_router[l]], axis=1)
        w_r = jnp.pad(w_r, ((0, 0), (0, LANES - w_r.shape[1])))
        r_hi = w_r.astype(BF16)
        r_lo = (w_r - r_hi.astype(F32)).astype(BF16)
        x1, hn, route = _outproj(
            x2, o_a, proj, [t[0] for t in dl], [t[1] for t in dl],
            w_branch_a[l].astype(BF16), w_branch_b[l].astype(BF16), w_out[l].astype(BF16),
            norm_ffn[l].reshape(1, D), r_hi, jnp.concatenate([r_hi, r_lo], axis=1))

        src_tok, dst_row, blk_expert, nxt_expert, n_used, n_rows = _dispatch_tables(route, T)
        y = _experts(hn, src_tok, dst_row, blk_expert, nxt_expert, n_used, w_gate_up[l], w_down[l], n_rows)
        x2 = _combine(x1, route, y)
    return x2.reshape(B, S, D)
```

```python
import functools
import math

import jax
import jax.numpy as jnp
import numpy as np
from jax import lax
from jax.experimental import pallas as pl
from jax.experimental.pallas import tpu as pltpu

F32 = jnp.float32
BF16 = jnp.bfloat16

EPS = 1e-6
LOG2E = 1.4426950408889634
NEG_BIG = -1e30

DA_HEADS = 8
DA_QK_DIM = 64
DA_V_DIM = 128
DA_TILE_GROUP = 4
DL_GROUPS = ((128, 1), (512, 4), (2048, 16))
DL_HEADS_PER_GROUP = 4
DL_HEAD_DIM = 128
DL_SPAN = 128
MOE_GROUPS = 4
MOE_EXPERTS_PER_GROUP = 8
MOE_N_EXPERTS = 32
MOE_TOP_K = 2
MOE_BLOCK = 256
WEIGHT_DMA_PRIORITY = 1

LANES = 128
COL_TILE = 512
VMEM_LIMIT = 56 * 1024 * 1024

INPROJ_ROWS = 1024
DA_TILE = 512
DA_ROW_CHUNK = 32
DL_TOKENS_PER_STEP = 2048
DL_RESIDUES_PER_TRIP = 4
DL_LOOKAHEAD = 4
OUTPROJ_ROWS = 256
COMBINE_ROWS = 512
CAST_ROWS = 256

CT_GATE_A, CT_GATE_B, CT_DA_Q, CT_DA_K, CT_DA_V, CT_DL, CT_MAIN_END, CT_END = 0, 4, 8, 10, 12, 14, 17, 23


def _params(sem, vmem=VMEM_LIMIT):
    return pltpu.CompilerParams(dimension_semantics=sem, vmem_limit_bytes=vmem)


def _dot(a, b):
    return jnp.dot(a, b, preferred_element_type=F32)


def _dot_nt(a, b):
    return lax.dot_general(a, b, (((1,), (1,)), ((), ())), preferred_element_type=F32)


def _inproj_kernel(perm_ref, x_ref, g_ref, w_ref, gain_ref, o_ref, d1_ref, d2_ref, h_scr, y_scr):
    j = pl.program_id(1)

    @pl.when(j == 0)
    def _():
        x = x_ref[...]
        ms = jnp.mean(x * x, axis=-1, keepdims=True)
        h_scr[...] = (x * lax.rsqrt(ms + EPS) * g_ref[...]).astype(BF16)

    gain = gain_ref[...]
    half = COL_TILE // 2

    def head_slices():
        ys = [_dot(h_scr[...], w_ref[:, hf * half:(hf + 1) * half]) for hf in range(2)]
        for hf in range(2):
            for hh in range(half // LANES):
                yield hf * (half // LANES) + hh, ys[hf][:, hh * LANES:(hh + 1) * LANES]

    is64 = (j >= CT_DA_Q) & (j < CT_DA_V)
    is128 = (j >= CT_DL) & (lax.rem(j - CT_DL, 3) < 2)
    main = j < CT_MAIN_END
    plain = jnp.logical_not(is64 | is128)

    def norm64(h, yh):
        sq = yh * yh
        lo = lax.broadcasted_iota(jnp.int32, yh.shape, 1) < DA_QK_DIM
        s_lo = jnp.sum(jnp.where(lo, sq, 0.0), axis=-1, keepdims=True)
        s_hi = jnp.sum(jnp.where(lo, 0.0, sq), axis=-1, keepdims=True)
        r = jnp.where(lo, lax.rsqrt(s_lo * (1.0 / DA_QK_DIM) + EPS), lax.rsqrt(s_hi * (1.0 / DA_QK_DIM) + EPS))
        return yh * r * gain[:, h * LANES:(h + 1) * LANES]

    def norm128(h, yh):
        ss = jnp.sum(yh * yh, axis=-1, keepdims=True)
        return yh * lax.rsqrt(ss * (1.0 / DL_HEAD_DIM) + EPS) * gain[:, h * LANES:(h + 1) * LANES]

    def emit(cond, fn, to_main):
        @pl.when(cond)
        def _():
            for h, yh in head_slices():
                if to_main:
                    o_ref[:, h * LANES:(h + 1) * LANES] = fn(h, yh).astype(o_ref.dtype)
                else:
                    y_scr[h] = fn(h, yh)

    emit(is64, norm64, True)
    emit(is128 & main, norm128, True)
    emit(is128 & jnp.logical_not(main), norm128, False)
    emit(plain & main, lambda h, yh: yh, True)
    emit(plain & jnp.logical_not(main), lambda h, yh: yh, False)

    def deinterleave(dst_ref):
        d, rows = dst_ref.shape[0], dst_ref.shape[1]
        for r in range(d):
            for h in range(COL_TILE // LANES):
                dst_ref[r, :, h * LANES:(h + 1) * LANES] = (
                    y_scr[h, pl.ds(r, rows, stride=d), :].astype(dst_ref.dtype))

    @pl.when((j >= CT_MAIN_END) & (j < CT_MAIN_END + 3))
    def _():
        deinterleave(d1_ref)

    @pl.when(j >= CT_MAIN_END + 3)
    def _():
        deinterleave(d2_ref)


def _inproj(x2, gain_mix, w_bf, tile_perm, gain_tab, B, S, tm=INPROJ_ROWS):
    T, D = x2.shape
    tiles_per_batch = S // tm
    d1, d2 = DL_GROUPS[1][1], DL_GROUPS[2][1]
    part1 = lambda j: jnp.clip(j - CT_MAIN_END, 0, 2)
    part2 = lambda j: jnp.clip(j - CT_MAIN_END - 3, 0, 2)
    grid_spec = pltpu.PrefetchScalarGridSpec(
        num_scalar_prefetch=1,
        grid=(T // tm, CT_END),
        in_specs=[
            pl.BlockSpec((tm, D), lambda i, j, perm: (i, 0)),
            pl.BlockSpec((1, D), lambda i, j, perm: (0, 0)),
            pl.BlockSpec((D, COL_TILE), lambda i, j, perm: (0, perm[j])),
            pl.BlockSpec((None, 1, COL_TILE), lambda i, j, perm: (j, 0, 0)),
        ],
        out_specs=[
            pl.BlockSpec((tm, COL_TILE), lambda i, j, perm: (i, jnp.minimum(j, CT_MAIN_END - 1))),
            pl.BlockSpec((d1, tm // d1, COL_TILE),
                         lambda i, j, perm: (i // tiles_per_batch, i % tiles_per_batch, part1(j))),
            pl.BlockSpec((d2, tm // d2, COL_TILE),
                         lambda i, j, perm: (i // tiles_per_batch, i % tiles_per_batch, part2(j))),
        ],
        scratch_shapes=[pltpu.VMEM((tm, D), BF16), pltpu.VMEM((COL_TILE // LANES, tm, LANES), F32)],
    )
    return pl.pallas_call(
        _inproj_kernel,
        grid_spec=grid_spec,
        out_shape=[
            jax.ShapeDtypeStruct((T, CT_MAIN_END * COL_TILE), BF16),
            jax.ShapeDtypeStruct((B * d1, S // d1, 3 * COL_TILE), BF16),
            jax.ShapeDtypeStruct((B * d2, S // d2, 3 * COL_TILE), BF16),
        ],
        compiler_params=_params(("parallel", "arbitrary")),
    )(tile_perm, x2, gain_mix, w_bf, gain_tab)


def _bf16_pieces(x, n=3):
    out = []
    r = np.float64(x)
    for _ in range(n):
        p = np.asarray(np.float32(r)).astype(jnp.bfloat16).astype(np.float64)
        out.append(float(p))
        r = r - p
    return out


def _alibi_features(tk):
    pieces = _bf16_pieces(LOG2E)
    qf = np.zeros((2, LANES), np.float32)
    kf = np.zeros((2, tk, LANES), np.float32)
    j = np.arange(tk)
    hi, lo = (j // 16) * 16, j % 16
    for m in range(2):
        f0 = DA_QK_DIM if m == 0 else 0
        for n, p in enumerate(pieces):
            qf[m, f0 + 2 * n] = p
            qf[m, f0 + 2 * n + 1] = p
            kf[m, :, f0 + 2 * n] = hi
            kf[m, :, f0 + 2 * n + 1] = lo
    return jnp.asarray(qf), jnp.asarray(kf, dtype=BF16)


def _da_kernel(q_ref, k_ref, v_ref, qf_ref, kf_ref, lq_ref, lk_ref, sg_ref, o_ref,
               s00, s01, s10, s11, p00, p01, p10, p11, pd0, pd1,
               m0_scr, m1_scr, l0_scr, l1_scr, a0_scr, a1_scr, acc0_scr, acc1_scr, *, tq, rc, lam_init):
    h = pl.program_id(1)
    qi = pl.program_id(2)
    nlb = tq // LANES
    pow2 = jnp.exp2(-(h + 1).astype(F32))
    slope2 = pow2 * LOG2E

    q = q_ref[...]
    lane = lax.broadcasted_iota(jnp.int32, (tq, LANES), 1)
    own = (lane < DA_QK_DIM, lane >= DA_QK_DIM)
    qfs = [jnp.where(own[mi], q, jnp.broadcast_to((qf_ref[mi:mi + 1, :] * pow2).astype(BF16), q.shape))
           for mi in range(2)]

    m_scrs, l_scrs, a_scrs, acc_scrs = (m0_scr, m1_scr), (l0_scr, l1_scr), (a0_scr, a1_scr), (acc0_scr, acc1_scr)
    for mi in range(2):
        m_scrs[mi][...] = jnp.full(m_scrs[mi].shape, NEG_BIG, F32)
        l_scrs[mi][...] = jnp.zeros(l_scrs[mi].shape, F32)
        acc_scrs[mi][...] = jnp.zeros(acc_scrs[mi].shape, F32)

    def scores(ki, mi, s_ref):
        k = k_ref[pl.ds(pl.multiple_of(ki * tq, tq), tq), :]
        s_ref[...] = _dot_nt(qfs[mi], jnp.where(own[mi], k, kf_ref[mi]))

    def softmax(ki, mi, s_ref, p_ref, masked):
        m_scr, l_scr, a_scr = m_scrs[mi], l_scrs[mi], a_scrs[mi]
        c = slope2 * ((ki - qi) * tq).astype(F32)
        for r in range(tq // rc):
            rows = slice(r * rc, (r + 1) * rc)
            nb = min(nlb, ((r + 1) * rc - 1) // LANES + 1) if masked else nlb
            sb = []
            for j in range(nb):
                cs = slice(j * LANES, (j + 1) * LANES)
                s = s_ref[rows, cs]
                if masked and (j + 1) * LANES - 1 > r * rc:
                    rr = lax.broadcasted_iota(jnp.int32, (rc, LANES), 0) + r * rc
                    cc = lax.broadcasted_iota(jnp.int32, (rc, LANES), 1) + j * LANES
                    s = jnp.where(cc <= rr, s, NEG_BIG)
                sb.append(s)
            mx = sb[0]
            for s in sb[1:]:
                mx = jnp.maximum(mx, s)
            m_prev = m_scr[rows, :]
            m_new = jnp.maximum(m_prev, jnp.max(mx, axis=-1, keepdims=True) + c)
            alpha = jnp.exp2(m_prev - m_new)
            a_scr[rows, :] = alpha
            m_scr[rows, :] = m_new
            mc = m_new - c
            psum = alpha * l_scr[rows, :]
            for j in range(nlb):
                cs = slice(j * LANES, (j + 1) * LANES)
                if j < nb:
                    p = jnp.exp2(sb[j] - mc)
                    psum = psum + p
                    p_ref[rows, cs] = p.astype(BF16)
                else:
                    p_ref[rows, cs] = jnp.zeros((rc, LANES), BF16)
            l_scr[rows, :] = psum

    def values(ki, mi, p_ref):
        v = v_ref[pl.ds(pl.multiple_of(ki * tq, tq), tq), :]
        acc_scrs[mi][...] = a_scrs[mi][...] * acc_scrs[mi][...] + _dot(p_ref[...], v)

    s_bufs, p_bufs = ((s00, s01), (s10, s11)), ((p00, p01), (p10, p11))

    def tile_group(k0, n, last_masked):
        for mi in range(2):
            scores(k0, mi, s_bufs[0][mi])
        for i in range(n):
            masked = last_masked and i == n - 1
            for mi in range(2):
                p_ref = (pd0, pd1)[mi] if masked else p_bufs[i % 2][mi]
                softmax(k0 + i, mi, s_bufs[i % 2][mi], p_ref, masked)
                values(k0 + i, mi, p_ref)
                if i + 1 < n:
                    scores(k0 + i + 1, mi, s_bufs[(i + 1) % 2][mi])

    def body(t, carry):
        tile_group(DA_TILE_GROUP * t, DA_TILE_GROUP, False)
        return carry

    n_full = qi // DA_TILE_GROUP
    lax.fori_loop(0, n_full, body, 0)
    for rem in range(1, DA_TILE_GROUP + 1):
        @pl.when(qi - DA_TILE_GROUP * n_full == rem - 1)
        def _(rem=rem):
            tile_group(qi - (rem - 1), rem, True)

    lam_e = jnp.exp(jnp.sum(lq_ref[...] * lk_ref[...], axis=-1, keepdims=True))
    lam = lam_e[0:1, :] - lam_e[1:2, :] + lam_init
    l0 = jnp.sum(l0_scr[...], axis=-1, keepdims=True)
    l1 = jnp.sum(l1_scr[...], axis=-1, keepdims=True)
    o = acc0_scr[...] / l0 - lam * (acc1_scr[...] / l1)
    ms = jnp.mean(o * o, axis=-1, keepdims=True)
    o = o * lax.rsqrt(ms + EPS) * sg_ref[...] * (1.0 - lam_init)
    o_ref[...] = o.astype(o_ref.dtype)


def _diff_attention(proj, lam_q, lam_k, sub_gain, B, S, lam_init, tq=DA_TILE, rc=DA_ROW_CHUNK):
    T = proj.shape[0]
    nq = S // tq
    lb = LANES
    q_blk0, k_blk0, v_blk0 = (CT_DA_Q * COL_TILE) // lb, (CT_DA_K * COL_TILE) // lb, (CT_DA_V * COL_TILE) // lb
    qfeat, kfeat = _alibi_features(tq)
    const = lambda shape: pl.BlockSpec(shape, lambda b, h, i: (0,) * len(shape))
    return pl.pallas_call(
        functools.partial(_da_kernel, tq=tq, rc=rc, lam_init=lam_init),
        grid=(B, DA_HEADS, nq),
        in_specs=[
            pl.BlockSpec((tq, lb), lambda b, h, i: (b * nq + i, q_blk0 + h)),
            pl.BlockSpec((S, lb), lambda b, h, i: (b, k_blk0 + h)),
            pl.BlockSpec((S, lb), lambda b, h, i: (b, v_blk0 + h)),
            const((2, LANES)), const((2, tq, LANES)),
            const((2, DA_QK_DIM)), const((2, DA_QK_DIM)), const((1, DA_V_DIM)),
        ],
        out_specs=pl.BlockSpec((tq, lb), lambda b, h, i: (b * nq + i, h)),
        out_shape=jax.ShapeDtypeStruct((T, DA_HEADS * DA_V_DIM), BF16),
        scratch_shapes=[pltpu.VMEM((tq, tq), F32)] * 4 + [pltpu.VMEM((tq, tq), BF16)] * 6
        + [pltpu.VMEM((tq, LANES), F32)] * 6 + [pltpu.VMEM((tq, DA_V_DIM), F32)] * 2,
        compiler_params=_params(("parallel", "parallel", "arbitrary")),
    )(proj, proj, proj, qfeat, kfeat, lam_q, lam_k, sub_gain)


def _dl_kernel(q_ref, kc_ref, kp_ref, vc_ref, vp_ref, o_ref, lse_ref, *, slopes2, d, tq, ru):
    n = pl.program_id(1)
    sp = DL_SPAN
    row = lax.broadcasted_iota(jnp.int32, (sp, sp), 0)
    col = lax.broadcasted_iota(jnp.int32, (sp, sp), 1)
    dcur = row - col
    cur_ok = dcur >= 0
    prev_ok = dcur <= 0
    dcur_f = dcur.astype(F32)

    def scores(r, hh, j):
        hs = slice(hh * LANES, (hh + 1) * LANES)
        rs = slice(j * sp, (j + 1) * sp)
        q = q_ref[r, rs, hs]
        if j == 0:
            kp, vp, p_ok = kp_ref[r, :, hs], vp_ref[r, :, hs], prev_ok & (n > 0)
        else:
            ps = slice((j - 1) * sp, j * sp)
            kp, vp, p_ok = kc_ref[r, ps, hs], vc_ref[r, ps, hs], prev_ok
        s_c = jnp.where(cur_ok, _dot_nt(q, kc_ref[r, rs, hs]) - slopes2[hh] * dcur_f, NEG_BIG)
        s_p = jnp.where(p_ok, _dot_nt(q, kp) - slopes2[hh] * (dcur_f + float(sp)), NEG_BIG)
        return s_c, s_p, vc_ref[r, rs, hs], vp

    def finish(r, hh, j, s_c, s_p, vc, vp):
        m = jnp.max(jnp.maximum(s_c, s_p), axis=-1, keepdims=True)
        p_c = jnp.exp2(s_c - m)
        p_p = jnp.exp2(s_p - m)
        den = jnp.sum(p_c + p_p, axis=-1, keepdims=True)
        acc = _dot(p_c.astype(BF16), vc) + _dot(p_p.astype(BF16), vp)
        out_rows = pl.ds(j * sp, sp) if d == 1 else pl.ds(r + j * sp * d, sp, stride=d)
        o_ref[hh, out_rows, :] = acc / den
        lse_ref[hh, out_rows, :] = jnp.broadcast_to(m + jnp.log2(den), (sp, LANES))

    def residues(t, carry):
        units = [(t * ru + rr, hh, j) for rr in range(ru) for hh in range(DL_HEADS_PER_GROUP)
                 for j in range(tq // sp)]
        pending = []
        for u in units:
            pending.append((u, scores(*u)))
            if len(pending) > DL_LOOKAHEAD:
                u0, vals = pending.pop(0)
                finish(*u0, *vals)
        for u0, vals in pending:
            finish(*u0, *vals)
        return carry

    lax.fori_loop(0, d // ru, residues, 0)


def _dilated_group(src, col0, g, B, S, tok_per_step=DL_TOKENS_PER_STEP):
    window, d = DL_GROUPS[g]
    assert window // d == DL_SPAN
    L = S // d
    tq = min(tok_per_step, S) // d
    assert tq % DL_SPAN == 0 and L % tq == 0
    nh = DL_HEADS_PER_GROUP * len(DL_GROUPS)
    slopes2 = tuple(2.0 ** (-8.0 * (g * DL_HEADS_PER_GROUP + hh + 1) / nh) * d * LOG2E
                    for hh in range(DL_HEADS_PER_GROUP))
    spb = tq // DL_SPAN
    nsteps = L // tq
    cur = lambda c: pl.BlockSpec((d, tq, COL_TILE), lambda b, n: (b, n, c))
    prev = lambda c: pl.BlockSpec((d, DL_SPAN, COL_TILE), lambda b, n: (b, jnp.maximum(n * spb - 1, 0), c))
    out_spec = pl.BlockSpec((DL_HEADS_PER_GROUP, d * tq, LANES), lambda b, n: (0, b * nsteps + n, 0))
    return pl.pallas_call(
        functools.partial(_dl_kernel, slopes2=slopes2, d=d, tq=tq, ru=min(d, DL_RESIDUES_PER_TRIP)),
        grid=(B, nsteps),
        in_specs=[cur(col0), cur(col0 + 1), prev(col0 + 1), cur(col0 + 2), prev(col0 + 2)],
        out_specs=[out_spec, out_spec],
        out_shape=[jax.ShapeDtypeStruct((DL_HEADS_PER_GROUP, B * S, LANES), F32)] * 2,
        compiler_params=_params(("parallel", "arbitrary")),
    )(src, src, src, src, src)


def _route(logits):
    lane = lax.broadcasted_iota(jnp.int32, logits.shape, 1)
    big = jnp.int32(1 << 20)
    is_g = (lane >= MOE_N_EXPERTS) & (lane < MOE_N_EXPERTS + MOE_GROUPS)
    lg = jnp.where(is_g, logits, -jnp.inf)
    gmax = jnp.max(lg, axis=-1, keepdims=True)
    gsum = jnp.sum(jnp.exp(lg - gmax), axis=-1, keepdims=True)
    g_w = 1.0 / gsum
    g_idx = jnp.min(jnp.where(lg == gmax, lane - MOE_N_EXPERTS, big), axis=-1, keepdims=True)
    in_grp = (lane < MOE_N_EXPERTS) & ((lane // MOE_EXPERTS_PER_GROUP) == g_idx)
    le = jnp.where(in_grp, logits, -jnp.inf)
    t1 = jnp.max(le, axis=-1, keepdims=True)
    e1 = jnp.min(jnp.where(le == t1, lane, big), axis=-1, keepdims=True)
    le2 = jnp.where(lane == e1, -jnp.inf, le)
    t2 = jnp.max(le2, axis=-1, keepdims=True)
    e2 = jnp.min(jnp.where(le2 == t2, lane, big), axis=-1, keepdims=True)
    r = jnp.exp(t2 - t1)
    w1 = g_w / (1.0 + r)
    w2 = w1 * r
    out = jnp.where(lane == 0, e1.astype(F32),
                    jnp.where(lane == 1, e2.astype(F32),
                              jnp.where(lane == 2, w1, jnp.where(lane == 3, w2, 0.0))))
    return out


def _outproj_kernel(x_ref, oa_ref, ga_ref, gb_ref, o0_ref, o1_ref, o2_ref, l0_ref, l1_ref, l2_ref,
                    wa_ref, wb_ref, wo_ref, gf_ref, rh_ref, rc_ref,
                    x1_ref, hn_ref, rt_ref):
    obs = []
    for hh in range(DL_HEADS_PER_GROUP):
        l0, l1, l2 = l0_ref[hh], l1_ref[hh], l2_ref[hh]
        lm = jnp.maximum(jnp.maximum(l0, l1), l2)
        e0, e1, e2 = jnp.exp2(l0 - lm), jnp.exp2(l1 - lm), jnp.exp2(l2 - lm)
        obs.append((e0 * o0_ref[hh] + e1 * o1_ref[hh] + e2 * o2_ref[hh]) / (e0 + e1 + e2))
    ob = jnp.concatenate(obs, axis=1)
    a = _dot(oa_ref[...], wa_ref[...])
    b = _dot(ob.astype(BF16), wb_ref[...])
    mixed = jax.nn.sigmoid(ga_ref[...].astype(F32)) * a + jax.nn.sigmoid(gb_ref[...].astype(F32)) * b
    x1 = x_ref[...] + _dot(mixed.astype(BF16), wo_ref[...])
    x1_ref[...] = x1
    ms = jnp.mean(x1 * x1, axis=-1, keepdims=True)
    hn = x1 * lax.rsqrt(ms + EPS) * gf_ref[...]
    hn_ref[...] = hn
    hn_hi = hn.astype(BF16)
    hn_lo = (hn - hn_hi.astype(F32)).astype(BF16)
    t = _dot(hn_hi, rc_ref[...])
    logits = t[:, 0:LANES] + (_dot(hn_lo, rh_ref[...]) + t[:, LANES:2 * LANES])
    rt_ref[...] = _route(logits)


def _outproj(x2, o_a, proj, dl_o, dl_lse, wa, wb, wo, gain_ffn, r_hi, r_cat, tm=OUTPROJ_ROWS):
    T, D = x2.shape
    row = lambda w: pl.BlockSpec((tm, w), lambda i: (i, 0))
    full = lambda s: pl.BlockSpec(s, lambda i: (0, 0), pipeline_mode=pl.Buffered(1))
    hrow = pl.BlockSpec((DL_HEADS_PER_GROUP, tm, LANES), lambda i: (0, i, 0))
    return pl.pallas_call(
        _outproj_kernel,
        grid=(T // tm,),
        in_specs=[
            row(D), row(o_a.shape[1]),
            pl.BlockSpec((tm, D), lambda i: (i, (CT_GATE_A * COL_TILE) // D)),
            pl.BlockSpec((tm, D), lambda i: (i, (CT_GATE_B * COL_TILE) // D)),
            hrow, hrow, hrow, hrow, hrow, hrow,
            full(wa.shape), full(wb.shape), full(wo.shape), full((1, D)), full(r_hi.shape), full(r_cat.shape),
        ],
        out_specs=[row(D), row(D), row(LANES)],
        out_shape=[jax.ShapeDtypeStruct((T, D), F32), jax.ShapeDtypeStruct((T, D), F32),
                   jax.ShapeDtypeStruct((T, LANES), F32)],
        compiler_params=_params(("parallel",)),
    )(x2, o_a, proj, proj, dl_o[0], dl_o[1], dl_o[2], dl_lse[0], dl_lse[1], dl_lse[2],
      wa, wb, wo, gain_ffn, r_hi, r_cat)


def _cast_rows(src_ref, dst_ref, chunk=CAST_ROWS):
    def body(c, carry):
        r0 = pl.multiple_of(c * chunk, chunk)
        dst_ref[pl.ds(r0, chunk), :] = src_ref[pl.ds(r0, chunk), :].astype(dst_ref.dtype)
        return carry
    lax.fori_loop(0, src_ref.shape[0] // chunk, body, 0)


def _expert_changed(be_ref, i):
    return (i == 0) | (be_ref[i] != be_ref[jnp.maximum(i - 1, 0)])


def _stream_expert_weights(i, live, be_ref, ne_ref, w_hbm, stage, wbf, wsem):
    changed = live & _expert_changed(be_ref, i)

    def copy(e):
        return pltpu.make_async_copy(w_hbm.at[e], stage, wsem)

    @pl.when(changed & (i == 0))
    def _():
        copy(be_ref[0]).start(priority=WEIGHT_DMA_PRIORITY)

    @pl.when(changed)
    def _():
        copy(be_ref[i]).wait()
        _cast_rows(stage, wbf)

    @pl.when(changed & (ne_ref[i] >= 0))
    def _():
        copy(ne_ref[i]).start(priority=WEIGHT_DMA_PRIORITY)


def _moe_up_kernel(be_ref, ne_ref, nu_ref, tc_ref, tn_ref, hn_ref, wgu_ref, act_ref,
                   xa, xb, wstage, wbf, gsem, wsem, *, d_ff):
    i = pl.program_id(0)
    nu = nu_ref[0]
    rows = xa.shape[0]
    even = i % 2 == 0

    def gather(tok_ref, r, buf, s):
        return pltpu.make_async_copy(hn_ref.at[tok_ref[0, r]], buf.at[r], gsem.at[s])

    def wait_gather(buf, s):
        pltpu.make_async_copy(hn_ref.at[pl.ds(0, rows)], buf, gsem.at[s]).wait()

    @pl.when((i == 0) & (nu > 0))
    def _():
        def body(r, c):
            gather(tc_ref, r, xa, 0).start()
            return c
        lax.fori_loop(0, rows, body, 0)

    _stream_expert_weights(i, i < nu, be_ref, ne_ref, wgu_ref, wstage, wbf, wsem)

    def live_step(cur, nxt, s):
        wait_gather(cur, s)
        for r in range(rows):
            gather(tn_ref, r, nxt, 1 - s).start()
        h = _dot(cur[...].astype(BF16), wbf[...])
        gate = h[:, :d_ff]
        up = h[:, d_ff:]
        act_ref[...] = (gate * jax.nn.sigmoid(gate) * up).astype(act_ref.dtype)

    @pl.when((i < nu) & even)
    def _():
        live_step(xa, xb, 0)

    @pl.when((i < nu) & jnp.logical_not(even))
    def _():
        live_step(xb, xa, 1)

    @pl.when((i == nu) & (nu > 0) & even)
    def _():
        wait_gather(xa, 0)

    @pl.when((i == nu) & (nu > 0) & jnp.logical_not(even))
    def _():
        wait_gather(xb, 1)

    @pl.when(i >= nu)
    def _():
        act_ref[...] = jnp.zeros(act_ref.shape, act_ref.dtype)


def _moe_down_kernel(be_ref, ne_ref, nu_ref, dp_ref, act_ref, wd_ref, y_ref, ya, yb, wstage, wbf, ssem, wsem):
    i = pl.program_id(0)
    nu = nu_ref[0]
    rows = ya.shape[0]
    even = i % 2 == 0

    def scatter(r, buf, s):
        return pltpu.make_async_copy(buf.at[r], y_ref.at[dp_ref[0, r]], ssem.at[s])

    def wait_scatter(buf, s):
        pltpu.make_async_copy(buf, y_ref.at[pl.ds(0, rows)], ssem.at[s]).wait()

    @pl.when(i == 0)
    def _():
        n_res = y_ref.shape[0] - 2 * rows
        for s, buf in enumerate((ya, yb)):
            buf[...] = jnp.zeros(buf.shape, buf.dtype)
            pltpu.make_async_copy(buf, y_ref.at[pl.ds(n_res + s * rows, rows)], ssem.at[s]).start()
        for s, buf in enumerate((ya, yb)):
            pltpu.make_async_copy(buf, y_ref.at[pl.ds(n_res + s * rows, rows)], ssem.at[s]).wait()

    @pl.when((i >= 2) & (i < nu + 2) & even)
    def _():
        wait_scatter(ya, 0)

    @pl.when((i >= 2) & (i < nu + 2) & jnp.logical_not(even))
    def _():
        wait_scatter(yb, 1)

    _stream_expert_weights(i, i < nu, be_ref, ne_ref, wd_ref, wstage, wbf, wsem)

    def step(cur, prv, s, do_scatter, do_compute):
        if do_scatter:
            for r in range(rows):
                scatter(r, prv, 1 - s).start()
        if do_compute:
            cur[...] = _dot(act_ref[...], wbf[...])

    for s, (cur, prv) in enumerate(((ya, yb), (yb, ya))):
        par = even if s == 0 else jnp.logical_not(even)

        @pl.when((i >= 1) & (i < nu) & par)
        def _(cur=cur, prv=prv, s=s):
            step(cur, prv, s, True, True)

        @pl.when((i == 0) & (nu > 0) & par)
        def _(cur=cur, prv=prv, s=s):
            step(cur, prv, s, False, True)

        @pl.when((i == nu) & (nu > 0) & par)
        def _(cur=cur, prv=prv, s=s):
            step(cur, prv, s, True, False)


def _experts(hn, src_tok, dst_row, blk_expert, nxt_expert, n_used, wgu, wd, n_out_rows):
    T, D = hn.shape
    nblk = src_tok.shape[0] // MOE_BLOCK
    d_ff = wd.shape[1]
    tok = src_tok.reshape(nblk, 1, MOE_BLOCK)
    dst = dst_row.reshape(nblk, 1, MOE_BLOCK)

    def live(i, nu):
        return jnp.maximum(jnp.minimum(i, nu[0] - 1), 0)

    smem = lambda f: pl.BlockSpec((None, 1, MOE_BLOCK), f, memory_space=pltpu.SMEM)
    act = pl.pallas_call(
        functools.partial(_moe_up_kernel, d_ff=d_ff),
        grid_spec=pltpu.PrefetchScalarGridSpec(
            num_scalar_prefetch=3,
            grid=(nblk - 1,),
            in_specs=[
                smem(lambda i, be, ne, nu: (i, 0, 0)),
                smem(lambda i, be, ne, nu: (i + 1, 0, 0)),
                pl.BlockSpec(memory_space=pl.ANY),
                pl.BlockSpec(memory_space=pl.ANY),
            ],
            out_specs=pl.BlockSpec((MOE_BLOCK, d_ff), lambda i, be, ne, nu: (i, 0)),
            scratch_shapes=[pltpu.VMEM((MOE_BLOCK, D), F32), pltpu.VMEM((MOE_BLOCK, D), F32),
                            pltpu.VMEM((D, 2 * d_ff), F32), pltpu.VMEM((D, 2 * d_ff), BF16),
                            pltpu.SemaphoreType.DMA((2,)), pltpu.SemaphoreType.DMA(())],
        ),
        out_shape=jax.ShapeDtypeStruct(((nblk - 1) * MOE_BLOCK, d_ff), BF16),
        compiler_params=_params(("arbitrary",)),
    )(blk_expert, nxt_expert, n_used, tok, tok, hn, wgu)
    return pl.pallas_call(
        _moe_down_kernel,
        grid_spec=pltpu.PrefetchScalarGridSpec(
            num_scalar_prefetch=3,
            grid=(nblk,),
            in_specs=[
                smem(lambda i, be, ne, nu: (jnp.maximum(i - 1, 0), 0, 0)),
                pl.BlockSpec((MOE_BLOCK, d_ff), lambda i, be, ne, nu: (live(i, nu), 0)),
                pl.BlockSpec(memory_space=pl.ANY),
            ],
            out_specs=pl.BlockSpec(memory_space=pl.ANY),
            scratch_shapes=[pltpu.VMEM((MOE_BLOCK, D), F32), pltpu.VMEM((MOE_BLOCK, D), F32),
                            pltpu.VMEM((d_ff, D), F32), pltpu.VMEM((d_ff, D), BF16),
                            pltpu.SemaphoreType.DMA((2,)), pltpu.SemaphoreType.DMA(())],
        ),
        out_shape=jax.ShapeDtypeStruct((n_out_rows, D), F32),
        compiler_params=_params(("arbitrary",)),
    )(blk_expert, nxt_expert, n_used, dst, act, wd)


def _combine_kernel(x1_ref, rt_ref, y1_ref, y2_ref, o_ref):
    rt = rt_ref[...]
    o_ref[...] = x1_ref[...] + (rt[:, 2:3] * y1_ref[...] + rt[:, 3:4] * y2_ref[...])


def _combine(x1, route, y, tm=COMBINE_ROWS):
    T, D = x1.shape
    nt = T // tm
    return pl.pallas_call(
        _combine_kernel,
        grid=(nt,),
        in_specs=[
            pl.BlockSpec((tm, D), lambda i: (i, 0)),
            pl.BlockSpec((tm, LANES), lambda i: (i, 0)),
            pl.BlockSpec((tm, D), lambda i: (i, 0)),
            pl.BlockSpec((tm, D), lambda i: (nt + i, 0)),
        ],
        out_specs=pl.BlockSpec((tm, D), lambda i: (i, 0)),
        out_shape=jax.ShapeDtypeStruct((T, D), F32),
        compiler_params=_params(("parallel",)),
    )(x1, route, y, y)


def _dispatch_tables(route, T):
    TK = T * MOE_TOP_K
    flat_e = route[:, :MOE_TOP_K].astype(jnp.int32).reshape(-1)
    order = jnp.argsort(flat_e).astype(jnp.int32)
    sizes = jnp.sum(flat_e[:, None] == jnp.arange(MOE_N_EXPERTS, dtype=jnp.int32)[None, :], axis=0,
                    dtype=jnp.int32)
    start = jnp.cumsum(sizes) - sizes
    padded = ((sizes + MOE_BLOCK - 1) // MOE_BLOCK) * MOE_BLOCK
    pad_end = jnp.cumsum(padded)
    pad_start = pad_end - padded
    n_blocks = TK // MOE_BLOCK + MOE_N_EXPERTS
    blk = jnp.arange(n_blocks + 2, dtype=jnp.int32)
    blk_expert = jnp.minimum(jnp.sum(pad_end[None, :] <= (blk * MOE_BLOCK)[:, None], axis=1, dtype=jnp.int32),
                             MOE_N_EXPERTS - 1)
    r = jnp.arange(MOE_BLOCK, dtype=jnp.int32)[None, :]
    pos = blk[:, None] * MOE_BLOCK + r
    off = pos - pad_start[blk_expert][:, None]
    live = (off < sizes[blk_expert][:, None]) & (pos < pad_end[-1])
    sorted_idx = jnp.where(live, off + start[blk_expert][:, None], 0)
    slot = order[sorted_idx]
    tok = slot // MOE_TOP_K
    src_tok = jnp.where(live, tok, 0).reshape(-1)
    trash = TK + (blk[:, None] % 2) * MOE_BLOCK + r
    dst_row = jnp.where(live, (slot % MOE_TOP_K) * T + tok, trash).reshape(-1)
    n_used = (pad_end[-1:] // MOE_BLOCK).astype(jnp.int32)
    eid = jnp.arange(MOE_N_EXPERTS, dtype=jnp.int32)
    later = (eid[None, :] > eid[:, None]) & (sizes[None, :] > 0)
    nxt = jnp.min(jnp.where(later, eid[None, :], MOE_N_EXPERTS), axis=1)
    nxt_expert = jnp.where(nxt < MOE_N_EXPERTS, nxt, -1).astype(jnp.int32)[blk_expert]
    return src_tok, dst_row, blk_expert, nxt_expert, n_used, TK + 2 * MOE_BLOCK


def _w_in_tile_order(d_gate):
    da = DA_HEADS * 2 * DA_QK_DIM
    dav = DA_HEADS * DA_V_DIM
    dl = len(DL_GROUPS) * DL_HEADS_PER_GROUP * DL_HEAD_DIM
    o = [int(v) // COL_TILE for v in np.cumsum([0, da, da, dav, dl, dl, dl, d_gate, d_gate])]
    order = list(range(o[6], o[8])) + list(range(o[0], o[3]))
    for g in range(len(DL_GROUPS)):
        order += [o[3] + g, o[4] + g, o[5] + g]
    assert sorted(order) == list(range(CT_END))
    return jnp.asarray(order, jnp.int32)


def _gain_table(da_q_norm, da_k_norm, dl_q_norm, dl_k_norm):
    ones = jnp.ones((COL_TILE,), F32)
    daq = jnp.tile(da_q_norm, COL_TILE // DA_QK_DIM) * (DA_QK_DIM ** -0.5 * LOG2E)
    dak = jnp.tile(da_k_norm, COL_TILE // DA_QK_DIM)
    dlq = jnp.tile(dl_q_norm, COL_TILE // DL_HEAD_DIM) * (DL_HEAD_DIM ** -0.5 * LOG2E)
    dlk = jnp.tile(dl_k_norm, COL_TILE // DL_HEAD_DIM)
    rows = []
    for j in range(CT_END):
        if CT_DA_Q <= j < CT_DA_K:
            rows.append(daq)
        elif CT_DA_K <= j < CT_DA_V:
            rows.append(dak)
        elif j >= CT_DL and (j - CT_DL) % 3 == 0:
            rows.append(dlq)
        elif j >= CT_DL and (j - CT_DL) % 3 == 1:
            rows.append(dlk)
        else:
            rows.append(ones)
    return jnp.stack(rows, axis=0).reshape(CT_END, 1, COL_TILE)


def kernel(x, norm_mix, w_in, da_q_norm, da_k_norm, da_lambda_q, da_lambda_k, da_sub_norm,
           dl_q_norm, dl_k_norm, w_branch_a, w_branch_b, w_out, norm_ffn,
           w_group_router, w_expert_router, w_gate_up, w_down):
    B, S, D = x.shape
    T = B * S
    depth = w_in.shape[0]
    x2 = x.reshape(T, D)
    for l in range(depth):
        lam_init = 0.8 - 0.6 * math.exp(-0.3 * l)
        gain_tab = _gain_table(da_q_norm[l], da_k_norm[l], dl_q_norm[l], dl_k_norm[l])
        proj, dl1, dl2 = _inproj(x2, norm_mix[l].reshape(1, D), w_in[l].astype(BF16), _w_in_tile_order(D),
                                 gain_tab, B, S)

        o_a = _diff_attention(proj, da_lambda_q[l], da_lambda_k[l], da_sub_norm[l].reshape(1, DA_V_DIM),
                              B, S, lam_init)
        dl = [_dilated_group(proj.reshape(B, S, proj.shape[1]), CT_DL, 0, B, S),
              _dilated_group(dl1, 0, 1, B, S), _dilated_group(dl2, 0, 2, B, S)]

        w_r = jnp.concatenate([w_expert_router[l], w_group_router[l]], axis=1)
        w_r = jnp.pad(w_r, ((0, 0), (0, LANES - w_r.shape[1])))
        r_hi = w_r.astype(BF16)
        r_lo = (w_r - r_hi.astype(F32)).astype(BF16)
        x1, hn, route = _outproj(
            x2, o_a, proj, [t[0] for t in dl], [t[1] for t in dl],
            w_branch_a[l].astype(BF16), w_branch_b[l].astype(BF16), w_out[l].astype(BF16),
            norm_ffn[l].reshape(1, D), r_hi, jnp.concatenate([r_hi, r_lo], axis=1))

        src_tok, dst_row, blk_expert, nxt_expert, n_used, n_rows = _dispatch_tables(route, T)
        y = _experts(hn, src_tok, dst_row, blk_expert, nxt_expert, n_used, w_gate_up[l], w_down[l], n_rows)
        x2 = _combine(x1, route, y)
    return x2.reshape(B, S, D)
```

```python
import functools
import math

import jax
import jax.numpy as jnp
import numpy as np
from jax import lax
from jax.experimental import pallas as pl
from jax.experimental.pallas import tpu as pltpu

F32 = jnp.float32
BF16 = jnp.bfloat16

EPS = 1e-6
LOG2E = 1.4426950408889634
NEG_BIG = -1e30

DA_HEADS = 8
DA_QK_DIM = 64
DA_V_DIM = 128
DA_TILE_GROUP = 4
DL_GROUPS = ((128, 1), (512, 4), (2048, 16))
DL_HEADS_PER_GROUP = 4
DL_HEAD_DIM = 128
DL_SPAN = 128
MOE_GROUPS = 4
MOE_EXPERTS_PER_GROUP = 8
MOE_N_EXPERTS = 32
MOE_TOP_K = 2
MOE_BLOCK = 256
WEIGHT_DMA_PRIORITY = 1

LANES = 128
COL_TILE = 512
VMEM_LIMIT = 56 * 1024 * 1024

INPROJ_ROWS = 1024
DA_TILE = 512
DA_ROW_CHUNK = 32
DL_TOKENS_PER_STEP = 2048
DL_RESIDUES_PER_TRIP = 4
DL_LOOKAHEAD = 4
OUTPROJ_ROWS = 256
COMBINE_ROWS = 512
CAST_ROWS = 256
UP_COL_CHUNK = 256

CT_GATE_A, CT_GATE_B, CT_DA_Q, CT_DA_K, CT_DA_V, CT_DL, CT_MAIN_END, CT_END = 0, 4, 8, 10, 12, 14, 17, 23


def _params(sem, vmem=VMEM_LIMIT):
    return pltpu.CompilerParams(dimension_semantics=sem, vmem_limit_bytes=vmem)


def _dot(a, b):
    return jnp.dot(a, b, preferred_element_type=F32)


def _dot_nt(a, b):
    return lax.dot_general(a, b, (((1,), (1,)), ((), ())), preferred_element_type=F32)


def _inproj_kernel(perm_ref, x_ref, g_ref, w_ref, gain_ref, o_ref, d1_ref, d2_ref, h_scr, y_scr):
    j = pl.program_id(1)

    @pl.when(j == 0)
    def _():
        x = x_ref[...]
        ms = jnp.mean(x * x, axis=-1, keepdims=True)
        h_scr[...] = (x * lax.rsqrt(ms + EPS) * g_ref[...]).astype(BF16)

    gain = gain_ref[...]
    half = COL_TILE // 2

    def head_slices():
        ys = [_dot(h_scr[...], w_ref[:, hf * half:(hf + 1) * half]) for hf in range(2)]
        for hf in range(2):
            for hh in range(half // LANES):
                yield hf * (half // LANES) + hh, ys[hf][:, hh * LANES:(hh + 1) * LANES]

    is64 = (j >= CT_DA_Q) & (j < CT_DA_V)
    is128 = (j >= CT_DL) & (lax.rem(j - CT_DL, 3) < 2)
    main = j < CT_MAIN_END
    plain = jnp.logical_not(is64 | is128)

    def norm64(h, yh):
        sq = yh * yh
        lo = lax.broadcasted_iota(jnp.int32, yh.shape, 1) < DA_QK_DIM
        s_lo = jnp.sum(jnp.where(lo, sq, 0.0), axis=-1, keepdims=True)
        s_hi = jnp.sum(jnp.where(lo, 0.0, sq), axis=-1, keepdims=True)
        r = jnp.where(lo, lax.rsqrt(s_lo * (1.0 / DA_QK_DIM) + EPS), lax.rsqrt(s_hi * (1.0 / DA_QK_DIM) + EPS))
        return yh * r * gain[:, h * LANES:(h + 1) * LANES]

    def norm128(h, yh):
        ss = jnp.sum(yh * yh, axis=-1, keepdims=True)
        return yh * lax.rsqrt(ss * (1.0 / DL_HEAD_DIM) + EPS) * gain[:, h * LANES:(h + 1) * LANES]

    def emit(cond, fn, to_main):
        @pl.when(cond)
        def _():
            for h, yh in head_slices():
                if to_main:
                    o_ref[:, h * LANES:(h + 1) * LANES] = fn(h, yh).astype(o_ref.dtype)
                else:
                    y_scr[h] = fn(h, yh)

    emit(is64, norm64, True)
    emit(is128 & main, norm128, True)
    emit(is128 & jnp.logical_not(main), norm128, False)
    emit(plain & main, lambda h, yh: yh, True)
    emit(plain & jnp.logical_not(main), lambda h, yh: yh, False)

    def deinterleave(dst_ref):
        d, rows = dst_ref.shape[0], dst_ref.shape[1]
        for r in range(d):
            for h in range(COL_TILE // LANES):
                dst_ref[r, :, h * LANES:(h + 1) * LANES] = (
                    y_scr[h, pl.ds(r, rows, stride=d), :].astype(dst_ref.dtype))

    @pl.when((j >= CT_MAIN_END) & (j < CT_MAIN_END + 3))
    def _():
        deinterleave(d1_ref)

    @pl.when(j >= CT_MAIN_END + 3)
    def _():
        deinterleave(d2_ref)


def _inproj(x2, gain_mix, w_bf, tile_perm, gain_tab, B, S, tm=INPROJ_ROWS):
    T, D = x2.shape
    tiles_per_batch = S // tm
    d1, d2 = DL_GROUPS[1][1], DL_GROUPS[2][1]
    part1 = lambda j: jnp.clip(j - CT_MAIN_END, 0, 2)
    part2 = lambda j: jnp.clip(j - CT_MAIN_END - 3, 0, 2)
    grid_spec = pltpu.PrefetchScalarGridSpec(
        num_scalar_prefetch=1,
        grid=(T // tm, CT_END),
        in_specs=[
            pl.BlockSpec((tm, D), lambda i, j, perm: (i, 0)),
            pl.BlockSpec((1, D), lambda i, j, perm: (0, 0)),
            pl.BlockSpec((D, COL_TILE), lambda i, j, perm: (0, perm[j])),
            pl.BlockSpec((None, 1, COL_TILE), lambda i, j, perm: (j, 0, 0)),
        ],
        out_specs=[
            pl.BlockSpec((tm, COL_TILE), lambda i, j, perm: (i, jnp.minimum(j, CT_MAIN_END - 1))),
            pl.BlockSpec((d1, tm // d1, COL_TILE),
                         lambda i, j, perm: (i // tiles_per_batch, i % tiles_per_batch, part1(j))),
            pl.BlockSpec((d2, tm // d2, COL_TILE),
                         lambda i, j, perm: (i // tiles_per_batch, i % tiles_per_batch, part2(j))),
        ],
        scratch_shapes=[pltpu.VMEM((tm, D), BF16), pltpu.VMEM((COL_TILE // LANES, tm, LANES), F32)],
    )
    return pl.pallas_call(
        _inproj_kernel,
        grid_spec=grid_spec,
        out_shape=[
            jax.ShapeDtypeStruct((T, CT_MAIN_END * COL_TILE), BF16),
            jax.ShapeDtypeStruct((B * d1, S // d1, 3 * COL_TILE), BF16),
            jax.ShapeDtypeStruct((B * d2, S // d2, 3 * COL_TILE), BF16),
        ],
        compiler_params=_params(("parallel", "arbitrary")),
    )(tile_perm, x2, gain_mix, w_bf, gain_tab)


def _bf16_pieces(x, n=3):
    out = []
    r = np.float64(x)
    for _ in range(n):
        p = np.asarray(np.float32(r)).astype(jnp.bfloat16).astype(np.float64)
        out.append(float(p))
        r = r - p
    return out


def _alibi_features(tk):
    pieces = _bf16_pieces(LOG2E)
    qf = np.zeros((2, LANES), np.float32)
    kf = np.zeros((2, tk, LANES), np.float32)
    j = np.arange(tk)
    hi, lo = (j // 16) * 16, j % 16
    for m in range(2):
        f0 = DA_QK_DIM if m == 0 else 0
        for n, p in enumerate(pieces):
            qf[m, f0 + 2 * n] = p
            qf[m, f0 + 2 * n + 1] = p
            kf[m, :, f0 + 2 * n] = hi
            kf[m, :, f0 + 2 * n + 1] = lo
    return jnp.asarray(qf), jnp.asarray(kf, dtype=BF16)


def _da_kernel(q_ref, k_ref, v_ref, qf_ref, kf_ref, lq_ref, lk_ref, sg_ref, o_ref,
               s00, s01, s10, s11, p00, p01, p10, p11, pd0, pd1,
               m0_scr, m1_scr, l0_scr, l1_scr, a0_scr, a1_scr, acc0_scr, acc1_scr, *, tq, rc, lam_init):
    h = pl.program_id(1)
    qi = pl.program_id(2)
    nlb = tq // LANES
    pow2 = jnp.exp2(-(h + 1).astype(F32))
    slope2 = pow2 * LOG2E

    q = q_ref[...]
    lane = lax.broadcasted_iota(jnp.int32, (tq, LANES), 1)
    own = (lane < DA_QK_DIM, lane >= DA_QK_DIM)
    qfs = [jnp.where(own[mi], q, jnp.broadcast_to((qf_ref[mi:mi + 1, :] * pow2).astype(BF16), q.shape))
           for mi in range(2)]

    m_scrs, l_scrs, a_scrs, acc_scrs = (m0_scr, m1_scr), (l0_scr, l1_scr), (a0_scr, a1_scr), (acc0_scr, acc1_scr)
    for mi in range(2):
        m_scrs[mi][...] = jnp.full(m_scrs[mi].shape, NEG_BIG, F32)
        l_scrs[mi][...] = jnp.zeros(l_scrs[mi].shape, F32)
        acc_scrs[mi][...] = jnp.zeros(acc_scrs[mi].shape, F32)

    def scores(ki, mi, s_ref):
        k = k_ref[pl.ds(pl.multiple_of(ki * tq, tq), tq), :]
        s_ref[...] = _dot_nt(qfs[mi], jnp.where(own[mi], k, kf_ref[mi]))

    def softmax(ki, mi, s_ref, p_ref, masked):
        m_scr, l_scr, a_scr = m_scrs[mi], l_scrs[mi], a_scrs[mi]
        c = slope2 * ((ki - qi) * tq).astype(F32)
        for r in range(tq // rc):
            rows = slice(r * rc, (r + 1) * rc)
            nb = min(nlb, ((r + 1) * rc - 1) // LANES + 1) if masked else nlb
            sb = []
            for j in range(nb):
                cs = slice(j * LANES, (j + 1) * LANES)
                s = s_ref[rows, cs]
                if masked and (j + 1) * LANES - 1 > r * rc:
                    rr = lax.broadcasted_iota(jnp.int32, (rc, LANES), 0) + r * rc
                    cc = lax.broadcasted_iota(jnp.int32, (rc, LANES), 1) + j * LANES
                    s = jnp.where(cc <= rr, s, NEG_BIG)
                sb.append(s)
            mx = sb[0]
            for s in sb[1:]:
                mx = jnp.maximum(mx, s)
            m_prev = m_scr[rows, :]
            m_new = jnp.maximum(m_prev, jnp.max(mx, axis=-1, keepdims=True) + c)
            alpha = jnp.exp2(m_prev - m_new)
            a_scr[rows, :] = alpha
            m_scr[rows, :] = m_new
            mc = m_new - c
            psum = alpha * l_scr[rows, :]
            for j in range(nlb):
                cs = slice(j * LANES, (j + 1) * LANES)
                if j < nb:
                    p = jnp.exp2(sb[j] - mc)
                    psum = psum + p
                    p_ref[rows, cs] = p.astype(BF16)
                else:
                    p_ref[rows, cs] = jnp.zeros((rc, LANES), BF16)
            l_scr[rows, :] = psum

    def values(ki, mi, p_ref):
        v = v_ref[pl.ds(pl.multiple_of(ki * tq, tq), tq), :]
        acc_scrs[mi][...] = a_scrs[mi][...] * acc_scrs[mi][...] + _dot(p_ref[...], v)

    s_bufs, p_bufs = ((s00, s01), (s10, s11)), ((p00, p01), (p10, p11))

    def tile_group(k0, n, last_masked):
        for mi in range(2):
            scores(k0, mi, s_bufs[0][mi])
        for i in range(n):
            masked = last_masked and i == n - 1
            for mi in range(2):
                p_ref = (pd0, pd1)[mi] if masked else p_bufs[i % 2][mi]
                softmax(k0 + i, mi, s_bufs[i % 2][mi], p_ref, masked)
                values(k0 + i, mi, p_ref)
                if i + 1 < n:
                    scores(k0 + i + 1, mi, s_bufs[(i + 1) % 2][mi])

    def body(t, carry):
        tile_group(DA_TILE_GROUP * t, DA_TILE_GROUP, False)
        return carry

    n_full = qi // DA_TILE_GROUP
    lax.fori_loop(0, n_full, body, 0)
    for rem in range(1, DA_TILE_GROUP + 1):
        @pl.when(qi - DA_TILE_GROUP * n_full == rem - 1)
        def _(rem=rem):
            tile_group(qi - (rem - 1), rem, True)

    lam_e = jnp.exp(jnp.sum(lq_ref[...] * lk_ref[...], axis=-1, keepdims=True))
    lam = lam_e[0:1, :] - lam_e[1:2, :] + lam_init
    l0 = jnp.sum(l0_scr[...], axis=-1, keepdims=True)
    l1 = jnp.sum(l1_scr[...], axis=-1, keepdims=True)
    o = acc0_scr[...] / l0 - lam * (acc1_scr[...] / l1)
    ms = jnp.mean(o * o, axis=-1, keepdims=True)
    o = o * lax.rsqrt(ms + EPS) * sg_ref[...] * (1.0 - lam_init)
    o_ref[...] = o.astype(o_ref.dtype)


def _diff_attention(proj, lam_q, lam_k, sub_gain, B, S, lam_init, tq=DA_TILE, rc=DA_ROW_CHUNK):
    T = proj.shape[0]
    nq = S // tq
    lb = LANES
    q_blk0, k_blk0, v_blk0 = (CT_DA_Q * COL_TILE) // lb, (CT_DA_K * COL_TILE) // lb, (CT_DA_V * COL_TILE) // lb
    qfeat, kfeat = _alibi_features(tq)
    const = lambda shape: pl.BlockSpec(shape, lambda b, h, i: (0,) * len(shape))
    return pl.pallas_call(
        functools.partial(_da_kernel, tq=tq, rc=rc, lam_init=lam_init),
        grid=(B, DA_HEADS, nq),
        in_specs=[
            pl.BlockSpec((tq, lb), lambda b, h, i: (b * nq + i, q_blk0 + h)),
            pl.BlockSpec((S, lb), lambda b, h, i: (b, k_blk0 + h)),
            pl.BlockSpec((S, lb), lambda b, h, i: (b, v_blk0 + h)),
            const((2, LANES)), const((2, tq, LANES)),
            const((2, DA_QK_DIM)), const((2, DA_QK_DIM)), const((1, DA_V_DIM)),
        ],
        out_specs=pl.BlockSpec((tq, lb), lambda b, h, i: (b * nq + i, h)),
        out_shape=jax.ShapeDtypeStruct((T, DA_HEADS * DA_V_DIM), BF16),
        scratch_shapes=[pltpu.VMEM((tq, tq), F32)] * 4 + [pltpu.VMEM((tq, tq), BF16)] * 6
        + [pltpu.VMEM((tq, LANES), F32)] * 6 + [pltpu.VMEM((tq, DA_V_DIM), F32)] * 2,
        compiler_params=_params(("parallel", "parallel", "arbitrary")),
    )(proj, proj, proj, qfeat, kfeat, lam_q, lam_k, sub_gain)


def _dl_kernel(q_ref, kc_ref, kp_ref, vc_ref, vp_ref, o_ref, lse_ref, *, slopes2, d, tq, ru):
    n = pl.program_id(1)
    sp = DL_SPAN
    row = lax.broadcasted_iota(jnp.int32, (sp, sp), 0)
    col = lax.broadcasted_iota(jnp.int32, (sp, sp), 1)
    dcur = row - col
    cur_ok = dcur >= 0
    prev_ok = dcur <= 0
    dcur_f = dcur.astype(F32)

    def scores(r, hh, j):
        hs = slice(hh * LANES, (hh + 1) * LANES)
        rs = slice(j * sp, (j + 1) * sp)
        q = q_ref[r, rs, hs]
        if j == 0:
            kp, vp, p_ok = kp_ref[r, :, hs], vp_ref[r, :, hs], prev_ok & (n > 0)
        else:
            ps = slice((j - 1) * sp, j * sp)
            kp, vp, p_ok = kc_ref[r, ps, hs], vc_ref[r, ps, hs], prev_ok
        s_c = jnp.where(cur_ok, _dot_nt(q, kc_ref[r, rs, hs]) - slopes2[hh] * dcur_f, NEG_BIG)
        s_p = jnp.where(p_ok, _dot_nt(q, kp) - slopes2[hh] * (dcur_f + float(sp)), NEG_BIG)
        return s_c, s_p, vc_ref[r, rs, hs], vp

    def finish(r, hh, j, s_c, s_p, vc, vp):
        m = jnp.max(jnp.maximum(s_c, s_p), axis=-1, keepdims=True)
        p_c = jnp.exp2(s_c - m)
        p_p = jnp.exp2(s_p - m)
        den = jnp.sum(p_c + p_p, axis=-1, keepdims=True)
        acc = _dot(p_c.astype(BF16), vc) + _dot(p_p.astype(BF16), vp)
        out_rows = pl.ds(j * sp, sp) if d == 1 else pl.ds(r + j * sp * d, sp, stride=d)
        o_ref[hh, out_rows, :] = acc / den
        lse_ref[hh, out_rows, :] = jnp.broadcast_to(m + jnp.log2(den), (sp, LANES))

    def residues(t, carry):
        units = [(t * ru + rr, hh, j) for rr in range(ru) for hh in range(DL_HEADS_PER_GROUP)
                 for j in range(tq // sp)]
        pending = []
        for u in units:
            pending.append((u, scores(*u)))
            if len(pending) > DL_LOOKAHEAD:
                u0, vals = pending.pop(0)
                finish(*u0, *vals)
        for u0, vals in pending:
            finish(*u0, *vals)
        return carry

    lax.fori_loop(0, d // ru, residues, 0)


def _dilated_group(src, col0, g, B, S, tok_per_step=DL_TOKENS_PER_STEP):
    window, d = DL_GROUPS[g]
    assert window // d == DL_SPAN
    L = S // d
    tq = min(tok_per_step, S) // d
    assert tq % DL_SPAN == 0 and L % tq == 0
    nh = DL_HEADS_PER_GROUP * len(DL_GROUPS)
    slopes2 = tuple(2.0 ** (-8.0 * (g * DL_HEADS_PER_GROUP + hh + 1) / nh) * d * LOG2E
                    for hh in range(DL_HEADS_PER_GROUP))
    spb = tq // DL_SPAN
    nsteps = L // tq
    cur = lambda c: pl.BlockSpec((d, tq, COL_TILE), lambda b, n: (b, n, c))
    prev = lambda c: pl.BlockSpec((d, DL_SPAN, COL_TILE), lambda b, n: (b, jnp.maximum(n * spb - 1, 0), c))
    out_spec = pl.BlockSpec((DL_HEADS_PER_GROUP, d * tq, LANES), lambda b, n: (0, b * nsteps + n, 0))
    return pl.pallas_call(
        functools.partial(_dl_kernel, slopes2=slopes2, d=d, tq=tq, ru=min(d, DL_RESIDUES_PER_TRIP)),
        grid=(B, nsteps),
        in_specs=[cur(col0), cur(col0 + 1), prev(col0 + 1), cur(col0 + 2), prev(col0 + 2)],
        out_specs=[out_spec, out_spec],
        out_shape=[jax.ShapeDtypeStruct((DL_HEADS_PER_GROUP, B * S, LANES), F32)] * 2,
        compiler_params=_params(("parallel", "arbitrary")),
    )(src, src, src, src, src)


def _route(logits):
    lane = lax.broadcasted_iota(jnp.int32, logits.shape, 1)
    big = jnp.int32(1 << 20)
    is_g = (lane >= MOE_N_EXPERTS) & (lane < MOE_N_EXPERTS + MOE_GROUPS)
    lg = jnp.where(is_g, logits, -jnp.inf)
    gmax = jnp.max(lg, axis=-1, keepdims=True)
    gsum = jnp.sum(jnp.exp(lg - gmax), axis=-1, keepdims=True)
    g_w = 1.0 / gsum
    g_idx = jnp.min(jnp.where(lg == gmax, lane - MOE_N_EXPERTS, big), axis=-1, keepdims=True)
    in_grp = (lane < MOE_N_EXPERTS) & ((lane // MOE_EXPERTS_PER_GROUP) == g_idx)
    le = jnp.where(in_grp, logits, -jnp.inf)
    t1 = jnp.max(le, axis=-1, keepdims=True)
    e1 = jnp.min(jnp.where(le == t1, lane, big), axis=-1, keepdims=True)
    le2 = jnp.where(lane == e1, -jnp.inf, le)
    t2 = jnp.max(le2, axis=-1, keepdims=True)
    e2 = jnp.min(jnp.where(le2 == t2, lane, big), axis=-1, keepdims=True)
    r = jnp.exp(t2 - t1)
    w1 = g_w / (1.0 + r)
    w2 = w1 * r
    out = jnp.where(lane == 0, e1.astype(F32),
                    jnp.where(lane == 1, e2.astype(F32),
                              jnp.where(lane == 2, w1, jnp.where(lane == 3, w2, 0.0))))
    return out


def _outproj_kernel(x_ref, oa_ref, ga_ref, gb_ref, o0_ref, o1_ref, o2_ref, l0_ref, l1_ref, l2_ref,
                    wa_ref, wb_ref, wo_ref, gf_ref, rh_ref, rc_ref,
                    x1_ref, hn_ref, rt_ref):
    obs = []
    for hh in range(DL_HEADS_PER_GROUP):
        l0, l1, l2 = l0_ref[hh], l1_ref[hh], l2_ref[hh]
        lm = jnp.maximum(jnp.maximum(l0, l1), l2)
        e0, e1, e2 = jnp.exp2(l0 - lm), jnp.exp2(l1 - lm), jnp.exp2(l2 - lm)
        obs.append((e0 * o0_ref[hh] + e1 * o1_ref[hh] + e2 * o2_ref[hh]) / (e0 + e1 + e2))
    ob = jnp.concatenate(obs, axis=1)
    a = _dot(oa_ref[...], wa_ref[...])
    b = _dot(ob.astype(BF16), wb_ref[...])
    mixed = jax.nn.sigmoid(ga_ref[...].astype(F32)) * a + jax.nn.sigmoid(gb_ref[...].astype(F32)) * b
    x1 = x_ref[...] + _dot(mixed.astype(BF16), wo_ref[...])
    x1_ref[...] = x1
    ms = jnp.mean(x1 * x1, axis=-1, keepdims=True)
    hn = x1 * lax.rsqrt(ms + EPS) * gf_ref[...]
    hn_ref[...] = hn
    hn_hi = hn.astype(BF16)
    hn_lo = (hn - hn_hi.astype(F32)).astype(BF16)
    t = _dot(hn_hi, rc_ref[...])
    logits = t[:, 0:LANES] + (_dot(hn_lo, rh_ref[...]) + t[:, LANES:2 * LANES])
    rt_ref[...] = _route(logits)


def _outproj(x2, o_a, proj, dl_o, dl_lse, wa, wb, wo, gain_ffn, r_hi, r_cat, tm=OUTPROJ_ROWS):
    T, D = x2.shape
    row = lambda w: pl.BlockSpec((tm, w), lambda i: (i, 0))
    full = lambda s: pl.BlockSpec(s, lambda i: (0, 0), pipeline_mode=pl.Buffered(1))
    hrow = pl.BlockSpec((DL_HEADS_PER_GROUP, tm, LANES), lambda i: (0, i, 0))
    return pl.pallas_call(
        _outproj_kernel,
        grid=(T // tm,),
        in_specs=[
            row(D), row(o_a.shape[1]),
            pl.BlockSpec((tm, D), lambda i: (i, (CT_GATE_A * COL_TILE) // D)),
            pl.BlockSpec((tm, D), lambda i: (i, (CT_GATE_B * COL_TILE) // D)),
            hrow, hrow, hrow, hrow, hrow, hrow,
            full(wa.shape), full(wb.shape), full(wo.shape), full((1, D)), full(r_hi.shape), full(r_cat.shape),
        ],
        out_specs=[row(D), row(D), row(LANES)],
        out_shape=[jax.ShapeDtypeStruct((T, D), F32), jax.ShapeDtypeStruct((T, D), F32),
                   jax.ShapeDtypeStruct((T, LANES), F32)],
        compiler_params=_params(("parallel",)),
    )(x2, o_a, proj, proj, dl_o[0], dl_o[1], dl_o[2], dl_lse[0], dl_lse[1], dl_lse[2],
      wa, wb, wo, gain_ffn, r_hi, r_cat)


def _cast_rows(src_ref, dst_ref, chunk=CAST_ROWS):
    def body(c, carry):
        r0 = pl.multiple_of(c * chunk, chunk)
        dst_ref[pl.ds(r0, chunk), :] = src_ref[pl.ds(r0, chunk), :].astype(dst_ref.dtype)
        return carry
    lax.fori_loop(0, src_ref.shape[0] // chunk, body, 0)


def _expert_changed(be_ref, i):
    return (i == 0) | (be_ref[i] != be_ref[jnp.maximum(i - 1, 0)])


def _stream_expert_weights(i, live, be_ref, ne_ref, w_hbm, stage, wbf, wsem):
    changed = live & _expert_changed(be_ref, i)

    def copy(e):
        return pltpu.make_async_copy(w_hbm.at[e], stage, wsem)

    @pl.when(changed & (i == 0))
    def _():
        copy(be_ref[0]).start(priority=WEIGHT_DMA_PRIORITY)

    @pl.when(changed)
    def _():
        copy(be_ref[i]).wait()
        _cast_rows(stage, wbf)

    @pl.when(changed & (ne_ref[i] >= 0))
    def _():
        copy(ne_ref[i]).start(priority=WEIGHT_DMA_PRIORITY)


def _moe_up_kernel(be_ref, ne_ref, nu_ref, tc_ref, tn_ref, hn_ref, wgu_ref, act_ref,
                   xa, xb, wstage, wbf, gsem, wsem, *, d_ff):
    i = pl.program_id(0)
    nu = nu_ref[0]
    rows = xa.shape[0]
    even = i % 2 == 0

    def gather(tok_ref, r, buf, s, pace=0):
        tok = jnp.minimum(tok_ref[0, r] + pace, hn_ref.shape[0] - 1)
        return pltpu.make_async_copy(hn_ref.at[tok], buf.at[r], gsem.at[s])

    def wait_gather(buf, s):
        pltpu.make_async_copy(hn_ref.at[pl.ds(0, rows)], buf, gsem.at[s]).wait()

    @pl.when((i == 0) & (nu > 0))
    def _():
        def body(r, c):
            gather(tc_ref, r, xa, 0).start()
            return c
        lax.fori_loop(0, rows, body, 0)

    _stream_expert_weights(i, i < nu, be_ref, ne_ref, wgu_ref, wstage, wbf, wsem)

    def live_step(cur, nxt, s):
        wait_gather(cur, s)
        x = cur[...].astype(BF16)
        nchunk = d_ff // UP_COL_CHUNK
        per = rows // nchunk
        pace = 0
        for c in range(nchunk):
            for r in range(c * per, (c + 1) * per):
                gather(tn_ref, r, nxt, 1 - s, pace).start()
            cs = slice(c * UP_COL_CHUNK, (c + 1) * UP_COL_CHUNK)
            gate = _dot(x, wbf[:, cs])
            up = _dot(x, wbf[:, d_ff + c * UP_COL_CHUNK:d_ff + (c + 1) * UP_COL_CHUNK])
            act_ref[:, cs] = (gate * jax.nn.sigmoid(gate) * up).astype(act_ref.dtype)
            if c + 1 < nchunk:
                probe = act_ref[0:16, c * UP_COL_CHUNK:c * UP_COL_CHUNK + LANES].astype(F32)[0, 0]
                pace = jnp.where(probe == probe, 0, 1).astype(jnp.int32)

    @pl.when((i < nu) & even)
    def _():
        live_step(xa, xb, 0)

    @pl.when((i < nu) & jnp.logical_not(even))
    def _():
        live_step(xb, xa, 1)

    @pl.when((i == nu) & (nu > 0) & even)
    def _():
        wait_gather(xa, 0)

    @pl.when((i == nu) & (nu > 0) & jnp.logical_not(even))
    def _():
        wait_gather(xb, 1)

    @pl.when(i >= nu)
    def _():
        act_ref[...] = jnp.zeros(act_ref.shape, act_ref.dtype)


def _moe_down_kernel(be_ref, ne_ref, nu_ref, dp_ref, act_ref, wd_ref, y_ref, ya, yb, wstage, wbf, ssem, wsem):
    i = pl.program_id(0)
    nu = nu_ref[0]
    rows = ya.shape[0]
    even = i % 2 == 0

    def scatter(r, buf, s):
        return pltpu.make_async_copy(buf.at[r], y_ref.at[dp_ref[0, r]], ssem.at[s])

    def wait_scatter(buf, s):
        pltpu.make_async_copy(buf, y_ref.at[pl.ds(0, rows)], ssem.at[s]).wait()

    @pl.when(i == 0)
    def _():
        n_res = y_ref.shape[0] - 2 * rows
        for s, buf in enumerate((ya, yb)):
            buf[...] = jnp.zeros(buf.shape, buf.dtype)
            pltpu.make_async_copy(buf, y_ref.at[pl.ds(n_res + s * rows, rows)], ssem.at[s]).start()
        for s, buf in enumerate((ya, yb)):
            pltpu.make_async_copy(buf, y_ref.at[pl.ds(n_res + s * rows, rows)], ssem.at[s]).wait()

    @pl.when((i >= 2) & (i < nu + 2) & even)
    def _():
        wait_scatter(ya, 0)

    @pl.when((i >= 2) & (i < nu + 2) & jnp.logical_not(even))
    def _():
        wait_scatter(yb, 1)

    _stream_expert_weights(i, i < nu, be_ref, ne_ref, wd_ref, wstage, wbf, wsem)

    def step(cur, prv, s, do_scatter, do_compute):
        if do_scatter:
            for r in range(rows):
                scatter(r, prv, 1 - s).start()
        if do_compute:
            cur[...] = _dot(act_ref[...], wbf[...])

    for s, (cur, prv) in enumerate(((ya, yb), (yb, ya))):
        par = even if s == 0 else jnp.logical_not(even)

        @pl.when((i >= 1) & (i < nu) & par)
        def _(cur=cur, prv=prv, s=s):
            step(cur, prv, s, True, True)

        @pl.when((i == 0) & (nu > 0) & par)
        def _(cur=cur, prv=prv, s=s):
            step(cur, prv, s, False, True)

        @pl.when((i == nu) & (nu > 0) & par)
        def _(cur=cur, prv=prv, s=s):
            step(cur, prv, s, True, False)


def _experts(hn, src_tok, dst_row, blk_expert, nxt_expert, n_used, wgu, wd, n_out_rows):
    T, D = hn.shape
    nblk = src_tok.shape[0] // MOE_BLOCK
    d_ff = wd.shape[1]
    tok = src_tok.reshape(nblk, 1, MOE_BLOCK)
    dst = dst_row.reshape(nblk, 1, MOE_BLOCK)

    def live(i, nu):
        return jnp.maximum(jnp.minimum(i, nu[0] - 1), 0)

    smem = lambda f: pl.BlockSpec((None, 1, MOE_BLOCK), f, memory_space=pltpu.SMEM)
    act = pl.pallas_call(
        functools.partial(_moe_up_kernel, d_ff=d_ff),
        grid_spec=pltpu.PrefetchScalarGridSpec(
            num_scalar_prefetch=3,
            grid=(nblk - 1,),
            in_specs=[
                smem(lambda i, be, ne, nu: (i, 0, 0)),
                smem(lambda i, be, ne, nu: (i + 1, 0, 0)),
                pl.BlockSpec(memory_space=pl.ANY),
                pl.BlockSpec(memory_space=pl.ANY),
            ],
            out_specs=pl.BlockSpec((MOE_BLOCK, d_ff), lambda i, be, ne, nu: (i, 0)),
            scratch_shapes=[pltpu.VMEM((MOE_BLOCK, D), F32), pltpu.VMEM((MOE_BLOCK, D), F32),
                            pltpu.VMEM((D, 2 * d_ff), F32), pltpu.VMEM((D, 2 * d_ff), BF16),
                            pltpu.SemaphoreType.DMA((2,)), pltpu.SemaphoreType.DMA(())],
        ),
        out_shape=jax.ShapeDtypeStruct(((nblk - 1) * MOE_BLOCK, d_ff), BF16),
        compiler_params=_params(("arbitrary",)),
    )(blk_expert, nxt_expert, n_used, tok, tok, hn, wgu)
    return pl.pallas_call(
        _moe_down_kernel,
        grid_spec=pltpu.PrefetchScalarGridSpec(
            num_scalar_prefetch=3,
            grid=(nblk,),
            in_specs=[
                smem(lambda i, be, ne, nu: (jnp.maximum(i - 1, 0), 0, 0)),
                pl.BlockSpec((MOE_BLOCK, d_ff), lambda i, be, ne, nu: (live(i, nu), 0)),
                pl.BlockSpec(memory_space=pl.ANY),
            ],
            out_specs=pl.BlockSpec(memory_space=pl.ANY),
            scratch_shapes=[pltpu.VMEM((MOE_BLOCK, D), F32), pltpu.VMEM((MOE_BLOCK, D), F32),
                            pltpu.VMEM((d_ff, D), F32), pltpu.VMEM((d_ff, D), BF16),
                            pltpu.SemaphoreType.DMA((2,)), pltpu.SemaphoreType.DMA(())],
        ),
        out_shape=jax.ShapeDtypeStruct((n_out_rows, D), F32),
        compiler_params=_params(("arbitrary",)),
    )(blk_expert, nxt_expert, n_used, dst, act, wd)


def _combine_kernel(x1_ref, rt_ref, y1_ref, y2_ref, o_ref):
    rt = rt_ref[...]
    o_ref[...] = x1_ref[...] + (rt[:, 2:3] * y1_ref[...] + rt[:, 3:4] * y2_ref[...])


def _combine(x1, route, y, tm=COMBINE_ROWS):
    T, D = x1.shape
    nt = T // tm
    return pl.pallas_call(
        _combine_kernel,
        grid=(nt,),
        in_specs=[
            pl.BlockSpec((tm, D), lambda i: (i, 0)),
            pl.BlockSpec((tm, LANES), lambda i: (i, 0)),
            pl.BlockSpec((tm, D), lambda i: (i, 0)),
            pl.BlockSpec((tm, D), lambda i: (nt + i, 0)),
        ],
        out_specs=pl.BlockSpec((tm, D), lambda i: (i, 0)),
        out_shape=jax.ShapeDtypeStruct((T, D), F32),
        compiler_params=_params(("parallel",)),
    )(x1, route, y, y)


def _dispatch_tables(route, T):
    TK = T * MOE_TOP_K
    flat_e = route[:, :MOE_TOP_K].astype(jnp.int32).reshape(-1)
    order = jnp.argsort(flat_e).astype(jnp.int32)
    sizes = jnp.sum(flat_e[:, None] == jnp.arange(MOE_N_EXPERTS, dtype=jnp.int32)[None, :], axis=0,
                    dtype=jnp.int32)
    start = jnp.cumsum(sizes) - sizes
    padded = ((sizes + MOE_BLOCK - 1) // MOE_BLOCK) * MOE_BLOCK
    pad_end = jnp.cumsum(padded)
    pad_start = pad_end - padded
    n_blocks = TK // MOE_BLOCK + MOE_N_EXPERTS
    blk = jnp.arange(n_blocks + 2, dtype=jnp.int32)
    blk_expert = jnp.minimum(jnp.sum(pad_end[None, :] <= (blk * MOE_BLOCK)[:, None], axis=1, dtype=jnp.int32),
                             MOE_N_EXPERTS - 1)
    r = jnp.arange(MOE_BLOCK, dtype=jnp.int32)[None, :]
    pos = blk[:, None] * MOE_BLOCK + r
    off = pos - pad_start[blk_expert][:, None]
    live = (off < sizes[blk_expert][:, None]) & (pos < pad_end[-1])
    sorted_idx = jnp.where(live, off + start[blk_expert][:, None], 0)
    slot = order[sorted_idx]
    tok = slot // MOE_TOP_K
    src_tok = jnp.where(live, tok, 0).reshape(-1)
    trash = TK + (blk[:, None] % 2) * MOE_BLOCK + r
    dst_row = jnp.where(live, (slot % MOE_TOP_K) * T + tok, trash).reshape(-1)
    n_used = (pad_end[-1:] // MOE_BLOCK).astype(jnp.int32)
    eid = jnp.arange(MOE_N_EXPERTS, dtype=jnp.int32)
    later = (eid[None, :] > eid[:, None]) & (sizes[None, :] > 0)
    nxt = jnp.min(jnp.where(later, eid[None, :], MOE_N_EXPERTS), axis=1)
    nxt_expert = jnp.where(nxt < MOE_N_EXPERTS, nxt, -1).astype(jnp.int32)[blk_expert]
    return src_tok, dst_row, blk_expert, nxt_expert, n_used, TK + 2 * MOE_BLOCK


def _w_in_tile_order(d_gate):
    da = DA_HEADS * 2 * DA_QK_DIM
    dav = DA_HEADS * DA_V_DIM
    dl = len(DL_GROUPS) * DL_HEADS_PER_GROUP * DL_HEAD_DIM
    o = [int(v) // COL_TILE for v in np.cumsum([0, da, da, dav, dl, dl, dl, d_gate, d_gate])]
    order = list(range(o[6], o[8])) + list(range(o[0], o[3]))
    for g in range(len(DL_GROUPS)):
        order += [o[3] + g, o[4] + g, o[5] + g]
    assert sorted(order) == list(range(CT_END))
    return jnp.asarray(order, jnp.int32)


def _gain_table(da_q_norm, da_k_norm, dl_q_norm, dl_k_norm):
    ones = jnp.ones((COL_TILE,), F32)
    daq = jnp.tile(da_q_norm, COL_TILE // DA_QK_DIM) * (DA_QK_DIM ** -0.5 * LOG2E)
    dak = jnp.tile(da_k_norm, COL_TILE // DA_QK_DIM)
    dlq = jnp.tile(dl_q_norm, COL_TILE // DL_HEAD_DIM) * (DL_HEAD_DIM ** -0.5 * LOG2E)
    dlk = jnp.tile(dl_k_norm, COL_TILE // DL_HEAD_DIM)
    rows = []
    for j in range(CT_END):
        if CT_DA_Q <= j < CT_DA_K:
            rows.append(daq)
        elif CT_DA_K <= j < CT_DA_V:
            rows.append(dak)
        elif j >= CT_DL and (j - CT_DL) % 3 == 0:
            rows.append(dlq)
        elif j >= CT_DL and (j - CT_DL) % 3 == 1:
            rows.append(dlk)
        else:
            rows.append(ones)
    return jnp.stack(rows, axis=0).reshape(CT_END, 1, COL_TILE)


def kernel(x, norm_mix, w_in, da_q_norm, da_k_norm, da_lambda_q, da_lambda_k, da_sub_norm,
           dl_q_norm, dl_k_norm, w_branch_a, w_branch_b, w_out, norm_ffn,
           w_group_router, w_expert_router, w_gate_up, w_down):
    B, S, D = x.shape
    T = B * S
    depth = w_in.shape[0]
    x2 = x.reshape(T, D)
    for l in range(depth):
        lam_init = 0.8 - 0.6 * math.exp(-0.3 * l)
        gain_tab = _gain_table(da_q_norm[l], da_k_norm[l], dl_q_norm[l], dl_k_norm[l])
        proj, dl1, dl2 = _inproj(x2, norm_mix[l].reshape(1, D), w_in[l].astype(BF16), _w_in_tile_order(D),
                                 gain_tab, B, S)

        o_a = _diff_attention(proj, da_lambda_q[l], da_lambda_k[l], da_sub_norm[l].reshape(1, DA_V_DIM),
                              B, S, lam_init)
        dl = [_dilated_group(proj.reshape(B, S, proj.shape[1]), CT_DL, 0, B, S),
              _dilated_group(dl1, 0, 1, B, S), _dilated_group(dl2, 0, 2, B, S)]

        w_r = jnp.concatenate([w_expert_router[l], w_group_router[l]], axis=1)
        w_r = jnp.pad(w_r, ((0, 0), (0, LANES - w_r.shape[1])))
        r_hi = w_r.astype(BF16)
        r_lo = (w_r - r_hi.astype(F32)).astype(BF16)
        x1, hn, route = _outproj(
            x2, o_a, proj, [t[0] for t in dl], [t[1] for t in dl],
            w_branch_a[l].astype(BF16), w_branch_b[l].astype(BF16), w_out[l].astype(BF16),
            norm_ffn[l].reshape(1, D), r_hi, jnp.concatenate([r_hi, r_lo], axis=1))

        src_tok, dst_row, blk_expert, nxt_expert, n_used, n_rows = _dispatch_tables(route, T)
        y = _experts(hn, src_tok, dst_row, blk_expert, nxt_expert, n_used, w_gate_up[l], w_down[l], n_rows)
        x2 = _combine(x1, route, y)
    return x2.reshape(B, S, D)
```

```python
import functools
import math

import jax
import jax.numpy as jnp
import numpy as np
from jax import lax
from jax.experimental import pallas as pl
from jax.experimental.pallas import tpu as pltpu

F32 = jnp.float32
BF16 = jnp.bfloat16

EPS = 1e-6
LOG2E = 1.4426950408889634
NEG_BIG = -1e30

DA_HEADS = 8
DA_QK_DIM = 64
DA_V_DIM = 128
DA_TILE_GROUP = 4
DL_GROUPS = ((128, 1), (512, 4), (2048, 16))
DL_HEADS_PER_GROUP = 4
DL_HEAD_DIM = 128
DL_SPAN = 128
MOE_GROUPS = 4
MOE_EXPERTS_PER_GROUP = 8
MOE_N_EXPERTS = 32
MOE_TOP_K = 2
MOE_BLOCK = 256
WEIGHT_DMA_PRIORITY = 1

LANES = 128
COL_TILE = 512
VMEM_LIMIT = 56 * 1024 * 1024

INPROJ_ROWS = 1024
DA_TILE = 512
DA_ROW_CHUNK = 32
DL_TOKENS_PER_STEP = 2048
DL_RESIDUES_PER_TRIP = 4
DL_LOOKAHEAD = 4
OUTPROJ_ROWS = 256
COMBINE_ROWS = 512
CAST_ROWS = 256

CT_GATE_A, CT_GATE_B, CT_DA_Q, CT_DA_K, CT_DA_V, CT_DL, CT_MAIN_END, CT_END = 0, 4, 8, 10, 12, 14, 17, 23


def _params(sem, vmem=VMEM_LIMIT):
    return pltpu.CompilerParams(dimension_semantics=sem, vmem_limit_bytes=vmem)


def _dot(a, b):
    return jnp.dot(a, b, preferred_element_type=F32)


def _dot_nt(a, b):
    return lax.dot_general(a, b, (((1,), (1,)), ((), ())), preferred_element_type=F32)


def _inproj_kernel(perm_ref, x_ref, g_ref, w_ref, gain_ref, o_ref, d1_ref, d2_ref, h_scr, y_scr):
    j = pl.program_id(1)

    @pl.when(j == 0)
    def _():
        x = x_ref[...]
        ms = jnp.mean(x * x, axis=-1, keepdims=True)
        h_scr[...] = (x * lax.rsqrt(ms + EPS) * g_ref[...]).astype(BF16)

    gain = gain_ref[...]
    half = COL_TILE // 2

    def head_slices():
        ys = [_dot(h_scr[...], w_ref[:, hf * half:(hf + 1) * half]) for hf in range(2)]
        for hf in range(2):
            for hh in range(half // LANES):
                yield hf * (half // LANES) + hh, ys[hf][:, hh * LANES:(hh + 1) * LANES]

    is64 = (j >= CT_DA_Q) & (j < CT_DA_V)
    is128 = (j >= CT_DL) & (lax.rem(j - CT_DL, 3) < 2)
    main = j < CT_MAIN_END
    plain = jnp.logical_not(is64 | is128)

    def norm64(h, yh):
        sq = yh * yh
        lo = lax.broadcasted_iota(jnp.int32, yh.shape, 1) < DA_QK_DIM
        s_lo = jnp.sum(jnp.where(lo, sq, 0.0), axis=-1, keepdims=True)
        s_hi = jnp.sum(jnp.where(lo, 0.0, sq), axis=-1, keepdims=True)
        r = jnp.where(lo, lax.rsqrt(s_lo * (1.0 / DA_QK_DIM) + EPS), lax.rsqrt(s_hi * (1.0 / DA_QK_DIM) + EPS))
        return yh * r * gain[:, h * LANES:(h + 1) * LANES]

    def norm128(h, yh):
        ss = jnp.sum(yh * yh, axis=-1, keepdims=True)
        return yh * lax.rsqrt(ss * (1.0 / DL_HEAD_DIM) + EPS) * gain[:, h * LANES:(h + 1) * LANES]

    def emit(cond, fn, to_main):
        @pl.when(cond)
        def _():
            for h, yh in head_slices():
                if to_main:
                    o_ref[:, h * LANES:(h + 1) * LANES] = fn(h, yh).astype(o_ref.dtype)
                else:
                    y_scr[h] = fn(h, yh)

    emit(is64, norm64, True)
    emit(is128 & main, norm128, True)
    emit(is128 & jnp.logical_not(main), norm128, False)
    emit(plain & main, lambda h, yh: yh, True)
    emit(plain & jnp.logical_not(main), lambda h, yh: yh, False)

    def deinterleave(dst_ref):
        d, rows = dst_ref.shape[0], dst_ref.shape[1]
        for r in range(d):
            for h in range(COL_TILE // LANES):
                dst_ref[r, :, h * LANES:(h + 1) * LANES] = (
                    y_scr[h, pl.ds(r, rows, stride=d), :].astype(dst_ref.dtype))

    @pl.when((j >= CT_MAIN_END) & (j < CT_MAIN_END + 3))
    def _():
        deinterleave(d1_ref)

    @pl.when(j >= CT_MAIN_END + 3)
    def _():
        deinterleave(d2_ref)


def _inproj(x2, gain_mix, w_bf, tile_perm, gain_tab, B, S, tm=INPROJ_ROWS):
    T, D = x2.shape
    tiles_per_batch = S // tm
    d1, d2 = DL_GROUPS[1][1], DL_GROUPS[2][1]
    part1 = lambda j: jnp.clip(j - CT_MAIN_END, 0, 2)
    part2 = lambda j: jnp.clip(j - CT_MAIN_END - 3, 0, 2)
    grid_spec = pltpu.PrefetchScalarGridSpec(
        num_scalar_prefetch=1,
        grid=(T // tm, CT_END),
        in_specs=[
            pl.BlockSpec((tm, D), lambda i, j, perm: (i, 0)),
            pl.BlockSpec((1, D), lambda i, j, perm: (0, 0)),
            pl.BlockSpec((D, COL_TILE), lambda i, j, perm: (0, perm[j])),
            pl.BlockSpec((None, 1, COL_TILE), lambda i, j, perm: (j, 0, 0)),
        ],
        out_specs=[
            pl.BlockSpec((tm, COL_TILE), lambda i, j, perm: (i, jnp.minimum(j, CT_MAIN_END - 1))),
            pl.BlockSpec((d1, tm // d1, COL_TILE),
                         lambda i, j, perm: (i // tiles_per_batch, i % tiles_per_batch, part1(j))),
            pl.BlockSpec((d2, tm // d2, COL_TILE),
                         lambda i, j, perm: (i // tiles_per_batch, i % tiles_per_batch, part2(j))),
        ],
        scratch_shapes=[pltpu.VMEM((tm, D), BF16), pltpu.VMEM((COL_TILE // LANES, tm, LANES), F32)],
    )
    return pl.pallas_call(
        _inproj_kernel,
        grid_spec=grid_spec,
        out_shape=[
            jax.ShapeDtypeStruct((T, CT_MAIN_END * COL_TILE), BF16),
            jax.ShapeDtypeStruct((B * d1, S // d1, 3 * COL_TILE), BF16),
            jax.ShapeDtypeStruct((B * d2, S // d2, 3 * COL_TILE), BF16),
        ],
        compiler_params=_params(("parallel", "arbitrary")),
    )(tile_perm, x2, gain_mix, w_bf, gain_tab)


def _bf16_pieces(x, n=3):
    out = []
    r = np.float64(x)
    for _ in range(n):
        p = np.asarray(np.float32(r)).astype(jnp.bfloat16).astype(np.float64)
        out.append(float(p))
        r = r - p
    return out


def _alibi_features(tk):
    pieces = _bf16_pieces(LOG2E)
    qf = np.zeros((2, LANES), np.float32)
    kf = np.zeros((2, tk, LANES), np.float32)
    j = np.arange(tk)
    hi, lo = (j // 16) * 16, j % 16
    for m in range(2):
        f0 = DA_QK_DIM if m == 0 else 0
        for n, p in enumerate(pieces):
            qf[m, f0 + 2 * n] = p
            qf[m, f0 + 2 * n + 1] = p
            kf[m, :, f0 + 2 * n] = hi
            kf[m, :, f0 + 2 * n + 1] = lo
    return jnp.asarray(qf), jnp.asarray(kf, dtype=BF16)


def _da_kernel(q_ref, k_ref, v_ref, qf_ref, kf_ref, lq_ref, lk_ref, sg_ref, o_ref,
               s00, s01, s10, s11, p00, p01, p10, p11, pd0, pd1,
               m0_scr, m1_scr, l0_scr, l1_scr, a0_scr, a1_scr, acc0_scr, acc1_scr, *, tq, rc, lam_init):
    h = pl.program_id(1)
    qi = pl.program_id(2)
    nlb = tq // LANES
    pow2 = jnp.exp2(-(h + 1).astype(F32))
    slope2 = pow2 * LOG2E

    q = q_ref[...]
    lane = lax.broadcasted_iota(jnp.int32, (tq, LANES), 1)
    own = (lane < DA_QK_DIM, lane >= DA_QK_DIM)
    qfs = [jnp.where(own[mi], q, jnp.broadcast_to((qf_ref[mi:mi + 1, :] * pow2).astype(BF16), q.shape))
           for mi in range(2)]

    m_scrs, l_scrs, a_scrs, acc_scrs = (m0_scr, m1_scr), (l0_scr, l1_scr), (a0_scr, a1_scr), (acc0_scr, acc1_scr)
    for mi in range(2):
        m_scrs[mi][...] = jnp.full(m_scrs[mi].shape, NEG_BIG, F32)
        l_scrs[mi][...] = jnp.zeros(l_scrs[mi].shape, F32)
        acc_scrs[mi][...] = jnp.zeros(acc_scrs[mi].shape, F32)

    def scores(ki, mi, s_ref):
        k = k_ref[pl.ds(pl.multiple_of(ki * tq, tq), tq), :]
        s_ref[...] = _dot_nt(qfs[mi], jnp.where(own[mi], k, kf_ref[mi]))

    def softmax(ki, mi, s_ref, p_ref, masked):
        m_scr, l_scr, a_scr = m_scrs[mi], l_scrs[mi], a_scrs[mi]
        c = slope2 * ((ki - qi) * tq).astype(F32)
        for r in range(tq // rc):
            rows = slice(r * rc, (r + 1) * rc)
            nb = min(nlb, ((r + 1) * rc - 1) // LANES + 1) if masked else nlb
            sb = []
            for j in range(nb):
                cs = slice(j * LANES, (j + 1) * LANES)
                s = s_ref[rows, cs]
                if masked and (j + 1) * LANES - 1 > r * rc:
                    rr = lax.broadcasted_iota(jnp.int32, (rc, LANES), 0) + r * rc
                    cc = lax.broadcasted_iota(jnp.int32, (rc, LANES), 1) + j * LANES
                    s = jnp.where(cc <= rr, s, NEG_BIG)
                sb.append(s)
            mx = sb[0]
            for s in sb[1:]:
                mx = jnp.maximum(mx, s)
            m_prev = m_scr[rows, :]
            m_new = jnp.maximum(m_prev, jnp.max(mx, axis=-1, keepdims=True) + c)
            alpha = jnp.exp2(m_prev - m_new)
            a_scr[rows, :] = alpha
            m_scr[rows, :] = m_new
            mc = m_new - c
            psum = alpha * l_scr[rows, :]
            for j in range(nlb):
                cs = slice(j * LANES, (j + 1) * LANES)
                if j < nb:
                    p = jnp.exp2(sb[j] - mc)
                    psum = psum + p
                    p_ref[rows, cs] = p.astype(BF16)
                else:
                    p_ref[rows, cs] = jnp.zeros((rc, LANES), BF16)
            l_scr[rows, :] = psum

    def values(ki, mi, p_ref):
        v = v_ref[pl.ds(pl.multiple_of(ki * tq, tq), tq), :]
        acc_scrs[mi][...] = a_scrs[mi][...] * acc_scrs[mi][...] + _dot(p_ref[...], v)

    s_bufs, p_bufs = ((s00, s01), (s10, s11)), ((p00, p01), (p10, p11))

    def tile_group(k0, n, last_masked):
        for mi in range(2):
            scores(k0, mi, s_bufs[0][mi])
        for i in range(n):
            masked = last_masked and i == n - 1
            for mi in range(2):
                p_ref = (pd0, pd1)[mi] if masked else p_bufs[i % 2][mi]
                softmax(k0 + i, mi, s_bufs[i % 2][mi], p_ref, masked)
                values(k0 + i, mi, p_ref)
                if i + 1 < n:
                    scores(k0 + i + 1, mi, s_bufs[(i + 1) % 2][mi])

    def body(t, carry):
        tile_group(DA_TILE_GROUP * t, DA_TILE_GROUP, False)
        return carry

    n_full = qi // DA_TILE_GROUP
    lax.fori_loop(0, n_full, body, 0)
    for rem in range(1, DA_TILE_GROUP + 1):
        @pl.when(qi - DA_TILE_GROUP * n_full == rem - 1)
        def _(rem=rem):
            tile_group(qi - (rem - 1), rem, True)

    lam_e = jnp.exp(jnp.sum(lq_ref[...] * lk_ref[...], axis=-1, keepdims=True))
    lam = lam_e[0:1, :] - lam_e[1:2, :] + lam_init
    l0 = jnp.sum(l0_scr[...], axis=-1, keepdims=True)
    l1 = jnp.sum(l1_scr[...], axis=-1, keepdims=True)
    o = acc0_scr[...] / l0 - lam * (acc1_scr[...] / l1)
    ms = jnp.mean(o * o, axis=-1, keepdims=True)
    o = o * lax.rsqrt(ms + EPS) * sg_ref[...] * (1.0 - lam_init)
    o_ref[...] = o.astype(o_ref.dtype)


def _diff_attention(proj, lam_q, lam_k, sub_gain, B, S, lam_init, tq=DA_TILE, rc=DA_ROW_CHUNK):
    T = proj.shape[0]
    nq = S // tq
    lb = LANES
    q_blk0, k_blk0, v_blk0 = (CT_DA_Q * COL_TILE) // lb, (CT_DA_K * COL_TILE) // lb, (CT_DA_V * COL_TILE) // lb
    qfeat, kfeat = _alibi_features(tq)
    const = lambda shape: pl.BlockSpec(shape, lambda b, h, i: (0,) * len(shape))
    return pl.pallas_call(
        functools.partial(_da_kernel, tq=tq, rc=rc, lam_init=lam_init),
        grid=(B, DA_HEADS, nq),
        in_specs=[
            pl.BlockSpec((tq, lb), lambda b, h, i: (b * nq + i, q_blk0 + h)),
            pl.BlockSpec((S, lb), lambda b, h, i: (b, k_blk0 + h)),
            pl.BlockSpec((S, lb), lambda b, h, i: (b, v_blk0 + h)),
            const((2, LANES)), const((2, tq, LANES)),
            const((2, DA_QK_DIM)), const((2, DA_QK_DIM)), const((1, DA_V_DIM)),
        ],
        out_specs=pl.BlockSpec((tq, lb), lambda b, h, i: (b * nq + i, h)),
        out_shape=jax.ShapeDtypeStruct((T, DA_HEADS * DA_V_DIM), BF16),
        scratch_shapes=[pltpu.VMEM((tq, tq), F32)] * 4 + [pltpu.VMEM((tq, tq), BF16)] * 6
        + [pltpu.VMEM((tq, LANES), F32)] * 6 + [pltpu.VMEM((tq, DA_V_DIM), F32)] * 2,
        compiler_params=_params(("parallel", "parallel", "arbitrary")),
    )(proj, proj, proj, qfeat, kfeat, lam_q, lam_k, sub_gain)


def _dl_kernel(q_ref, kc_ref, kp_ref, vc_ref, vp_ref, o_ref, lse_ref, *, slopes2, d, tq, ru):
    n = pl.program_id(1)
    sp = DL_SPAN
    row = lax.broadcasted_iota(jnp.int32, (sp, sp), 0)
    col = lax.broadcasted_iota(jnp.int32, (sp, sp), 1)
    dcur = row - col
    cur_ok = dcur >= 0
    prev_ok = dcur <= 0
    dcur_f = dcur.astype(F32)

    def scores(r, hh, j):
        hs = slice(hh * LANES, (hh + 1) * LANES)
        rs = slice(j * sp, (j + 1) * sp)
        q = q_ref[r, rs, hs]
        if j == 0:
            kp, vp, p_ok = kp_ref[r, :, hs], vp_ref[r, :, hs], prev_ok & (n > 0)
        else:
            ps = slice((j - 1) * sp, j * sp)
            kp, vp, p_ok = kc_ref[r, ps, hs], vc_ref[r, ps, hs], prev_ok
        s_c = jnp.where(cur_ok, _dot_nt(q, kc_ref[r, rs, hs]) - slopes2[hh] * dcur_f, NEG_BIG)
        s_p = jnp.where(p_ok, _dot_nt(q, kp) - slopes2[hh] * (dcur_f + float(sp)), NEG_BIG)
        return s_c, s_p, vc_ref[r, rs, hs], vp

    def finish(r, hh, j, s_c, s_p, vc, vp):
        m = jnp.max(jnp.maximum(s_c, s_p), axis=-1, keepdims=True)
        p_c = jnp.exp2(s_c - m)
        p_p = jnp.exp2(s_p - m)
        den = jnp.sum(p_c + p_p, axis=-1, keepdims=True)
        acc = _dot(p_c.astype(BF16), vc) + _dot(p_p.astype(BF16), vp)
        out_rows = pl.ds(j * sp, sp) if d == 1 else pl.ds(r + j * sp * d, sp, stride=d)
        o_ref[hh, out_rows, :] = acc / den
        lse_ref[hh, out_rows, :] = jnp.broadcast_to(m + jnp.log2(den), (sp, LANES))

    def residues(t, carry):
        units = [(t * ru + rr, hh, j) for rr in range(ru) for hh in range(DL_HEADS_PER_GROUP)
                 for j in range(tq // sp)]
        pending = []
        for u in units:
            pending.append((u, scores(*u)))
            if len(pending) > DL_LOOKAHEAD:
                u0, vals = pending.pop(0)
                finish(*u0, *vals)
        for u0, vals in pending:
            finish(*u0, *vals)
        return carry

    lax.fori_loop(0, d // ru, residues, 0)


def _dilated_group(src, col0, g, B, S, tok_per_step=DL_TOKENS_PER_STEP):
    window, d = DL_GROUPS[g]
    assert window // d == DL_SPAN
    L = S // d
    tq = min(tok_per_step, S) // d
    assert tq % DL_SPAN == 0 and L % tq == 0
    nh = DL_HEADS_PER_GROUP * len(DL_GROUPS)
    slopes2 = tuple(2.0 ** (-8.0 * (g * DL_HEADS_PER_GROUP + hh + 1) / nh) * d * LOG2E
                    for hh in range(DL_HEADS_PER_GROUP))
    spb = tq // DL_SPAN
    nsteps = L // tq
    cur = lambda c: pl.BlockSpec((d, tq, COL_TILE), lambda b, n: (b, n, c))
    prev = lambda c: pl.BlockSpec((d, DL_SPAN, COL_TILE), lambda b, n: (b, jnp.maximum(n * spb - 1, 0), c))
    out_spec = pl.BlockSpec((DL_HEADS_PER_GROUP, d * tq, LANES), lambda b, n: (0, b * nsteps + n, 0))
    return pl.pallas_call(
        functools.partial(_dl_kernel, slopes2=slopes2, d=d, tq=tq, ru=min(d, DL_RESIDUES_PER_TRIP)),
        grid=(B, nsteps),
        in_specs=[cur(col0), cur(col0 + 1), prev(col0 + 1), cur(col0 + 2), prev(col0 + 2)],
        out_specs=[out_spec, out_spec],
        out_shape=[jax.ShapeDtypeStruct((DL_HEADS_PER_GROUP, B * S, LANES), F32)] * 2,
        compiler_params=_params(("parallel", "arbitrary")),
    )(src, src, src, src, src)


def _route(logits):
    lane = lax.broadcasted_iota(jnp.int32, logits.shape, 1)
    big = jnp.int32(1 << 20)
    is_g = (lane >= MOE_N_EXPERTS) & (lane < MOE_N_EXPERTS + MOE_GROUPS)
    lg = jnp.where(is_g, logits, -jnp.inf)
    gmax = jnp.max(lg, axis=-1, keepdims=True)
    gsum = jnp.sum(jnp.exp(lg - gmax), axis=-1, keepdims=True)
    g_w = 1.0 / gsum
    g_idx = jnp.min(jnp.where(lg == gmax, lane - MOE_N_EXPERTS, big), axis=-1, keepdims=True)
    in_grp = (lane < MOE_N_EXPERTS) & ((lane // MOE_EXPERTS_PER_GROUP) == g_idx)
    le = jnp.where(in_grp, logits, -jnp.inf)
    t1 = jnp.max(le, axis=-1, keepdims=True)
    e1 = jnp.min(jnp.where(le == t1, lane, big), axis=-1, keepdims=True)
    le2 = jnp.where(lane == e1, -jnp.inf, le)
    t2 = jnp.max(le2, axis=-1, keepdims=True)
    e2 = jnp.min(jnp.where(le2 == t2, lane, big), axis=-1, keepdims=True)
    r = jnp.exp(t2 - t1)
    w1 = g_w / (1.0 + r)
    w2 = w1 * r
    out = jnp.where(lane == 0, e1.astype(F32),
                    jnp.where(lane == 1, e2.astype(F32),
                              jnp.where(lane == 2, w1, jnp.where(lane == 3, w2, 0.0))))
    return out


def _outproj_kernel(x_ref, oa_ref, ga_ref, gb_ref, o0_ref, o1_ref, o2_ref, l0_ref, l1_ref, l2_ref,
                    wa_ref, wb_ref, wo_ref, gf_ref, rh_ref, rc_ref,
                    x1_ref, hn_ref, rt_ref):
    obs = []
    for hh in range(DL_HEADS_PER_GROUP):
        l0, l1, l2 = l0_ref[hh], l1_ref[hh], l2_ref[hh]
        lm = jnp.maximum(jnp.maximum(l0, l1), l2)
        e0, e1, e2 = jnp.exp2(l0 - lm), jnp.exp2(l1 - lm), jnp.exp2(l2 - lm)
        obs.append((e0 * o0_ref[hh] + e1 * o1_ref[hh] + e2 * o2_ref[hh]) / (e0 + e1 + e2))
    ob = jnp.concatenate(obs, axis=1)
    a = _dot(oa_ref[...], wa_ref[...])
    b = _dot(ob.astype(BF16), wb_ref[...])
    mixed = jax.nn.sigmoid(ga_ref[...].astype(F32)) * a + jax.nn.sigmoid(gb_ref[...].astype(F32)) * b
    x1 = x_ref[...] + _dot(mixed.astype(BF16), wo_ref[...])
    x1_ref[...] = x1
    ms = jnp.mean(x1 * x1, axis=-1, keepdims=True)
    hn = x1 * lax.rsqrt(ms + EPS) * gf_ref[...]
    hn_hi = hn.astype(BF16)
    half = hn.shape[1] // 2
    bits = pltpu.bitcast(hn_hi.astype(F32), jnp.uint32)
    hn_ref[...] = (bits[:, :half] >> 16) | (bits[:, half:] & jnp.uint32(0xFFFF0000))
    hn_lo = (hn - hn_hi.astype(F32)).astype(BF16)
    t = _dot(hn_hi, rc_ref[...])
    logits = t[:, 0:LANES] + (_dot(hn_lo, rh_ref[...]) + t[:, LANES:2 * LANES])
    rt_ref[...] = _route(logits)


def _outproj(x2, o_a, proj, dl_o, dl_lse, wa, wb, wo, gain_ffn, r_hi, r_cat, tm=OUTPROJ_ROWS):
    T, D = x2.shape
    row = lambda w: pl.BlockSpec((tm, w), lambda i: (i, 0))
    full = lambda s: pl.BlockSpec(s, lambda i: (0, 0), pipeline_mode=pl.Buffered(1))
    hrow = pl.BlockSpec((DL_HEADS_PER_GROUP, tm, LANES), lambda i: (0, i, 0))
    return pl.pallas_call(
        _outproj_kernel,
        grid=(T // tm,),
        in_specs=[
            row(D), row(o_a.shape[1]),
            pl.BlockSpec((tm, D), lambda i: (i, (CT_GATE_A * COL_TILE) // D)),
            pl.BlockSpec((tm, D), lambda i: (i, (CT_GATE_B * COL_TILE) // D)),
            hrow, hrow, hrow, hrow, hrow, hrow,
            full(wa.shape), full(wb.shape), full(wo.shape), full((1, D)), full(r_hi.shape), full(r_cat.shape),
        ],
        out_specs=[row(D), row(D // 2), row(LANES)],
        out_shape=[jax.ShapeDtypeStruct((T, D), F32), jax.ShapeDtypeStruct((T, D // 2), jnp.uint32),
                   jax.ShapeDtypeStruct((T, LANES), F32)],
        compiler_params=_params(("parallel",)),
    )(x2, o_a, proj, proj, dl_o[0], dl_o[1], dl_o[2], dl_lse[0], dl_lse[1], dl_lse[2],
      wa, wb, wo, gain_ffn, r_hi, r_cat)


def _cast_rows(src_ref, dst_ref, chunk=CAST_ROWS):
    def body(c, carry):
        r0 = pl.multiple_of(c * chunk, chunk)
        dst_ref[pl.ds(r0, chunk), :] = src_ref[pl.ds(r0, chunk), :].astype(dst_ref.dtype)
        return carry
    lax.fori_loop(0, src_ref.shape[0] // chunk, body, 0)


def _expert_changed(be_ref, i):
    return (i == 0) | (be_ref[i] != be_ref[jnp.maximum(i - 1, 0)])


def _stream_expert_weights(i, live, be_ref, ne_ref, w_hbm, stage, wbf, wsem):
    changed = live & _expert_changed(be_ref, i)

    def copy(e):
        return pltpu.make_async_copy(w_hbm.at[e], stage, wsem)

    @pl.when(changed & (i == 0))
    def _():
        copy(be_ref[0]).start(priority=WEIGHT_DMA_PRIORITY)

    @pl.when(changed)
    def _():
        copy(be_ref[i]).wait()
        _cast_rows(stage, wbf)

    @pl.when(changed & (ne_ref[i] >= 0))
    def _():
        copy(ne_ref[i]).start(priority=WEIGHT_DMA_PRIORITY)


def _moe_up_kernel(be_ref, ne_ref, nu_ref, tc_ref, tn_ref, hn_ref, wgu_ref, act_ref,
                   xa, xb, wstage, wbf, gsem, wsem, *, d_ff):
    i = pl.program_id(0)
    nu = nu_ref[0]
    rows = xa.shape[0]
    even = i % 2 == 0

    def gather(tok_ref, r, buf, s):
        return pltpu.make_async_copy(hn_ref.at[tok_ref[0, r]], buf.at[r], gsem.at[s])

    def wait_gather(buf, s):
        pltpu.make_async_copy(hn_ref.at[pl.ds(0, rows)], buf, gsem.at[s]).wait()

    @pl.when((i == 0) & (nu > 0))
    def _():
        def body(r, c):
            gather(tc_ref, r, xa, 0).start()
            return c
        lax.fori_loop(0, rows, body, 0)

    _stream_expert_weights(i, i < nu, be_ref, ne_ref, wgu_ref, wstage, wbf, wsem)

    def live_step(cur, nxt, s):
        wait_gather(cur, s)
        for r in range(rows):
            gather(tn_ref, r, nxt, 1 - s).start()
        words = cur[...]
        x_lo = pltpu.bitcast(words << 16, F32).astype(BF16)
        x_hi = pltpu.bitcast(words & jnp.uint32(0xFFFF0000), F32).astype(BF16)
        half = words.shape[1]
        h = _dot(x_lo, wbf[0:half, :]) + _dot(x_hi, wbf[half:2 * half, :])
        gate = h[:, :d_ff]
        up = h[:, d_ff:]
        act_ref[...] = (gate * jax.nn.sigmoid(gate) * up).astype(act_ref.dtype)

    @pl.when((i < nu) & even)
    def _():
        live_step(xa, xb, 0)

    @pl.when((i < nu) & jnp.logical_not(even))
    def _():
        live_step(xb, xa, 1)

    @pl.when((i == nu) & (nu > 0) & even)
    def _():
        wait_gather(xa, 0)

    @pl.when((i == nu) & (nu > 0) & jnp.logical_not(even))
    def _():
        wait_gather(xb, 1)

    @pl.when(i >= nu)
    def _():
        act_ref[...] = jnp.zeros(act_ref.shape, act_ref.dtype)


def _moe_down_kernel(be_ref, ne_ref, nu_ref, dp_ref, act_ref, wd_ref, y_ref, ya, yb, wstage, wbf, ssem, wsem):
    i = pl.program_id(0)
    nu = nu_ref[0]
    rows = ya.shape[0]
    even = i % 2 == 0

    def scatter(r, buf, s):
        return pltpu.make_async_copy(buf.at[r], y_ref.at[dp_ref[0, r]], ssem.at[s])

    def wait_scatter(buf, s):
        pltpu.make_async_copy(buf, y_ref.at[pl.ds(0, rows)], ssem.at[s]).wait()

    @pl.when(i == 0)
    def _():
        n_res = y_ref.shape[0] - 2 * rows
        for s, buf in enumerate((ya, yb)):
            buf[...] = jnp.zeros(buf.shape, buf.dtype)
            pltpu.make_async_copy(buf, y_ref.at[pl.ds(n_res + s * rows, rows)], ssem.at[s]).start()
        for s, buf in enumerate((ya, yb)):
            pltpu.make_async_copy(buf, y_ref.at[pl.ds(n_res + s * rows, rows)], ssem.at[s]).wait()

    @pl.when((i >= 2) & (i < nu + 2) & even)
    def _():
        wait_scatter(ya, 0)

    @pl.when((i >= 2) & (i < nu + 2) & jnp.logical_not(even))
    def _():
        wait_scatter(yb, 1)

    _stream_expert_weights(i, i < nu, be_ref, ne_ref, wd_ref, wstage, wbf, wsem)

    def step(cur, prv, s, do_scatter, do_compute):
        if do_scatter:
            for r in range(rows):
                scatter(r, prv, 1 - s).start()
        if do_compute:
            cur[...] = _dot(act_ref[...], wbf[...])

    for s, (cur, prv) in enumerate(((ya, yb), (yb, ya))):
        par = even if s == 0 else jnp.logical_not(even)

        @pl.when((i >= 1) & (i < nu) & par)
        def _(cur=cur, prv=prv, s=s):
            step(cur, prv, s, True, True)

        @pl.when((i == 0) & (nu > 0) & par)
        def _(cur=cur, prv=prv, s=s):
            step(cur, prv, s, False, True)

        @pl.when((i == nu) & (nu > 0) & par)
        def _(cur=cur, prv=prv, s=s):
            step(cur, prv, s, True, False)


def _experts(hn, src_tok, dst_row, blk_expert, nxt_expert, n_used, wgu, wd, n_out_rows):
    D = wgu.shape[1]
    nblk = src_tok.shape[0] // MOE_BLOCK
    d_ff = wd.shape[1]
    tok = src_tok.reshape(nblk, 1, MOE_BLOCK)
    dst = dst_row.reshape(nblk, 1, MOE_BLOCK)

    def live(i, nu):
        return jnp.maximum(jnp.minimum(i, nu[0] - 1), 0)

    smem = lambda f: pl.BlockSpec((None, 1, MOE_BLOCK), f, memory_space=pltpu.SMEM)
    act = pl.pallas_call(
        functools.partial(_moe_up_kernel, d_ff=d_ff),
        grid_spec=pltpu.PrefetchScalarGridSpec(
            num_scalar_prefetch=3,
            grid=(nblk - 1,),
            in_specs=[
                smem(lambda i, be, ne, nu: (i, 0, 0)),
                smem(lambda i, be, ne, nu: (i + 1, 0, 0)),
                pl.BlockSpec(memory_space=pl.ANY),
                pl.BlockSpec(memory_space=pl.ANY),
            ],
            out_specs=pl.BlockSpec((MOE_BLOCK, d_ff), lambda i, be, ne, nu: (i, 0)),
            scratch_shapes=[pltpu.VMEM((MOE_BLOCK, D // 2), jnp.uint32), pltpu.VMEM((MOE_BLOCK, D // 2), jnp.uint32),
                            pltpu.VMEM((D, 2 * d_ff), F32), pltpu.VMEM((D, 2 * d_ff), BF16),
                            pltpu.SemaphoreType.DMA((2,)), pltpu.SemaphoreType.DMA(())],
        ),
        out_shape=jax.ShapeDtypeStruct(((nblk - 1) * MOE_BLOCK, d_ff), BF16),
        compiler_params=_params(("arbitrary",)),
    )(blk_expert, nxt_expert, n_used, tok, tok, hn, wgu)
    return pl.pallas_call(
        _moe_down_kernel,
        grid_spec=pltpu.PrefetchScalarGridSpec(
            num_scalar_prefetch=3,
            grid=(nblk,),
            in_specs=[
                smem(lambda i, be, ne, nu: (jnp.maximum(i - 1, 0), 0, 0)),
                pl.BlockSpec((MOE_BLOCK, d_ff), lambda i, be, ne, nu: (live(i, nu), 0)),
                pl.BlockSpec(memory_space=pl.ANY),
            ],
            out_specs=pl.BlockSpec(memory_space=pl.ANY),
            scratch_shapes=[pltpu.VMEM((MOE_BLOCK, D), F32), pltpu.VMEM((MOE_BLOCK, D), F32),
                            pltpu.VMEM((d_ff, D), F32), pltpu.VMEM((d_ff, D), BF16),
                            pltpu.SemaphoreType.DMA((2,)), pltpu.SemaphoreType.DMA(())],
        ),
        out_shape=jax.ShapeDtypeStruct((n_out_rows, D), F32),
        compiler_params=_params(("arbitrary",)),
    )(blk_expert, nxt_expert, n_used, dst, act, wd)


def _combine_kernel(x1_ref, rt_ref, y1_ref, y2_ref, o_ref):
    rt = rt_ref[...]
    o_ref[...] = x1_ref[...] + (rt[:, 2:3] * y1_ref[...] + rt[:, 3:4] * y2_ref[...])


def _combine(x1, route, y, tm=COMBINE_ROWS):
    T, D = x1.shape
    nt = T // tm
    return pl.pallas_call(
        _combine_kernel,
        grid=(nt,),
        in_specs=[
            pl.BlockSpec((tm, D), lambda i: (i, 0)),
            pl.BlockSpec((tm, LANES), lambda i: (i, 0)),
            pl.BlockSpec((tm, D), lambda i: (i, 0)),
            pl.BlockSpec((tm, D), lambda i: (nt + i, 0)),
        ],
        out_specs=pl.BlockSpec((tm, D), lambda i: (i, 0)),
        out_shape=jax.ShapeDtypeStruct((T, D), F32),
        compiler_params=_params(("parallel",)),
    )(x1, route, y, y)


def _dispatch_tables(route, T):
    TK = T * MOE_TOP_K
    flat_e = route[:, :MOE_TOP_K].astype(jnp.int32).reshape(-1)
    order = jnp.argsort(flat_e).astype(jnp.int32)
    sizes = jnp.sum(flat_e[:, None] == jnp.arange(MOE_N_EXPERTS, dtype=jnp.int32)[None, :], axis=0,
                    dtype=jnp.int32)
    start = jnp.cumsum(sizes) - sizes
    padded = ((sizes + MOE_BLOCK - 1) // MOE_BLOCK) * MOE_BLOCK
    pad_end = jnp.cumsum(padded)
    pad_start = pad_end - padded
    n_blocks = TK // MOE_BLOCK + MOE_N_EXPERTS
    blk = jnp.arange(n_blocks + 2, dtype=jnp.int32)
    blk_expert = jnp.minimum(jnp.sum(pad_end[None, :] <= (blk * MOE_BLOCK)[:, None], axis=1, dtype=jnp.int32),
                             MOE_N_EXPERTS - 1)
    r = jnp.arange(MOE_BLOCK, dtype=jnp.int32)[None, :]
    pos = blk[:, None] * MOE_BLOCK + r
    off = pos - pad_start[blk_expert][:, None]
    live = (off < sizes[blk_expert][:, None]) & (pos < pad_end[-1])
    sorted_idx = jnp.where(live, off + start[blk_expert][:, None], 0)
    slot = order[sorted_idx]
    tok = slot // MOE_TOP_K
    src_tok = jnp.where(live, tok, 0).reshape(-1)
    trash = TK + (blk[:, None] % 2) * MOE_BLOCK + r
    dst_row = jnp.where(live, (slot % MOE_TOP_K) * T + tok, trash).reshape(-1)
    n_used = (pad_end[-1:] // MOE_BLOCK).astype(jnp.int32)
    eid = jnp.arange(MOE_N_EXPERTS, dtype=jnp.int32)
    later = (eid[None, :] > eid[:, None]) & (sizes[None, :] > 0)
    nxt = jnp.min(jnp.where(later, eid[None, :], MOE_N_EXPERTS), axis=1)
    nxt_expert = jnp.where(nxt < MOE_N_EXPERTS, nxt, -1).astype(jnp.int32)[blk_expert]
    return src_tok, dst_row, blk_expert, nxt_expert, n_used, TK + 2 * MOE_BLOCK


def _w_in_tile_order(d_gate):
    da = DA_HEADS * 2 * DA_QK_DIM
    dav = DA_HEADS * DA_V_DIM
    dl = len(DL_GROUPS) * DL_HEADS_PER_GROUP * DL_HEAD_DIM
    o = [int(v) // COL_TILE for v in np.cumsum([0, da, da, dav, dl, dl, dl, d_gate, d_gate])]
    order = list(range(o[6], o[8])) + list(range(o[0], o[3]))
    for g in range(len(DL_GROUPS)):
        order += [o[3] + g, o[4] + g, o[5] + g]
    assert sorted(order) == list(range(CT_END))
    return jnp.asarray(order, jnp.int32)


def _gain_table(da_q_norm, da_k_norm, dl_q_norm, dl_k_norm):
    ones = jnp.ones((COL_TILE,), F32)
    daq = jnp.tile(da_q_norm, COL_TILE // DA_QK_DIM) * (DA_QK_DIM ** -0.5 * LOG2E)
    dak = jnp.tile(da_k_norm, COL_TILE // DA_QK_DIM)
    dlq = jnp.tile(dl_q_norm, COL_TILE // DL_HEAD_DIM) * (DL_HEAD_DIM ** -0.5 * LOG2E)
    dlk = jnp.tile(dl_k_norm, COL_TILE // DL_HEAD_DIM)
    rows = []
    for j in range(CT_END):
        if CT_DA_Q <= j < CT_DA_K:
            rows.append(daq)
        elif CT_DA_K <= j < CT_DA_V:
            rows.append(dak)
        elif j >= CT_DL and (j - CT_DL) % 3 == 0:
            rows.append(dlq)
        elif j >= CT_DL and (j - CT_DL) % 3 == 1:
            rows.append(dlk)
        else:
            rows.append(ones)
    return jnp.stack(rows, axis=0).reshape(CT_END, 1, COL_TILE)


def kernel(x, norm_mix, w_in, da_q_norm, da_k_norm, da_lambda_q, da_lambda_k, da_sub_norm,
           dl_q_norm, dl_k_norm, w_branch_a, w_branch_b, w_out, norm_ffn,
           w_group_router, w_expert_router, w_gate_up, w_down):
    B, S, D = x.shape
    T = B * S
    depth = w_in.shape[0]
    x2 = x.reshape(T, D)
    for l in range(depth):
        lam_init = 0.8 - 0.6 * math.exp(-0.3 * l)
        gain_tab = _gain_table(da_q_norm[l], da_k_norm[l], dl_q_norm[l], dl_k_norm[l])
        proj, dl1, dl2 = _inproj(x2, norm_mix[l].reshape(1, D), w_in[l].astype(BF16), _w_in_tile_order(D),
                                 gain_tab, B, S)

        o_a = _diff_attention(proj, da_lambda_q[l], da_lambda_k[l], da_sub_norm[l].reshape(1, DA_V_DIM),
                              B, S, lam_init)
        dl = [_dilated_group(proj.reshape(B, S, proj.shape[1]), CT_DL, 0, B, S),
              _dilated_group(dl1, 0, 1, B, S), _dilated_group(dl2, 0, 2, B, S)]

        w_r = jnp.concatenate([w_expert_router[l], w_group_router[l]], axis=1)
        w_r = jnp.pad(w_r, ((0, 0), (0, LANES - w_r.shape[1])))
        r_hi = w_r.astype(BF16)
        r_lo = (w_r - r_hi.astype(F32)).astype(BF16)
        x1, hn, route = _outproj(
            x2, o_a, proj, [t[0] for t in dl], [t[1] for t in dl],
            w_branch_a[l].astype(BF16), w_branch_b[l].astype(BF16), w_out[l].astype(BF16),
            norm_ffn[l].reshape(1, D), r_hi, jnp.concatenate([r_hi, r_lo], axis=1))

        src_tok, dst_row, blk_expert, nxt_expert, n_used, n_rows = _dispatch_tables(route, T)
        y = _experts(hn, src_tok, dst_row, blk_expert, nxt_expert, n_used, w_gate_up[l], w_down[l], n_rows)
        x2 = _combine(x1, route, y)
    return x2.reshape(B, S, D)
```

```python
import functools
import math

import jax
import jax.numpy as jnp
import numpy as np
from jax import lax
from jax.experimental import pallas as pl
from jax.experimental.pallas import tpu as pltpu

F32 = jnp.float32
BF16 = jnp.bfloat16

EPS = 1e-6
LOG2E = 1.4426950408889634
NEG_BIG = -1e30

DA_HEADS = 8
DA_QK_DIM = 64
DA_V_DIM = 128
DA_TILE_GROUP = 4
DL_GROUPS = ((128, 1), (512, 4), (2048, 16))
DL_HEADS_PER_GROUP = 4
DL_HEAD_DIM = 128
DL_SPAN = 128
MOE_GROUPS = 4
MOE_EXPERTS_PER_GROUP = 8
MOE_N_EXPERTS = 32
MOE_TOP_K = 2
MOE_BLOCK = 256
WEIGHT_DMA_PRIORITY = 1

LANES = 128
COL_TILE = 512
VMEM_LIMIT = 56 * 1024 * 1024

INPROJ_ROWS = 1024
DA_TILE = 512
DA_ROW_CHUNK = 32
DL_TOKENS_PER_STEP = 2048
DL_RESIDUES_PER_TRIP = 4
DL_LOOKAHEAD = 4
OUTPROJ_ROWS = 256
COMBINE_ROWS = 512
CAST_ROWS = 256

CT_GATE_A, CT_GATE_B, CT_DA_Q, CT_DA_K, CT_DA_V, CT_DL, CT_MAIN_END, CT_END = 0, 4, 8, 10, 12, 14, 17, 23


def _params(sem, vmem=VMEM_LIMIT):
    return pltpu.CompilerParams(dimension_semantics=sem, vmem_limit_bytes=vmem)


def _dot(a, b):
    return jnp.dot(a, b, preferred_element_type=F32)


def _dot_nt(a, b):
    return lax.dot_general(a, b, (((1,), (1,)), ((), ())), preferred_element_type=F32)


def _inproj_kernel(perm_ref, x_ref, g_ref, w_ref, gain_ref, o_ref, d1_ref, d2_ref, h_scr, y_scr):
    j = pl.program_id(1)

    @pl.when(j == 0)
    def _():
        x = x_ref[...]
        ms = jnp.mean(x * x, axis=-1, keepdims=True)
        h_scr[...] = (x * lax.rsqrt(ms + EPS) * g_ref[...]).astype(BF16)

    gain = gain_ref[...]
    half = COL_TILE // 2

    def head_slices():
        ys = [_dot(h_scr[...], w_ref[:, hf * half:(hf + 1) * half]) for hf in range(2)]
        for hf in range(2):
            for hh in range(half // LANES):
                yield hf * (half // LANES) + hh, ys[hf][:, hh * LANES:(hh + 1) * LANES]

    is64 = (j >= CT_DA_Q) & (j < CT_DA_V)
    is128 = (j >= CT_DL) & (lax.rem(j - CT_DL, 3) < 2)
    main = j < CT_MAIN_END
    plain = jnp.logical_not(is64 | is128)

    def norm64(h, yh):
        sq = yh * yh
        lo = lax.broadcasted_iota(jnp.int32, yh.shape, 1) < DA_QK_DIM
        s_lo = jnp.sum(jnp.where(lo, sq, 0.0), axis=-1, keepdims=True)
        s_hi = jnp.sum(jnp.where(lo, 0.0, sq), axis=-1, keepdims=True)
        r = jnp.where(lo, lax.rsqrt(s_lo * (1.0 / DA_QK_DIM) + EPS), lax.rsqrt(s_hi * (1.0 / DA_QK_DIM) + EPS))
        return yh * r * gain[:, h * LANES:(h + 1) * LANES]

    def norm128(h, yh):
        ss = jnp.sum(yh * yh, axis=-1, keepdims=True)
        return yh * lax.rsqrt(ss * (1.0 / DL_HEAD_DIM) + EPS) * gain[:, h * LANES:(h + 1) * LANES]

    def emit(cond, fn, to_main):
        @pl.when(cond)
        def _():
            for h, yh in head_slices():
                if to_main:
                    o_ref[:, h * LANES:(h + 1) * LANES] = fn(h, yh).astype(o_ref.dtype)
                else:
                    y_scr[h] = fn(h, yh)

    emit(is64, norm64, True)
    emit(is128 & main, norm128, True)
    emit(is128 & jnp.logical_not(main), norm128, False)
    emit(plain & main, lambda h, yh: yh, True)
    emit(plain & jnp.logical_not(main), lambda h, yh: yh, False)

    def deinterleave(dst_ref):
        d, rows = dst_ref.shape[0], dst_ref.shape[1]
        for r in range(d):
            for h in range(COL_TILE // LANES):
                dst_ref[r, :, h * LANES:(h + 1) * LANES] = (
                    y_scr[h, pl.ds(r, rows, stride=d), :].astype(dst_ref.dtype))

    @pl.when((j >= CT_MAIN_END) & (j < CT_MAIN_END + 3))
    def _():
        deinterleave(d1_ref)

    @pl.when(j >= CT_MAIN_END + 3)
    def _():
        deinterleave(d2_ref)


def _inproj(x2, gain_mix, w_bf, tile_perm, gain_tab, B, S, tm=INPROJ_ROWS):
    T, D = x2.shape
    tiles_per_batch = S // tm
    d1, d2 = DL_GROUPS[1][1], DL_GROUPS[2][1]
    part1 = lambda j: jnp.clip(j - CT_MAIN_END, 0, 2)
    part2 = lambda j: jnp.clip(j - CT_MAIN_END - 3, 0, 2)
    grid_spec = pltpu.PrefetchScalarGridSpec(
        num_scalar_prefetch=1,
        grid=(T // tm, CT_END),
        in_specs=[
            pl.BlockSpec((tm, D), lambda i, j, perm: (i, 0)),
            pl.BlockSpec((1, D), lambda i, j, perm: (0, 0)),
            pl.BlockSpec((D, COL_TILE), lambda i, j, perm: (0, perm[j])),
            pl.BlockSpec((None, 1, COL_TILE), lambda i, j, perm: (j, 0, 0)),
        ],
        out_specs=[
            pl.BlockSpec((tm, COL_TILE), lambda i, j, perm: (i, jnp.minimum(j, CT_MAIN_END - 1))),
            pl.BlockSpec((d1, tm // d1, COL_TILE),
                         lambda i, j, perm: (i // tiles_per_batch, i % tiles_per_batch, part1(j))),
            pl.BlockSpec((d2, tm // d2, COL_TILE),
                         lambda i, j, perm: (i // tiles_per_batch, i % tiles_per_batch, part2(j))),
        ],
        scratch_shapes=[pltpu.VMEM((tm, D), BF16), pltpu.VMEM((COL_TILE // LANES, tm, LANES), F32)],
    )
    return pl.pallas_call(
        _inproj_kernel,
        grid_spec=grid_spec,
        out_shape=[
            jax.ShapeDtypeStruct((T, CT_MAIN_END * COL_TILE), BF16),
            jax.ShapeDtypeStruct((B * d1, S // d1, 3 * COL_TILE), BF16),
            jax.ShapeDtypeStruct((B * d2, S // d2, 3 * COL_TILE), BF16),
        ],
        compiler_params=_params(("parallel", "arbitrary")),
    )(tile_perm, x2, gain_mix, w_bf, gain_tab)


def _bf16_pieces(x, n=3):
    out = []
    r = np.float64(x)
    for _ in range(n):
        p = np.asarray(np.float32(r)).astype(jnp.bfloat16).astype(np.float64)
        out.append(float(p))
        r = r - p
    return out


def _alibi_features(tk):
    pieces = _bf16_pieces(LOG2E)
    qf = np.zeros((2, LANES), np.float32)
    kf = np.zeros((2, tk, LANES), np.float32)
    j = np.arange(tk)
    hi, lo = (j // 16) * 16, j % 16
    for m in range(2):
        f0 = DA_QK_DIM if m == 0 else 0
        for n, p in enumerate(pieces):
            qf[m, f0 + 2 * n] = p
            qf[m, f0 + 2 * n + 1] = p
            kf[m, :, f0 + 2 * n] = hi
            kf[m, :, f0 + 2 * n + 1] = lo
    return jnp.asarray(qf), jnp.asarray(kf, dtype=BF16)


def _da_kernel(q_ref, k_ref, v_ref, qf_ref, kf_ref, lq_ref, lk_ref, sg_ref, o_ref,
               s00, s01, s10, s11, p00, p01, p10, p11, pd0, pd1,
               m0_scr, m1_scr, l0_scr, l1_scr, a0_scr, a1_scr, acc0_scr, acc1_scr, *, tq, rc, lam_init):
    h = pl.program_id(1)
    qi = pl.program_id(2)
    nlb = tq // LANES
    pow2 = jnp.exp2(-(h + 1).astype(F32))
    slope2 = pow2 * LOG2E

    q = q_ref[...]
    lane = lax.broadcasted_iota(jnp.int32, (tq, LANES), 1)
    own = (lane < DA_QK_DIM, lane >= DA_QK_DIM)
    qfs = [jnp.where(own[mi], q, jnp.broadcast_to((qf_ref[mi:mi + 1, :] * pow2).astype(BF16), q.shape))
           for mi in range(2)]

    m_scrs, l_scrs, a_scrs, acc_scrs = (m0_scr, m1_scr), (l0_scr, l1_scr), (a0_scr, a1_scr), (acc0_scr, acc1_scr)
    for mi in range(2):
        m_scrs[mi][...] = jnp.full(m_scrs[mi].shape, NEG_BIG, F32)
        l_scrs[mi][...] = jnp.zeros(l_scrs[mi].shape, F32)
        acc_scrs[mi][...] = jnp.zeros(acc_scrs[mi].shape, F32)

    def scores(ki, mi, s_ref):
        k = k_ref[pl.ds(pl.multiple_of(ki * tq, tq), tq), :]
        s_ref[...] = _dot_nt(qfs[mi], jnp.where(own[mi], k, kf_ref[mi]))

    def softmax(ki, mi, s_ref, p_ref, masked):
        m_scr, l_scr, a_scr = m_scrs[mi], l_scrs[mi], a_scrs[mi]
        c = slope2 * ((ki - qi) * tq).astype(F32)
        for r in range(tq // rc):
            rows = slice(r * rc, (r + 1) * rc)
            nb = min(nlb, ((r + 1) * rc - 1) // LANES + 1) if masked else nlb
            sb = []
            for j in range(nb):
                cs = slice(j * LANES, (j + 1) * LANES)
                s = s_ref[rows, cs]
                if masked and (j + 1) * LANES - 1 > r * rc:
                    rr = lax.broadcasted_iota(jnp.int32, (rc, LANES), 0) + r * rc
                    cc = lax.broadcasted_iota(jnp.int32, (rc, LANES), 1) + j * LANES
                    s = jnp.where(cc <= rr, s, NEG_BIG)
                sb.append(s)
            mx = sb[0]
            for s in sb[1:]:
                mx = jnp.maximum(mx, s)
            m_prev = m_scr[rows, :]
            m_new = jnp.maximum(m_prev, jnp.max(mx, axis=-1, keepdims=True) + c)
            alpha = jnp.exp2(m_prev - m_new)
            a_scr[rows, :] = alpha
            m_scr[rows, :] = m_new
            mc = m_new - c
            psum = alpha * l_scr[rows, :]
            for j in range(nlb):
                cs = slice(j * LANES, (j + 1) * LANES)
                if j < nb:
                    p = jnp.exp2(sb[j] - mc)
                    psum = psum + p
                    p_ref[rows, cs] = p.astype(BF16)
                else:
                    p_ref[rows, cs] = jnp.zeros((rc, LANES), BF16)
            l_scr[rows, :] = psum

    def values(ki, mi, p_ref):
        v = v_ref[pl.ds(pl.multiple_of(ki * tq, tq), tq), :]
        acc_scrs[mi][...] = a_scrs[mi][...] * acc_scrs[mi][...] + _dot(p_ref[...], v)

    s_bufs, p_bufs = ((s00, s01), (s10, s11)), ((p00, p01), (p10, p11))

    def tile_group(k0, n, last_masked):
        for mi in range(2):
            scores(k0, mi, s_bufs[0][mi])
        for i in range(n):
            masked = last_masked and i == n - 1
            for mi in range(2):
                p_ref = (pd0, pd1)[mi] if masked else p_bufs[i % 2][mi]
                softmax(k0 + i, mi, s_bufs[i % 2][mi], p_ref, masked)
                values(k0 + i, mi, p_ref)
                if i + 1 < n:
                    scores(k0 + i + 1, mi, s_bufs[(i + 1) % 2][mi])

    def body(t, carry):
        tile_group(DA_TILE_GROUP * t, DA_TILE_GROUP, False)
        return carry

    n_full = qi // DA_TILE_GROUP
    lax.fori_loop(0, n_full, body, 0)
    for rem in range(1, DA_TILE_GROUP + 1):
        @pl.when(qi - DA_TILE_GROUP * n_full == rem - 1)
        def _(rem=rem):
            tile_group(qi - (rem - 1), rem, True)

    lam_e = jnp.exp(jnp.sum(lq_ref[...] * lk_ref[...], axis=-1, keepdims=True))
    lam = lam_e[0:1, :] - lam_e[1:2, :] + lam_init
    l0 = jnp.sum(l0_scr[...], axis=-1, keepdims=True)
    l1 = jnp.sum(l1_scr[...], axis=-1, keepdims=True)
    o = acc0_scr[...] / l0 - lam * (acc1_scr[...] / l1)
    ms = jnp.mean(o * o, axis=-1, keepdims=True)
    o = o * lax.rsqrt(ms + EPS) * sg_ref[...] * (1.0 - lam_init)
    o_ref[...] = o.astype(o_ref.dtype)


def _diff_attention(proj, lam_q, lam_k, sub_gain, B, S, lam_init, tq=DA_TILE, rc=DA_ROW_CHUNK):
    T = proj.shape[0]
    nq = S // tq
    lb = LANES
    q_blk0, k_blk0, v_blk0 = (CT_DA_Q * COL_TILE) // lb, (CT_DA_K * COL_TILE) // lb, (CT_DA_V * COL_TILE) // lb
    qfeat, kfeat = _alibi_features(tq)
    const = lambda shape: pl.BlockSpec(shape, lambda b, h, i: (0,) * len(shape))
    return pl.pallas_call(
        functools.partial(_da_kernel, tq=tq, rc=rc, lam_init=lam_init),
        grid=(B, DA_HEADS, nq),
        in_specs=[
            pl.BlockSpec((tq, lb), lambda b, h, i: (b * nq + i, q_blk0 + h)),
            pl.BlockSpec((S, lb), lambda b, h, i: (b, k_blk0 + h)),
            pl.BlockSpec((S, lb), lambda b, h, i: (b, v_blk0 + h)),
            const((2, LANES)), const((2, tq, LANES)),
            const((2, DA_QK_DIM)), const((2, DA_QK_DIM)), const((1, DA_V_DIM)),
        ],
        out_specs=pl.BlockSpec((tq, lb), lambda b, h, i: (b * nq + i, h)),
        out_shape=jax.ShapeDtypeStruct((T, DA_HEADS * DA_V_DIM), BF16),
        scratch_shapes=[pltpu.VMEM((tq, tq), F32)] * 4 + [pltpu.VMEM((tq, tq), BF16)] * 6
        + [pltpu.VMEM((tq, LANES), F32)] * 6 + [pltpu.VMEM((tq, DA_V_DIM), F32)] * 2,
        compiler_params=_params(("parallel", "parallel", "arbitrary")),
    )(proj, proj, proj, qfeat, kfeat, lam_q, lam_k, sub_gain)


def _dl_kernel(q_ref, kc_ref, kp_ref, vc_ref, vp_ref, o_ref, lse_ref, *, slopes2, d, tq, ru):
    n = pl.program_id(1)
    sp = DL_SPAN
    row = lax.broadcasted_iota(jnp.int32, (sp, sp), 0)
    col = lax.broadcasted_iota(jnp.int32, (sp, sp), 1)
    dcur = row - col
    cur_ok = dcur >= 0
    prev_ok = dcur <= 0
    dcur_f = dcur.astype(F32)

    def scores(r, hh, j):
        hs = slice(hh * LANES, (hh + 1) * LANES)
        rs = slice(j * sp, (j + 1) * sp)
        q = q_ref[r, rs, hs]
        if j == 0:
            kp, vp, p_ok = kp_ref[r, :, hs], vp_ref[r, :, hs], prev_ok & (n > 0)
        else:
            ps = slice((j - 1) * sp, j * sp)
            kp, vp, p_ok = kc_ref[r, ps, hs], vc_ref[r, ps, hs], prev_ok
        s_c = jnp.where(cur_ok, _dot_nt(q, kc_ref[r, rs, hs]) - slopes2[hh] * dcur_f, NEG_BIG)
        s_p = jnp.where(p_ok, _dot_nt(q, kp) - slopes2[hh] * (dcur_f + float(sp)), NEG_BIG)
        return s_c, s_p, vc_ref[r, rs, hs], vp

    def finish(r, hh, j, s_c, s_p, vc, vp):
        m = jnp.max(jnp.maximum(s_c, s_p), axis=-1, keepdims=True)
        p_c = jnp.exp2(s_c - m)
        p_p = jnp.exp2(s_p - m)
        den = jnp.sum(p_c + p_p, axis=-1, keepdims=True)
        acc = _dot(p_c.astype(BF16), vc) + _dot(p_p.astype(BF16), vp)
        out_rows = pl.ds(j * sp, sp) if d == 1 else pl.ds(r + j * sp * d, sp, stride=d)
        o_ref[hh, out_rows, :] = acc / den
        lse_ref[hh, out_rows, :] = jnp.broadcast_to(m + jnp.log2(den), (sp, LANES))

    def residues(t, carry):
        units = [(t * ru + rr, hh, j) for rr in range(ru) for hh in range(DL_HEADS_PER_GROUP)
                 for j in range(tq // sp)]
        pending = []
        for u in units:
            pending.append((u, scores(*u)))
            if len(pending) > DL_LOOKAHEAD:
                u0, vals = pending.pop(0)
                finish(*u0, *vals)
        for u0, vals in pending:
            finish(*u0, *vals)
        return carry

    lax.fori_loop(0, d // ru, residues, 0)


def _dilated_group(src, col0, g, B, S, tok_per_step=DL_TOKENS_PER_STEP):
    window, d = DL_GROUPS[g]
    assert window // d == DL_SPAN
    L = S // d
    tq = min(tok_per_step, S) // d
    assert tq % DL_SPAN == 0 and L % tq == 0
    nh = DL_HEADS_PER_GROUP * len(DL_GROUPS)
    slopes2 = tuple(2.0 ** (-8.0 * (g * DL_HEADS_PER_GROUP + hh + 1) / nh) * d * LOG2E
                    for hh in range(DL_HEADS_PER_GROUP))
    spb = tq // DL_SPAN
    nsteps = L // tq
    cur = lambda c: pl.BlockSpec((d, tq, COL_TILE), lambda b, n: (b, n, c))
    prev = lambda c: pl.BlockSpec((d, DL_SPAN, COL_TILE), lambda b, n: (b, jnp.maximum(n * spb - 1, 0), c))
    out_spec = pl.BlockSpec((DL_HEADS_PER_GROUP, d * tq, LANES), lambda b, n: (0, b * nsteps + n, 0))
    return pl.pallas_call(
        functools.partial(_dl_kernel, slopes2=slopes2, d=d, tq=tq, ru=min(d, DL_RESIDUES_PER_TRIP)),
        grid=(B, nsteps),
        in_specs=[cur(col0), cur(col0 + 1), prev(col0 + 1), cur(col0 + 2), prev(col0 + 2)],
        out_specs=[out_spec, out_spec],
        out_shape=[jax.ShapeDtypeStruct((DL_HEADS_PER_GROUP, B * S, LANES), F32)] * 2,
        compiler_params=_params(("parallel", "arbitrary")),
    )(src, src, src, src, src)


def _route(logits):
    lane = lax.broadcasted_iota(jnp.int32, logits.shape, 1)
    big = jnp.int32(1 << 20)
    is_g = (lane >= MOE_N_EXPERTS) & (lane < MOE_N_EXPERTS + MOE_GROUPS)
    lg = jnp.where(is_g, logits, -jnp.inf)
    gmax = jnp.max(lg, axis=-1, keepdims=True)
    gsum = jnp.sum(jnp.exp(lg - gmax), axis=-1, keepdims=True)
    g_w = 1.0 / gsum
    g_idx = jnp.min(jnp.where(lg == gmax, lane - MOE_N_EXPERTS, big), axis=-1, keepdims=True)
    in_grp = (lane < MOE_N_EXPERTS) & ((lane // MOE_EXPERTS_PER_GROUP) == g_idx)
    le = jnp.where(in_grp, logits, -jnp.inf)
    t1 = jnp.max(le, axis=-1, keepdims=True)
    e1 = jnp.min(jnp.where(le == t1, lane, big), axis=-1, keepdims=True)
    le2 = jnp.where(lane == e1, -jnp.inf, le)
    t2 = jnp.max(le2, axis=-1, keepdims=True)
    e2 = jnp.min(jnp.where(le2 == t2, lane, big), axis=-1, keepdims=True)
    r = jnp.exp(t2 - t1)
    w1 = g_w / (1.0 + r)
    w2 = w1 * r
    out = jnp.where(lane == 0, e1.astype(F32),
                    jnp.where(lane == 1, e2.astype(F32),
                              jnp.where(lane == 2, w1, jnp.where(lane == 3, w2, 0.0))))
    return out


def _outproj_kernel(x_ref, oa_ref, ga_ref, gb_ref, o0_ref, o1_ref, o2_ref, l0_ref, l1_ref, l2_ref,
                    wa_ref, wb_ref, wo_ref, gf_ref, rh_ref, rc_ref,
                    x1_ref, hn_ref, rt_ref):
    obs = []
    for hh in range(DL_HEADS_PER_GROUP):
        l0, l1, l2 = l0_ref[hh], l1_ref[hh], l2_ref[hh]
        lm = jnp.maximum(jnp.maximum(l0, l1), l2)
        e0, e1, e2 = jnp.exp2(l0 - lm), jnp.exp2(l1 - lm), jnp.exp2(l2 - lm)
        obs.append((e0 * o0_ref[hh] + e1 * o1_ref[hh] + e2 * o2_ref[hh]) / (e0 + e1 + e2))
    ob = jnp.concatenate(obs, axis=1)
    a = _dot(oa_ref[...], wa_ref[...])
    b = _dot(ob.astype(BF16), wb_ref[...])
    mixed = jax.nn.sigmoid(ga_ref[...].astype(F32)) * a + jax.nn.sigmoid(gb_ref[...].astype(F32)) * b
    x1 = x_ref[...] + _dot(mixed.astype(BF16), wo_ref[...])
    x1_ref[...] = x1
    ms = jnp.mean(x1 * x1, axis=-1, keepdims=True)
    hn = x1 * lax.rsqrt(ms + EPS) * gf_ref[...]
    hn_hi = hn.astype(BF16)
    hn_ref[...] = _pack_bf16_pairs(hn)
    hn_lo = (hn - hn_hi.astype(F32)).astype(BF16)
    t = _dot(hn_hi, rc_ref[...])
    logits = t[:, 0:LANES] + (_dot(hn_lo, rh_ref[...]) + t[:, LANES:2 * LANES])
    rt_ref[...] = _route(logits)


def _outproj(x2, o_a, proj, dl_o, dl_lse, wa, wb, wo, gain_ffn, r_hi, r_cat, tm=OUTPROJ_ROWS):
    T, D = x2.shape
    row = lambda w: pl.BlockSpec((tm, w), lambda i: (i, 0))
    full = lambda s: pl.BlockSpec(s, lambda i: (0, 0), pipeline_mode=pl.Buffered(1))
    hrow = pl.BlockSpec((DL_HEADS_PER_GROUP, tm, LANES), lambda i: (0, i, 0))
    return pl.pallas_call(
        _outproj_kernel,
        grid=(T // tm,),
        in_specs=[
            row(D), row(o_a.shape[1]),
            pl.BlockSpec((tm, D), lambda i: (i, (CT_GATE_A * COL_TILE) // D)),
            pl.BlockSpec((tm, D), lambda i: (i, (CT_GATE_B * COL_TILE) // D)),
            hrow, hrow, hrow, hrow, hrow, hrow,
            full(wa.shape), full(wb.shape), full(wo.shape), full((1, D)), full(r_hi.shape), full(r_cat.shape),
        ],
        out_specs=[row(D), row(D // 2), row(LANES)],
        out_shape=[jax.ShapeDtypeStruct((T, D), F32), jax.ShapeDtypeStruct((T, D // 2), jnp.uint32),
                   jax.ShapeDtypeStruct((T, LANES), F32)],
        compiler_params=_params(("parallel",)),
    )(x2, o_a, proj, proj, dl_o[0], dl_o[1], dl_o[2], dl_lse[0], dl_lse[1], dl_lse[2],
      wa, wb, wo, gain_ffn, r_hi, r_cat)


def _pack_bf16_pairs(v):
    h = v.shape[1] // 2
    bits = pltpu.bitcast(v.astype(BF16).astype(F32), jnp.uint32)
    return (bits[:, :h] >> 16) | (bits[:, h:] & jnp.uint32(0xFFFF0000))


def _unpack_bf16_pairs(words):
    return pltpu.bitcast(words << 16, F32), pltpu.bitcast(words & jnp.uint32(0xFFFF0000), F32)


def _cast_rows(src_ref, dst_ref, chunk=CAST_ROWS):
    def body(c, carry):
        r0 = pl.multiple_of(c * chunk, chunk)
        dst_ref[pl.ds(r0, chunk), :] = src_ref[pl.ds(r0, chunk), :].astype(dst_ref.dtype)
        return carry
    lax.fori_loop(0, src_ref.shape[0] // chunk, body, 0)


def _expert_changed(be_ref, i):
    return (i == 0) | (be_ref[i] != be_ref[jnp.maximum(i - 1, 0)])


def _stream_expert_weights(i, live, be_ref, ne_ref, w_hbm, stage, wbf, wsem):
    changed = live & _expert_changed(be_ref, i)

    def copy(e):
        return pltpu.make_async_copy(w_hbm.at[e], stage, wsem)

    @pl.when(changed & (i == 0))
    def _():
        copy(be_ref[0]).start(priority=WEIGHT_DMA_PRIORITY)

    @pl.when(changed)
    def _():
        copy(be_ref[i]).wait()
        _cast_rows(stage, wbf)

    @pl.when(changed & (ne_ref[i] >= 0))
    def _():
        copy(ne_ref[i]).start(priority=WEIGHT_DMA_PRIORITY)


def _moe_up_kernel(be_ref, ne_ref, nu_ref, tc_ref, tn_ref, hn_ref, wgu_ref, act_ref,
                   xa, xb, wstage, wbf, gsem, wsem, *, d_ff):
    i = pl.program_id(0)
    nu = nu_ref[0]
    rows = xa.shape[0]
    even = i % 2 == 0

    def gather(tok_ref, r, buf, s):
        return pltpu.make_async_copy(hn_ref.at[tok_ref[0, r]], buf.at[r], gsem.at[s])

    def wait_gather(buf, s):
        pltpu.make_async_copy(hn_ref.at[pl.ds(0, rows)], buf, gsem.at[s]).wait()

    @pl.when((i == 0) & (nu > 0))
    def _():
        def body(r, c):
            gather(tc_ref, r, xa, 0).start()
            return c
        lax.fori_loop(0, rows, body, 0)

    _stream_expert_weights(i, i < nu, be_ref, ne_ref, wgu_ref, wstage, wbf, wsem)

    def live_step(cur, nxt, s):
        wait_gather(cur, s)
        for r in range(rows):
            gather(tn_ref, r, nxt, 1 - s).start()
        x_lo, x_hi = _unpack_bf16_pairs(cur[...])
        half = x_lo.shape[1]
        h = _dot(x_lo.astype(BF16), wbf[0:half, :]) + _dot(x_hi.astype(BF16), wbf[half:2 * half, :])
        gate = h[:, :d_ff]
        up = h[:, d_ff:]
        act_ref[...] = (gate * jax.nn.sigmoid(gate) * up).astype(act_ref.dtype)

    @pl.when((i < nu) & even)
    def _():
        live_step(xa, xb, 0)

    @pl.when((i < nu) & jnp.logical_not(even))
    def _():
        live_step(xb, xa, 1)

    @pl.when((i == nu) & (nu > 0) & even)
    def _():
        wait_gather(xa, 0)

    @pl.when((i == nu) & (nu > 0) & jnp.logical_not(even))
    def _():
        wait_gather(xb, 1)

    @pl.when(i >= nu)
    def _():
        act_ref[...] = jnp.zeros(act_ref.shape, act_ref.dtype)


def _moe_down_kernel(be_ref, ne_ref, nu_ref, dp_ref, act_ref, wd_ref, y_ref, ya, yb, wstage, wbf, ssem, wsem):
    i = pl.program_id(0)
    nu = nu_ref[0]
    rows = ya.shape[0]
    even = i % 2 == 0

    def scatter(r, buf, s):
        return pltpu.make_async_copy(buf.at[r], y_ref.at[dp_ref[0, r]], ssem.at[s])

    def wait_scatter(buf, s):
        pltpu.make_async_copy(buf, y_ref.at[pl.ds(0, rows)], ssem.at[s]).wait()

    @pl.when(i == 0)
    def _():
        n_res = y_ref.shape[0] - 2 * rows
        for s, buf in enumerate((ya, yb)):
            buf[...] = jnp.zeros(buf.shape, buf.dtype)
            pltpu.make_async_copy(buf, y_ref.at[pl.ds(n_res + s * rows, rows)], ssem.at[s]).start()
        for s, buf in enumerate((ya, yb)):
            pltpu.make_async_copy(buf, y_ref.at[pl.ds(n_res + s * rows, rows)], ssem.at[s]).wait()

    @pl.when((i >= 2) & (i < nu + 2) & even)
    def _():
        wait_scatter(ya, 0)

    @pl.when((i >= 2) & (i < nu + 2) & jnp.logical_not(even))
    def _():
        wait_scatter(yb, 1)

    _stream_expert_weights(i, i < nu, be_ref, ne_ref, wd_ref, wstage, wbf, wsem)

    def step(cur, prv, s, do_scatter, do_compute):
        if do_scatter:
            for r in range(rows):
                scatter(r, prv, 1 - s).start()
        if do_compute:
            cur[...] = _pack_bf16_pairs(_dot(act_ref[...], wbf[...]))

    for s, (cur, prv) in enumerate(((ya, yb), (yb, ya))):
        par = even if s == 0 else jnp.logical_not(even)

        @pl.when((i >= 1) & (i < nu) & par)
        def _(cur=cur, prv=prv, s=s):
            step(cur, prv, s, True, True)

        @pl.when((i == 0) & (nu > 0) & par)
        def _(cur=cur, prv=prv, s=s):
            step(cur, prv, s, False, True)

        @pl.when((i == nu) & (nu > 0) & par)
        def _(cur=cur, prv=prv, s=s):
            step(cur, prv, s, True, False)


def _experts(hn, src_tok, dst_row, blk_expert, nxt_expert, n_used, wgu, wd, n_out_rows):
    D = wgu.shape[1]
    nblk = src_tok.shape[0] // MOE_BLOCK
    d_ff = wd.shape[1]
    tok = src_tok.reshape(nblk, 1, MOE_BLOCK)
    dst = dst_row.reshape(nblk, 1, MOE_BLOCK)

    def live(i, nu):
        return jnp.maximum(jnp.minimum(i, nu[0] - 1), 0)

    smem = lambda f: pl.BlockSpec((None, 1, MOE_BLOCK), f, memory_space=pltpu.SMEM)
    act = pl.pallas_call(
        functools.partial(_moe_up_kernel, d_ff=d_ff),
        grid_spec=pltpu.PrefetchScalarGridSpec(
            num_scalar_prefetch=3,
            grid=(nblk - 1,),
            in_specs=[
                smem(lambda i, be, ne, nu: (i, 0, 0)),
                smem(lambda i, be, ne, nu: (i + 1, 0, 0)),
                pl.BlockSpec(memory_space=pl.ANY),
                pl.BlockSpec(memory_space=pl.ANY),
            ],
            out_specs=pl.BlockSpec((MOE_BLOCK, d_ff), lambda i, be, ne, nu: (i, 0)),
            scratch_shapes=[pltpu.VMEM((MOE_BLOCK, D // 2), jnp.uint32), pltpu.VMEM((MOE_BLOCK, D // 2), jnp.uint32),
                            pltpu.VMEM((D, 2 * d_ff), F32), pltpu.VMEM((D, 2 * d_ff), BF16),
                            pltpu.SemaphoreType.DMA((2,)), pltpu.SemaphoreType.DMA(())],
        ),
        out_shape=jax.ShapeDtypeStruct(((nblk - 1) * MOE_BLOCK, d_ff), BF16),
        compiler_params=_params(("arbitrary",)),
    )(blk_expert, nxt_expert, n_used, tok, tok, hn, wgu)
    return pl.pallas_call(
        _moe_down_kernel,
        grid_spec=pltpu.PrefetchScalarGridSpec(
            num_scalar_prefetch=3,
            grid=(nblk,),
            in_specs=[
                smem(lambda i, be, ne, nu: (jnp.maximum(i - 1, 0), 0, 0)),
                pl.BlockSpec((MOE_BLOCK, d_ff), lambda i, be, ne, nu: (live(i, nu), 0)),
                pl.BlockSpec(memory_space=pl.ANY),
            ],
            out_specs=pl.BlockSpec(memory_space=pl.ANY),
            scratch_shapes=[pltpu.VMEM((MOE_BLOCK, D // 2), jnp.uint32), pltpu.VMEM((MOE_BLOCK, D // 2), jnp.uint32),
                            pltpu.VMEM((d_ff, D), F32), pltpu.VMEM((d_ff, D), BF16),
                            pltpu.SemaphoreType.DMA((2,)), pltpu.SemaphoreType.DMA(())],
        ),
        out_shape=jax.ShapeDtypeStruct((n_out_rows, D // 2), jnp.uint32),
        compiler_params=_params(("arbitrary",)),
    )(blk_expert, nxt_expert, n_used, dst, act, wd)


def _combine_kernel(x1_ref, rt_ref, y1_ref, y2_ref, o_ref):
    rt = rt_ref[...]
    w1, w2 = rt[:, 2:3], rt[:, 3:4]
    half = o_ref.shape[1] // 2
    y1 = _unpack_bf16_pairs(y1_ref[...])
    y2 = _unpack_bf16_pairs(y2_ref[...])
    for p in range(2):
        cs = slice(p * half, (p + 1) * half)
        o_ref[:, cs] = x1_ref[:, cs] + (w1 * y1[p] + w2 * y2[p])


def _combine(x1, route, y, tm=COMBINE_ROWS):
    T, D = x1.shape
    nt = T // tm
    return pl.pallas_call(
        _combine_kernel,
        grid=(nt,),
        in_specs=[
            pl.BlockSpec((tm, D), lambda i: (i, 0)),
            pl.BlockSpec((tm, LANES), lambda i: (i, 0)),
            pl.BlockSpec((tm, D // 2), lambda i: (i, 0)),
            pl.BlockSpec((tm, D // 2), lambda i: (nt + i, 0)),
        ],
        out_specs=pl.BlockSpec((tm, D), lambda i: (i, 0)),
        out_shape=jax.ShapeDtypeStruct((T, D), F32),
        compiler_params=_params(("parallel",)),
    )(x1, route, y, y)


def _dispatch_tables(route, T):
    TK = T * MOE_TOP_K
    flat_e = route[:, :MOE_TOP_K].astype(jnp.int32).reshape(-1)
    order = jnp.argsort(flat_e).astype(jnp.int32)
    sizes = jnp.sum(flat_e[:, None] == jnp.arange(MOE_N_EXPERTS, dtype=jnp.int32)[None, :], axis=0,
                    dtype=jnp.int32)
    start = jnp.cumsum(sizes) - sizes
    padded = ((sizes + MOE_BLOCK - 1) // MOE_BLOCK) * MOE_BLOCK
    pad_end = jnp.cumsum(padded)
    pad_start = pad_end - padded
    n_blocks = TK // MOE_BLOCK + MOE_N_EXPERTS
    blk = jnp.arange(n_blocks + 2, dtype=jnp.int32)
    blk_expert = jnp.minimum(jnp.sum(pad_end[None, :] <= (blk * MOE_BLOCK)[:, None], axis=1, dtype=jnp.int32),
                             MOE_N_EXPERTS - 1)
    r = jnp.arange(MOE_BLOCK, dtype=jnp.int32)[None, :]
    pos = blk[:, None] * MOE_BLOCK + r
    off = pos - pad_start[blk_expert][:, None]
    live = (off < sizes[blk_expert][:, None]) & (pos < pad_end[-1])
    sorted_idx = jnp.where(live, off + start[blk_expert][:, None], 0)
    slot = order[sorted_idx]
    tok = slot // MOE_TOP_K
    src_tok = jnp.where(live, tok, 0).reshape(-1)
    trash = TK + (blk[:, None] % 2) * MOE_BLOCK + r
    dst_row = jnp.where(live, (slot % MOE_TOP_K) * T + tok, trash).reshape(-1)
    n_used = (pad_end[-1:] // MOE_BLOCK).astype(jnp.int32)
    eid = jnp.arange(MOE_N_EXPERTS, dtype=jnp.int32)
    later = (eid[None, :] > eid[:, None]) & (sizes[None, :] > 0)
    nxt = jnp.min(jnp.where(later, eid[None, :], MOE_N_EXPERTS), axis=1)
    nxt_expert = jnp.where(nxt < MOE_N_EXPERTS, nxt, -1).astype(jnp.int32)[blk_expert]
    return src_tok, dst_row, blk_expert, nxt_expert, n_used, TK + 2 * MOE_BLOCK


def _w_in_tile_order(d_gate):
    da = DA_HEADS * 2 * DA_QK_DIM
    dav = DA_HEADS * DA_V_DIM
    dl = len(DL_GROUPS) * DL_HEADS_PER_GROUP * DL_HEAD_DIM
    o = [int(v) // COL_TILE for v in np.cumsum([0, da, da, dav, dl, dl, dl, d_gate, d_gate])]
    order = list(range(o[6], o[8])) + list(range(o[0], o[3]))
    for g in range(len(DL_GROUPS)):
        order += [o[3] + g, o[4] + g, o[5] + g]
    assert sorted(order) == list(range(CT_END))
    return jnp.asarray(order, jnp.int32)


def _gain_table(da_q_norm, da_k_norm, dl_q_norm, dl_k_norm):
    ones = jnp.ones((COL_TILE,), F32)
    daq = jnp.tile(da_q_norm, COL_TILE // DA_QK_DIM) * (DA_QK_DIM ** -0.5 * LOG2E)
    dak = jnp.tile(da_k_norm, COL_TILE // DA_QK_DIM)
    dlq = jnp.tile(dl_q_norm, COL_TILE // DL_HEAD_DIM) * (DL_HEAD_DIM ** -0.5 * LOG2E)
    dlk = jnp.tile(dl_k_norm, COL_TILE // DL_HEAD_DIM)
    rows = []
    for j in range(CT_END):
        if CT_DA_Q <= j < CT_DA_K:
            rows.append(daq)
        elif CT_DA_K <= j < CT_DA_V:
            rows.append(dak)
        elif j >= CT_DL and (j - CT_DL) % 3 == 0:
            rows.append(dlq)
        elif j >= CT_DL and (j - CT_DL) % 3 == 1:
            rows.append(dlk)
        else:
            rows.append(ones)
    return jnp.stack(rows, axis=0).reshape(CT_END, 1, COL_TILE)


def kernel(x, norm_mix, w_in, da_q_norm, da_k_norm, da_lambda_q, da_lambda_k, da_sub_norm,
           dl_q_norm, dl_k_norm, w_branch_a, w_branch_b, w_out, norm_ffn,
           w_group_router, w_expert_router, w_gate_up, w_down):
    B, S, D = x.shape
    T = B * S
    depth = w_in.shape[0]
    x2 = x.reshape(T, D)
    for l in range(depth):
        lam_init = 0.8 - 0.6 * math.exp(-0.3 * l)
        gain_tab = _gain_table(da_q_norm[l], da_k_norm[l], dl_q_norm[l], dl_k_norm[l])
        proj, dl1, dl2 = _inproj(x2, norm_mix[l].reshape(1, D), w_in[l].astype(BF16), _w_in_tile_order(D),
                                 gain_tab, B, S)

        o_a = _diff_attention(proj, da_lambda_q[l], da_lambda_k[l], da_sub_norm[l].reshape(1, DA_V_DIM),
                              B, S, lam_init)
        dl = [_dilated_group(proj.reshape(B, S, proj.shape[1]), CT_DL, 0, B, S),
              _dilated_group(dl1, 0, 1, B, S), _dilated_group(dl2, 0, 2, B, S)]

        w_r = jnp.concatenate([w_expert_router[l], w_group_router[l]], axis=1)
        w_r = jnp.pad(w_r, ((0, 0), (0, LANES - w_r.shape[1])))
        r_hi = w_r.astype(BF16)
        r_lo = (w_r - r_hi.astype(F32)).astype(BF16)
        x1, hn, route = _outproj(
            x2, o_a, proj, [t[0] for t in dl], [t[1] for t in dl],
            w_branch_a[l].astype(BF16), w_branch_b[l].astype(BF16), w_out[l].astype(BF16),
            norm_ffn[l].reshape(1, D), r_hi, jnp.concatenate([r_hi, r_lo], axis=1))

        src_tok, dst_row, blk_expert, nxt_expert, n_used, n_rows = _dispatch_tables(route, T)
        y = _experts(hn, src_tok, dst_row, blk_expert, nxt_expert, n_used, w_gate_up[l], w_down[l], n_rows)
        x2 = _combine(x1, route, y)
    return x2.reshape(B, S, D)
```

```python
import functools
import math

import jax
import jax.numpy as jnp
import numpy as np
from jax import lax
from jax.experimental import pallas as pl
from jax.experimental.pallas import tpu as pltpu

F32 = jnp.float32
BF16 = jnp.bfloat16

EPS = 1e-6
LOG2E = 1.4426950408889634
NEG_BIG = -1e30

DA_HEADS = 8
DA_QK_DIM = 64
DA_V_DIM = 128
DA_TILE_GROUP = 4
DL_GROUPS = ((128, 1), (512, 4), (2048, 16))
DL_HEADS_PER_GROUP = 4
DL_HEAD_DIM = 128
DL_SPAN = 128
MOE_GROUPS = 4
MOE_EXPERTS_PER_GROUP = 8
MOE_N_EXPERTS = 32
MOE_TOP_K = 2
MOE_BLOCK = 256
WEIGHT_DMA_PRIORITY = 1

LANES = 128
COL_TILE = 512
VMEM_LIMIT = 56 * 1024 * 1024

INPROJ_ROWS = 1024
DA_TILE = 512
DA_ROW_CHUNK = 32
DL_TOKENS_PER_STEP = 2048
DL_RESIDUES_PER_TRIP = 4
DL_LOOKAHEAD = 4
OUTPROJ_ROWS = 256
COMBINE_ROWS = 512
CAST_ROWS = 256

CT_GATE_A, CT_GATE_B, CT_DA_Q, CT_DA_K, CT_DA_V, CT_DL, CT_MAIN_END, CT_END = 0, 4, 8, 10, 12, 14, 17, 23


def _params(sem, vmem=VMEM_LIMIT):
    return pltpu.CompilerParams(dimension_semantics=sem, vmem_limit_bytes=vmem)


def _dot(a, b):
    return jnp.dot(a, b, preferred_element_type=F32)


def _dot_nt(a, b):
    return lax.dot_general(a, b, (((1,), (1,)), ((), ())), preferred_element_type=F32)


def _inproj_kernel(perm_ref, x_ref, g_ref, w_ref, gain_ref, o_ref, d1_ref, d2_ref, h_scr, y_scr):
    j = pl.program_id(1)

    @pl.when(j == 0)
    def _():
        x = x_ref[...]
        ms = jnp.mean(x * x, axis=-1, keepdims=True)
        h_scr[...] = (x * lax.rsqrt(ms + EPS) * g_ref[...]).astype(BF16)

    gain = gain_ref[...]
    half = COL_TILE // 2

    def head_slices():
        ys = [_dot(h_scr[...], w_ref[:, hf * half:(hf + 1) * half]) for hf in range(2)]
        for hf in range(2):
            for hh in range(half // LANES):
                yield hf * (half // LANES) + hh, ys[hf][:, hh * LANES:(hh + 1) * LANES]

    is64 = (j >= CT_DA_Q) & (j < CT_DA_V)
    is128 = (j >= CT_DL) & (lax.rem(j - CT_DL, 3) < 2)
    main = j < CT_MAIN_END
    plain = jnp.logical_not(is64 | is128)

    def norm64(h, yh):
        sq = yh * yh
        lo = lax.broadcasted_iota(jnp.int32, yh.shape, 1) < DA_QK_DIM
        s_lo = jnp.sum(jnp.where(lo, sq, 0.0), axis=-1, keepdims=True)
        s_hi = jnp.sum(jnp.where(lo, 0.0, sq), axis=-1, keepdims=True)
        r = jnp.where(lo, lax.rsqrt(s_lo * (1.0 / DA_QK_DIM) + EPS), lax.rsqrt(s_hi * (1.0 / DA_QK_DIM) + EPS))
        return yh * r * gain[:, h * LANES:(h + 1) * LANES]

    def norm128(h, yh):
        ss = jnp.sum(yh * yh, axis=-1, keepdims=True)
        return yh * lax.rsqrt(ss * (1.0 / DL_HEAD_DIM) + EPS) * gain[:, h * LANES:(h + 1) * LANES]

    def emit(cond, fn, to_main):
        @pl.when(cond)
        def _():
            for h, yh in head_slices():
                if to_main:
                    o_ref[:, h * LANES:(h + 1) * LANES] = fn(h, yh).astype(o_ref.dtype)
                else:
                    y_scr[h] = fn(h, yh)

    emit(is64, norm64, True)
    emit(is128 & main, norm128, True)
    emit(is128 & jnp.logical_not(main), norm128, False)
    emit(plain & main, lambda h, yh: yh, True)
    emit(plain & jnp.logical_not(main), lambda h, yh: yh, False)

    def deinterleave(dst_ref):
        d, rows = dst_ref.shape[0], dst_ref.shape[1]
        for r in range(d):
            for h in range(COL_TILE // LANES):
                dst_ref[r, :, h * LANES:(h + 1) * LANES] = (
                    y_scr[h, pl.ds(r, rows, stride=d), :].astype(dst_ref.dtype))

    @pl.when((j >= CT_MAIN_END) & (j < CT_MAIN_END + 3))
    def _():
        deinterleave(d1_ref)

    @pl.when(j >= CT_MAIN_END + 3)
    def _():
        deinterleave(d2_ref)


def _inproj(x2, gain_mix, w_bf, tile_perm, gain_tab, B, S, tm=INPROJ_ROWS):
    T, D = x2.shape
    tiles_per_batch = S // tm
    d1, d2 = DL_GROUPS[1][1], DL_GROUPS[2][1]
    part1 = lambda j: jnp.clip(j - CT_MAIN_END, 0, 2)
    part2 = lambda j: jnp.clip(j - CT_MAIN_END - 3, 0, 2)
    grid_spec = pltpu.PrefetchScalarGridSpec(
        num_scalar_prefetch=1,
        grid=(T // tm, CT_END),
        in_specs=[
            pl.BlockSpec((tm, D), lambda i, j, perm: (i, 0)),
            pl.BlockSpec((1, D), lambda i, j, perm: (0, 0)),
            pl.BlockSpec((D, COL_TILE), lambda i, j, perm: (0, perm[j])),
            pl.BlockSpec((None, 1, COL_TILE), lambda i, j, perm: (j, 0, 0)),
        ],
        out_specs=[
            pl.BlockSpec((tm, COL_TILE), lambda i, j, perm: (i, jnp.minimum(j, CT_MAIN_END - 1))),
            pl.BlockSpec((d1, tm // d1, COL_TILE),
                         lambda i, j, perm: (i // tiles_per_batch, i % tiles_per_batch, part1(j))),
            pl.BlockSpec((d2, tm // d2, COL_TILE),
                         lambda i, j, perm: (i // tiles_per_batch, i % tiles_per_batch, part2(j))),
        ],
        scratch_shapes=[pltpu.VMEM((tm, D), BF16), pltpu.VMEM((COL_TILE // LANES, tm, LANES), F32)],
    )
    return pl.pallas_call(
        _inproj_kernel,
        grid_spec=grid_spec,
        out_shape=[
            jax.ShapeDtypeStruct((T, CT_MAIN_END * COL_TILE), BF16),
            jax.ShapeDtypeStruct((B * d1, S // d1, 3 * COL_TILE), BF16),
            jax.ShapeDtypeStruct((B * d2, S // d2, 3 * COL_TILE), BF16),
        ],
        compiler_params=_params(("parallel", "arbitrary")),
    )(tile_perm, x2, gain_mix, w_bf, gain_tab)


def _bf16_pieces(x, n=3):
    out = []
    r = np.float64(x)
    for _ in range(n):
        p = np.asarray(np.float32(r)).astype(jnp.bfloat16).astype(np.float64)
        out.append(float(p))
        r = r - p
    return out


def _alibi_features(tk):
    pieces = _bf16_pieces(LOG2E)
    qf = np.zeros((2, LANES), np.float32)
    kf = np.zeros((2, tk, LANES), np.float32)
    j = np.arange(tk)
    hi, lo = (j // 16) * 16, j % 16
    for m in range(2):
        f0 = DA_QK_DIM if m == 0 else 0
        for n, p in enumerate(pieces):
            qf[m, f0 + 2 * n] = p
            qf[m, f0 + 2 * n + 1] = p
            kf[m, :, f0 + 2 * n] = hi
            kf[m, :, f0 + 2 * n + 1] = lo
    return jnp.asarray(qf), jnp.asarray(kf, dtype=BF16)


def _da_kernel(q_ref, k_ref, v_ref, qf_ref, kf_ref, lq_ref, lk_ref, sg_ref, o_ref,
               s00, s01, s10, s11, p00, p01, p10, p11, pd0, pd1,
               m0_scr, m1_scr, l0_scr, l1_scr, a0_scr, a1_scr, acc0_scr, acc1_scr, *, tq, rc, lam_init):
    h = pl.program_id(1)
    qi = pl.program_id(2)
    nlb = tq // LANES
    pow2 = jnp.exp2(-(h + 1).astype(F32))
    slope2 = pow2 * LOG2E

    q = q_ref[...]
    lane = lax.broadcasted_iota(jnp.int32, (tq, LANES), 1)
    own = (lane < DA_QK_DIM, lane >= DA_QK_DIM)
    qfs = [jnp.where(own[mi], q, jnp.broadcast_to((qf_ref[mi:mi + 1, :] * pow2).astype(BF16), q.shape))
           for mi in range(2)]

    m_scrs, l_scrs, a_scrs, acc_scrs = (m0_scr, m1_scr), (l0_scr, l1_scr), (a0_scr, a1_scr), (acc0_scr, acc1_scr)
    for mi in range(2):
        m_scrs[mi][...] = jnp.full(m_scrs[mi].shape, NEG_BIG, F32)
        l_scrs[mi][...] = jnp.zeros(l_scrs[mi].shape, F32)
        acc_scrs[mi][...] = jnp.zeros(acc_scrs[mi].shape, F32)

    def scores(ki, mi, s_ref):
        k = k_ref[pl.ds(pl.multiple_of(ki * tq, tq), tq), :]
        s_ref[...] = _dot_nt(qfs[mi], jnp.where(own[mi], k, kf_ref[mi]))

    def softmax(ki, mi, s_ref, p_ref, masked):
        m_scr, l_scr, a_scr = m_scrs[mi], l_scrs[mi], a_scrs[mi]
        c = slope2 * ((ki - qi) * tq).astype(F32)
        for r in range(tq // rc):
            rows = slice(r * rc, (r + 1) * rc)
            nb = min(nlb, ((r + 1) * rc - 1) // LANES + 1) if masked else nlb
            sb = []
            for j in range(nb):
                cs = slice(j * LANES, (j + 1) * LANES)
                s = s_ref[rows, cs]
                if masked and (j + 1) * LANES - 1 > r * rc:
                    rr = lax.broadcasted_iota(jnp.int32, (rc, LANES), 0) + r * rc
                    cc = lax.broadcasted_iota(jnp.int32, (rc, LANES), 1) + j * LANES
                    s = jnp.where(cc <= rr, s, NEG_BIG)
                sb.append(s)
            mx = sb[0]
            for s in sb[1:]:
                mx = jnp.maximum(mx, s)
            m_prev = m_scr[rows, :]
            m_new = jnp.maximum(m_prev, jnp.max(mx, axis=-1, keepdims=True) + c)
            alpha = jnp.exp2(m_prev - m_new)
            a_scr[rows, :] = alpha
            m_scr[rows, :] = m_new
            mc = m_new - c
            psum = alpha * l_scr[rows, :]
            for j in range(nlb):
                cs = slice(j * LANES, (j + 1) * LANES)
                if j < nb:
                    p = jnp.exp2(sb[j] - mc)
                    psum = psum + p
                    p_ref[rows, cs] = p.astype(BF16)
                else:
                    p_ref[rows, cs] = jnp.zeros((rc, LANES), BF16)
            l_scr[rows, :] = psum

    def values(ki, mi, p_ref):
        v = v_ref[pl.ds(pl.multiple_of(ki * tq, tq), tq), :]
        acc_scrs[mi][...] = a_scrs[mi][...] * acc_scrs[mi][...] + _dot(p_ref[...], v)

    s_bufs, p_bufs = ((s00, s01), (s10, s11)), ((p00, p01), (p10, p11))

    def tile_group(k0, n, last_masked):
        for mi in range(2):
            scores(k0, mi, s_bufs[0][mi])
        for i in range(n):
            masked = last_masked and i == n - 1
            for mi in range(2):
                p_ref = (pd0, pd1)[mi] if masked else p_bufs[i % 2][mi]
                softmax(k0 + i, mi, s_bufs[i % 2][mi], p_ref, masked)
                values(k0 + i, mi, p_ref)
                if i + 1 < n:
                    scores(k0 + i + 1, mi, s_bufs[(i + 1) % 2][mi])

    def body(t, carry):
        tile_group(DA_TILE_GROUP * t, DA_TILE_GROUP, False)
        return carry

    n_full = qi // DA_TILE_GROUP
    lax.fori_loop(0, n_full, body, 0)
    for rem in range(1, DA_TILE_GROUP + 1):
        @pl.when(qi - DA_TILE_GROUP * n_full == rem - 1)
        def _(rem=rem):
            tile_group(qi - (rem - 1), rem, True)

    lam_e = jnp.exp(jnp.sum(lq_ref[...] * lk_ref[...], axis=-1, keepdims=True))
    lam = lam_e[0:1, :] - lam_e[1:2, :] + lam_init
    l0 = jnp.sum(l0_scr[...], axis=-1, keepdims=True)
    l1 = jnp.sum(l1_scr[...], axis=-1, keepdims=True)
    o = acc0_scr[...] / l0 - lam * (acc1_scr[...] / l1)
    ms = jnp.mean(o * o, axis=-1, keepdims=True)
    o = o * lax.rsqrt(ms + EPS) * sg_ref[...] * (1.0 - lam_init)
    o_ref[...] = o.astype(o_ref.dtype)


def _diff_attention(proj, lam_q, lam_k, sub_gain, B, S, lam_init, tq=DA_TILE, rc=DA_ROW_CHUNK):
    T = proj.shape[0]
    nq = S // tq
    lb = LANES
    q_blk0, k_blk0, v_blk0 = (CT_DA_Q * COL_TILE) // lb, (CT_DA_K * COL_TILE) // lb, (CT_DA_V * COL_TILE) // lb
    qfeat, kfeat = _alibi_features(tq)
    const = lambda shape: pl.BlockSpec(shape, lambda b, h, i: (0,) * len(shape))
    return pl.pallas_call(
        functools.partial(_da_kernel, tq=tq, rc=rc, lam_init=lam_init),
        grid=(B, DA_HEADS, nq),
        in_specs=[
            pl.BlockSpec((tq, lb), lambda b, h, i: (b * nq + i, q_blk0 + h)),
            pl.BlockSpec((S, lb), lambda b, h, i: (b, k_blk0 + h)),
            pl.BlockSpec((S, lb), lambda b, h, i: (b, v_blk0 + h)),
            const((2, LANES)), const((2, tq, LANES)),
            const((2, DA_QK_DIM)), const((2, DA_QK_DIM)), const((1, DA_V_DIM)),
        ],
        out_specs=pl.BlockSpec((tq, lb), lambda b, h, i: (b * nq + i, h)),
        out_shape=jax.ShapeDtypeStruct((T, DA_HEADS * DA_V_DIM), BF16),
        scratch_shapes=[pltpu.VMEM((tq, tq), F32)] * 4 + [pltpu.VMEM((tq, tq), BF16)] * 6
        + [pltpu.VMEM((tq, LANES), F32)] * 6 + [pltpu.VMEM((tq, DA_V_DIM), F32)] * 2,
        compiler_params=_params(("parallel", "parallel", "arbitrary")),
    )(proj, proj, proj, qfeat, kfeat, lam_q, lam_k, sub_gain)


def _dl_kernel(q_ref, kc_ref, kp_ref, vc_ref, vp_ref, o_ref, lse_ref, *, slopes2, d, tq, ru):
    n = pl.program_id(1)
    sp = DL_SPAN
    row = lax.broadcasted_iota(jnp.int32, (sp, sp), 0)
    col = lax.broadcasted_iota(jnp.int32, (sp, sp), 1)
    dcur = row - col
    cur_ok = dcur >= 0
    prev_ok = dcur <= 0
    dcur_f = dcur.astype(F32)

    def scores(r, hh, j):
        hs = slice(hh * LANES, (hh + 1) * LANES)
        rs = slice(j * sp, (j + 1) * sp)
        q = q_ref[r, rs, hs]
        if j == 0:
            kp, vp, p_ok = kp_ref[r, :, hs], vp_ref[r, :, hs], prev_ok & (n > 0)
        else:
            ps = slice((j - 1) * sp, j * sp)
            kp, vp, p_ok = kc_ref[r, ps, hs], vc_ref[r, ps, hs], prev_ok
        s_c = jnp.where(cur_ok, _dot_nt(q, kc_ref[r, rs, hs]) - slopes2[hh] * dcur_f, NEG_BIG)
        s_p = jnp.where(p_ok, _dot_nt(q, kp) - slopes2[hh] * (dcur_f + float(sp)), NEG_BIG)
        return s_c, s_p, vc_ref[r, rs, hs], vp

    def finish(r, hh, j, s_c, s_p, vc, vp):
        m = jnp.max(jnp.maximum(s_c, s_p), axis=-1, keepdims=True)
        p_c = jnp.exp2(s_c - m)
        p_p = jnp.exp2(s_p - m)
        den = jnp.sum(p_c + p_p, axis=-1, keepdims=True)
        acc = _dot(p_c.astype(BF16), vc) + _dot(p_p.astype(BF16), vp)
        out_rows = pl.ds(j * sp, sp) if d == 1 else pl.ds(r + j * sp * d, sp, stride=d)
        o_ref[hh, out_rows, :] = acc / den
        lse_ref[hh, out_rows, :] = jnp.broadcast_to(m + jnp.log2(den), (sp, LANES))

    def residues(t, carry):
        units = [(t * ru + rr, hh, j) for rr in range(ru) for hh in range(DL_HEADS_PER_GROUP)
                 for j in range(tq // sp)]
        pending = []
        for u in units:
            pending.append((u, scores(*u)))
            if len(pending) > DL_LOOKAHEAD:
                u0, vals = pending.pop(0)
                finish(*u0, *vals)
        for u0, vals in pending:
            finish(*u0, *vals)
        return carry

    lax.fori_loop(0, d // ru, residues, 0)


def _dilated_group(src, col0, g, B, S, tok_per_step=DL_TOKENS_PER_STEP):
    window, d = DL_GROUPS[g]
    assert window // d == DL_SPAN
    L = S // d
    tq = min(tok_per_step, S) // d
    assert tq % DL_SPAN == 0 and L % tq == 0
    nh = DL_HEADS_PER_GROUP * len(DL_GROUPS)
    slopes2 = tuple(2.0 ** (-8.0 * (g * DL_HEADS_PER_GROUP + hh + 1) / nh) * d * LOG2E
                    for hh in range(DL_HEADS_PER_GROUP))
    spb = tq // DL_SPAN
    nsteps = L // tq
    cur = lambda c: pl.BlockSpec((d, tq, COL_TILE), lambda b, n: (b, n, c))
    prev = lambda c: pl.BlockSpec((d, DL_SPAN, COL_TILE), lambda b, n: (b, jnp.maximum(n * spb - 1, 0), c))
    out_spec = pl.BlockSpec((DL_HEADS_PER_GROUP, d * tq, LANES), lambda b, n: (0, b * nsteps + n, 0))
    return pl.pallas_call(
        functools.partial(_dl_kernel, slopes2=slopes2, d=d, tq=tq, ru=min(d, DL_RESIDUES_PER_TRIP)),
        grid=(B, nsteps),
        in_specs=[cur(col0), cur(col0 + 1), prev(col0 + 1), cur(col0 + 2), prev(col0 + 2)],
        out_specs=[out_spec, out_spec],
        out_shape=[jax.ShapeDtypeStruct((DL_HEADS_PER_GROUP, B * S, LANES), F32)] * 2,
        compiler_params=_params(("parallel", "arbitrary")),
    )(src, src, src, src, src)


def _route(logits):
    lane = lax.broadcasted_iota(jnp.int32, logits.shape, 1)
    big = jnp.int32(1 << 20)
    is_g = (lane >= MOE_N_EXPERTS) & (lane < MOE_N_EXPERTS + MOE_GROUPS)
    lg = jnp.where(is_g, logits, -jnp.inf)
    gmax = jnp.max(lg, axis=-1, keepdims=True)
    gsum = jnp.sum(jnp.exp(lg - gmax), axis=-1, keepdims=True)
    g_w = 1.0 / gsum
    g_idx = jnp.min(jnp.where(lg == gmax, lane - MOE_N_EXPERTS, big), axis=-1, keepdims=True)
    in_grp = (lane < MOE_N_EXPERTS) & ((lane // MOE_EXPERTS_PER_GROUP) == g_idx)
    le = jnp.where(in_grp, logits, -jnp.inf)
    t1 = jnp.max(le, axis=-1, keepdims=True)
    e1 = jnp.min(jnp.where(le == t1, lane, big), axis=-1, keepdims=True)
    le2 = jnp.where(lane == e1, -jnp.inf, le)
    t2 = jnp.max(le2, axis=-1, keepdims=True)
    e2 = jnp.min(jnp.where(le2 == t2, lane, big), axis=-1, keepdims=True)
    r = jnp.exp(t2 - t1)
    w1 = g_w / (1.0 + r)
    w2 = w1 * r
    out = jnp.where(lane == 0, e1.astype(F32),
                    jnp.where(lane == 1, e2.astype(F32),
                              jnp.where(lane == 2, w1, jnp.where(lane == 3, w2, 0.0))))
    return out


def _outproj_kernel(x_ref, oa_ref, ga_ref, gb_ref, o0_ref, o1_ref, o2_ref, l0_ref, l1_ref, l2_ref,
                    wa_ref, wb_ref, wo_ref, gf_ref, rh_ref, rc_ref,
                    x1_ref, hn_ref, rt_ref):
    obs = []
    for hh in range(DL_HEADS_PER_GROUP):
        l0, l1, l2 = l0_ref[hh], l1_ref[hh], l2_ref[hh]
        lm = jnp.maximum(jnp.maximum(l0, l1), l2)
        e0, e1, e2 = jnp.exp2(l0 - lm), jnp.exp2(l1 - lm), jnp.exp2(l2 - lm)
        obs.append((e0 * o0_ref[hh] + e1 * o1_ref[hh] + e2 * o2_ref[hh]) / (e0 + e1 + e2))
    ob = jnp.concatenate(obs, axis=1)
    a = _dot(oa_ref[...], wa_ref[...])
    b = _dot(ob.astype(BF16), wb_ref[...])
    mixed = jax.nn.sigmoid(ga_ref[...].astype(F32)) * a + jax.nn.sigmoid(gb_ref[...].astype(F32)) * b
    x1 = x_ref[...] + _dot(mixed.astype(BF16), wo_ref[...])
    x1_ref[...] = x1
    ms = jnp.mean(x1 * x1, axis=-1, keepdims=True)
    hn = x1 * lax.rsqrt(ms + EPS) * gf_ref[...]
    hn_hi = hn.astype(BF16)
    hn_ref[...] = _pack_bf16_pairs(hn)
    hn_lo = (hn - hn_hi.astype(F32)).astype(BF16)
    t = _dot(hn_hi, rc_ref[...])
    logits = t[:, 0:LANES] + (_dot(hn_lo, rh_ref[...]) + t[:, LANES:2 * LANES])
    rt_ref[...] = _route(logits)


def _outproj(x2, o_a, proj, dl_o, dl_lse, wa, wb, wo, gain_ffn, r_hi, r_cat, tm=OUTPROJ_ROWS):
    T, D = x2.shape
    row = lambda w: pl.BlockSpec((tm, w), lambda i: (i, 0))
    full = lambda s: pl.BlockSpec(s, lambda i: (0, 0), pipeline_mode=pl.Buffered(1))
    hrow = pl.BlockSpec((DL_HEADS_PER_GROUP, tm, LANES), lambda i: (0, i, 0))
    return pl.pallas_call(
        _outproj_kernel,
        grid=(T // tm,),
        in_specs=[
            row(D), row(o_a.shape[1]),
            pl.BlockSpec((tm, D), lambda i: (i, (CT_GATE_A * COL_TILE) // D)),
            pl.BlockSpec((tm, D), lambda i: (i, (CT_GATE_B * COL_TILE) // D)),
            hrow, hrow, hrow, hrow, hrow, hrow,
            full(wa.shape), full(wb.shape), full(wo.shape), full((1, D)), full(r_hi.shape), full(r_cat.shape),
        ],
        out_specs=[row(D), row(D // 2), row(LANES)],
        out_shape=[jax.ShapeDtypeStruct((T, D), F32), jax.ShapeDtypeStruct((T, D // 2), jnp.uint32),
                   jax.ShapeDtypeStruct((T, LANES), F32)],
        compiler_params=_params(("parallel",)),
    )(x2, o_a, proj, proj, dl_o[0], dl_o[1], dl_o[2], dl_lse[0], dl_lse[1], dl_lse[2],
      wa, wb, wo, gain_ffn, r_hi, r_cat)


def _pack_bf16_pairs(v):
    h = v.shape[1] // 2
    bits = pltpu.bitcast(v.astype(BF16).astype(F32), jnp.uint32)
    return (bits[:, :h] >> 16) | (bits[:, h:] & jnp.uint32(0xFFFF0000))


def _unpack_bf16_pairs(words):
    return pltpu.bitcast(words << 16, F32), pltpu.bitcast(words & jnp.uint32(0xFFFF0000), F32)


def _cast_rows(src_ref, dst_ref, chunk=CAST_ROWS):
    def body(c, carry):
        r0 = pl.multiple_of(c * chunk, chunk)
        dst_ref[pl.ds(r0, chunk), :] = src_ref[pl.ds(r0, chunk), :].astype(dst_ref.dtype)
        return carry
    lax.fori_loop(0, src_ref.shape[0] // chunk, body, 0)


def _expert_changed(be_ref, i):
    return (i == 0) | (be_ref[i] != be_ref[jnp.maximum(i - 1, 0)])


def _stream_expert_weights(i, live, be_ref, ne_ref, w_hbm, stage, wbf, wsem):
    changed = live & _expert_changed(be_ref, i)

    def copy(e):
        return pltpu.make_async_copy(w_hbm.at[e], stage, wsem)

    @pl.when(changed & (i == 0))
    def _():
        copy(be_ref[0]).start(priority=WEIGHT_DMA_PRIORITY)

    @pl.when(changed)
    def _():
        copy(be_ref[i]).wait()
        _cast_rows(stage, wbf)

    @pl.when(changed & (ne_ref[i] >= 0))
    def _():
        copy(ne_ref[i]).start(priority=WEIGHT_DMA_PRIORITY)


def _moe_up_kernel(be_ref, ne_ref, nu_ref, tc_ref, tn_ref, hn_ref, wgu_ref, act_ref,
                   xa, xb, wstage, wbf, gsem, wsem, *, d_ff):
    i = pl.program_id(0)
    nu = nu_ref[0]
    rows = xa.shape[0]
    even = i % 2 == 0

    def gather(tok_ref, r, buf, s):
        return pltpu.make_async_copy(hn_ref.at[tok_ref[0, r]], buf.at[r], gsem.at[s])

    def wait_gather(buf, s):
        pltpu.make_async_copy(hn_ref.at[pl.ds(0, rows)], buf, gsem.at[s]).wait()

    @pl.when((i == 0) & (nu > 0))
    def _():
        def body(r, c):
            gather(tc_ref, r, xa, 0).start()
            return c
        lax.fori_loop(0, rows, body, 0)

    _stream_expert_weights(i, i < nu, be_ref, ne_ref, wgu_ref, wstage, wbf, wsem)

    def live_step(cur, nxt, s):
        wait_gather(cur, s)
        for r in range(rows):
            gather(tn_ref, r, nxt, 1 - s).start(priority=r % 2)
        x_lo, x_hi = _unpack_bf16_pairs(cur[...])
        half = x_lo.shape[1]
        h = _dot(x_lo.astype(BF16), wbf[0:half, :]) + _dot(x_hi.astype(BF16), wbf[half:2 * half, :])
        gate = h[:, :d_ff]
        up = h[:, d_ff:]
        act_ref[...] = (gate * jax.nn.sigmoid(gate) * up).astype(act_ref.dtype)

    @pl.when((i < nu) & even)
    def _():
        live_step(xa, xb, 0)

    @pl.when((i < nu) & jnp.logical_not(even))
    def _():
        live_step(xb, xa, 1)

    @pl.when((i == nu) & (nu > 0) & even)
    def _():
        wait_gather(xa, 0)

    @pl.when((i == nu) & (nu > 0) & jnp.logical_not(even))
    def _():
        wait_gather(xb, 1)

    @pl.when(i >= nu)
    def _():
        act_ref[...] = jnp.zeros(act_ref.shape, act_ref.dtype)


def _moe_down_kernel(be_ref, ne_ref, nu_ref, dp_ref, act_ref, wd_ref, y_ref, ya, yb, wstage, wbf, ssem, wsem):
    i = pl.program_id(0)
    nu = nu_ref[0]
    rows = ya.shape[0]
    even = i % 2 == 0

    def scatter(r, buf, s):
        return pltpu.make_async_copy(buf.at[r], y_ref.at[dp_ref[0, r]], ssem.at[s])

    def wait_scatter(buf, s):
        pltpu.make_async_copy(buf, y_ref.at[pl.ds(0, rows)], ssem.at[s]).wait()

    @pl.when(i == 0)
    def _():
        n_res = y_ref.shape[0] - 2 * rows
        for s, buf in enumerate((ya, yb)):
            buf[...] = jnp.zeros(buf.shape, buf.dtype)
            pltpu.make_async_copy(buf, y_ref.at[pl.ds(n_res + s * rows, rows)], ssem.at[s]).start()
        for s, buf in enumerate((ya, yb)):
            pltpu.make_async_copy(buf, y_ref.at[pl.ds(n_res + s * rows, rows)], ssem.at[s]).wait()

    @pl.when((i >= 2) & (i < nu + 2) & even)
    def _():
        wait_scatter(ya, 0)

    @pl.when((i >= 2) & (i < nu + 2) & jnp.logical_not(even))
    def _():
        wait_scatter(yb, 1)

    _stream_expert_weights(i, i < nu, be_ref, ne_ref, wd_ref, wstage, wbf, wsem)

    def step(cur, prv, s, do_scatter, do_compute):
        if do_scatter:
            for r in range(rows):
                scatter(r, prv, 1 - s).start()
        if do_compute:
            cur[...] = _pack_bf16_pairs(_dot(act_ref[...], wbf[...]))

    for s, (cur, prv) in enumerate(((ya, yb), (yb, ya))):
        par = even if s == 0 else jnp.logical_not(even)

        @pl.when((i >= 1) & (i < nu) & par)
        def _(cur=cur, prv=prv, s=s):
            step(cur, prv, s, True, True)

        @pl.when((i == 0) & (nu > 0) & par)
        def _(cur=cur, prv=prv, s=s):
            step(cur, prv, s, False, True)

        @pl.when((i == nu) & (nu > 0) & par)
        def _(cur=cur, prv=prv, s=s):
            step(cur, prv, s, True, False)


def _experts(hn, src_tok, dst_row, blk_expert, nxt_expert, n_used, wgu, wd, n_out_rows):
    D = wgu.shape[1]
    nblk = src_tok.shape[0] // MOE_BLOCK
    d_ff = wd.shape[1]
    tok = src_tok.reshape(nblk, 1, MOE_BLOCK)
    dst = dst_row.reshape(nblk, 1, MOE_BLOCK)

    def live(i, nu):
        return jnp.maximum(jnp.minimum(i, nu[0] - 1), 0)

    smem = lambda f: pl.BlockSpec((None, 1, MOE_BLOCK), f, memory_space=pltpu.SMEM)
    act = pl.pallas_call(
        functools.partial(_moe_up_kernel, d_ff=d_ff),
        grid_spec=pltpu.PrefetchScalarGridSpec(
            num_scalar_prefetch=3,
            grid=(nblk - 1,),
            in_specs=[
                smem(lambda i, be, ne, nu: (i, 0, 0)),
                smem(lambda i, be, ne, nu: (i + 1, 0, 0)),
                pl.BlockSpec(memory_space=pl.ANY),
                pl.BlockSpec(memory_space=pl.ANY),
            ],
            out_specs=pl.BlockSpec((MOE_BLOCK, d_ff), lambda i, be, ne, nu: (i, 0)),
            scratch_shapes=[pltpu.VMEM((MOE_BLOCK, D // 2), jnp.uint32), pltpu.VMEM((MOE_BLOCK, D // 2), jnp.uint32),
                            pltpu.VMEM((D, 2 * d_ff), F32), pltpu.VMEM((D, 2 * d_ff), BF16),
                            pltpu.SemaphoreType.DMA((2,)), pltpu.SemaphoreType.DMA(())],
        ),
        out_shape=jax.ShapeDtypeStruct(((nblk - 1) * MOE_BLOCK, d_ff), BF16),
        compiler_params=_params(("arbitrary",)),
    )(blk_expert, nxt_expert, n_used, tok, tok, hn, wgu)
    return pl.pallas_call(
        _moe_down_kernel,
        grid_spec=pltpu.PrefetchScalarGridSpec(
            num_scalar_prefetch=3,
            grid=(nblk,),
            in_specs=[
                smem(lambda i, be, ne, nu: (jnp.maximum(i - 1, 0), 0, 0)),
                pl.BlockSpec((MOE_BLOCK, d_ff), lambda i, be, ne, nu: (live(i, nu), 0)),
                pl.BlockSpec(memory_space=pl.ANY),
            ],
            out_specs=pl.BlockSpec(memory_space=pl.ANY),
            scratch_shapes=[pltpu.VMEM((MOE_BLOCK, D // 2), jnp.uint32), pltpu.VMEM((MOE_BLOCK, D // 2), jnp.uint32),
                            pltpu.VMEM((d_ff, D), F32), pltpu.VMEM((d_ff, D), BF16),
                            pltpu.SemaphoreType.DMA((2,)), pltpu.SemaphoreType.DMA(())],
        ),
        out_shape=jax.ShapeDtypeStruct((n_out_rows, D // 2), jnp.uint32),
        compiler_params=_params(("arbitrary",)),
    )(blk_expert, nxt_expert, n_used, dst, act, wd)


def _combine_kernel(x1_ref, rt_ref, y1_ref, y2_ref, o_ref):
    rt = rt_ref[...]
    w1, w2 = rt[:, 2:3], rt[:, 3:4]
    half = o_ref.shape[1] // 2
    y1 = _unpack_bf16_pairs(y1_ref[...])
    y2 = _unpack_bf16_pairs(y2_ref[...])
    for p in range(2):
        cs = slice(p * half, (p + 1) * half)
        o_ref[:, cs] = x1_ref[:, cs] + (w1 * y1[p] + w2 * y2[p])


def _combine(x1, route, y, tm=COMBINE_ROWS):
    T, D = x1.shape
    nt = T // tm
    return pl.pallas_call(
        _combine_kernel,
        grid=(nt,),
        in_specs=[
            pl.BlockSpec((tm, D), lambda i: (i, 0)),
            pl.BlockSpec((tm, LANES), lambda i: (i, 0)),
            pl.BlockSpec((tm, D // 2), lambda i: (i, 0)),
            pl.BlockSpec((tm, D // 2), lambda i: (nt + i, 0)),
        ],
        out_specs=pl.BlockSpec((tm, D), lambda i: (i, 0)),
        out_shape=jax.ShapeDtypeStruct((T, D), F32),
        compiler_params=_params(("parallel",)),
    )(x1, route, y, y)


def _dispatch_tables(route, T):
    TK = T * MOE_TOP_K
    flat_e = route[:, :MOE_TOP_K].astype(jnp.int32).reshape(-1)
    order = jnp.argsort(flat_e).astype(jnp.int32)
    sizes = jnp.sum(flat_e[:, None] == jnp.arange(MOE_N_EXPERTS, dtype=jnp.int32)[None, :], axis=0,
                    dtype=jnp.int32)
    start = jnp.cumsum(sizes) - sizes
    padded = ((sizes + MOE_BLOCK - 1) // MOE_BLOCK) * MOE_BLOCK
    pad_end = jnp.cumsum(padded)
    pad_start = pad_end - padded
    n_blocks = TK // MOE_BLOCK + MOE_N_EXPERTS
    blk = jnp.arange(n_blocks + 2, dtype=jnp.int32)
    blk_expert = jnp.minimum(jnp.sum(pad_end[None, :] <= (blk * MOE_BLOCK)[:, None], axis=1, dtype=jnp.int32),
                             MOE_N_EXPERTS - 1)
    r = jnp.arange(MOE_BLOCK, dtype=jnp.int32)[None, :]
    pos = blk[:, None] * MOE_BLOCK + r
    off = pos - pad_start[blk_expert][:, None]
    live = (off < sizes[blk_expert][:, None]) & (pos < pad_end[-1])
    sorted_idx = jnp.where(live, off + start[blk_expert][:, None], 0)
    slot = order[sorted_idx]
    tok = slot // MOE_TOP_K
    src_tok = jnp.where(live, tok, 0).reshape(-1)
    trash = TK + (blk[:, None] % 2) * MOE_BLOCK + r
    dst_row = jnp.where(live, (slot % MOE_TOP_K) * T + tok, trash).reshape(-1)
    n_used = (pad_end[-1:] // MOE_BLOCK).astype(jnp.int32)
    eid = jnp.arange(MOE_N_EXPERTS, dtype=jnp.int32)
    later = (eid[None, :] > eid[:, None]) & (sizes[None, :] > 0)
    nxt = jnp.min(jnp.where(later, eid[None, :], MOE_N_EXPERTS), axis=1)
    nxt_expert = jnp.where(nxt < MOE_N_EXPERTS, nxt, -1).astype(jnp.int32)[blk_expert]
    return src_tok, dst_row, blk_expert, nxt_expert, n_used, TK + 2 * MOE_BLOCK


def _w_in_tile_order(d_gate):
    da = DA_HEADS * 2 * DA_QK_DIM
    dav = DA_HEADS * DA_V_DIM
    dl = len(DL_GROUPS) * DL_HEADS_PER_GROUP * DL_HEAD_DIM
    o = [int(v) // COL_TILE for v in np.cumsum([0, da, da, dav, dl, dl, dl, d_gate, d_gate])]
    order = list(range(o[6], o[8])) + list(range(o[0], o[3]))
    for g in range(len(DL_GROUPS)):
        order += [o[3] + g, o[4] + g, o[5] + g]
    assert sorted(order) == list(range(CT_END))
    return jnp.asarray(order, jnp.int32)


def _gain_table(da_q_norm, da_k_norm, dl_q_norm, dl_k_norm):
    ones = jnp.ones((COL_TILE,), F32)
    daq = jnp.tile(da_q_norm, COL_TILE // DA_QK_DIM) * (DA_QK_DIM ** -0.5 * LOG2E)
    dak = jnp.tile(da_k_norm, COL_TILE // DA_QK_DIM)
    dlq = jnp.tile(dl_q_norm, COL_TILE // DL_HEAD_DIM) * (DL_HEAD_DIM ** -0.5 * LOG2E)
    dlk = jnp.tile(dl_k_norm, COL_TILE // DL_HEAD_DIM)
    rows = []
    for j in range(CT_END):
        if CT_DA_Q <= j < CT_DA_K:
            rows.append(daq)
        elif CT_DA_K <= j < CT_DA_V:
            rows.append(dak)
        elif j >= CT_DL and (j - CT_DL) % 3 == 0:
            rows.append(dlq)
        elif j >= CT_DL and (j - CT_DL) % 3 == 1:
            rows.append(dlk)
        else:
            rows.append(ones)
    return jnp.stack(rows, axis=0).reshape(CT_END, 1, COL_TILE)


def kernel(x, norm_mix, w_in, da_q_norm, da_k_norm, da_lambda_q, da_lambda_k, da_sub_norm,
           dl_q_norm, dl_k_norm, w_branch_a, w_branch_b, w_out, norm_ffn,
           w_group_router, w_expert_router, w_gate_up, w_down):
    B, S, D = x.shape
    T = B * S
    depth = w_in.shape[0]
    x2 = x.reshape(T, D)
    for l in range(depth):
        lam_init = 0.8 - 0.6 * math.exp(-0.3 * l)
        gain_tab = _gain_table(da_q_norm[l], da_k_norm[l], dl_q_norm[l], dl_k_norm[l])
        proj, dl1, dl2 = _inproj(x2, norm_mix[l].reshape(1, D), w_in[l].astype(BF16), _w_in_tile_order(D),
                                 gain_tab, B, S)

        o_a = _diff_attention(proj, da_lambda_q[l], da_lambda_k[l], da_sub_norm[l].reshape(1, DA_V_DIM),
                              B, S, lam_init)
        dl = [_dilated_group(proj.reshape(B, S, proj.shape[1]), CT_DL, 0, B, S),
              _dilated_group(dl1, 0, 1, B, S), _dilated_group(dl2, 0, 2, B, S)]

        w_r = jnp.concatenate([w_expert_router[l], w_group_router[l]], axis=1)
        w_r = jnp.pad(w_r, ((0, 0), (0, LANES - w_r.shape[1])))
        r_hi = w_r.astype(BF16)
        r_lo = (w_r - r_hi.astype(F32)).astype(BF16)
        x1, hn, route = _outproj(
            x2, o_a, proj, [t[0] for t in dl], [t[1] for t in dl],
            w_branch_a[l].astype(BF16), w_branch_b[l].astype(BF16), w_out[l].astype(BF16),
            norm_ffn[l].reshape(1, D), r_hi, jnp.concatenate([r_hi, r_lo], axis=1))

        src_tok, dst_row, blk_expert, nxt_expert, n_used, n_rows = _dispatch_tables(route, T)
        y = _experts(hn, src_tok, dst_row, blk_expert, nxt_expert, n_used, w_gate_up[l], w_down[l], n_rows)
        x2 = _combine(x1, route, y)
    return x2.reshape(B, S, D)
```

```python
import functools
import math

import jax
import jax.numpy as jnp
import numpy as np
from jax import lax
from jax.experimental import pallas as pl
from jax.experimental.pallas import tpu as pltpu

F32 = jnp.float32
BF16 = jnp.bfloat16

EPS = 1e-6
LOG2E = 1.4426950408889634
NEG_BIG = -1e30

DA_HEADS = 8
DA_QK_DIM = 64
DA_V_DIM = 128
DA_TILE_GROUP = 4
DL_GROUPS = ((128, 1), (512, 4), (2048, 16))
DL_HEADS_PER_GROUP = 4
DL_HEAD_DIM = 128
DL_SPAN = 128
MOE_GROUPS = 4
MOE_EXPERTS_PER_GROUP = 8
MOE_N_EXPERTS = 32
MOE_TOP_K = 2
MOE_BLOCK = 256
WEIGHT_DMA_PRIORITY = 1

LANES = 128
COL_TILE = 512
VMEM_LIMIT = 56 * 1024 * 1024

INPROJ_ROWS = 1024
DA_TILE = 512
DA_ROW_CHUNK = 32
DL_TOKENS_PER_STEP = 2048
DL_RESIDUES_PER_TRIP = 4
DL_LOOKAHEAD = 4
OUTPROJ_ROWS = 256
COMBINE_ROWS = 512
CAST_ROWS = 256

CT_GATE_A, CT_GATE_B, CT_DA_Q, CT_DA_K, CT_DA_V, CT_DL, CT_MAIN_END, CT_END = 0, 4, 8, 10, 12, 14, 17, 23


def _params(sem, vmem=VMEM_LIMIT):
    return pltpu.CompilerParams(dimension_semantics=sem, vmem_limit_bytes=vmem)


def _dot(a, b):
    return jnp.dot(a, b, preferred_element_type=F32)


def _dot_nt(a, b):
    return lax.dot_general(a, b, (((1,), (1,)), ((), ())), preferred_element_type=F32)


def _inproj_kernel(perm_ref, x_ref, g_ref, w_ref, gain_ref, o_ref, d1_ref, d2_ref, h_scr, y_scr):
    j = pl.program_id(1)

    @pl.when(j == 0)
    def _():
        x = x_ref[...]
        ms = jnp.mean(x * x, axis=-1, keepdims=True)
        h_scr[...] = (x * lax.rsqrt(ms + EPS) * g_ref[...]).astype(BF16)

    gain = gain_ref[...]
    half = COL_TILE // 2

    def head_slices():
        ys = [_dot(h_scr[...], w_ref[:, hf * half:(hf + 1) * half]) for hf in range(2)]
        for hf in range(2):
            for hh in range(half // LANES):
                yield hf * (half // LANES) + hh, ys[hf][:, hh * LANES:(hh + 1) * LANES]

    is64 = (j >= CT_DA_Q) & (j < CT_DA_V)
    is128 = (j >= CT_DL) & (lax.rem(j - CT_DL, 3) < 2)
    main = j < CT_MAIN_END
    plain = jnp.logical_not(is64 | is128)

    def norm64(h, yh):
        sq = yh * yh
        lo = lax.broadcasted_iota(jnp.int32, yh.shape, 1) < DA_QK_DIM
        s_lo = jnp.sum(jnp.where(lo, sq, 0.0), axis=-1, keepdims=True)
        s_hi = jnp.sum(jnp.where(lo, 0.0, sq), axis=-1, keepdims=True)
        r = jnp.where(lo, lax.rsqrt(s_lo * (1.0 / DA_QK_DIM) + EPS), lax.rsqrt(s_hi * (1.0 / DA_QK_DIM) + EPS))
        return yh * r * gain[:, h * LANES:(h + 1) * LANES]

    def norm128(h, yh):
        ss = jnp.sum(yh * yh, axis=-1, keepdims=True)
        return yh * lax.rsqrt(ss * (1.0 / DL_HEAD_DIM) + EPS) * gain[:, h * LANES:(h + 1) * LANES]

    def emit(cond, fn, to_main):
        @pl.when(cond)
        def _():
            for h, yh in head_slices():
                if to_main:
                    o_ref[:, h * LANES:(h + 1) * LANES] = fn(h, yh).astype(o_ref.dtype)
                else:
                    y_scr[h] = fn(h, yh)

    emit(is64, norm64, True)
    emit(is128 & main, norm128, True)
    emit(is128 & jnp.logical_not(main), norm128, False)
    emit(plain & main, lambda h, yh: yh, True)
    emit(plain & jnp.logical_not(main), lambda h, yh: yh, False)

    def deinterleave(dst_ref):
        d, rows = dst_ref.shape[0], dst_ref.shape[1]
        for r in range(d):
            for h in range(COL_TILE // LANES):
                dst_ref[r, :, h * LANES:(h + 1) * LANES] = (
                    y_scr[h, pl.ds(r, rows, stride=d), :].astype(dst_ref.dtype))

    @pl.when((j >= CT_MAIN_END) & (j < CT_MAIN_END + 3))
    def _():
        deinterleave(d1_ref)

    @pl.when(j >= CT_MAIN_END + 3)
    def _():
        deinterleave(d2_ref)


def _inproj(x2, gain_mix, w_bf, tile_perm, gain_tab, B, S, tm=INPROJ_ROWS):
    T, D = x2.shape
    tiles_per_batch = S // tm
    d1, d2 = DL_GROUPS[1][1], DL_GROUPS[2][1]
    part1 = lambda j: jnp.clip(j - CT_MAIN_END, 0, 2)
    part2 = lambda j: jnp.clip(j - CT_MAIN_END - 3, 0, 2)
    grid_spec = pltpu.PrefetchScalarGridSpec(
        num_scalar_prefetch=1,
        grid=(T // tm, CT_END),
        in_specs=[
            pl.BlockSpec((tm, D), lambda i, j, perm: (i, 0)),
            pl.BlockSpec((1, D), lambda i, j, perm: (0, 0)),
            pl.BlockSpec((D, COL_TILE), lambda i, j, perm: (0, perm[j])),
            pl.BlockSpec((None, 1, COL_TILE), lambda i, j, perm: (j, 0, 0)),
        ],
        out_specs=[
            pl.BlockSpec((tm, COL_TILE), lambda i, j, perm: (i, jnp.minimum(j, CT_MAIN_END - 1))),
            pl.BlockSpec((d1, tm // d1, COL_TILE),
                         lambda i, j, perm: (i // tiles_per_batch, i % tiles_per_batch, part1(j))),
            pl.BlockSpec((d2, tm // d2, COL_TILE),
                         lambda i, j, perm: (i // tiles_per_batch, i % tiles_per_batch, part2(j))),
        ],
        scratch_shapes=[pltpu.VMEM((tm, D), BF16), pltpu.VMEM((COL_TILE // LANES, tm, LANES), F32)],
    )
    return pl.pallas_call(
        _inproj_kernel,
        grid_spec=grid_spec,
        out_shape=[
            jax.ShapeDtypeStruct((T, CT_MAIN_END * COL_TILE), BF16),
            jax.ShapeDtypeStruct((B * d1, S // d1, 3 * COL_TILE), BF16),
            jax.ShapeDtypeStruct((B * d2, S // d2, 3 * COL_TILE), BF16),
        ],
        compiler_params=_params(("parallel", "arbitrary")),
    )(tile_perm, x2, gain_mix, w_bf, gain_tab)


def _bf16_pieces(x, n=3):
    out = []
    r = np.float64(x)
    for _ in range(n):
        p = np.asarray(np.float32(r)).astype(jnp.bfloat16).astype(np.float64)
        out.append(float(p))
        r = r - p
    return out


def _alibi_features(tk):
    pieces = _bf16_pieces(LOG2E)
    qf = np.zeros((2, LANES), np.float32)
    kf = np.zeros((2, tk, LANES), np.float32)
    j = np.arange(tk)
    hi, lo = (j // 16) * 16, j % 16
    for m in range(2):
        f0 = DA_QK_DIM if m == 0 else 0
        for n, p in enumerate(pieces):
            qf[m, f0 + 2 * n] = p
            qf[m, f0 + 2 * n + 1] = p
            kf[m, :, f0 + 2 * n] = hi
            kf[m, :, f0 + 2 * n + 1] = lo
    return jnp.asarray(qf), jnp.asarray(kf, dtype=BF16)


def _da_kernel(q_ref, k_ref, v_ref, qf_ref, kf_ref, lq_ref, lk_ref, sg_ref, o_ref,
               s00, s01, s10, s11, p00, p01, p10, p11, pd0, pd1,
               m0_scr, m1_scr, l0_scr, l1_scr, a0_scr, a1_scr, acc0_scr, acc1_scr, *, tq, rc, lam_init):
    h = pl.program_id(1)
    qi = pl.program_id(2)
    nlb = tq // LANES
    pow2 = jnp.exp2(-(h + 1).astype(F32))
    slope2 = pow2 * LOG2E

    q = q_ref[...]
    lane = lax.broadcasted_iota(jnp.int32, (tq, LANES), 1)
    own = (lane < DA_QK_DIM, lane >= DA_QK_DIM)
    qfs = [jnp.where(own[mi], q, jnp.broadcast_to((qf_ref[mi:mi + 1, :] * pow2).astype(BF16), q.shape))
           for mi in range(2)]

    m_scrs, l_scrs, a_scrs, acc_scrs = (m0_scr, m1_scr), (l0_scr, l1_scr), (a0_scr, a1_scr), (acc0_scr, acc1_scr)
    for mi in range(2):
        m_scrs[mi][...] = jnp.full(m_scrs[mi].shape, NEG_BIG, F32)
        l_scrs[mi][...] = jnp.zeros(l_scrs[mi].shape, F32)
        acc_scrs[mi][...] = jnp.zeros(acc_scrs[mi].shape, F32)

    def scores(ki, mi, s_ref):
        k = k_ref[pl.ds(pl.multiple_of(ki * tq, tq), tq), :]
        s_ref[...] = _dot_nt(qfs[mi], jnp.where(own[mi], k, kf_ref[mi]))

    def softmax(ki, mi, s_ref, p_ref, masked):
        m_scr, l_scr, a_scr = m_scrs[mi], l_scrs[mi], a_scrs[mi]
        c = slope2 * ((ki - qi) * tq).astype(F32)
        for r in range(tq // rc):
            rows = slice(r * rc, (r + 1) * rc)
            nb = min(nlb, ((r + 1) * rc - 1) // LANES + 1) if masked else nlb
            sb = []
            for j in range(nb):
                cs = slice(j * LANES, (j + 1) * LANES)
                s = s_ref[rows, cs]
                if masked and (j + 1) * LANES - 1 > r * rc:
                    rr = lax.broadcasted_iota(jnp.int32, (rc, LANES), 0) + r * rc
                    cc = lax.broadcasted_iota(jnp.int32, (rc, LANES), 1) + j * LANES
                    s = jnp.where(cc <= rr, s, NEG_BIG)
                sb.append(s)
            mx = sb[0]
            for s in sb[1:]:
                mx = jnp.maximum(mx, s)
            m_prev = m_scr[rows, :]
            m_new = jnp.maximum(m_prev, jnp.max(mx, axis=-1, keepdims=True) + c)
            alpha = jnp.exp2(m_prev - m_new)
            a_scr[rows, :] = alpha
            m_scr[rows, :] = m_new
            mc = m_new - c
            psum = alpha * l_scr[rows, :]
            for j in range(nlb):
                cs = slice(j * LANES, (j + 1) * LANES)
                if j < nb:
                    p = jnp.exp2(sb[j] - mc)
                    psum = psum + p
                    p_ref[rows, cs] = p.astype(BF16)
                else:
                    p_ref[rows, cs] = jnp.zeros((rc, LANES), BF16)
            l_scr[rows, :] = psum

    def values(ki, mi, p_ref):
        v = v_ref[pl.ds(pl.multiple_of(ki * tq, tq), tq), :]
        acc_scrs[mi][...] = a_scrs[mi][...] * acc_scrs[mi][...] + _dot(p_ref[...], v)

    s_bufs, p_bufs = ((s00, s01), (s10, s11)), ((p00, p01), (p10, p11))

    def tile_group(k0, n, last_masked):
        for mi in range(2):
            scores(k0, mi, s_bufs[0][mi])
        for i in range(n):
            masked = last_masked and i == n - 1
            for mi in range(2):
                p_ref = (pd0, pd1)[mi] if masked else p_bufs[i % 2][mi]
                softmax(k0 + i, mi, s_bufs[i % 2][mi], p_ref, masked)
                values(k0 + i, mi, p_ref)
                if i + 1 < n:
                    scores(k0 + i + 1, mi, s_bufs[(i + 1) % 2][mi])

    def body(t, carry):
        tile_group(DA_TILE_GROUP * t, DA_TILE_GROUP, False)
        return carry

    n_full = qi // DA_TILE_GROUP
    lax.fori_loop(0, n_full, body, 0)
    for rem in range(1, DA_TILE_GROUP + 1):
        @pl.when(qi - DA_TILE_GROUP * n_full == rem - 1)
        def _(rem=rem):
            tile_group(qi - (rem - 1), rem, True)

    lam_e = jnp.exp(jnp.sum(lq_ref[...] * lk_ref[...], axis=-1, keepdims=True))
    lam = lam_e[0:1, :] - lam_e[1:2, :] + lam_init
    l0 = jnp.sum(l0_scr[...], axis=-1, keepdims=True)
    l1 = jnp.sum(l1_scr[...], axis=-1, keepdims=True)
    o = acc0_scr[...] / l0 - lam * (acc1_scr[...] / l1)
    ms = jnp.mean(o * o, axis=-1, keepdims=True)
    o = o * lax.rsqrt(ms + EPS) * sg_ref[...] * (1.0 - lam_init)
    o_ref[...] = o.astype(o_ref.dtype)


def _diff_attention(proj, lam_q, lam_k, sub_gain, B, S, lam_init, tq=DA_TILE, rc=DA_ROW_CHUNK):
    T = proj.shape[0]
    nq = S // tq
    lb = LANES
    q_blk0, k_blk0, v_blk0 = (CT_DA_Q * COL_TILE) // lb, (CT_DA_K * COL_TILE) // lb, (CT_DA_V * COL_TILE) // lb
    qfeat, kfeat = _alibi_features(tq)
    const = lambda shape: pl.BlockSpec(shape, lambda b, h, i: (0,) * len(shape))
    return pl.pallas_call(
        functools.partial(_da_kernel, tq=tq, rc=rc, lam_init=lam_init),
        grid=(B, DA_HEADS, nq),
        in_specs=[
            pl.BlockSpec((tq, lb), lambda b, h, i: (b * nq + i, q_blk0 + h)),
            pl.BlockSpec((S, lb), lambda b, h, i: (b, k_blk0 + h)),
            pl.BlockSpec((S, lb), lambda b, h, i: (b, v_blk0 + h)),
            const((2, LANES)), const((2, tq, LANES)),
            const((2, DA_QK_DIM)), const((2, DA_QK_DIM)), const((1, DA_V_DIM)),
        ],
        out_specs=pl.BlockSpec((tq, lb), lambda b, h, i: (b * nq + i, h)),
        out_shape=jax.ShapeDtypeStruct((T, DA_HEADS * DA_V_DIM), BF16),
        scratch_shapes=[pltpu.VMEM((tq, tq), F32)] * 4 + [pltpu.VMEM((tq, tq), BF16)] * 6
        + [pltpu.VMEM((tq, LANES), F32)] * 6 + [pltpu.VMEM((tq, DA_V_DIM), F32)] * 2,
        compiler_params=_params(("parallel", "parallel", "arbitrary")),
    )(proj, proj, proj, qfeat, kfeat, lam_q, lam_k, sub_gain)


def _dl_kernel(q_ref, kc_ref, kp_ref, vc_ref, vp_ref, o_ref, lse_ref, *, slopes2, d, tq, ru):
    n = pl.program_id(1)
    sp = DL_SPAN
    row = lax.broadcasted_iota(jnp.int32, (sp, sp), 0)
    col = lax.broadcasted_iota(jnp.int32, (sp, sp), 1)
    dcur = row - col
    cur_ok = dcur >= 0
    prev_ok = dcur <= 0
    dcur_f = dcur.astype(F32)

    def scores(r, hh, j):
        hs = slice(hh * LANES, (hh + 1) * LANES)
        rs = slice(j * sp, (j + 1) * sp)
        q = q_ref[r, rs, hs]
        if j == 0:
            kp, vp, p_ok = kp_ref[r, :, hs], vp_ref[r, :, hs], prev_ok & (n > 0)
        else:
            ps = slice((j - 1) * sp, j * sp)
            kp, vp, p_ok = kc_ref[r, ps, hs], vc_ref[r, ps, hs], prev_ok
        s_c = jnp.where(cur_ok, _dot_nt(q, kc_ref[r, rs, hs]) - slopes2[hh] * dcur_f, NEG_BIG)
        s_p = jnp.where(p_ok, _dot_nt(q, kp) - slopes2[hh] * (dcur_f + float(sp)), NEG_BIG)
        return s_c, s_p, vc_ref[r, rs, hs], vp

    def finish(r, hh, j, s_c, s_p, vc, vp):
        m = jnp.max(jnp.maximum(s_c, s_p), axis=-1, keepdims=True)
        p_c = jnp.exp2(s_c - m)
        p_p = jnp.exp2(s_p - m)
        den = jnp.sum(p_c + p_p, axis=-1, keepdims=True)
        acc = _dot(p_c.astype(BF16), vc) + _dot(p_p.astype(BF16), vp)
        out_rows = pl.ds(j * sp, sp) if d == 1 else pl.ds(r + j * sp * d, sp, stride=d)
        o_ref[hh, out_rows, :] = acc / den
        lse_ref[hh, out_rows, :] = jnp.broadcast_to(m + jnp.log2(den), (sp, LANES))

    def residues(t, carry):
        units = [(t * ru + rr, hh, j) for rr in range(ru) for hh in range(DL_HEADS_PER_GROUP)
                 for j in range(tq // sp)]
        pending = []
        for u in units:
            pending.append((u, scores(*u)))
            if len(pending) > DL_LOOKAHEAD:
                u0, vals = pending.pop(0)
                finish(*u0, *vals)
        for u0, vals in pending:
            finish(*u0, *vals)
        return carry

    lax.fori_loop(0, d // ru, residues, 0)


def _dilated_group(src, col0, g, B, S, tok_per_step=DL_TOKENS_PER_STEP):
    window, d = DL_GROUPS[g]
    assert window // d == DL_SPAN
    L = S // d
    tq = min(tok_per_step, S) // d
    assert tq % DL_SPAN == 0 and L % tq == 0
    nh = DL_HEADS_PER_GROUP * len(DL_GROUPS)
    slopes2 = tuple(2.0 ** (-8.0 * (g * DL_HEADS_PER_GROUP + hh + 1) / nh) * d * LOG2E
                    for hh in range(DL_HEADS_PER_GROUP))
    spb = tq // DL_SPAN
    nsteps = L // tq
    cur = lambda c: pl.BlockSpec((d, tq, COL_TILE), lambda b, n: (b, n, c))
    prev = lambda c: pl.BlockSpec((d, DL_SPAN, COL_TILE), lambda b, n: (b, jnp.maximum(n * spb - 1, 0), c))
    out_spec = pl.BlockSpec((DL_HEADS_PER_GROUP, d * tq, LANES), lambda b, n: (0, b * nsteps + n, 0))
    return pl.pallas_call(
        functools.partial(_dl_kernel, slopes2=slopes2, d=d, tq=tq, ru=min(d, DL_RESIDUES_PER_TRIP)),
        grid=(B, nsteps),
        in_specs=[cur(col0), cur(col0 + 1), prev(col0 + 1), cur(col0 + 2), prev(col0 + 2)],
        out_specs=[out_spec, out_spec],
        out_shape=[jax.ShapeDtypeStruct((DL_HEADS_PER_GROUP, B * S, LANES), F32)] * 2,
        compiler_params=_params(("parallel", "arbitrary")),
    )(src, src, src, src, src)


def _route(logits):
    lane = lax.broadcasted_iota(jnp.int32, logits.shape, 1)
    big = jnp.int32(1 << 20)
    is_g = (lane >= MOE_N_EXPERTS) & (lane < MOE_N_EXPERTS + MOE_GROUPS)
    lg = jnp.where(is_g, logits, -jnp.inf)
    gmax = jnp.max(lg, axis=-1, keepdims=True)
    gsum = jnp.sum(jnp.exp(lg - gmax), axis=-1, keepdims=True)
    g_w = 1.0 / gsum
    g_idx = jnp.min(jnp.where(lg == gmax, lane - MOE_N_EXPERTS, big), axis=-1, keepdims=True)
    in_grp = (lane < MOE_N_EXPERTS) & ((lane // MOE_EXPERTS_PER_GROUP) == g_idx)
    le = jnp.where(in_grp, logits, -jnp.inf)
    t1 = jnp.max(le, axis=-1, keepdims=True)
    e1 = jnp.min(jnp.where(le == t1, lane, big), axis=-1, keepdims=True)
    le2 = jnp.where(lane == e1, -jnp.inf, le)
    t2 = jnp.max(le2, axis=-1, keepdims=True)
    e2 = jnp.min(jnp.where(le2 == t2, lane, big), axis=-1, keepdims=True)
    r = jnp.exp(t2 - t1)
    w1 = g_w / (1.0 + r)
    w2 = w1 * r
    out = jnp.where(lane == 0, e1.astype(F32),
                    jnp.where(lane == 1, e2.astype(F32),
                              jnp.where(lane == 2, w1, jnp.where(lane == 3, w2, 0.0))))
    return out


def _outproj_kernel(x_ref, oa_ref, ga_ref, gb_ref, o0_ref, o1_ref, o2_ref, l0_ref, l1_ref, l2_ref,
                    wa_ref, wb_ref, wo_ref, gf_ref, rh_ref, rc_ref,
                    x1_ref, hn_ref, rt_ref):
    obs = []
    for hh in range(DL_HEADS_PER_GROUP):
        l0, l1, l2 = l0_ref[hh], l1_ref[hh], l2_ref[hh]
        lm = jnp.maximum(jnp.maximum(l0, l1), l2)
        e0, e1, e2 = jnp.exp2(l0 - lm), jnp.exp2(l1 - lm), jnp.exp2(l2 - lm)
        obs.append((e0 * o0_ref[hh] + e1 * o1_ref[hh] + e2 * o2_ref[hh]) / (e0 + e1 + e2))
    ob = jnp.concatenate(obs, axis=1)
    a = _dot(oa_ref[...], wa_ref[...])
    b = _dot(ob.astype(BF16), wb_ref[...])
    mixed = jax.nn.sigmoid(ga_ref[...].astype(F32)) * a + jax.nn.sigmoid(gb_ref[...].astype(F32)) * b
    x1 = x_ref[...] + _dot(mixed.astype(BF16), wo_ref[...])
    x1_ref[...] = x1
    ms = jnp.mean(x1 * x1, axis=-1, keepdims=True)
    hn = x1 * lax.rsqrt(ms + EPS) * gf_ref[...]
    hn_hi = hn.astype(BF16)
    hn_ref[...] = _pack_bf16_pairs(hn)
    hn_lo = (hn - hn_hi.astype(F32)).astype(BF16)
    t = _dot(hn_hi, rc_ref[...])
    logits = t[:, 0:LANES] + (_dot(hn_lo, rh_ref[...]) + t[:, LANES:2 * LANES])
    rt_ref[...] = _route(logits)


def _outproj(x2, o_a, proj, dl_o, dl_lse, wa, wb, wo, gain_ffn, r_hi, r_cat, tm=OUTPROJ_ROWS):
    T, D = x2.shape
    row = lambda w: pl.BlockSpec((tm, w), lambda i: (i, 0))
    full = lambda s: pl.BlockSpec(s, lambda i: (0, 0), pipeline_mode=pl.Buffered(1))
    hrow = pl.BlockSpec((DL_HEADS_PER_GROUP, tm, LANES), lambda i: (0, i, 0))
    return pl.pallas_call(
        _outproj_kernel,
        grid=(T // tm,),
        in_specs=[
            row(D), row(o_a.shape[1]),
            pl.BlockSpec((tm, D), lambda i: (i, (CT_GATE_A * COL_TILE) // D)),
            pl.BlockSpec((tm, D), lambda i: (i, (CT_GATE_B * COL_TILE) // D)),
            hrow, hrow, hrow, hrow, hrow, hrow,
            full(wa.shape), full(wb.shape), full(wo.shape), full((1, D)), full(r_hi.shape), full(r_cat.shape),
        ],
        out_specs=[row(D), row(D // 2), row(LANES)],
        out_shape=[jax.ShapeDtypeStruct((T, D), F32), jax.ShapeDtypeStruct((T, D // 2), jnp.uint32),
                   jax.ShapeDtypeStruct((T, LANES), F32)],
        compiler_params=_params(("parallel",)),
    )(x2, o_a, proj, proj, dl_o[0], dl_o[1], dl_o[2], dl_lse[0], dl_lse[1], dl_lse[2],
      wa, wb, wo, gain_ffn, r_hi, r_cat)


def _pack_bf16_pairs(v):
    h = v.shape[1] // 2
    bits = pltpu.bitcast(v.astype(BF16).astype(F32), jnp.uint32)
    return (bits[:, :h] >> 16) | (bits[:, h:] & jnp.uint32(0xFFFF0000))


def _unpack_bf16_pairs(words):
    return pltpu.bitcast(words << 16, F32), pltpu.bitcast(words & jnp.uint32(0xFFFF0000), F32)


def _cast_rows(src_ref, dst_ref, chunk=CAST_ROWS):
    def body(c, carry):
        r0 = pl.multiple_of(c * chunk, chunk)
        dst_ref[pl.ds(r0, chunk), :] = src_ref[pl.ds(r0, chunk), :].astype(dst_ref.dtype)
        return carry
    lax.fori_loop(0, src_ref.shape[0] // chunk, body, 0)


def _expert_changed(be_ref, i):
    return (i == 0) | (be_ref[i] != be_ref[jnp.maximum(i - 1, 0)])


def _stream_expert_weights(i, live, be_ref, ne_ref, w_hbm, stage, wbf, wsem):
    changed = live & _expert_changed(be_ref, i)

    def copy(e):
        return pltpu.make_async_copy(w_hbm.at[e], stage, wsem)

    @pl.when(changed & (i == 0))
    def _():
        copy(be_ref[0]).start(priority=WEIGHT_DMA_PRIORITY)

    @pl.when(changed)
    def _():
        copy(be_ref[i]).wait()
        _cast_rows(stage, wbf)

    @pl.when(changed & (ne_ref[i] >= 0))
    def _():
        copy(ne_ref[i]).start(priority=WEIGHT_DMA_PRIORITY)


def _moe_up_kernel(be_ref, ne_ref, nu_ref, tc_ref, tn_ref, hn_ref, wgu_ref, act_ref,
                   xa, xb, wstage, wbf, gsem, wsem, *, d_ff):
    i = pl.program_id(0)
    nu = nu_ref[0]
    rows = xa.shape[0]
    even = i % 2 == 0

    def gather(tok_ref, r, buf, s):
        return pltpu.make_async_copy(hn_ref.at[tok_ref[0, r]], buf.at[r], gsem.at[s])

    def wait_gather(buf, s):
        pltpu.make_async_copy(hn_ref.at[pl.ds(0, rows)], buf, gsem.at[s]).wait()

    @pl.when((i == 0) & (nu > 0))
    def _():
        def body(r, c):
            gather(tc_ref, r, xa, 0).start()
            return c
        lax.fori_loop(0, rows, body, 0)

    _stream_expert_weights(i, i < nu, be_ref, ne_ref, wgu_ref, wstage, wbf, wsem)

    def live_step(cur, nxt, s):
        wait_gather(cur, s)
        for r in range(rows):
            gather(tn_ref, r, nxt, 1 - s).start()
        x_lo, x_hi = _unpack_bf16_pairs(cur[...])
        half = x_lo.shape[1]
        h = _dot(x_lo.astype(BF16), wbf[0:half, :]) + _dot(x_hi.astype(BF16), wbf[half:2 * half, :])
        gate = h[:, :d_ff]
        up = h[:, d_ff:]
        act_ref[...] = (gate * jax.nn.sigmoid(gate) * up).astype(act_ref.dtype)

    @pl.when((i < nu) & even)
    def _():
        live_step(xa, xb, 0)

    @pl.when((i < nu) & jnp.logical_not(even))
    def _():
        live_step(xb, xa, 1)

    @pl.when((i == nu) & (nu > 0) & even)
    def _():
        wait_gather(xa, 0)

    @pl.when((i == nu) & (nu > 0) & jnp.logical_not(even))
    def _():
        wait_gather(xb, 1)

    @pl.when(i >= nu)
    def _():
        act_ref[...] = jnp.zeros(act_ref.shape, act_ref.dtype)


def _moe_down_kernel(be_ref, ne_ref, nu_ref, dp_ref, act_ref, wd_ref, y_ref, ya, yb, wstage, wbf, ssem, wsem):
    i = pl.program_id(0)
    nu = nu_ref[0]
    rows = ya.shape[0]
    even = i % 2 == 0

    def scatter(r, buf, s):
        return pltpu.make_async_copy(buf.at[r], y_ref.at[dp_ref[0, r]], ssem.at[s])

    def wait_scatter(buf, s):
        pltpu.make_async_copy(buf, y_ref.at[pl.ds(0, rows)], ssem.at[s]).wait()

    @pl.when(i == 0)
    def _():
        n_res = y_ref.shape[0] - 2 * rows
        for s, buf in enumerate((ya, yb)):
            buf[...] = jnp.zeros(buf.shape, buf.dtype)
            pltpu.make_async_copy(buf, y_ref.at[pl.ds(n_res + s * rows, rows)], ssem.at[s]).start()
        for s, buf in enumerate((ya, yb)):
            pltpu.make_async_copy(buf, y_ref.at[pl.ds(n_res + s * rows, rows)], ssem.at[s]).wait()

    @pl.when((i >= 2) & (i < nu + 2) & even)
    def _():
        wait_scatter(ya, 0)

    @pl.when((i >= 2) & (i < nu + 2) & jnp.logical_not(even))
    def _():
        wait_scatter(yb, 1)

    _stream_expert_weights(i, i < nu, be_ref, ne_ref, wd_ref, wstage, wbf, wsem)

    def step(cur, prv, s, do_scatter, do_compute):
        if do_scatter:
            for r in range(rows):
                scatter(r, prv, 1 - s).start()
        if do_compute:
            cur[...] = _pack_bf16_pairs(_dot(act_ref[...], wbf[...]))

    for s, (cur, prv) in enumerate(((ya, yb), (yb, ya))):
        par = even if s == 0 else jnp.logical_not(even)

        @pl.when((i >= 1) & (i < nu) & par)
        def _(cur=cur, prv=prv, s=s):
            step(cur, prv, s, True, True)

        @pl.when((i == 0) & (nu > 0) & par)
        def _(cur=cur, prv=prv, s=s):
            step(cur, prv, s, False, True)

        @pl.when((i == nu) & (nu > 0) & par)
        def _(cur=cur, prv=prv, s=s):
            step(cur, prv, s, True, False)


def _experts(hn, src_tok, dst_row, blk_expert, nxt_expert, n_used, wgu, wd, n_out_rows):
    D = wgu.shape[1]
    nblk = src_tok.shape[0] // MOE_BLOCK
    d_ff = wd.shape[1]
    tok = src_tok.reshape(nblk, 1, MOE_BLOCK)
    dst = dst_row.reshape(nblk, 1, MOE_BLOCK)

    def live(i, nu):
        return jnp.maximum(jnp.minimum(i, nu[0] - 1), 0)

    smem = lambda f: pl.BlockSpec((None, 1, MOE_BLOCK), f, memory_space=pltpu.SMEM)
    act = pl.pallas_call(
        functools.partial(_moe_up_kernel, d_ff=d_ff),
        grid_spec=pltpu.PrefetchScalarGridSpec(
            num_scalar_prefetch=3,
            grid=(nblk - 1,),
            in_specs=[
                smem(lambda i, be, ne, nu: (i, 0, 0)),
                smem(lambda i, be, ne, nu: (i + 1, 0, 0)),
                pl.BlockSpec(memory_space=pl.ANY),
                pl.BlockSpec(memory_space=pl.ANY),
            ],
            out_specs=pl.BlockSpec((MOE_BLOCK, d_ff), lambda i, be, ne, nu: (i, 0)),
            scratch_shapes=[pltpu.VMEM((MOE_BLOCK, D // 2), jnp.uint32), pltpu.VMEM((MOE_BLOCK, D // 2), jnp.uint32),
                            pltpu.VMEM((D, 2 * d_ff), F32), pltpu.VMEM((D, 2 * d_ff), BF16),
                            pltpu.SemaphoreType.DMA((2,)), pltpu.SemaphoreType.DMA(())],
        ),
        out_shape=jax.ShapeDtypeStruct(((nblk - 1) * MOE_BLOCK, d_ff), BF16),
        compiler_params=_params(("arbitrary",)),
    )(blk_expert, nxt_expert, n_used, tok, tok, hn, wgu)
    return pl.pallas_call(
        _moe_down_kernel,
        grid_spec=pltpu.PrefetchScalarGridSpec(
            num_scalar_prefetch=3,
            grid=(nblk,),
            in_specs=[
                smem(lambda i, be, ne, nu: (jnp.maximum(i - 1, 0), 0, 0)),
                pl.BlockSpec((MOE_BLOCK, d_ff), lambda i, be, ne, nu: (live(i, nu), 0)),
                pl.BlockSpec(memory_space=pl.ANY),
            ],
            out_specs=pl.BlockSpec(memory_space=pl.ANY),
            scratch_shapes=[pltpu.VMEM((MOE_BLOCK, D // 2), jnp.uint32), pltpu.VMEM((MOE_BLOCK, D // 2), jnp.uint32),
                            pltpu.VMEM((d_ff, D), F32), pltpu.VMEM((d_ff, D), BF16),
                            pltpu.SemaphoreType.DMA((2,)), pltpu.SemaphoreType.DMA(())],
        ),
        out_shape=jax.ShapeDtypeStruct((n_out_rows, D // 2), jnp.uint32),
        compiler_params=_params(("arbitrary",)),
    )(blk_expert, nxt_expert, n_used, dst, act, wd)


def _moe_fused_kernel(be_ref, ne_ref, nu_ref, tc_ref, tn_ref, dp_ref, hn_ref, wgu_ref, wd_ref, y_ref,
                      xa, xb, ya, yb, gstage, gbf, dstage, dbf, gsem, ssem, wsem, *, d_ff):
    i = pl.program_id(0)
    nu = nu_ref[0]
    rows = xa.shape[0]
    even = i % 2 == 0

    def gather(tok_ref, r, buf, s):
        return pltpu.make_async_copy(hn_ref.at[tok_ref[0, r]], buf.at[r], gsem.at[s])

    def wait_gather(buf, s):
        pltpu.make_async_copy(hn_ref.at[pl.ds(0, rows)], buf, gsem.at[s]).wait()

    def scatter(r, buf, s):
        return pltpu.make_async_copy(buf.at[r], y_ref.at[dp_ref[0, r]], ssem.at[s])

    def wait_scatter(buf, s):
        pltpu.make_async_copy(buf, y_ref.at[pl.ds(0, rows)], ssem.at[s]).wait()

    @pl.when(i == 0)
    def _():
        n_res = y_ref.shape[0] - 2 * rows
        for s, buf in enumerate((ya, yb)):
            buf[...] = jnp.zeros(buf.shape, buf.dtype)
            pltpu.make_async_copy(buf, y_ref.at[pl.ds(n_res + s * rows, rows)], ssem.at[s]).start()
        for s, buf in enumerate((ya, yb)):
            pltpu.make_async_copy(buf, y_ref.at[pl.ds(n_res + s * rows, rows)], ssem.at[s]).wait()

    @pl.when((i == 0) & (nu > 0))
    def _():
        def body(r, c):
            gather(tc_ref, r, xa, 0).start()
            return c
        lax.fori_loop(0, rows, body, 0)

    @pl.when((i >= 2) & (i < nu + 2) & even)
    def _():
        wait_scatter(ya, 0)

    @pl.when((i >= 2) & (i < nu + 2) & jnp.logical_not(even))
    def _():
        wait_scatter(yb, 1)

    _stream_expert_weights(i, i < nu, be_ref, ne_ref, wgu_ref, gstage, gbf, wsem.at[0])
    _stream_expert_weights(i, i < nu, be_ref, ne_ref, wd_ref, dstage, dbf, wsem.at[1])

    def step(xcur, xnxt, ycur, yprv, s, do_scatter, do_compute):
        if do_compute:
            wait_gather(xcur, s)
            for r in range(rows):
                gather(tn_ref, r, xnxt, 1 - s).start()
        if do_scatter:
            for r in range(rows):
                scatter(r, yprv, 1 - s).start()
        if do_compute:
            x_lo, x_hi = _unpack_bf16_pairs(xcur[...])
            half = x_lo.shape[1]
            h = _dot(x_lo.astype(BF16), gbf[0:half, :]) + _dot(x_hi.astype(BF16), gbf[half:2 * half, :])
            gate = h[:, :d_ff]
            up = h[:, d_ff:]
            act = (gate * jax.nn.sigmoid(gate) * up).astype(BF16)
            ycur[...] = _pack_bf16_pairs(_dot(act, dbf[...]))
        else:
            wait_gather(xcur, s)

    for s, (xcur, xnxt, ycur, yprv) in enumerate(((xa, xb, ya, yb), (xb, xa, yb, ya))):
        par = even if s == 0 else jnp.logical_not(even)

        @pl.when((i >= 1) & (i < nu) & par)
        def _(a=(xcur, xnxt, ycur, yprv, s)):
            step(*a, True, True)

        @pl.when((i == 0) & (nu > 0) & par)
        def _(a=(xcur, xnxt, ycur, yprv, s)):
            step(*a, False, True)

        @pl.when((i == nu) & (nu > 0) & par)
        def _(a=(xcur, xnxt, ycur, yprv, s)):
            step(*a, True, False)


def _experts_fused(hn, src_tok, dst_row, blk_expert, nxt_expert, n_used, wgu, wd, n_out_rows):
    D = wgu.shape[1]
    nblk = src_tok.shape[0] // MOE_BLOCK
    d_ff = wd.shape[1]
    tok = src_tok.reshape(nblk, 1, MOE_BLOCK)
    dst = dst_row.reshape(nblk, 1, MOE_BLOCK)
    smem = lambda f: pl.BlockSpec((None, 1, MOE_BLOCK), f, memory_space=pltpu.SMEM)
    u32 = jnp.uint32
    return pl.pallas_call(
        functools.partial(_moe_fused_kernel, d_ff=d_ff),
        grid_spec=pltpu.PrefetchScalarGridSpec(
            num_scalar_prefetch=3,
            grid=(nblk,),
            in_specs=[
                smem(lambda i, be, ne, nu: (i, 0, 0)),
                smem(lambda i, be, ne, nu: (jnp.minimum(i + 1, nblk - 1), 0, 0)),
                smem(lambda i, be, ne, nu: (jnp.maximum(i - 1, 0), 0, 0)),
                pl.BlockSpec(memory_space=pl.ANY),
                pl.BlockSpec(memory_space=pl.ANY),
                pl.BlockSpec(memory_space=pl.ANY),
            ],
            out_specs=pl.BlockSpec(memory_space=pl.ANY),
            scratch_shapes=[pltpu.VMEM((MOE_BLOCK, D // 2), u32), pltpu.VMEM((MOE_BLOCK, D // 2), u32),
                            pltpu.VMEM((MOE_BLOCK, D // 2), u32), pltpu.VMEM((MOE_BLOCK, D // 2), u32),
                            pltpu.VMEM((D, 2 * d_ff), F32), pltpu.VMEM((D, 2 * d_ff), BF16),
                            pltpu.VMEM((d_ff, D), F32), pltpu.VMEM((d_ff, D), BF16),
                            pltpu.SemaphoreType.DMA((2,)), pltpu.SemaphoreType.DMA((2,)),
                            pltpu.SemaphoreType.DMA((2,))],
        ),
        out_shape=jax.ShapeDtypeStruct((n_out_rows, D // 2), u32),
        compiler_params=_params(("arbitrary",)),
    )(blk_expert, nxt_expert, n_used, tok, tok, dst, hn, wgu, wd)


def _combine_kernel(x1_ref, rt_ref, y1_ref, y2_ref, o_ref):
    rt = rt_ref[...]
    w1, w2 = rt[:, 2:3], rt[:, 3:4]
    half = o_ref.shape[1] // 2
    y1 = _unpack_bf16_pairs(y1_ref[...])
    y2 = _unpack_bf16_pairs(y2_ref[...])
    for p in range(2):
        cs = slice(p * half, (p + 1) * half)
        o_ref[:, cs] = x1_ref[:, cs] + (w1 * y1[p] + w2 * y2[p])


def _combine(x1, route, y, tm=COMBINE_ROWS):
    T, D = x1.shape
    nt = T // tm
    return pl.pallas_call(
        _combine_kernel,
        grid=(nt,),
        in_specs=[
            pl.BlockSpec((tm, D), lambda i: (i, 0)),
            pl.BlockSpec((tm, LANES), lambda i: (i, 0)),
            pl.BlockSpec((tm, D // 2), lambda i: (i, 0)),
            pl.BlockSpec((tm, D // 2), lambda i: (nt + i, 0)),
        ],
        out_specs=pl.BlockSpec((tm, D), lambda i: (i, 0)),
        out_shape=jax.ShapeDtypeStruct((T, D), F32),
        compiler_params=_params(("parallel",)),
    )(x1, route, y, y)


def _dispatch_tables(route, T):
    TK = T * MOE_TOP_K
    flat_e = route[:, :MOE_TOP_K].astype(jnp.int32).reshape(-1)
    order = jnp.argsort(flat_e).astype(jnp.int32)
    sizes = jnp.sum(flat_e[:, None] == jnp.arange(MOE_N_EXPERTS, dtype=jnp.int32)[None, :], axis=0,
                    dtype=jnp.int32)
    start = jnp.cumsum(sizes) - sizes
    padded = ((sizes + MOE_BLOCK - 1) // MOE_BLOCK) * MOE_BLOCK
    pad_end = jnp.cumsum(padded)
    pad_start = pad_end - padded
    n_blocks = TK // MOE_BLOCK + MOE_N_EXPERTS
    blk = jnp.arange(n_blocks + 2, dtype=jnp.int32)
    blk_expert = jnp.minimum(jnp.sum(pad_end[None, :] <= (blk * MOE_BLOCK)[:, None], axis=1, dtype=jnp.int32),
                             MOE_N_EXPERTS - 1)
    r = jnp.arange(MOE_BLOCK, dtype=jnp.int32)[None, :]
    pos = blk[:, None] * MOE_BLOCK + r
    off = pos - pad_start[blk_expert][:, None]
    live = (off < sizes[blk_expert][:, None]) & (pos < pad_end[-1])
    sorted_idx = jnp.where(live, off + start[blk_expert][:, None], 0)
    slot = order[sorted_idx]
    tok = slot // MOE_TOP_K
    src_tok = jnp.where(live, tok, 0).reshape(-1)
    trash = TK + (blk[:, None] % 2) * MOE_BLOCK + r
    dst_row = jnp.where(live, (slot % MOE_TOP_K) * T + tok, trash).reshape(-1)
    n_used = (pad_end[-1:] // MOE_BLOCK).astype(jnp.int32)
    eid = jnp.arange(MOE_N_EXPERTS, dtype=jnp.int32)
    later = (eid[None, :] > eid[:, None]) & (sizes[None, :] > 0)
    nxt = jnp.min(jnp.where(later, eid[None, :], MOE_N_EXPERTS), axis=1)
    nxt_expert = jnp.where(nxt < MOE_N_EXPERTS, nxt, -1).astype(jnp.int32)[blk_expert]
    return src_tok, dst_row, blk_expert, nxt_expert, n_used, TK + 2 * MOE_BLOCK


def _w_in_tile_order(d_gate):
    da = DA_HEADS * 2 * DA_QK_DIM
    dav = DA_HEADS * DA_V_DIM
    dl = len(DL_GROUPS) * DL_HEADS_PER_GROUP * DL_HEAD_DIM
    o = [int(v) // COL_TILE for v in np.cumsum([0, da, da, dav, dl, dl, dl, d_gate, d_gate])]
    order = list(range(o[6], o[8])) + list(range(o[0], o[3]))
    for g in range(len(DL_GROUPS)):
        order += [o[3] + g, o[4] + g, o[5] + g]
    assert sorted(order) == list(range(CT_END))
    return jnp.asarray(order, jnp.int32)


def _gain_table(da_q_norm, da_k_norm, dl_q_norm, dl_k_norm):
    ones = jnp.ones((COL_TILE,), F32)
    daq = jnp.tile(da_q_norm, COL_TILE // DA_QK_DIM) * (DA_QK_DIM ** -0.5 * LOG2E)
    dak = jnp.tile(da_k_norm, COL_TILE // DA_QK_DIM)
    dlq = jnp.tile(dl_q_norm, COL_TILE // DL_HEAD_DIM) * (DL_HEAD_DIM ** -0.5 * LOG2E)
    dlk = jnp.tile(dl_k_norm, COL_TILE // DL_HEAD_DIM)
    rows = []
    for j in range(CT_END):
        if CT_DA_Q <= j < CT_DA_K:
            rows.append(daq)
        elif CT_DA_K <= j < CT_DA_V:
            rows.append(dak)
        elif j >= CT_DL and (j - CT_DL) % 3 == 0:
            rows.append(dlq)
        elif j >= CT_DL and (j - CT_DL) % 3 == 1:
            rows.append(dlk)
        else:
            rows.append(ones)
    return jnp.stack(rows, axis=0).reshape(CT_END, 1, COL_TILE)


def kernel(x, norm_mix, w_in, da_q_norm, da_k_norm, da_lambda_q, da_lambda_k, da_sub_norm,
           dl_q_norm, dl_k_norm, w_branch_a, w_branch_b, w_out, norm_ffn,
           w_group_router, w_expert_router, w_gate_up, w_down):
    B, S, D = x.shape
    T = B * S
    depth = w_in.shape[0]
    x2 = x.reshape(T, D)
    for l in range(depth):
        lam_init = 0.8 - 0.6 * math.exp(-0.3 * l)
        gain_tab = _gain_table(da_q_norm[l], da_k_norm[l], dl_q_norm[l], dl_k_norm[l])
        proj, dl1, dl2 = _inproj(x2, norm_mix[l].reshape(1, D), w_in[l].astype(BF16), _w_in_tile_order(D),
                                 gain_tab, B, S)

        o_a = _diff_attention(proj, da_lambda_q[l], da_lambda_k[l], da_sub_norm[l].reshape(1, DA_V_DIM),
                              B, S, lam_init)
        dl = [_dilated_group(proj.reshape(B, S, proj.shape[1]), CT_DL, 0, B, S),
              _dilated_group(dl1, 0, 1, B, S), _dilated_group(dl2, 0, 2, B, S)]

        w_r = jnp.concatenate([w_expert_router[l], w_group_router[l]], axis=1)
        w_r = jnp.pad(w_r, ((0, 0), (0, LANES - w_r.shape[1])))
        r_hi = w_r.astype(BF16)
        r_lo = (w_r - r_hi.astype(F32)).astype(BF16)
        x1, hn, route = _outproj(
            x2, o_a, proj, [t[0] for t in dl], [t[1] for t in dl],
            w_branch_a[l].astype(BF16), w_branch_b[l].astype(BF16), w_out[l].astype(BF16),
            norm_ffn[l].reshape(1, D), r_hi, jnp.concatenate([r_hi, r_lo], axis=1))

        src_tok, dst_row, blk_expert, nxt_expert, n_used, n_rows = _dispatch_tables(route, T)
        y = _experts_fused(hn, src_tok, dst_row, blk_expert, nxt_expert, n_used, w_gate_up[l], w_down[l], n_rows)
        x2 = _combine(x1, route, y)
    return x2.reshape(B, S, D)
```

```python
import functools
import math

import jax
import jax.numpy as jnp
import numpy as np
from jax import lax
from jax.experimental import pallas as pl
from jax.experimental.pallas import tpu as pltpu

F32 = jnp.float32
BF16 = jnp.bfloat16

EPS = 1e-6
LOG2E = 1.4426950408889634
NEG_BIG = -1e30

DA_HEADS = 8
DA_QK_DIM = 64
DA_V_DIM = 128
DA_TILE_GROUP = 4
DL_GROUPS = ((128, 1), (512, 4), (2048, 16))
DL_HEADS_PER_GROUP = 4
DL_HEAD_DIM = 128
DL_SPAN = 128
MOE_GROUPS = 4
MOE_EXPERTS_PER_GROUP = 8
MOE_N_EXPERTS = 32
MOE_TOP_K = 2
MOE_BLOCK = 256
WEIGHT_DMA_PRIORITY = 1

LANES = 128
COL_TILE = 512
VMEM_LIMIT = 56 * 1024 * 1024

INPROJ_ROWS = 1024
DA_TILE = 512
DA_ROW_CHUNK = 32
DL_TOKENS_PER_STEP = 2048
DL_RESIDUES_PER_TRIP = 4
DL_LOOKAHEAD = 4
OUTPROJ_ROWS = 256
COMBINE_ROWS = 512
CAST_ROWS = 256

CT_GATE_A, CT_GATE_B, CT_DA_Q, CT_DA_K, CT_DA_V, CT_DL, CT_MAIN_END, CT_END = 0, 4, 8, 10, 12, 14, 17, 23


def _params(sem, vmem=VMEM_LIMIT):
    return pltpu.CompilerParams(dimension_semantics=sem, vmem_limit_bytes=vmem)


def _dot(a, b):
    return jnp.dot(a, b, preferred_element_type=F32)


def _dot_nt(a, b):
    return lax.dot_general(a, b, (((1,), (1,)), ((), ())), preferred_element_type=F32)


def _inproj_kernel(perm_ref, x_ref, g_ref, w_ref, gain_ref, o_ref, d1_ref, d2_ref, h_scr, y_scr):
    j = pl.program_id(1)

    @pl.when(j == 0)
    def _():
        x = x_ref[...]
        ms = jnp.mean(x * x, axis=-1, keepdims=True)
        h_scr[...] = (x * lax.rsqrt(ms + EPS) * g_ref[...]).astype(BF16)

    gain = gain_ref[...]
    half = COL_TILE // 2

    def head_slices():
        ys = [_dot(h_scr[...], w_ref[:, hf * half:(hf + 1) * half]) for hf in range(2)]
        for hf in range(2):
            for hh in range(half // LANES):
                yield hf * (half // LANES) + hh, ys[hf][:, hh * LANES:(hh + 1) * LANES]

    is64 = (j >= CT_DA_Q) & (j < CT_DA_V)
    is128 = (j >= CT_DL) & (lax.rem(j - CT_DL, 3) < 2)
    main = j < CT_MAIN_END
    plain = jnp.logical_not(is64 | is128)

    def norm64(h, yh):
        sq = yh * yh
        lo = lax.broadcasted_iota(jnp.int32, yh.shape, 1) < DA_QK_DIM
        s_lo = jnp.sum(jnp.where(lo, sq, 0.0), axis=-1, keepdims=True)
        s_hi = jnp.sum(jnp.where(lo, 0.0, sq), axis=-1, keepdims=True)
        r = jnp.where(lo, lax.rsqrt(s_lo * (1.0 / DA_QK_DIM) + EPS), lax.rsqrt(s_hi * (1.0 / DA_QK_DIM) + EPS))
        return yh * r * gain[:, h * LANES:(h + 1) * LANES]

    def norm128(h, yh):
        ss = jnp.sum(yh * yh, axis=-1, keepdims=True)
        return yh * lax.rsqrt(ss * (1.0 / DL_HEAD_DIM) + EPS) * gain[:, h * LANES:(h + 1) * LANES]

    def emit(cond, fn, to_main):
        @pl.when(cond)
        def _():
            for h, yh in head_slices():
                if to_main:
                    o_ref[:, h * LANES:(h + 1) * LANES] = fn(h, yh).astype(o_ref.dtype)
                else:
                    y_scr[h] = fn(h, yh)

    emit(is64, norm64, True)
    emit(is128 & main, norm128, True)
    emit(is128 & jnp.logical_not(main), norm128, False)
    emit(plain & main, lambda h, yh: yh, True)
    emit(plain & jnp.logical_not(main), lambda h, yh: yh, False)

    def deinterleave(dst_ref):
        d, rows = dst_ref.shape[0], dst_ref.shape[1]
        for r in range(d):
            for h in range(COL_TILE // LANES):
                dst_ref[r, :, h * LANES:(h + 1) * LANES] = (
                    y_scr[h, pl.ds(r, rows, stride=d), :].astype(dst_ref.dtype))

    @pl.when((j >= CT_MAIN_END) & (j < CT_MAIN_END + 3))
    def _():
        deinterleave(d1_ref)

    @pl.when(j >= CT_MAIN_END + 3)
    def _():
        deinterleave(d2_ref)


def _inproj(x2, gain_mix, w_bf, tile_perm, gain_tab, B, S, tm=INPROJ_ROWS):
    T, D = x2.shape
    tiles_per_batch = S // tm
    d1, d2 = DL_GROUPS[1][1], DL_GROUPS[2][1]
    part1 = lambda j: jnp.clip(j - CT_MAIN_END, 0, 2)
    part2 = lambda j: jnp.clip(j - CT_MAIN_END - 3, 0, 2)
    grid_spec = pltpu.PrefetchScalarGridSpec(
        num_scalar_prefetch=1,
        grid=(T // tm, CT_END),
        in_specs=[
            pl.BlockSpec((tm, D), lambda i, j, perm: (i, 0)),
            pl.BlockSpec((1, D), lambda i, j, perm: (0, 0)),
            pl.BlockSpec((D, COL_TILE), lambda i, j, perm: (0, perm[j])),
            pl.BlockSpec((None, 1, COL_TILE), lambda i, j, perm: (j, 0, 0)),
        ],
        out_specs=[
            pl.BlockSpec((tm, COL_TILE), lambda i, j, perm: (i, jnp.minimum(j, CT_MAIN_END - 1))),
            pl.BlockSpec((d1, tm // d1, COL_TILE),
                         lambda i, j, perm: (i // tiles_per_batch, i % tiles_per_batch, part1(j))),
            pl.BlockSpec((d2, tm // d2, COL_TILE),
                         lambda i, j, perm: (i // tiles_per_batch, i % tiles_per_batch, part2(j))),
        ],
        scratch_shapes=[pltpu.VMEM((tm, D), BF16), pltpu.VMEM((COL_TILE // LANES, tm, LANES), F32)],
    )
    return pl.pallas_call(
        _inproj_kernel,
        grid_spec=grid_spec,
        out_shape=[
            jax.ShapeDtypeStruct((T, CT_MAIN_END * COL_TILE), BF16),
            jax.ShapeDtypeStruct((B * d1, S // d1, 3 * COL_TILE), BF16),
            jax.ShapeDtypeStruct((B * d2, S // d2, 3 * COL_TILE), BF16),
        ],
        compiler_params=_params(("parallel", "arbitrary")),
    )(tile_perm, x2, gain_mix, w_bf, gain_tab)


def _bf16_pieces(x, n=3):
    out = []
    r = np.float64(x)
    for _ in range(n):
        p = np.asarray(np.float32(r)).astype(jnp.bfloat16).astype(np.float64)
        out.append(float(p))
        r = r - p
    return out


def _alibi_features(tk):
    pieces = _bf16_pieces(LOG2E)
    qf = np.zeros((2, LANES), np.float32)
    kf = np.zeros((2, tk, LANES), np.float32)
    j = np.arange(tk)
    hi, lo = (j // 16) * 16, j % 16
    for m in range(2):
        f0 = DA_QK_DIM if m == 0 else 0
        for n, p in enumerate(pieces):
            qf[m, f0 + 2 * n] = p
            qf[m, f0 + 2 * n + 1] = p
            kf[m, :, f0 + 2 * n] = hi
            kf[m, :, f0 + 2 * n + 1] = lo
    return jnp.asarray(qf), jnp.asarray(kf, dtype=BF16)


def _da_kernel(q_ref, k_ref, v_ref, qf_ref, kf_ref, lq_ref, lk_ref, sg_ref, o_ref,
               s00, s01, s10, s11, p00, p01, p10, p11, pd0, pd1,
               m0_scr, m1_scr, l0_scr, l1_scr, a0_scr, a1_scr, acc0_scr, acc1_scr, *, tq, rc, lam_init):
    h = pl.program_id(1)
    qi = pl.program_id(2)
    nlb = tq // LANES
    pow2 = jnp.exp2(-(h + 1).astype(F32))
    slope2 = pow2 * LOG2E

    q = q_ref[...]
    lane = lax.broadcasted_iota(jnp.int32, (tq, LANES), 1)
    own = (lane < DA_QK_DIM, lane >= DA_QK_DIM)
    qfs = [jnp.where(own[mi], q, jnp.broadcast_to((qf_ref[mi:mi + 1, :] * pow2).astype(BF16), q.shape))
           for mi in range(2)]

    m_scrs, l_scrs, a_scrs, acc_scrs = (m0_scr, m1_scr), (l0_scr, l1_scr), (a0_scr, a1_scr), (acc0_scr, acc1_scr)
    for mi in range(2):
        m_scrs[mi][...] = jnp.full(m_scrs[mi].shape, NEG_BIG, F32)
        l_scrs[mi][...] = jnp.zeros(l_scrs[mi].shape, F32)
        acc_scrs[mi][...] = jnp.zeros(acc_scrs[mi].shape, F32)

    def scores(ki, mi, s_ref):
        k = k_ref[pl.ds(pl.multiple_of(ki * tq, tq), tq), :]
        s_ref[...] = _dot_nt(qfs[mi], jnp.where(own[mi], k, kf_ref[mi]))

    def softmax(ki, mi, s_ref, p_ref, masked):
        m_scr, l_scr, a_scr = m_scrs[mi], l_scrs[mi], a_scrs[mi]
        c = slope2 * ((ki - qi) * tq).astype(F32)
        for r in range(tq // rc):
            rows = slice(r * rc, (r + 1) * rc)
            nb = min(nlb, ((r + 1) * rc - 1) // LANES + 1) if masked else nlb
            sb = []
            for j in range(nb):
                cs = slice(j * LANES, (j + 1) * LANES)
                s = s_ref[rows, cs]
                if masked and (j + 1) * LANES - 1 > r * rc:
                    rr = lax.broadcasted_iota(jnp.int32, (rc, LANES), 0) + r * rc
                    cc = lax.broadcasted_iota(jnp.int32, (rc, LANES), 1) + j * LANES
                    s = jnp.where(cc <= rr, s, NEG_BIG)
                sb.append(s)
            mx = sb[0]
            for s in sb[1:]:
                mx = jnp.maximum(mx, s)
            m_prev = m_scr[rows, :]
            m_new = jnp.maximum(m_prev, jnp.max(mx, axis=-1, keepdims=True) + c)
            alpha = jnp.exp2(m_prev - m_new)
            a_scr[rows, :] = alpha
            m_scr[rows, :] = m_new
            mc = m_new - c
            psum = alpha * l_scr[rows, :]
            for j in range(nlb):
                cs = slice(j * LANES, (j + 1) * LANES)
                if j < nb:
                    p = jnp.exp2(sb[j] - mc)
                    psum = psum + p
                    p_ref[rows, cs] = p.astype(BF16)
                else:
                    p_ref[rows, cs] = jnp.zeros((rc, LANES), BF16)
            l_scr[rows, :] = psum

    def values(ki, mi, p_ref):
        v = v_ref[pl.ds(pl.multiple_of(ki * tq, tq), tq), :]
        acc_scrs[mi][...] = a_scrs[mi][...] * acc_scrs[mi][...] + _dot(p_ref[...], v)

    s_bufs, p_bufs = ((s00, s01), (s10, s11)), ((p00, p01), (p10, p11))

    def tile_group(k0, n, last_masked):
        for mi in range(2):
            scores(k0, mi, s_bufs[0][mi])
        for i in range(n):
            masked = last_masked and i == n - 1
            for mi in range(2):
                p_ref = (pd0, pd1)[mi] if masked else p_bufs[i % 2][mi]
                softmax(k0 + i, mi, s_bufs[i % 2][mi], p_ref, masked)
                values(k0 + i, mi, p_ref)
                if i + 1 < n:
                    scores(k0 + i + 1, mi, s_bufs[(i + 1) % 2][mi])

    def body(t, carry):
        tile_group(DA_TILE_GROUP * t, DA_TILE_GROUP, False)
        return carry

    n_full = qi // DA_TILE_GROUP
    lax.fori_loop(0, n_full, body, 0)
    for rem in range(1, DA_TILE_GROUP + 1):
        @pl.when(qi - DA_TILE_GROUP * n_full == rem - 1)
        def _(rem=rem):
            tile_group(qi - (rem - 1), rem, True)

    lam_e = jnp.exp(jnp.sum(lq_ref[...] * lk_ref[...], axis=-1, keepdims=True))
    lam = lam_e[0:1, :] - lam_e[1:2, :] + lam_init
    l0 = jnp.sum(l0_scr[...], axis=-1, keepdims=True)
    l1 = jnp.sum(l1_scr[...], axis=-1, keepdims=True)
    o = acc0_scr[...] / l0 - lam * (acc1_scr[...] / l1)
    ms = jnp.mean(o * o, axis=-1, keepdims=True)
    o = o * lax.rsqrt(ms + EPS) * sg_ref[...] * (1.0 - lam_init)
    o_ref[...] = o.astype(o_ref.dtype)


def _diff_attention(proj, lam_q, lam_k, sub_gain, B, S, lam_init, tq=DA_TILE, rc=DA_ROW_CHUNK):
    T = proj.shape[0]
    nq = S // tq
    lb = LANES
    q_blk0, k_blk0, v_blk0 = (CT_DA_Q * COL_TILE) // lb, (CT_DA_K * COL_TILE) // lb, (CT_DA_V * COL_TILE) // lb
    qfeat, kfeat = _alibi_features(tq)
    const = lambda shape: pl.BlockSpec(shape, lambda b, h, i: (0,) * len(shape))
    return pl.pallas_call(
        functools.partial(_da_kernel, tq=tq, rc=rc, lam_init=lam_init),
        grid=(B, DA_HEADS, nq),
        in_specs=[
            pl.BlockSpec((tq, lb), lambda b, h, i: (b * nq + i, q_blk0 + h)),
            pl.BlockSpec((S, lb), lambda b, h, i: (b, k_blk0 + h)),
            pl.BlockSpec((S, lb), lambda b, h, i: (b, v_blk0 + h)),
            const((2, LANES)), const((2, tq, LANES)),
            const((2, DA_QK_DIM)), const((2, DA_QK_DIM)), const((1, DA_V_DIM)),
        ],
        out_specs=pl.BlockSpec((tq, lb), lambda b, h, i: (b * nq + i, h)),
        out_shape=jax.ShapeDtypeStruct((T, DA_HEADS * DA_V_DIM), BF16),
        scratch_shapes=[pltpu.VMEM((tq, tq), F32)] * 4 + [pltpu.VMEM((tq, tq), BF16)] * 6
        + [pltpu.VMEM((tq, LANES), F32)] * 6 + [pltpu.VMEM((tq, DA_V_DIM), F32)] * 2,
        compiler_params=_params(("parallel", "parallel", "arbitrary")),
    )(proj, proj, proj, qfeat, kfeat, lam_q, lam_k, sub_gain)


def _dl_kernel(q_ref, kc_ref, kp_ref, vc_ref, vp_ref, o_ref, lse_ref, *, slopes2, d, tq, ru):
    n = pl.program_id(1)
    sp = DL_SPAN
    row = lax.broadcasted_iota(jnp.int32, (sp, sp), 0)
    col = lax.broadcasted_iota(jnp.int32, (sp, sp), 1)
    dcur = row - col
    cur_ok = dcur >= 0
    prev_ok = dcur <= 0
    dcur_f = dcur.astype(F32)

    def scores(r, hh, j):
        hs = slice(hh * LANES, (hh + 1) * LANES)
        rs = slice(j * sp, (j + 1) * sp)
        q = q_ref[r, rs, hs]
        if j == 0:
            kp, vp, p_ok = kp_ref[r, :, hs], vp_ref[r, :, hs], prev_ok & (n > 0)
        else:
            ps = slice((j - 1) * sp, j * sp)
            kp, vp, p_ok = kc_ref[r, ps, hs], vc_ref[r, ps, hs], prev_ok
        s_c = jnp.where(cur_ok, _dot_nt(q, kc_ref[r, rs, hs]) - slopes2[hh] * dcur_f, NEG_BIG)
        s_p = jnp.where(p_ok, _dot_nt(q, kp) - slopes2[hh] * (dcur_f + float(sp)), NEG_BIG)
        return s_c, s_p, vc_ref[r, rs, hs], vp

    def finish(r, hh, j, s_c, s_p, vc, vp):
        m = jnp.max(jnp.maximum(s_c, s_p), axis=-1, keepdims=True)
        p_c = jnp.exp2(s_c - m)
        p_p = jnp.exp2(s_p - m)
        den = jnp.sum(p_c + p_p, axis=-1, keepdims=True)
        acc = _dot(p_c.astype(BF16), vc) + _dot(p_p.astype(BF16), vp)
        out_rows = pl.ds(j * sp, sp) if d == 1 else pl.ds(r + j * sp * d, sp, stride=d)
        o_ref[hh, out_rows, :] = acc / den
        lse_ref[hh, out_rows, :] = jnp.broadcast_to(m + jnp.log2(den), (sp, LANES))

    def residues(t, carry):
        units = [(t * ru + rr, hh, j) for rr in range(ru) for hh in range(DL_HEADS_PER_GROUP)
                 for j in range(tq // sp)]
        pending = []
        for u in units:
            pending.append((u, scores(*u)))
            if len(pending) > DL_LOOKAHEAD:
                u0, vals = pending.pop(0)
                finish(*u0, *vals)
        for u0, vals in pending:
            finish(*u0, *vals)
        return carry

    lax.fori_loop(0, d // ru, residues, 0)


def _dilated_group(src, col0, g, B, S, tok_per_step=DL_TOKENS_PER_STEP):
    window, d = DL_GROUPS[g]
    assert window // d == DL_SPAN
    L = S // d
    tq = min(tok_per_step, S) // d
    assert tq % DL_SPAN == 0 and L % tq == 0
    nh = DL_HEADS_PER_GROUP * len(DL_GROUPS)
    slopes2 = tuple(2.0 ** (-8.0 * (g * DL_HEADS_PER_GROUP + hh + 1) / nh) * d * LOG2E
                    for hh in range(DL_HEADS_PER_GROUP))
    spb = tq // DL_SPAN
    nsteps = L // tq
    cur = lambda c: pl.BlockSpec((d, tq, COL_TILE), lambda b, n: (b, n, c))
    prev = lambda c: pl.BlockSpec((d, DL_SPAN, COL_TILE), lambda b, n: (b, jnp.maximum(n * spb - 1, 0), c))
    out_spec = pl.BlockSpec((DL_HEADS_PER_GROUP, d * tq, LANES), lambda b, n: (0, b * nsteps + n, 0))
    return pl.pallas_call(
        functools.partial(_dl_kernel, slopes2=slopes2, d=d, tq=tq, ru=min(d, DL_RESIDUES_PER_TRIP)),
        grid=(B, nsteps),
        in_specs=[cur(col0), cur(col0 + 1), prev(col0 + 1), cur(col0 + 2), prev(col0 + 2)],
        out_specs=[out_spec, out_spec],
        out_shape=[jax.ShapeDtypeStruct((DL_HEADS_PER_GROUP, B * S, LANES), F32)] * 2,
        compiler_params=_params(("parallel", "arbitrary")),
    )(src, src, src, src, src)


def _route(logits):
    lane = lax.broadcasted_iota(jnp.int32, logits.shape, 1)
    big = jnp.int32(1 << 20)
    is_g = (lane >= MOE_N_EXPERTS) & (lane < MOE_N_EXPERTS + MOE_GROUPS)
    lg = jnp.where(is_g, logits, -jnp.inf)
    gmax = jnp.max(lg, axis=-1, keepdims=True)
    gsum = jnp.sum(jnp.exp(lg - gmax), axis=-1, keepdims=True)
    g_w = 1.0 / gsum
    g_idx = jnp.min(jnp.where(lg == gmax, lane - MOE_N_EXPERTS, big), axis=-1, keepdims=True)
    in_grp = (lane < MOE_N_EXPERTS) & ((lane // MOE_EXPERTS_PER_GROUP) == g_idx)
    le = jnp.where(in_grp, logits, -jnp.inf)
    t1 = jnp.max(le, axis=-1, keepdims=True)
    e1 = jnp.min(jnp.where(le == t1, lane, big), axis=-1, keepdims=True)
    le2 = jnp.where(lane == e1, -jnp.inf, le)
    t2 = jnp.max(le2, axis=-1, keepdims=True)
    e2 = jnp.min(jnp.where(le2 == t2, lane, big), axis=-1, keepdims=True)
    r = jnp.exp(t2 - t1)
    w1 = g_w / (1.0 + r)
    w2 = w1 * r
    out = jnp.where(lane == 0, e1.astype(F32),
                    jnp.where(lane == 1, e2.astype(F32),
                              jnp.where(lane == 2, w1, jnp.where(lane == 3, w2, 0.0))))
    return out


def _outproj_kernel(x_ref, oa_ref, ga_ref, gb_ref, o0_ref, o1_ref, o2_ref, l0_ref, l1_ref, l2_ref,
                    wa_ref, wb_ref, wo_ref, gf_ref, rh_ref, rc_ref,
                    x1_ref, hn_ref, rt_ref):
    obs = []
    for hh in range(DL_HEADS_PER_GROUP):
        l0, l1, l2 = l0_ref[hh], l1_ref[hh], l2_ref[hh]
        lm = jnp.maximum(jnp.maximum(l0, l1), l2)
        e0, e1, e2 = jnp.exp2(l0 - lm), jnp.exp2(l1 - lm), jnp.exp2(l2 - lm)
        obs.append((e0 * o0_ref[hh] + e1 * o1_ref[hh] + e2 * o2_ref[hh]) / (e0 + e1 + e2))
    ob = jnp.concatenate(obs, axis=1)
    a = _dot(oa_ref[...], wa_ref[...])
    b = _dot(ob.astype(BF16), wb_ref[...])
    mixed = jax.nn.sigmoid(ga_ref[...].astype(F32)) * a + jax.nn.sigmoid(gb_ref[...].astype(F32)) * b
    x1 = x_ref[...] + _dot(mixed.astype(BF16), wo_ref[...])
    x1_ref[...] = x1
    ms = jnp.mean(x1 * x1, axis=-1, keepdims=True)
    hn = x1 * lax.rsqrt(ms + EPS) * gf_ref[...]
    hn_hi = hn.astype(BF16)
    hn_ref[...] = _pack_bf16_pairs(hn)
    hn_lo = (hn - hn_hi.astype(F32)).astype(BF16)
    t = _dot(hn_hi, rc_ref[...])
    logits = t[:, 0:LANES] + (_dot(hn_lo, rh_ref[...]) + t[:, LANES:2 * LANES])
    rt_ref[...] = _route(logits)


def _outproj(x2, o_a, proj, dl_o, dl_lse, wa, wb, wo, gain_ffn, r_hi, r_cat, tm=OUTPROJ_ROWS):
    T, D = x2.shape
    row = lambda w: pl.BlockSpec((tm, w), lambda i: (i, 0))
    full = lambda s: pl.BlockSpec(s, lambda i: (0, 0), pipeline_mode=pl.Buffered(1))
    hrow = pl.BlockSpec((DL_HEADS_PER_GROUP, tm, LANES), lambda i: (0, i, 0))
    return pl.pallas_call(
        _outproj_kernel,
        grid=(T // tm,),
        in_specs=[
            row(D), row(o_a.shape[1]),
            pl.BlockSpec((tm, D), lambda i: (i, (CT_GATE_A * COL_TILE) // D)),
            pl.BlockSpec((tm, D), lambda i: (i, (CT_GATE_B * COL_TILE) // D)),
            hrow, hrow, hrow, hrow, hrow, hrow,
            full(wa.shape), full(wb.shape), full(wo.shape), full((1, D)), full(r_hi.shape), full(r_cat.shape),
        ],
        out_specs=[row(D), row(D // 2), row(LANES)],
        out_shape=[jax.ShapeDtypeStruct((T, D), F32), jax.ShapeDtypeStruct((T, D // 2), jnp.uint32),
                   jax.ShapeDtypeStruct((T, LANES), F32)],
        compiler_params=_params(("parallel",)),
    )(x2, o_a, proj, proj, dl_o[0], dl_o[1], dl_o[2], dl_lse[0], dl_lse[1], dl_lse[2],
      wa, wb, wo, gain_ffn, r_hi, r_cat)


def _pack_bf16_pairs(v):
    h = v.shape[1] // 2
    bits = pltpu.bitcast(v.astype(BF16).astype(F32), jnp.uint32)
    return (bits[:, :h] >> 16) | (bits[:, h:] & jnp.uint32(0xFFFF0000))


def _unpack_bf16_pairs(words):
    return pltpu.bitcast(words << 16, F32), pltpu.bitcast(words & jnp.uint32(0xFFFF0000), F32)


def _cast_rows(src_ref, dst_ref, chunk=CAST_ROWS):
    def body(c, carry):
        r0 = pl.multiple_of(c * chunk, chunk)
        dst_ref[pl.ds(r0, chunk), :] = src_ref[pl.ds(r0, chunk), :].astype(dst_ref.dtype)
        return carry
    lax.fori_loop(0, src_ref.shape[0] // chunk, body, 0)


def _expert_changed(be_ref, i):
    return (i == 0) | (be_ref[i] != be_ref[jnp.maximum(i - 1, 0)])


def _stream_expert_weights(i, live, be_ref, ne_ref, w_hbm, stage, wbf, wsem):
    changed = live & _expert_changed(be_ref, i)

    def copy(e):
        return pltpu.make_async_copy(w_hbm.at[e], stage, wsem)

    @pl.when(changed & (i == 0))
    def _():
        copy(be_ref[0]).start(priority=WEIGHT_DMA_PRIORITY)

    @pl.when(changed)
    def _():
        copy(be_ref[i]).wait()
        _cast_rows(stage, wbf)

    @pl.when(changed & (ne_ref[i] >= 0))
    def _():
        copy(ne_ref[i]).start(priority=WEIGHT_DMA_PRIORITY)


def _moe_kernel(be_ref, ne_ref, nu_ref, tc_ref, tn_ref, dp_ref, hn_ref, wgu_ref, wd_ref, y_ref,
                      xa, xb, ya, yb, gstage, gbf, dstage, dbf, gsem, ssem, wsem, *, d_ff):
    i = pl.program_id(0)
    nu = nu_ref[0]
    rows = xa.shape[0]
    even = i % 2 == 0

    def gather(tok_ref, r, buf, s):
        return pltpu.make_async_copy(hn_ref.at[tok_ref[0, r]], buf.at[r], gsem.at[s])

    def wait_gather(buf, s):
        pltpu.make_async_copy(hn_ref.at[pl.ds(0, rows)], buf, gsem.at[s]).wait()

    def scatter(r, buf, s):
        return pltpu.make_async_copy(buf.at[r], y_ref.at[dp_ref[0, r]], ssem.at[s])

    def wait_scatter(buf, s):
        pltpu.make_async_copy(buf, y_ref.at[pl.ds(0, rows)], ssem.at[s]).wait()

    @pl.when(i == 0)
    def _():
        n_res = y_ref.shape[0] - 2 * rows
        for s, buf in enumerate((ya, yb)):
            buf[...] = jnp.zeros(buf.shape, buf.dtype)
            pltpu.make_async_copy(buf, y_ref.at[pl.ds(n_res + s * rows, rows)], ssem.at[s]).start()
        for s, buf in enumerate((ya, yb)):
            pltpu.make_async_copy(buf, y_ref.at[pl.ds(n_res + s * rows, rows)], ssem.at[s]).wait()

    @pl.when((i == 0) & (nu > 0))
    def _():
        def body(r, c):
            gather(tc_ref, r, xa, 0).start()
            return c
        lax.fori_loop(0, rows, body, 0)

    @pl.when((i >= 2) & (i < nu + 2) & even)
    def _():
        wait_scatter(ya, 0)

    @pl.when((i >= 2) & (i < nu + 2) & jnp.logical_not(even))
    def _():
        wait_scatter(yb, 1)

    _stream_expert_weights(i, i < nu, be_ref, ne_ref, wgu_ref, gstage, gbf, wsem.at[0])
    _stream_expert_weights(i, i < nu, be_ref, ne_ref, wd_ref, dstage, dbf, wsem.at[1])

    def step(xcur, xnxt, ycur, yprv, s, do_scatter, do_compute):
        if do_compute:
            wait_gather(xcur, s)
            for r in range(rows):
                gather(tn_ref, r, xnxt, 1 - s).start()
        if do_scatter:
            for r in range(rows):
                scatter(r, yprv, 1 - s).start()
        if do_compute:
            x_lo, x_hi = _unpack_bf16_pairs(xcur[...])
            half = x_lo.shape[1]
            h = _dot(x_lo.astype(BF16), gbf[0:half, :]) + _dot(x_hi.astype(BF16), gbf[half:2 * half, :])
            gate = h[:, :d_ff]
            up = h[:, d_ff:]
            act = (gate * jax.nn.sigmoid(gate) * up).astype(BF16)
            ycur[...] = _pack_bf16_pairs(_dot(act, dbf[...]))
        else:
            wait_gather(xcur, s)

    for s, (xcur, xnxt, ycur, yprv) in enumerate(((xa, xb, ya, yb), (xb, xa, yb, ya))):
        par = even if s == 0 else jnp.logical_not(even)

        @pl.when((i >= 1) & (i < nu) & par)
        def _(a=(xcur, xnxt, ycur, yprv, s)):
            step(*a, True, True)

        @pl.when((i == 0) & (nu > 0) & par)
        def _(a=(xcur, xnxt, ycur, yprv, s)):
            step(*a, False, True)

        @pl.when((i == nu) & (nu > 0) & par)
        def _(a=(xcur, xnxt, ycur, yprv, s)):
            step(*a, True, False)


def _experts(hn, src_tok, dst_row, blk_expert, nxt_expert, n_used, wgu, wd, n_out_rows):
    D = wgu.shape[1]
    nblk = src_tok.shape[0] // MOE_BLOCK
    d_ff = wd.shape[1]
    tok = src_tok.reshape(nblk, 1, MOE_BLOCK)
    dst = dst_row.reshape(nblk, 1, MOE_BLOCK)
    smem = lambda f: pl.BlockSpec((None, 1, MOE_BLOCK), f, memory_space=pltpu.SMEM)
    u32 = jnp.uint32
    return pl.pallas_call(
        functools.partial(_moe_kernel, d_ff=d_ff),
        grid_spec=pltpu.PrefetchScalarGridSpec(
            num_scalar_prefetch=3,
            grid=(nblk,),
            in_specs=[
                smem(lambda i, be, ne, nu: (i, 0, 0)),
                smem(lambda i, be, ne, nu: (jnp.minimum(i + 1, nblk - 1), 0, 0)),
                smem(lambda i, be, ne, nu: (jnp.maximum(i - 1, 0), 0, 0)),
                pl.BlockSpec(memory_space=pl.ANY),
                pl.BlockSpec(memory_space=pl.ANY),
                pl.BlockSpec(memory_space=pl.ANY),
            ],
            out_specs=pl.BlockSpec(memory_space=pl.ANY),
            scratch_shapes=[pltpu.VMEM((MOE_BLOCK, D // 2), u32), pltpu.VMEM((MOE_BLOCK, D // 2), u32),
                            pltpu.VMEM((MOE_BLOCK, D // 2), u32), pltpu.VMEM((MOE_BLOCK, D // 2), u32),
                            pltpu.VMEM((D, 2 * d_ff), F32), pltpu.VMEM((D, 2 * d_ff), BF16),
                            pltpu.VMEM((d_ff, D), F32), pltpu.VMEM((d_ff, D), BF16),
                            pltpu.SemaphoreType.DMA((2,)), pltpu.SemaphoreType.DMA((2,)),
                            pltpu.SemaphoreType.DMA((2,))],
        ),
        out_shape=jax.ShapeDtypeStruct((n_out_rows, D // 2), u32),
        compiler_params=_params(("arbitrary",)),
    )(blk_expert, nxt_expert, n_used, tok, tok, dst, hn, wgu, wd)


def _combine_kernel(x1_ref, rt_ref, y1_ref, y2_ref, o_ref):
    rt = rt_ref[...]
    w1, w2 = rt[:, 2:3], rt[:, 3:4]
    half = o_ref.shape[1] // 2
    y1 = _unpack_bf16_pairs(y1_ref[...])
    y2 = _unpack_bf16_pairs(y2_ref[...])
    for p in range(2):
        cs = slice(p * half, (p + 1) * half)
        o_ref[:, cs] = x1_ref[:, cs] + (w1 * y1[p] + w2 * y2[p])


def _combine(x1, route, y, tm=COMBINE_ROWS):
    T, D = x1.shape
    nt = T // tm
    return pl.pallas_call(
        _combine_kernel,
        grid=(nt,),
        in_specs=[
            pl.BlockSpec((tm, D), lambda i: (i, 0)),
            pl.BlockSpec((tm, LANES), lambda i: (i, 0)),
            pl.BlockSpec((tm, D // 2), lambda i: (i, 0)),
            pl.BlockSpec((tm, D // 2), lambda i: (nt + i, 0)),
        ],
        out_specs=pl.BlockSpec((tm, D), lambda i: (i, 0)),
        out_shape=jax.ShapeDtypeStruct((T, D), F32),
        compiler_params=_params(("parallel",)),
    )(x1, route, y, y)


def _dispatch_tables(route, T):
    TK = T * MOE_TOP_K
    flat_e = route[:, :MOE_TOP_K].astype(jnp.int32).reshape(-1)
    order = jnp.argsort(flat_e).astype(jnp.int32)
    sizes = jnp.sum(flat_e[:, None] == jnp.arange(MOE_N_EXPERTS, dtype=jnp.int32)[None, :], axis=0,
                    dtype=jnp.int32)
    start = jnp.cumsum(sizes) - sizes
    padded = ((sizes + MOE_BLOCK - 1) // MOE_BLOCK) * MOE_BLOCK
    pad_end = jnp.cumsum(padded)
    pad_start = pad_end - padded
    n_blocks = TK // MOE_BLOCK + MOE_N_EXPERTS
    blk = jnp.arange(n_blocks + 2, dtype=jnp.int32)
    blk_expert = jnp.minimum(jnp.sum(pad_end[None, :] <= (blk * MOE_BLOCK)[:, None], axis=1, dtype=jnp.int32),
                             MOE_N_EXPERTS - 1)
    r = jnp.arange(MOE_BLOCK, dtype=jnp.int32)[None, :]
    pos = blk[:, None] * MOE_BLOCK + r
    off = pos - pad_start[blk_expert][:, None]
    live = (off < sizes[blk_expert][:, None]) & (pos < pad_end[-1])
    sorted_idx = jnp.where(live, off + start[blk_expert][:, None], 0)
    slot = order[sorted_idx]
    tok = slot // MOE_TOP_K
    src_tok = jnp.where(live, tok, 0).reshape(-1)
    trash = TK + (blk[:, None] % 2) * MOE_BLOCK + r
    dst_row = jnp.where(live, (slot % MOE_TOP_K) * T + tok, trash).reshape(-1)
    n_used = (pad_end[-1:] // MOE_BLOCK).astype(jnp.int32)
    eid = jnp.arange(MOE_N_EXPERTS, dtype=jnp.int32)
    later = (eid[None, :] > eid[:, None]) & (sizes[None, :] > 0)
    nxt = jnp.min(jnp.where(later, eid[None, :], MOE_N_EXPERTS), axis=1)
    nxt_expert = jnp.where(nxt < MOE_N_EXPERTS, nxt, -1).astype(jnp.int32)[blk_expert]
    return src_tok, dst_row, blk_expert, nxt_expert, n_used, TK + 2 * MOE_BLOCK


def _w_in_tile_order(d_gate):
    da = DA_HEADS * 2 * DA_QK_DIM
    dav = DA_HEADS * DA_V_DIM
    dl = len(DL_GROUPS) * DL_HEADS_PER_GROUP * DL_HEAD_DIM
    o = [int(v) // COL_TILE for v in np.cumsum([0, da, da, dav, dl, dl, dl, d_gate, d_gate])]
    order = list(range(o[6], o[8])) + list(range(o[0], o[3]))
    for g in range(len(DL_GROUPS)):
        order += [o[3] + g, o[4] + g, o[5] + g]
    assert sorted(order) == list(range(CT_END))
    return jnp.asarray(order, jnp.int32)


def _gain_table(da_q_norm, da_k_norm, dl_q_norm, dl_k_norm):
    ones = jnp.ones((COL_TILE,), F32)
    daq = jnp.tile(da_q_norm, COL_TILE // DA_QK_DIM) * (DA_QK_DIM ** -0.5 * LOG2E)
    dak = jnp.tile(da_k_norm, COL_TILE // DA_QK_DIM)
    dlq = jnp.tile(dl_q_norm, COL_TILE // DL_HEAD_DIM) * (DL_HEAD_DIM ** -0.5 * LOG2E)
    dlk = jnp.tile(dl_k_norm, COL_TILE // DL_HEAD_DIM)
    rows = []
    for j in range(CT_END):
        if CT_DA_Q <= j < CT_DA_K:
            rows.append(daq)
        elif CT_DA_K <= j < CT_DA_V:
            rows.append(dak)
        elif j >= CT_DL and (j - CT_DL) % 3 == 0:
            rows.append(dlq)
        elif j >= CT_DL and (j - CT_DL) % 3 == 1:
            rows.append(dlk)
        else:
            rows.append(ones)
    return jnp.stack(rows, axis=0).reshape(CT_END, 1, COL_TILE)


def kernel(x, norm_mix, w_in, da_q_norm, da_k_norm, da_lambda_q, da_lambda_k, da_sub_norm,
           dl_q_norm, dl_k_norm, w_branch_a, w_branch_b, w_out, norm_ffn,
           w_group_router, w_expert_router, w_gate_up, w_down):
    B, S, D = x.shape
    T = B * S
    depth = w_in.shape[0]
    x2 = x.reshape(T, D)
    for l in range(depth):
        lam_init = 0.8 - 0.6 * math.exp(-0.3 * l)
        gain_tab = _gain_table(da_q_norm[l], da_k_norm[l], dl_q_norm[l], dl_k_norm[l])
        proj, dl1, dl2 = _inproj(x2, norm_mix[l].reshape(1, D), w_in[l].astype(BF16), _w_in_tile_order(D),
                                 gain_tab, B, S)

        o_a = _diff_attention(proj, da_lambda_q[l], da_lambda_k[l], da_sub_norm[l].reshape(1, DA_V_DIM),
                              B, S, lam_init)
        dl = [_dilated_group(proj.reshape(B, S, proj.shape[1]), CT_DL, 0, B, S),
              _dilated_group(dl1, 0, 1, B, S), _dilated_group(dl2, 0, 2, B, S)]

        w_r = jnp.concatenate([w_expert_router[l], w_group_router[l]], axis=1)
        w_r = jnp.pad(w_r, ((0, 0), (0, LANES - w_r.shape[1])))
        r_hi = w_r.astype(BF16)
        r_lo = (w_r - r_hi.astype(F32)).astype(BF16)
        x1, hn, route = _outproj(
            x2, o_a, proj, [t[0] for t in dl], [t[1] for t in dl],
            w_branch_a[l].astype(BF16), w_branch_b[l].astype(BF16), w_out[l].astype(BF16),
            norm_ffn[l].reshape(1, D), r_hi, jnp.concatenate([r_hi, r_lo], axis=1))

        src_tok, dst_row, blk_expert, nxt_expert, n_used, n_rows = _dispatch_tables(route, T)
        y = _experts(hn, src_tok, dst_row, blk_expert, nxt_expert, n_used, w_gate_up[l], w_down[l], n_rows)
        x2 = _combine(x1, route, y)
    return x2.reshape(B, S, D)
```
